```python
import math
import jax, jax.numpy as jnp
from jax import lax
import numpy as np

D_MODEL = 1024
BATCH = 2
SEQ = 8192
DEPTH = 2

N_META = 16
HEAD_DIM = 64
N_Q_HEADS = 16
N_KV_HEADS = 4
Q_PER_KV = N_Q_HEADS // N_KV_HEADS
D_ATTN = N_Q_HEADS * HEAD_DIM
D_KV = N_KV_HEADS * HEAD_DIM
WINDOW = 128
BLOCK = 128
ROPE_THETA = 10000.0
ATTN_SCALE = HEAD_DIM ** -0.5
NEG_INF = -1e30
D_SSM = D_MODEL // 2
SSM_GROUP = 16
N_SSM_GROUPS = D_SSM // SSM_GROUP
SSM_STATE = 64
DT_MIN = 1e-3
DT_MAX = 1e-1
D_FF = 4 * D_MODEL
RMS_EPS = 1e-6
D_IN = D_SSM + D_ATTN + 2 * D_KV + 2 * D_MODEL

kernel_name = "hybrid_s5_swa_sink_gated_block"


def rmsnorm(x, gain):
    xf = x.astype(jnp.float32)
    xf = xf * lax.rsqrt(jnp.mean(xf * xf, axis=-1, keepdims=True) + RMS_EPS)
    return (xf * gain.astype(jnp.float32)).astype(x.dtype)


def rotary(t, pos):
    inv_freq = 1.0 / (ROPE_THETA ** (jnp.arange(0, HEAD_DIM, 2, dtype=jnp.float32) / HEAD_DIM))
    ang = pos.astype(jnp.float32)[:, None] * inv_freq[None, :]
    ang = jnp.concatenate([ang, ang], axis=-1)[None, :, None, :]
    tf = t.astype(jnp.float32)
    half = HEAD_DIM // 2
    rot = jnp.concatenate([-tf[..., half:], tf[..., :half]], axis=-1)
    return (tf * jnp.cos(ang) + rot * jnp.sin(ang)).astype(t.dtype)


def s5_branch(u, a_re, a_im, log_dt, b_re, b_im, c_re, c_im, d_skip, w_glu, b_glu):
    bsz, seq_len, _ = u.shape
    f32 = jnp.float32
    uf = u.astype(f32).reshape(bsz, seq_len, N_SSM_GROUPS, SSM_GROUP)
    a = lax.complex(a_re.astype(f32), a_im.astype(f32))
    dt = jnp.exp(log_dt.astype(f32))[:, None]
    a_bar = jnp.exp(a * dt)
    b = lax.complex(b_re.astype(f32), b_im.astype(f32))
    b_bar = ((a_bar - 1.0) / a)[..., None] * b
    bu = jnp.einsum('blgc,gnc->blgn', uf.astype(jnp.complex64), b_bar)

    def combine(e1, e2):
        a1, s1 = e1
        a2, s2 = e2
        return a1 * a2, a2 * s1 + s2

    a_seq = jnp.broadcast_to(a_bar, bu.shape)
    _, states = lax.associative_scan(combine, (a_seq, bu), axis=1)
    c = lax.complex(c_re.astype(f32), c_im.astype(f32))
    y = jnp.einsum('blgn,gcn->blgc', states, c).real + d_skip.astype(f32) * uf
    y = y.reshape(bsz, seq_len, D_SSM)
    z = jax.nn.gelu(y)
    z = z * jax.nn.sigmoid(z @ w_glu.astype(f32) + b_glu.astype(f32))
    return z.astype(u.dtype)


def sliding_window_attention(q, k, v, sinks):
    bsz, seq_len = q.shape[0], q.shape[1]
    pad = (-seq_len) % BLOCK
    n_blk = (seq_len + pad) // BLOCK
    qb = jnp.pad(q, ((0, 0), (pad, 0), (0, 0), (0, 0))).reshape(
        bsz, n_blk, BLOCK, N_KV_HEADS, Q_PER_KV, HEAD_DIM)
    front = ((0, 0), (pad + BLOCK, 0), (0, 0), (0, 0))
    kb = jnp.pad(k, front).reshape(bsz, n_blk + 1, BLOCK, N_KV_HEADS, HEAD_DIM)
    vb = jnp.pad(v, front).reshape(bsz, n_blk + 1, BLOCK, N_KV_HEADS, HEAD_DIM)
    k_win = jnp.concatenate([kb[:, :-1], kb[:, 1:]], axis=2)
    v_win = jnp.concatenate([vb[:, :-1], vb[:, 1:]], axis=2)
    k_meta, v_meta = k[:, :N_META], v[:, :N_META]

    s_meta = jnp.einsum('bnqkgd,bskd->bnkgqs', qb, k_meta, preferred_element_type=jnp.float32)
    s_win = jnp.einsum('bnqkgd,bnskd->bnkgqs', qb, k_win, preferred_element_type=jnp.float32)
    s = jnp.concatenate([s_meta, s_win], axis=-1) * ATTN_SCALE

    q_idx = (jnp.arange(n_blk)[:, None] * BLOCK + jnp.arange(BLOCK)[None, :])[:, :, None]
    kw_idx = ((jnp.arange(n_blk)[:, None] - 1) * BLOCK + jnp.arange(2 * BLOCK)[None, :])[:, None, :]
    km_idx = (pad + jnp.arange(N_META))[None, None, :]
    win_ok = (kw_idx >= pad) & (kw_idx <= q_idx) & (q_idx - kw_idx < WINDOW)
    meta_ok = (km_idx <= q_idx) & (q_idx - km_idx >= WINDOW)
    ok = jnp.concatenate([jnp.broadcast_to(meta_ok, (n_blk, BLOCK, N_META)), win_ok], axis=-1)
    s = jnp.where(ok[None, :, None, None], s, NEG_INF)

    sink = sinks.astype(jnp.float32).reshape(N_KV_HEADS, Q_PER_KV)[None, None, :, :, None, None]
    m = jnp.maximum(jnp.max(s, axis=-1, keepdims=True), sink)
    p = jnp.exp(s - m)
    p = p / (jnp.sum(p, axis=-1, keepdims=True) + jnp.exp(sink - m))
    p = p.astype(v.dtype)
    o = (jnp.einsum('bnkgqs,bskd->bnqkgd', p[..., :N_META], v_meta)
         + jnp.einsum('bnkgqs,bnskd->bnqkgd', p[..., N_META:], v_win))
    return o.reshape(bsz, n_blk * BLOCK, D_ATTN)[:, pad:]


def hybrid_mixer(h, pos, w_in, a_re, a_im, log_dt, b_re, b_im, c_re, c_im, d_skip, w_glu, b_glu,
                 sinks, w_o_ssm, w_o_attn, w_out):
    bsz, seq_len, _ = h.shape
    proj = h @ w_in
    cuts = [D_SSM, D_SSM + D_ATTN, D_SSM + D_ATTN + D_KV, D_SSM + D_ATTN + 2 * D_KV,
            D_SSM + D_ATTN + 2 * D_KV + D_MODEL]
    u, q, k, v, g_ssm, g_attn = jnp.split(proj, cuts, axis=-1)
    y_ssm = s5_branch(u, a_re, a_im, log_dt, b_re, b_im, c_re, c_im, d_skip, w_glu, b_glu)
    q = rotary(q.reshape(bsz, seq_len, N_Q_HEADS, HEAD_DIM), pos)
    k = rotary(k.reshape(bsz, seq_len, N_KV_HEADS, HEAD_DIM), pos)
    v = v.reshape(bsz, seq_len, N_KV_HEADS, HEAD_DIM)
    y_attn = sliding_window_attention(q, k, v, sinks)
    merged = jax.nn.sigmoid(g_ssm) * (y_ssm @ w_o_ssm) + jax.nn.sigmoid(g_attn) * (y_attn @ w_o_attn)
    return merged @ w_out


def setup_inputs(seed: int = 0) -> dict:
    key = jax.random.key(seed)
    ks = jax.random.split(key, 24)
    f32 = jnp.float32

    def nrm(k, shape, std):
        return std * jax.random.normal(k, shape, f32)

    G, N, C = N_SSM_GROUPS, SSM_STATE, SSM_GROUP
    a_im_init = math.pi * jnp.arange(N, dtype=f32)[None, None, :]
    return {
        "x": nrm(ks[0], (BATCH, SEQ, D_MODEL), 1.0),
        "meta_tokens": nrm(ks[1], (N_META, D_MODEL), 1.0),
        "norm_mix_pre": 1.0 + nrm(ks[2], (DEPTH, D_MODEL), 0.02),
        "norm_mix_post": 1.0 + nrm(ks[3], (DEPTH, D_MODEL), 0.02),
        "norm_mlp_pre": 1.0 + nrm(ks[4], (DEPTH, D_MODEL), 0.02),
        "norm_mlp_post": 1.0 + nrm(ks[5], (DEPTH, D_MODEL), 0.02),
        "w_in": nrm(ks[6], (DEPTH, D_MODEL, D_IN), D_MODEL ** -0.5),
        "ssm_a_re": -0.5 + nrm(ks[7], (DEPTH, G, N), 0.01),
        "ssm_a_im": a_im_init + nrm(ks[8], (DEPTH, G, N), 0.01),
        "ssm_log_dt": jax.random.uniform(ks[9], (DEPTH, G), f32, math.log(DT_MIN), math.log(DT_MAX)),
        "ssm_b_re": nrm(ks[10], (DEPTH, G, N, C), (2.0 * C) ** -0.5),
        "ssm_b_im": nrm(ks[11], (DEPTH, G, N, C), (2.0 * C) ** -0.5),
        "ssm_c_re": nrm(ks[12], (DEPTH, G, C, N), 0.5),
        "ssm_c_im": nrm(ks[13], (DEPTH, G, C, N), 0.5),
        "ssm_d": nrm(ks[14], (DEPTH, G, C), 1.0),
        "w_glu": nrm(ks[15], (DEPTH, D_SSM, D_SSM), D_SSM ** -0.5),
        "b_glu": nrm(ks[16], (DEPTH, D_SSM), 0.01),
        "attn_sinks": nrm(ks[17], (DEPTH, N_Q_HEADS), 0.5),
        "w_o_ssm": nrm(ks[18], (DEPTH, D_SSM, D_MODEL), D_SSM ** -0.5),
        "w_o_attn": nrm(ks[19], (DEPTH, D_ATTN, D_MODEL), D_ATTN ** -0.5),
        "w_out": nrm(ks[20], (DEPTH, D_MODEL, D_MODEL), D_MODEL ** -0.5),
        "w_up": nrm(ks[21], (DEPTH, D_MODEL, D_FF), D_MODEL ** -0.5),
        "w_down": nrm(ks[22], (DEPTH, D_FF, D_MODEL), D_FF ** -0.5),
    }


def reference(x, meta_tokens, norm_mix_pre, norm_mix_post, norm_mlp_pre, norm_mlp_post, w_in,
              ssm_a_re, ssm_a_im, ssm_log_dt, ssm_b_re, ssm_b_im, ssm_c_re, ssm_c_im, ssm_d,
              w_glu, b_glu, attn_sinks, w_o_ssm, w_o_attn, w_out, w_up, w_down):
    bsz = x.shape[0]
    meta = jnp.broadcast_to(meta_tokens[None].astype(x.dtype), (bsz, N_META, D_MODEL))
    hres = jnp.concatenate([meta, x], axis=1)
    pos = jnp.arange(hres.shape[1], dtype=jnp.int32)
    for l in range(DEPTH):
        h = rmsnorm(hres, norm_mix_pre[l])
        mix = hybrid_mixer(h, pos, w_in[l], ssm_a_re[l], ssm_a_im[l], ssm_log_dt[l], ssm_b_re[l],
                           ssm_b_im[l], ssm_c_re[l], ssm_c_im[l], ssm_d[l], w_glu[l], b_glu[l],
                           attn_sinks[l], w_o_ssm[l], w_o_attn[l], w_out[l])
        hres = hres + rmsnorm(mix, norm_mix_post[l])
        h = rmsnorm(hres, norm_mlp_pre[l])
        ff = jnp.square(jax.nn.relu(h @ w_up[l])) @ w_down[l]
        hres = hres + rmsnorm(ff, norm_mlp_post[l])
    return hres[:, N_META:]
```

```python
import functools
import math

import jax
import jax.numpy as jnp
import numpy as np
from jax import lax
from jax.experimental import pallas as pl
from jax.experimental.pallas import tpu as pltpu

D_MODEL = 1024
N_META = 16
HEAD_DIM = 64
N_Q_HEADS = 16
N_KV_HEADS = 4
D_ATTN = N_Q_HEADS * HEAD_DIM
D_KV = N_KV_HEADS * HEAD_DIM
BLOCK = 128
ROPE_THETA = 10000.0
ATTN_SCALE = HEAD_DIM ** -0.5
NEG_INF = -1e30
D_SSM = D_MODEL // 2
SSM_GROUP = 16
N_SSM_GROUPS = D_SSM // SSM_GROUP
N_SSM_PAIRS = N_SSM_GROUPS // 2
SSM_STATE = 64
D_FF = 4 * D_MODEL
RMS_EPS = 1e-6

PAD = BLOCK - N_META
CHUNK = 16
CHUNK_W = CHUNK * SSM_GROUP
LANES = 128
VMEM_LIMIT = 56 * 1024 * 1024

C_U = 0
C_Q = C_U + D_SSM
C_K = C_Q + D_ATTN
C_V = C_K + 2 * D_KV
C_GS = C_V + 2 * D_KV
C_GA = C_GS + D_MODEL
C_END = C_GA + D_MODEL

BF16 = jnp.bfloat16
F32 = jnp.float32


def _dot(a, b):
    return jnp.dot(a, b, preferred_element_type=F32)


def _rms(x, gain):
    return x * lax.rsqrt(jnp.mean(x * x, axis=-1, keepdims=True) + RMS_EPS) * gain


def _const_spec(shape):
    return pl.BlockSpec(shape, lambda *_: (0,) * len(shape), pipeline_mode=pl.Buffered(1))


def _params(sem):
    return pltpu.CompilerParams(dimension_semantics=sem, vmem_limit_bytes=VMEM_LIMIT)


def _inproj_kernel(x_ref, gain_ref, w_ref, cos_ref, sa_ref, sb_ref,
                   u_ref, q_ref, k_ref, v_ref, gs_ref, ga_ref):
    h = _rms(x_ref[...], gain_ref[...]).astype(BF16)
    cos, sa, sb = cos_ref[...], sa_ref[...], sb_ref[...]

    def rope(t):
        return t * cos + pltpu.roll(t, LANES - HEAD_DIM // 2, 1) * sa + pltpu.roll(t, HEAD_DIM // 2, 1) * sb

    step = 512
    u_ref[...] = _dot(h, w_ref[:, C_U:C_U + D_SSM]).astype(BF16)
    for c in range(0, D_ATTN, step):
        t = _dot(h, w_ref[:, C_Q + c:C_Q + c + step])
        for j in range(0, step, LANES):
            q_ref[:, c + j:c + j + LANES] = (rope(t[:, j:j + LANES]) * ATTN_SCALE).astype(BF16)
    t = _dot(h, w_ref[:, C_K:C_K + 2 * D_KV])
    for j in range(0, 2 * D_KV, LANES):
        k_ref[:, j:j + LANES] = rope(t[:, j:j + LANES]).astype(BF16)
    v_ref[...] = _dot(h, w_ref[:, C_V:C_V + 2 * D_KV]).astype(BF16)
    for c in range(0, D_MODEL, step):
        gs_ref[:, c:c + step] = _dot(h, w_ref[:, C_GS + c:C_GS + c + step]).astype(BF16)
        ga_ref[:, c:c + step] = _dot(h, w_ref[:, C_GA + c:C_GA + c + step]).astype(BF16)


def _inproj(x, gain, w, cos, sa, sb, tm, tiles_per_batch):
    rows = x.shape[0]
    row = lambda width: pl.BlockSpec((tm, width), lambda i: (i, 0))
    tab = pl.BlockSpec((tm, LANES), lambda i: (i % tiles_per_batch, 0))
    widths = (D_SSM, D_ATTN, 2 * D_KV, 2 * D_KV, D_MODEL, D_MODEL)
    return pl.pallas_call(
        _inproj_kernel,
        grid=(rows // tm,),
        in_specs=[row(D_MODEL), _const_spec((1, D_MODEL)), _const_spec((D_MODEL, C_END)), tab, tab, tab],
        out_specs=[row(wd) for wd in widths],
        out_shape=[jax.ShapeDtypeStruct((rows, wd), BF16) for wd in widths],
        compiler_params=_params(("parallel",)),
        name="inproj",
    )(x, gain, w, cos, sa, sb)


def _ssm_state_kernel(u_ref, pre_ref, pim_ref, vre_ref, vim_ref):
    u0, u1 = u_ref[0], u_ref[1]
    vre_ref[0] = _dot(u0, pre_ref[0, :CHUNK_W]) + _dot(u1, pre_ref[0, CHUNK_W:])
    vim_ref[0] = _dot(u0, pim_ref[0, :CHUNK_W]) + _dot(u1, pim_ref[0, CHUNK_W:])


def _ssm_state(u, pre, pim):
    rows = u.shape[1]
    out = pl.BlockSpec((1, rows, LANES), lambda p: (p, 0, 0))
    wspec = pl.BlockSpec((1, 2 * CHUNK_W, LANES), lambda p: (p, 0, 0))
    return pl.pallas_call(
        _ssm_state_kernel,
        grid=(N_SSM_PAIRS,),
        in_specs=[pl.BlockSpec((2, rows, CHUNK_W), lambda p: (p, 0, 0)), wspec, wspec],
        out_specs=[out, out],
        out_shape=[jax.ShapeDtypeStruct((N_SSM_PAIRS, rows, LANES), F32)] * 2,
        compiler_params=_params(("parallel",)),
        name="ssm_state",
    )(u, pre, pim)


def _ssm_scan_kernel(vre_ref, vim_ref, are_ref, aim_ref, sre_ref, sim_ref, cre_ref, cim_ref):
    @pl.when(pl.program_id(0) == 0)
    def _():
        cre_ref[...] = jnp.zeros_like(cre_ref)
        cim_ref[...] = jnp.zeros_like(cim_ref)

    are, aim = are_ref[...], aim_ref[...]

    def body(j, carry):
        sre, sim = carry
        sre_ref[j] = sre
        sim_ref[j] = sim
        return (are * sre - aim * sim + vre_ref[j], are * sim + aim * sre + vim_ref[j])

    sre, sim = lax.fori_loop(0, vre_ref.shape[0], body, (cre_ref[...], cim_ref[...]))
    cre_ref[...] = sre
    cim_ref[...] = sim


def _ssm_scan(vre, vim, are, aim, steps):
    n_chunks, n_rows, _ = vre.shape
    blk = pl.BlockSpec((steps, n_rows, LANES), lambda i: (i, 0, 0))
    return pl.pallas_call(
        _ssm_scan_kernel,
        grid=(n_chunks // steps,),
        in_specs=[blk, blk, _const_spec((n_rows, LANES)), _const_spec((n_rows, LANES))],
        out_specs=[blk, blk],
        out_shape=[jax.ShapeDtypeStruct(vre.shape, F32)] * 2,
        scratch_shapes=[pltpu.VMEM((n_rows, LANES), F32)] * 2,
        compiler_params=_params(("arbitrary",)),
        name="ssm_scan",
    )(vre, vim, are, aim)


def _ssm_out_kernel(u_ref, s_ref, toep_ref, q_ref, y_ref):
    y_ref[0] = (_dot(u_ref[0], toep_ref[0]) + _dot(s_ref[0], q_ref[0])).astype(BF16)


def _ssm_out(u, s, toep, q):
    rows = u.shape[1]
    sq = pl.BlockSpec((1, CHUNK_W, CHUNK_W), lambda g: (g, 0, 0))
    return pl.pallas_call(
        _ssm_out_kernel,
        grid=(N_SSM_GROUPS,),
        in_specs=[pl.BlockSpec((1, rows, CHUNK_W), lambda g: (g, 0, 0)),
                  pl.BlockSpec((1, rows, 2 * LANES), lambda g: (g // 2, 0, 0)), sq, sq],
        out_specs=pl.BlockSpec((1, rows, CHUNK_W), lambda g: (g, 0, 0)),
        out_shape=jax.ShapeDtypeStruct((N_SSM_GROUPS, rows, CHUNK_W), BF16),
        compiler_params=_params(("parallel",)),
        name="ssm_out",
    )(u, s, toep, q)


def _ssm_prep(a_re, a_im, log_dt, b_re, b_im, c_re, c_im, d_skip, bsz):
    hp = lax.Precision.HIGHEST
    G, N, C, T = N_SSM_GROUPS, SSM_STATE, SSM_GROUP, CHUNK
    dt = jnp.exp(log_dt)[:, None]
    lam_re, lam_im = a_re * dt, a_im * dt
    mag = jnp.exp(lam_re)
    abar_re, abar_im = mag * jnp.cos(lam_im), mag * jnp.sin(lam_im)
    den = a_re * a_re + a_im * a_im
    nre, nim = abar_re - 1.0, abar_im
    coef_re = (nre * a_re + nim * a_im) / den
    coef_im = (nim * a_re - nre * a_im) / den
    bb_re = coef_re[..., None] * b_re - coef_im[..., None] * b_im
    bb_im = coef_re[..., None] * b_im + coef_im[..., None] * b_re
    j = jnp.arange(T + 1, dtype=F32)[:, None, None]
    pmag = jnp.exp(lam_re[None] * j)
    pw_re, pw_im = pmag * jnp.cos(lam_im[None] * j), pmag * jnp.sin(lam_im[None] * j)
    w_re = pw_re[..., None] * bb_re[None] - pw_im[..., None] * bb_im[None]
    w_im = pw_re[..., None] * bb_im[None] + pw_im[..., None] * bb_re[None]

    taps = (jnp.einsum('gdn,jgnc->gjdc', c_re, w_re[:T], precision=hp)
            - jnp.einsum('gdn,jgnc->gjdc', c_im, w_im[:T], precision=hp))
    taps = taps.at[:, 0].add(d_skip[:, :, None] * jnp.eye(C, dtype=F32)[None])
    lag = np.arange(T)[None, :] - np.arange(T)[:, None]
    toep = taps[:, np.clip(lag, 0, T - 1)]
    toep = jnp.where((lag >= 0)[None, :, :, None, None], toep, 0.0)
    toep = toep.transpose(0, 1, 4, 2, 3).reshape(G, CHUNK_W, CHUNK_W)

    def pair_cols(w):
        w = w[T - 1::-1].transpose(1, 0, 3, 2).reshape(G // 2, 2, CHUNK_W, N)
        z = jnp.zeros_like(w[:, 0])
        return jnp.concatenate([jnp.concatenate([w[:, 0], z], -1), jnp.concatenate([z, w[:, 1]], -1)], 1)

    pre, pim = pair_cols(w_re[:T]), pair_cols(w_im[:T])

    p1_re, p1_im = pw_re[1:].transpose(1, 0, 2), pw_im[1:].transpose(1, 0, 2)
    q_re = (c_re[:, None] * p1_re[:, :, None] - c_im[:, None] * p1_im[:, :, None])
    q_im = -(c_re[:, None] * p1_im[:, :, None] + c_im[:, None] * p1_re[:, :, None])
    q_re = q_re.transpose(0, 3, 1, 2).reshape(G, N, CHUNK_W)
    q_im = q_im.transpose(0, 3, 1, 2).reshape(G, N, CHUNK_W)
    z = jnp.zeros_like(q_re)
    even = (jnp.arange(G) % 2 == 0)[:, None, None]
    lo = lambda m: jnp.where(even, m, z)
    hi = lambda m: jnp.where(even, z, m)
    qmat = jnp.concatenate([lo(q_re), hi(q_re), lo(q_im), hi(q_im)], axis=1)

    a16_re = jnp.repeat(pw_re[T].reshape(G // 2, 2 * N), bsz, axis=0)
    a16_im = jnp.repeat(pw_im[T].reshape(G // 2, 2 * N), bsz, axis=0)
    return toep.astype(BF16), pre.astype(BF16), pim.astype(BF16), qmat.astype(BF16), a16_re, a16_im


def _s5_branch(u, prep, bsz, n_chunks):
    toep, pre, pim, qmat, a16_re, a16_im = prep
    G, P = N_SSM_GROUPS, N_SSM_PAIRS
    rows = bsz * n_chunks
    ug = u.reshape(bsz, n_chunks, CHUNK, G, SSM_GROUP).transpose(3, 0, 1, 2, 4).reshape(G, rows, CHUNK_W)
    vre, vim = _ssm_state(ug, pre, pim)
    to_scan = lambda v: v.reshape(P, bsz, n_chunks, LANES).transpose(2, 0, 1, 3).reshape(n_chunks, P * bsz, LANES)
    sre, sim = _ssm_scan(to_scan(vre), to_scan(vim), a16_re, a16_im, steps=n_chunks // 8)
    from_scan = lambda s: s.reshape(n_chunks, P, bsz, LANES).transpose(1, 2, 0, 3).reshape(P, rows, LANES)
    s = jnp.concatenate([from_scan(sre), from_scan(sim)], axis=-1).astype(BF16)
    y = _ssm_out(ug, s, toep, qmat)
    return y.reshape(G, bsz, n_chunks, CHUNK, SSM_GROUP).transpose(1, 2, 3, 0, 4).reshape(bsz * n_chunks * CHUNK, D_SSM)


def _attn_kernel(sink_ref, bias_ref, q_ref, km_ref, kp_ref, kc_ref, vm_ref, vp_ref, vc_ref, o_ref):
    bias = bias_ref[0]
    lane = lax.broadcasted_iota(jnp.int32, (3 * BLOCK, LANES), 1)
    first = lane < HEAD_DIM
    olane = lax.broadcasted_iota(jnp.int32, (BLOCK, LANES), 1) < HEAD_DIM
    zero = jnp.zeros((3 * BLOCK, LANES), BF16)
    for kh in range(N_KV_HEADS):
        cols = slice(kh * LANES, (kh + 1) * LANES)
        kd = jnp.concatenate([km_ref[:, cols], kp_ref[:, cols], kc_ref[:, cols]], axis=0)
        vd = jnp.concatenate([vm_ref[:, cols], vp_ref[:, cols], vc_ref[:, cols]], axis=0)
        kab = jnp.concatenate([jnp.where(first, kd, zero), jnp.where(first, zero, kd)], axis=0)
        vab = jnp.concatenate([jnp.where(first, vd, zero), jnp.where(first, zero, vd)], axis=0)
        for pr in range(2):
            hp = kh * 2 + pr
            q2 = q_ref[:, hp * LANES:(hp + 1) * LANES]
            s = lax.dot_general(q2, kab, (((1,), (1,)), ((), ())), preferred_element_type=F32)
            ps, rs = [], []
            for e in range(2):
                se = s[:, e * 3 * BLOCK:(e + 1) * 3 * BLOCK] + bias
                sink = sink_ref[2 * hp + e]
                m = jnp.maximum(jnp.max(se, axis=-1, keepdims=True), sink)
                p = jnp.exp(se - m)
                den = jnp.sum(p, axis=-1, keepdims=True) + jnp.exp(sink - m)
                ps.append(p.astype(BF16))
                rs.append(1.0 / den)
            o2 = _dot(jnp.concatenate(ps, axis=1), vab)
            o_ref[:, hp * LANES:(hp + 1) * LANES] = (o2 * jnp.where(olane, rs[0], rs[1])).astype(BF16)


def _attention(q, kd, vd, sinks, bias, bsz, n_blk):
    cur = lambda b, n: (b * n_blk + n, 0)
    prev = lambda b, n: (b * n_blk + jnp.maximum(n - 1, 0), 0)
    meta = lambda b, n: (b * n_blk, 0)
    kv = lambda imap: pl.BlockSpec((BLOCK, 2 * D_KV), imap)
    return pl.pallas_call(
        _attn_kernel,
        grid=(bsz, n_blk),
        in_specs=[pl.BlockSpec(memory_space=pltpu.SMEM),
                  pl.BlockSpec((1, BLOCK, 3 * BLOCK), lambda b, n: (jnp.minimum(n, 2), 0, 0)),
                  pl.BlockSpec((BLOCK, D_ATTN), cur),
                  kv(meta), kv(prev), kv(cur), kv(meta), kv(prev), kv(cur)],
        out_specs=pl.BlockSpec((BLOCK, D_ATTN), cur),
        out_shape=jax.ShapeDtypeStruct(q.shape, BF16),
        compiler_params=_params(("parallel", "parallel")),
        name="attention",
    )(sinks, bias, q, kd, kd, kd, vd, vd, vd)


def _attn_bias():
    i = np.arange(BLOCK)[:, None]
    j = np.arange(BLOCK)[None, :]
    is_meta = np.broadcast_to(j >= PAD, (BLOCK, BLOCK))
    none = np.zeros((BLOCK, BLOCK), bool)
    causal = j <= i
    blk0 = [none, none, causal & is_meta]
    blk1 = [none, is_meta, causal]
    blk2 = [is_meta, j > i, causal]
    ok = np.stack([np.concatenate(b, axis=1) for b in (blk0, blk1, blk2)])
    return jnp.asarray(np.where(ok, 0.0, NEG_INF), F32)


def _merge_kernel(x_ref, y_ref, a_ref, gs_ref, ga_ref, wglu_ref, bglu_ref, wos_ref, woa_ref, wout_ref,
                  gain_ref, o_ref):
    z = jax.nn.gelu(y_ref[...].astype(F32))
    z = z * jax.nn.sigmoid(_dot(z.astype(BF16), wglu_ref[...]) + bglu_ref[...])
    merged = (jax.nn.sigmoid(gs_ref[...].astype(F32)) * _dot(z.astype(BF16), wos_ref[...])
              + jax.nn.sigmoid(ga_ref[...].astype(F32)) * _dot(a_ref[...], woa_ref[...]))
    mix = _dot(merged.astype(BF16), wout_ref[...])
    o_ref[...] = x_ref[...] + _rms(mix, gain_ref[...])


def _merge(x, y, a, gs, ga, wglu, bglu, wos, woa, wout, gain, tm):
    rows = x.shape[0]
    row = lambda width: pl.BlockSpec((tm, width), lambda i: (i, 0))
    return pl.pallas_call(
        _merge_kernel,
        grid=(rows // tm,),
        in_specs=[row(D_MODEL), row(D_SSM), row(D_ATTN), row(D_MODEL), row(D_MODEL),
                  _const_spec((D_SSM, D_SSM)), _const_spec((1, D_SSM)), _const_spec((D_SSM, D_MODEL)),
                  _const_spec((D_ATTN, D_MODEL)), _const_spec((D_MODEL, D_MODEL)), _const_spec((1, D_MODEL))],
        out_specs=row(D_MODEL),
        out_shape=jax.ShapeDtypeStruct(x.shape, F32),
        compiler_params=_params(("parallel",)),
        name="merge",
    )(x, y, a, gs, ga, wglu, bglu, wos, woa, wout, gain)


def _ffn_kernel(x_ref, gpre_ref, wup_ref, wdown_ref, gpost_ref, o_ref):
    x = x_ref[...]
    h = _rms(x, gpre_ref[...]).astype(BF16)
    step = 512
    acc = jnp.zeros(x.shape, F32)
    for c in range(0, D_FF, step):
        a = jnp.maximum(_dot(h, wup_ref[:, c:c + step]), 0.0)
        acc = acc + _dot((a * a).astype(BF16), wdown_ref[c:c + step, :])
    o_ref[...] = x + _rms(acc, gpost_ref[...])


def _ffn(x, gpre, wup, wdown, gpost, tm):
    rows = x.shape[0]
    row = pl.BlockSpec((tm, D_MODEL), lambda i: (i, 0))
    return pl.pallas_call(
        _ffn_kernel,
        grid=(rows // tm,),
        in_specs=[row, _const_spec((1, D_MODEL)), _const_spec((D_MODEL, D_FF)),
                  _const_spec((D_FF, D_MODEL)), _const_spec((1, D_MODEL))],
        out_specs=row,
        out_shape=jax.ShapeDtypeStruct(x.shape, F32),
        compiler_params=_params(("parallel",)),
        name="ffn",
    )(x, gpre, wup, wdown, gpost)


def _rope_tables(seq_pad):
    pos = (jnp.arange(seq_pad, dtype=jnp.int32) - PAD).astype(F32)
    inv_freq = 1.0 / (ROPE_THETA ** (jnp.arange(0, HEAD_DIM, 2, dtype=F32) / HEAD_DIM))
    ang = pos[:, None] * inv_freq[None, :]
    ang = jnp.concatenate([ang, ang, ang, ang], axis=-1)
    first_half = (np.arange(LANES) % HEAD_DIM) < HEAD_DIM // 2
    sin = jnp.sin(ang)
    return jnp.cos(ang), jnp.where(first_half, -sin, 0.0), jnp.where(first_half, 0.0, sin)


def _widen_w_in(w):
    cuts = np.cumsum([D_SSM, D_ATTN, D_KV, D_KV, D_MODEL])
    wu, wq, wk, wv, wgs, wga = jnp.split(w, cuts, axis=-1)
    dup = lambda m: jnp.tile(m.reshape(D_MODEL, N_KV_HEADS, 1, HEAD_DIM), (1, 1, 2, 1)).reshape(D_MODEL, 2 * D_KV)
    return jnp.concatenate([wu, wq, dup(wk), dup(wv), wgs, wga], axis=-1).astype(BF16)


def kernel(x, meta_tokens, norm_mix_pre, norm_mix_post, norm_mlp_pre, norm_mlp_post, w_in, ssm_a_re, ssm_a_im, ssm_log_dt, ssm_b_re, ssm_b_im, ssm_c_re, ssm_c_im, ssm_d, w_glu, b_glu, attn_sinks, w_o_ssm, w_o_attn, w_out, w_up, w_down):
    bsz, seq, _ = x.shape
    depth = w_in.shape[0]
    seq_pad = PAD + N_META + seq
    assert seq_pad % BLOCK == 0 and (seq_pad // CHUNK) % 8 == 0
    n_blk, n_chunks = seq_pad // BLOCK, seq_pad // CHUNK
    tm = 640
    assert seq_pad % tm == 0

    meta = jnp.broadcast_to(meta_tokens[None].astype(x.dtype), (bsz, N_META, D_MODEL))
    hres = jnp.concatenate([jnp.zeros((bsz, PAD, D_MODEL), x.dtype), meta, x], axis=1)
    hres = hres.reshape(bsz * seq_pad, D_MODEL)
    cos, sa, sb = _rope_tables(seq_pad)
    bias = _attn_bias()

    for l in range(depth):
        u, q, kd, vd, gs, ga = _inproj(hres, norm_mix_pre[l][None], _widen_w_in(w_in[l]), cos, sa, sb,
                                       tm, seq_pad // tm)
        prep = _ssm_prep(ssm_a_re[l], ssm_a_im[l], ssm_log_dt[l], ssm_b_re[l], ssm_b_im[l],
                         ssm_c_re[l], ssm_c_im[l], ssm_d[l], bsz)
        y = _s5_branch(u, prep, bsz, n_chunks)
        att = _attention(q, kd, vd, attn_sinks[l], bias, bsz, n_blk)
        hres = _merge(hres, y, att, gs, ga, w_glu[l].astype(BF16), b_glu[l][None], w_o_ssm[l].astype(BF16),
                      w_o_attn[l].astype(BF16), w_out[l].astype(BF16), norm_mix_post[l][None], tm)
        hres = _ffn(hres, norm_mlp_pre[l][None], w_up[l].astype(BF16), w_down[l].astype(BF16),
                    norm_mlp_post[l][None], tm)
    return hres.reshape(bsz, seq_pad, D_MODEL)[:, BLOCK:]
```

```python
import jax
import jax.numpy as jnp
import numpy as np
from jax import lax
from jax.experimental import pallas as pl
from jax.experimental.pallas import tpu as pltpu

D_MODEL = 1024
N_META = 16
HEAD_DIM = 64
N_Q_HEADS = 16
N_KV_HEADS = 4
D_ATTN = N_Q_HEADS * HEAD_DIM
D_KV = N_KV_HEADS * HEAD_DIM
BLOCK = 128
ROPE_THETA = 10000.0
ATTN_SCALE = HEAD_DIM ** -0.5
NEG_INF = -1e30
D_SSM = D_MODEL // 2
SSM_GROUP = 16
N_SSM_GROUPS = D_SSM // SSM_GROUP
SSM_STATE = 64
D_FF = 4 * D_MODEL
RMS_EPS = 1e-6

LANES = 128
SUBLANES = 8
MXU = 256
PAD = BLOCK - N_META
CHUNK = 16
TILE_GROUPS = LANES // SSM_GROUP
N_SSM_TILES = D_SSM // LANES
TILE_STATE = TILE_GROUPS * SSM_STATE
CHUNK_W = CHUNK * LANES
N_SEG = SUBLANES
VMEM_LIMIT = 56 * 1024 * 1024

C_U = 0
C_Q = C_U + D_SSM
C_K = C_Q + D_ATTN
C_V = C_K + 2 * D_KV
C_GS = C_V + 2 * D_KV
C_GA = C_GS + D_MODEL
C_END = C_GA + D_MODEL

BF16 = jnp.bfloat16
F32 = jnp.float32


def _dot(a, b):
    return jnp.dot(a, b, preferred_element_type=F32)


def _rms(x, gain):
    return x * lax.rsqrt(jnp.mean(x * x, axis=-1, keepdims=True) + RMS_EPS) * gain


def _const_spec(shape):
    return pl.BlockSpec(shape, lambda *_: (0,) * len(shape), pipeline_mode=pl.Buffered(1))


def _params(sem):
    return pltpu.CompilerParams(dimension_semantics=sem, vmem_limit_bytes=VMEM_LIMIT)


def _inproj_kernel(x_ref, gain_ref, w_ref, cos_ref, sa_ref, sb_ref,
                   u_ref, q_ref, k_ref, v_ref, gs_ref, ga_ref):
    h = _rms(x_ref[...], gain_ref[...]).astype(BF16)
    cos, sa, sb = cos_ref[...], sa_ref[...], sb_ref[...]

    def rope(t):
        return t * cos + pltpu.roll(t, LANES - HEAD_DIM // 2, 1) * sa + pltpu.roll(t, HEAD_DIM // 2, 1) * sb

    step = 512
    u_ref[...] = _dot(h, w_ref[:, C_U:C_U + D_SSM])
    for c in range(0, D_ATTN, step):
        t = _dot(h, w_ref[:, C_Q + c:C_Q + c + step])
        for j in range(0, step, LANES):
            q_ref[:, c + j:c + j + LANES] = (rope(t[:, j:j + LANES]) * ATTN_SCALE).astype(BF16)
    t = _dot(h, w_ref[:, C_K:C_K + 2 * D_KV])
    for j in range(0, 2 * D_KV, LANES):
        k_ref[:, j:j + LANES] = rope(t[:, j:j + LANES]).astype(BF16)
    v_ref[...] = _dot(h, w_ref[:, C_V:C_V + 2 * D_KV]).astype(BF16)
    for c in range(0, D_MODEL, step):
        gs_ref[:, c:c + step] = _dot(h, w_ref[:, C_GS + c:C_GS + c + step]).astype(BF16)
        ga_ref[:, c:c + step] = _dot(h, w_ref[:, C_GA + c:C_GA + c + step]).astype(BF16)


def _inproj(x, gain, w, cos, sa, sb, tm, tiles_per_batch):
    rows = x.shape[0]
    row = lambda width: pl.BlockSpec((tm, width), lambda i: (i, 0))
    tab = pl.BlockSpec((tm, LANES), lambda i: (i % tiles_per_batch, 0))
    widths = (D_SSM, D_ATTN, 2 * D_KV, 2 * D_KV, D_MODEL, D_MODEL)
    dtypes = (F32, BF16, BF16, BF16, BF16, BF16)
    return pl.pallas_call(
        _inproj_kernel,
        grid=(rows // tm,),
        in_specs=[row(D_MODEL), _const_spec((1, D_MODEL)), _const_spec((D_MODEL, C_END)), tab, tab, tab],
        out_specs=[row(wd) for wd in widths],
        out_shape=[jax.ShapeDtypeStruct((rows, wd), dt) for wd, dt in zip(widths, dtypes)],
        compiler_params=_params(("parallel",)),
        name="inproj",
    )(x, gain, w, cos, sa, sb)


def _ssm_kernel(u_ref, wt_ref, pb_ref, qc_ref, a_ref, pw_ref, y_ref, z_s, v_s, l_s, s_s):
    n_chunks = z_s.shape[0]
    seg = n_chunks // N_SEG
    n_re = TILE_STATE // LANES

    for t in range(CHUNK):
        z_s[:, t * LANES:(t + 1) * LANES] = u_ref[pl.ds(t, n_chunks, stride=CHUNK), :].astype(BF16)
    v = _dot(z_s[...], pb_ref[0])
    for c in range(2 * n_re):
        v_s[c] = v[:, c * LANES:(c + 1) * LANES]

    def load(ref, rows, part):
        return jnp.concatenate([ref[part * n_re + c, rows, :] for c in range(n_re)], axis=1)

    def store(ref, rows, part, val):
        for c in range(n_re):
            ref[part * n_re + c, rows, :] = val[:, c * LANES:(c + 1) * LANES]

    are, aim = a_ref[0, 0], a_ref[0, 1]

    def local_scan(i, carry):
        sre, sim = carry
        rows = pl.ds(i, N_SEG, stride=seg)
        store(l_s, rows, 0, sre)
        store(l_s, rows, 1, sim)
        return (are * sre - aim * sim + load(v_s, rows, 0), are * sim + aim * sre + load(v_s, rows, 1))

    zero = jnp.zeros((N_SEG, TILE_STATE), F32)
    ere, eim = lax.fori_loop(0, seg, local_scan, (zero, zero))

    bre, bim = a_ref[0, 2][:1], a_ref[0, 3][:1]
    tre, tim = [zero[:1]], [zero[:1]]
    for m in range(1, N_SEG):
        pre, pim = tre[-1], tim[-1]
        tre.append(bre * pre - bim * pim + ere[m - 1:m])
        tim.append(bre * pim + bim * pre + eim[m - 1:m])
    tre, tim = jnp.concatenate(tre, axis=0), jnp.concatenate(tim, axis=0)

    def add_carry(i, _):
        rows = pl.ds(i, N_SEG, stride=seg)
        pre, pim = pw_ref[0, i, 0], pw_ref[0, i, 1]
        store(l_s, rows, 0, load(l_s, rows, 0) + (pre * tre - pim * tim))
        store(l_s, rows, 1, load(l_s, rows, 1) + (pre * tim + pim * tre))
        return 0

    lax.fori_loop(0, seg, add_carry, 0)

    for c in range(2 * n_re):
        s_s[:, c * LANES:(c + 1) * LANES] = l_s[c].astype(BF16)
    for tp in range(CHUNK // 2):
        acc = _dot(s_s[...], qc_ref[0, :, tp * MXU:(tp + 1) * MXU])
        for tq in range(tp + 1):
            acc = acc + _dot(z_s[:, tq * MXU:(tq + 1) * MXU], wt_ref[0, tp - tq])
        for e in range(2):
            y_ref[pl.ds(2 * tp + e, n_chunks, stride=CHUNK), :] = acc[:, e * LANES:(e + 1) * LANES]


def _ssm(u, prep, bsz, seq_pad):
    wt, pb, qc, a, pw = prep
    n_chunks = seq_pad // CHUNK
    tile = lambda shape: pl.BlockSpec((1,) + shape, lambda j, b: (j,) + (0,) * len(shape))
    io = pl.BlockSpec((seq_pad, LANES), lambda j, b: (b, j))
    return pl.pallas_call(
        _ssm_kernel,
        grid=(N_SSM_TILES, bsz),
        in_specs=[io, tile(wt.shape[1:]), tile(pb.shape[1:]), tile(qc.shape[1:]), tile(a.shape[1:]), tile(pw.shape[1:])],
        out_specs=io,
        out_shape=jax.ShapeDtypeStruct(u.shape, F32),
        scratch_shapes=[pltpu.VMEM((n_chunks, CHUNK_W), BF16),
                        pltpu.VMEM((2 * TILE_STATE // LANES, n_chunks, LANES), F32),
                        pltpu.VMEM((2 * TILE_STATE // LANES, n_chunks, LANES), F32),
                        pltpu.VMEM((n_chunks, 2 * TILE_STATE), BF16)],
        compiler_params=_params(("parallel", "parallel")),
        name="ssm",
    )(u, wt, pb, qc, a, pw)


def _ssm_prep(a_re, a_im, log_dt, b_re, b_im, c_re, c_im, d_skip, seg):
    hp = lax.Precision.HIGHEST
    G, N, C, T, J, E = N_SSM_GROUPS, SSM_STATE, SSM_GROUP, CHUNK, N_SSM_TILES, TILE_GROUPS
    eye = jnp.eye(E, dtype=F32)
    dt = jnp.exp(log_dt)[:, None]
    lam_re, lam_im = a_re * dt, a_im * dt

    def powers(j):
        j = jnp.asarray(j, F32)[:, None, None]
        mag = jnp.exp(lam_re[None] * j)
        return mag * jnp.cos(lam_im[None] * j), mag * jnp.sin(lam_im[None] * j)

    pw_re, pw_im = powers(np.arange(T + 1))
    den = a_re * a_re + a_im * a_im
    nre, nim = pw_re[1] - 1.0, pw_im[1]
    coef_re = (nre * a_re + nim * a_im) / den
    coef_im = (nim * a_re - nre * a_im) / den
    bb_re = coef_re[..., None] * b_re - coef_im[..., None] * b_im
    bb_im = coef_re[..., None] * b_im + coef_im[..., None] * b_re
    w_re = pw_re[:T, ..., None] * bb_re[None] - pw_im[:T, ..., None] * bb_im[None]
    w_im = pw_re[:T, ..., None] * bb_im[None] + pw_im[:T, ..., None] * bb_re[None]

    taps = (jnp.einsum('gdn,jgnc->gjdc', c_re, w_re, precision=hp)
            - jnp.einsum('gdn,jgnc->gjdc', c_im, w_im, precision=hp))
    taps = taps.at[:, 0].add(d_skip[:, :, None] * jnp.eye(C, dtype=F32)[None])
    lag = 2 * np.arange(T // 2)[:, None, None] + np.arange(2)[None, None, :] - np.arange(2)[None, :, None]
    tsel = jnp.where((lag >= 0)[None, ..., None, None], taps[:, np.clip(lag, 0, T - 1)], 0.0)
    tsel = tsel.reshape(J, E, T // 2, 2, 2, C, C)
    wt = jnp.einsum('jgaefdc,gh->jaegcfhd', tsel, eye).reshape(J, T // 2, MXU, MXU)

    def state_in(w):
        w = w[::-1].reshape(T, J, E, N, C)
        return jnp.einsum('sjgnc,gh->jsgchn', w, eye).reshape(J, CHUNK_W, TILE_STATE)

    pb = jnp.concatenate([state_in(w_re), state_in(w_im)], axis=-1)

    p1_re, p1_im = pw_re[1:].transpose(1, 0, 2), pw_im[1:].transpose(1, 0, 2)
    q_re = c_re[:, None] * p1_re[:, :, None] - c_im[:, None] * p1_im[:, :, None]
    q_im = -(c_re[:, None] * p1_im[:, :, None] + c_im[:, None] * p1_re[:, :, None])

    def state_out(q):
        return jnp.einsum('jgtdn,gh->jgnthd', q.reshape(J, E, T, C, N), eye).reshape(J, TILE_STATE, CHUNK_W)

    qc = jnp.concatenate([state_out(q_re), state_out(q_im)], axis=1)

    tile_row = lambda m: jnp.broadcast_to(m.reshape(-1, J, 1, TILE_STATE), (m.shape[0], J, SUBLANES, TILE_STATE))
    t_re, t_im = powers(T * np.array([1, seg]))
    a = tile_row(jnp.stack([t_re[0], t_im[0], t_re[1], t_im[1]])).transpose(1, 0, 2, 3)
    s_re, s_im = powers(T * np.arange(seg))
    pw = jnp.stack([tile_row(s_re), tile_row(s_im)], axis=2).transpose(1, 0, 2, 3, 4)
    return wt.astype(BF16), pb.astype(BF16), qc.astype(BF16), a, pw


def _attn_kernel(sink_ref, bias_ref, q_ref, km_ref, kp_ref, kc_ref, vm_ref, vp_ref, vc_ref, o_ref):
    bias = bias_ref[0]
    lane = lax.broadcasted_iota(jnp.int32, (3 * BLOCK, LANES), 1)
    first = lane < HEAD_DIM
    olane = lax.broadcasted_iota(jnp.int32, (BLOCK, LANES), 1) < HEAD_DIM
    zero = jnp.zeros((3 * BLOCK, LANES), BF16)
    for kh in range(N_KV_HEADS):
        cols = slice(kh * LANES, (kh + 1) * LANES)
        kd = jnp.concatenate([km_ref[:, cols], kp_ref[:, cols], kc_ref[:, cols]], axis=0)
        vd = jnp.concatenate([vm_ref[:, cols], vp_ref[:, cols], vc_ref[:, cols]], axis=0)
        kab = jnp.concatenate([jnp.where(first, kd, zero), jnp.where(first, zero, kd)], axis=0)
        vab = jnp.concatenate([jnp.where(first, vd, zero), jnp.where(first, zero, vd)], axis=0)
        for pr in range(2):
            hp = kh * 2 + pr
            q2 = q_ref[:, hp * LANES:(hp + 1) * LANES]
            s = lax.dot_general(q2, kab, (((1,), (1,)), ((), ())), preferred_element_type=F32)
            ps, rs = [], []
            for e in range(2):
                se = s[:, e * 3 * BLOCK:(e + 1) * 3 * BLOCK] + bias
                sink = sink_ref[2 * hp + e]
                m = jnp.maximum(jnp.max(se, axis=-1, keepdims=True), sink)
                p = jnp.exp(se - m)
                den = jnp.sum(p, axis=-1, keepdims=True) + jnp.exp(sink - m)
                ps.append(p.astype(BF16))
                rs.append(1.0 / den)
            o2 = _dot(jnp.concatenate(ps, axis=1), vab)
            o_ref[:, hp * LANES:(hp + 1) * LANES] = (o2 * jnp.where(olane, rs[0], rs[1])).astype(BF16)


def _attention(q, kd, vd, sinks, bias, bsz, n_blk):
    cur = lambda b, n: (b * n_blk + n, 0)
    prev = lambda b, n: (b * n_blk + jnp.maximum(n - 1, 0), 0)
    meta = lambda b, n: (b * n_blk, 0)
    kv = lambda imap: pl.BlockSpec((BLOCK, 2 * D_KV), imap)
    return pl.pallas_call(
        _attn_kernel,
        grid=(bsz, n_blk),
        in_specs=[pl.BlockSpec(memory_space=pltpu.SMEM),
                  pl.BlockSpec((1, BLOCK, 3 * BLOCK), lambda b, n: (jnp.minimum(n, 2), 0, 0)),
                  pl.BlockSpec((BLOCK, D_ATTN), cur),
                  kv(meta), kv(prev), kv(cur), kv(meta), kv(prev), kv(cur)],
        out_specs=pl.BlockSpec((BLOCK, D_ATTN), cur),
        out_shape=jax.ShapeDtypeStruct(q.shape, BF16),
        compiler_params=_params(("parallel", "parallel")),
        name="attention",
    )(sinks, bias, q, kd, kd, kd, vd, vd, vd)


def _attn_bias():
    i = np.arange(BLOCK)[:, None]
    j = np.arange(BLOCK)[None, :]
    is_meta = np.broadcast_to(j >= PAD, (BLOCK, BLOCK))
    none = np.zeros((BLOCK, BLOCK), bool)
    causal = j <= i
    blk0 = [none, none, causal & is_meta]
    blk1 = [none, is_meta, causal]
    blk2 = [is_meta, j > i, causal]
    ok = np.stack([np.concatenate(b, axis=1) for b in (blk0, blk1, blk2)])
    return jnp.asarray(np.where(ok, 0.0, NEG_INF), F32)


def _merge_kernel(x_ref, y_ref, a_ref, gs_ref, ga_ref, wglu_ref, bglu_ref, wos_ref, woa_ref, wout_ref,
                  gain_ref, o_ref):
    z = jax.nn.gelu(y_ref[...])
    z = z * jax.nn.sigmoid(_dot(z.astype(BF16), wglu_ref[...]) + bglu_ref[...])
    merged = (jax.nn.sigmoid(gs_ref[...].astype(F32)) * _dot(z.astype(BF16), wos_ref[...])
              + jax.nn.sigmoid(ga_ref[...].astype(F32)) * _dot(a_ref[...], woa_ref[...]))
    mix = _dot(merged.astype(BF16), wout_ref[...])
    o_ref[...] = x_ref[...] + _rms(mix, gain_ref[...])


def _merge(x, y, a, gs, ga, wglu, bglu, wos, woa, wout, gain, tm):
    rows = x.shape[0]
    row = lambda width: pl.BlockSpec((tm, width), lambda i: (i, 0))
    return pl.pallas_call(
        _merge_kernel,
        grid=(rows // tm,),
        in_specs=[row(D_MODEL), row(D_SSM), row(D_ATTN), row(D_MODEL), row(D_MODEL),
                  _const_spec((D_SSM, D_SSM)), _const_spec((1, D_SSM)), _const_spec((D_SSM, D_MODEL)),
                  _const_spec((D_ATTN, D_MODEL)), _const_spec((D_MODEL, D_MODEL)), _const_spec((1, D_MODEL))],
        out_specs=row(D_MODEL),
        out_shape=jax.ShapeDtypeStruct(x.shape, F32),
        compiler_params=_params(("parallel",)),
        name="merge",
    )(x, y, a, gs, ga, wglu, bglu, wos, woa, wout, gain)


def _ffn_kernel(x_ref, gpre_ref, wup_ref, wdown_ref, gpost_ref, o_ref):
    x = x_ref[...]
    h = _rms(x, gpre_ref[...]).astype(BF16)
    step = 512
    acc = jnp.zeros(x.shape, F32)
    for c in range(0, D_FF, step):
        a = jnp.maximum(_dot(h, wup_ref[:, c:c + step]), 0.0)
        acc = acc + _dot((a * a).astype(BF16), wdown_ref[c:c + step, :])
    o_ref[...] = x + _rms(acc, gpost_ref[...])


def _ffn(x, gpre, wup, wdown, gpost, tm):
    rows = x.shape[0]
    row = pl.BlockSpec((tm, D_MODEL), lambda i: (i, 0))
    return pl.pallas_call(
        _ffn_kernel,
        grid=(rows // tm,),
        in_specs=[row, _const_spec((1, D_MODEL)), _const_spec((D_MODEL, D_FF)),
                  _const_spec((D_FF, D_MODEL)), _const_spec((1, D_MODEL))],
        out_specs=row,
        out_shape=jax.ShapeDtypeStruct(x.shape, F32),
        compiler_params=_params(("parallel",)),
        name="ffn",
    )(x, gpre, wup, wdown, gpost)


def _rope_tables(seq_pad):
    pos = (jnp.arange(seq_pad, dtype=jnp.int32) - PAD).astype(F32)
    inv_freq = 1.0 / (ROPE_THETA ** (jnp.arange(0, HEAD_DIM, 2, dtype=F32) / HEAD_DIM))
    ang = pos[:, None] * inv_freq[None, :]
    ang = jnp.concatenate([ang, ang, ang, ang], axis=-1)
    first_half = (np.arange(LANES) % HEAD_DIM) < HEAD_DIM // 2
    sin = jnp.sin(ang)
    return jnp.cos(ang), jnp.where(first_half, -sin, 0.0), jnp.where(first_half, 0.0, sin)


def _widen_w_in(w):
    cuts = np.cumsum([D_SSM, D_ATTN, D_KV, D_KV, D_MODEL])
    wu, wq, wk, wv, wgs, wga = jnp.split(w, cuts, axis=-1)
    dup = lambda m: jnp.tile(m.reshape(D_MODEL, N_KV_HEADS, 1, HEAD_DIM), (1, 1, 2, 1)).reshape(D_MODEL, 2 * D_KV)
    return jnp.concatenate([wu, wq, dup(wk), dup(wv), wgs, wga], axis=-1).astype(BF16)


def kernel(x, meta_tokens, norm_mix_pre, norm_mix_post, norm_mlp_pre, norm_mlp_post, w_in, ssm_a_re, ssm_a_im, ssm_log_dt, ssm_b_re, ssm_b_im, ssm_c_re, ssm_c_im, ssm_d, w_glu, b_glu, attn_sinks, w_o_ssm, w_o_attn, w_out, w_up, w_down):
    bsz, seq, _ = x.shape
    depth = w_in.shape[0]
    seq_pad = PAD + N_META + seq
    assert seq_pad % BLOCK == 0 and (seq_pad // CHUNK) % N_SEG == 0
    n_blk, n_chunks = seq_pad // BLOCK, seq_pad // CHUNK
    tm = 640
    assert seq_pad % tm == 0

    meta = jnp.broadcast_to(meta_tokens[None].astype(x.dtype), (bsz, N_META, D_MODEL))
    hres = jnp.concatenate([jnp.zeros((bsz, PAD, D_MODEL), x.dtype), meta, x], axis=1)
    hres = hres.reshape(bsz * seq_pad, D_MODEL)
    cos, sa, sb = _rope_tables(seq_pad)
    bias = _attn_bias()

    for l in range(depth):
        u, q, kd, vd, gs, ga = _inproj(hres, norm_mix_pre[l][None], _widen_w_in(w_in[l]), cos, sa, sb,
                                       tm, seq_pad // tm)
        prep = _ssm_prep(ssm_a_re[l], ssm_a_im[l], ssm_log_dt[l], ssm_b_re[l], ssm_b_im[l],
                         ssm_c_re[l], ssm_c_im[l], ssm_d[l], n_chunks // N_SEG)
        y = _ssm(u, prep, bsz, seq_pad)
        att = _attention(q, kd, vd, attn_sinks[l], bias, bsz, n_blk)
        hres = _merge(hres, y, att, gs, ga, w_glu[l].astype(BF16), b_glu[l][None], w_o_ssm[l].astype(BF16),
                      w_o_attn[l].astype(BF16), w_out[l].astype(BF16), norm_mix_post[l][None], tm)
        hres = _ffn(hres, norm_mlp_pre[l][None], w_up[l].astype(BF16), w_down[l].astype(BF16),
                    norm_mlp_post[l][None], tm)
    return hres.reshape(bsz, seq_pad, D_MODEL)[:, BLOCK:]
```

```python
import functools

import jax
import jax.numpy as jnp
import numpy as np
from jax import lax
from jax.experimental import pallas as pl
from jax.experimental.pallas import tpu as pltpu

D_MODEL = 1024
N_META = 16
HEAD_DIM = 64
N_Q_HEADS = 16
N_KV_HEADS = 4
D_ATTN = N_Q_HEADS * HEAD_DIM
D_KV = N_KV_HEADS * HEAD_DIM
BLOCK = 128
ROPE_THETA = 10000.0
ATTN_SCALE = HEAD_DIM ** -0.5
NEG_INF = -1e30
D_SSM = D_MODEL // 2
SSM_GROUP = 16
N_SSM_GROUPS = D_SSM // SSM_GROUP
SSM_STATE = 64
D_FF = 4 * D_MODEL
RMS_EPS = 1e-6

LANES = 128
SUBLANES = 8
MXU = 256
PAD = BLOCK - N_META
CHUNK = 16
TILE_GROUPS = LANES // SSM_GROUP
N_SSM_TILES = D_SSM // LANES
TILE_STATE = TILE_GROUPS * SSM_STATE
CHUNK_W = CHUNK * LANES
N_SEG = SUBLANES
VMEM_LIMIT = 56 * 1024 * 1024

C_U = 0
C_Q = C_U + D_SSM
C_K = C_Q + D_ATTN
C_V = C_K + 2 * D_KV
C_GS = C_V + 2 * D_KV
C_GA = C_GS + D_MODEL
C_END = C_GA + D_MODEL

BF16 = jnp.bfloat16
F32 = jnp.float32


def _dot(a, b):
    return jnp.dot(a, b, preferred_element_type=F32)


def _rms(x, gain):
    return x * lax.rsqrt(jnp.mean(x * x, axis=-1, keepdims=True) + RMS_EPS) * gain


def _const_spec(shape):
    return pl.BlockSpec(shape, lambda *_: (0,) * len(shape), pipeline_mode=pl.Buffered(1))


def _params(sem):
    return pltpu.CompilerParams(dimension_semantics=sem, vmem_limit_bytes=VMEM_LIMIT)


class _Rows:
    def __init__(self, kind, array, tm, tiles_per_batch, blocks_per_batch, meta_block=None):
        self.has_meta = kind == "input"
        self.tiles_per_batch = tiles_per_batch
        nb = tm // BLOCK
        if kind == "padded":
            self.specs = [pl.BlockSpec((tm, D_MODEL), lambda i: (i, 0))]
            self.operands = [array]
            return
        shift = -1 if kind == "input" else 1

        def imap(j):
            return lambda i: ((i // tiles_per_batch) * blocks_per_batch
                              + jnp.maximum((i % tiles_per_batch) * nb + j + shift, 0), 0)

        self.specs = [pl.BlockSpec((BLOCK, D_MODEL), imap(j)) for j in range(nb)]
        self.operands = [array] * nb
        if self.has_meta:
            self.specs.append(_const_spec((BLOCK, D_MODEL)))
            self.operands.append(meta_block)

    def load(self, refs):
        n = len(self.specs)
        if n == 1:
            return refs[0][...]
        blocks = [r[...] for r in refs[:n - self.has_meta]]
        if self.has_meta:
            head = pl.program_id(0) % self.tiles_per_batch == 0
            blocks[0] = jnp.where(head, refs[n - 1][...], blocks[0])
        return jnp.concatenate(blocks, axis=0)


def _inproj_kernel(rows, *refs):
    n = len(rows.specs)
    gain_ref, w_ref, cos_ref, sa_ref, sb_ref, u_ref, q_ref, k_ref, v_ref, gs_ref, ga_ref = refs[n:]
    h = _rms(rows.load(refs[:n]), gain_ref[...]).astype(BF16)
    cos, sa, sb = cos_ref[...], sa_ref[...], sb_ref[...]

    def rope(t):
        return t * cos + pltpu.roll(t, LANES - HEAD_DIM // 2, 1) * sa + pltpu.roll(t, HEAD_DIM // 2, 1) * sb

    step = 512
    u_ref[...] = _dot(h, w_ref[:, C_U:C_U + D_SSM])
    for c in range(0, D_ATTN, step):
        t = _dot(h, w_ref[:, C_Q + c:C_Q + c + step])
        for j in range(0, step, LANES):
            q_ref[:, c + j:c + j + LANES] = (rope(t[:, j:j + LANES]) * ATTN_SCALE).astype(BF16)
    t = _dot(h, w_ref[:, C_K:C_K + 2 * D_KV])
    for j in range(0, 2 * D_KV, LANES):
        k_ref[:, j:j + LANES] = rope(t[:, j:j + LANES]).astype(BF16)
    v_ref[...] = _dot(h, w_ref[:, C_V:C_V + 2 * D_KV]).astype(BF16)
    for c in range(0, D_MODEL, step):
        gs_ref[:, c:c + step] = _dot(h, w_ref[:, C_GS + c:C_GS + c + step]).astype(BF16)
        ga_ref[:, c:c + step] = _dot(h, w_ref[:, C_GA + c:C_GA + c + step]).astype(BF16)


def _inproj(rows, n_rows, gain, w, cos, sa, sb, tm):
    row = lambda width: pl.BlockSpec((tm, width), lambda i: (i, 0))
    tab = pl.BlockSpec((tm, LANES), lambda i: (i % rows.tiles_per_batch, 0))
    widths = (D_SSM, D_ATTN, 2 * D_KV, 2 * D_KV, D_MODEL, D_MODEL)
    dtypes = (F32, BF16, BF16, BF16, BF16, BF16)
    return pl.pallas_call(
        functools.partial(_inproj_kernel, rows),
        grid=(n_rows // tm,),
        in_specs=rows.specs + [_const_spec((1, D_MODEL)), _const_spec((D_MODEL, C_END)), tab, tab, tab],
        out_specs=[row(wd) for wd in widths],
        out_shape=[jax.ShapeDtypeStruct((n_rows, wd), dt) for wd, dt in zip(widths, dtypes)],
        compiler_params=_params(("parallel",)),
        name="inproj",
    )(*rows.operands, gain, w, cos, sa, sb)


def _ssm_spread(kc_ref, wc_ref, qc_ref, wt_s, pb_s, qm_s):
    def group_of(shape, axis, width):
        return (lax.broadcasted_iota(jnp.int32, shape, axis) // width) % TILE_GROUPS

    def copies(width):
        shape = (width, TILE_GROUPS * width)
        same = lax.broadcasted_iota(jnp.int32, shape, 1) % width == lax.broadcasted_iota(jnp.int32, shape, 0)
        return jnp.where(same, 1.0, 0.0).astype(BF16)

    shape = (CHUNK_W, LANES)
    diag = group_of(shape, 0, SSM_GROUP) == group_of(shape, 1, SSM_GROUP)
    taps = jnp.where(diag, _dot(kc_ref[0].astype(BF16), copies(SSM_GROUP)), 0.0).astype(BF16)
    lag_block = lambda lag: taps[lag * LANES:(lag + 1) * LANES]
    for a in range(CHUNK // 2):
        wt_s[a, :LANES, :LANES] = lag_block(2 * a)
        wt_s[a, :LANES, LANES:] = lag_block(2 * a + 1)
        wt_s[a, LANES:, :LANES] = lag_block(2 * a - 1) if a else jnp.zeros((LANES, LANES), BF16)
        wt_s[a, LANES:, LANES:] = lag_block(2 * a)

    rows = CHUNK_W // 4
    shape = (rows, TILE_STATE)
    diag = group_of(shape, 0, SSM_GROUP) == group_of(shape, 1, SSM_STATE)
    spread = copies(SSM_STATE)
    for part in range(2):
        for r in range(0, CHUNK_W, rows):
            w = _dot(wc_ref[0, part, r:r + rows].astype(BF16), spread)
            pb_s[r:r + rows, part * TILE_STATE:(part + 1) * TILE_STATE] = jnp.where(diag, w, 0.0).astype(BF16)

    col_group = group_of((SSM_STATE, CHUNK_W), 1, SSM_GROUP)
    for part in range(2):
        q = qc_ref[0, part]
        for h in range(TILE_GROUPS):
            r = part * TILE_STATE + h * SSM_STATE
            qm_s[r:r + SSM_STATE, :] = jnp.where(col_group == h, q, 0.0).astype(BF16)


def _ssm_kernel(u_ref, kc_ref, wc_ref, qc_ref, a_ref, pw_ref, y_ref, z_s, v_s, l_s, s_s, wt_s, pb_s, qm_s):
    n_chunks = z_s.shape[0]
    seg = n_chunks // N_SEG
    n_re = TILE_STATE // LANES

    @pl.when(pl.program_id(1) == 0)
    def _():
        _ssm_spread(kc_ref, wc_ref, qc_ref, wt_s, pb_s, qm_s)

    for t in range(CHUNK):
        z_s[:, t * LANES:(t + 1) * LANES] = u_ref[pl.ds(t, n_chunks, stride=CHUNK), :].astype(BF16)
    v = _dot(z_s[...], pb_s[...])
    for c in range(2 * n_re):
        v_s[c] = v[:, c * LANES:(c + 1) * LANES]

    def load(ref, rows, part):
        return jnp.concatenate([ref[part * n_re + c, rows, :] for c in range(n_re)], axis=1)

    def store(ref, rows, part, val):
        for c in range(n_re):
            ref[part * n_re + c, rows, :] = val[:, c * LANES:(c + 1) * LANES]

    are, aim = a_ref[0, 0], a_ref[0, 1]

    def local_scan(i, carry):
        sre, sim = carry
        rows = pl.ds(i, N_SEG, stride=seg)
        store(l_s, rows, 0, sre)
        store(l_s, rows, 1, sim)
        return (are * sre - aim * sim + load(v_s, rows, 0), are * sim + aim * sre + load(v_s, rows, 1))

    zero = jnp.zeros((N_SEG, TILE_STATE), F32)
    ere, eim = lax.fori_loop(0, seg, local_scan, (zero, zero))

    bre, bim = a_ref[0, 2][:1], a_ref[0, 3][:1]
    tre, tim = [zero[:1]], [zero[:1]]
    for m in range(1, N_SEG):
        pre, pim = tre[-1], tim[-1]
        tre.append(bre * pre - bim * pim + ere[m - 1:m])
        tim.append(bre * pim + bim * pre + eim[m - 1:m])
    tre, tim = jnp.concatenate(tre, axis=0), jnp.concatenate(tim, axis=0)

    def add_carry(i, _):
        rows = pl.ds(i, N_SEG, stride=seg)
        pre, pim = pw_ref[0, i, 0], pw_ref[0, i, 1]
        store(l_s, rows, 0, load(l_s, rows, 0) + (pre * tre - pim * tim))
        store(l_s, rows, 1, load(l_s, rows, 1) + (pre * tim + pim * tre))
        return 0

    lax.fori_loop(0, seg, add_carry, 0)

    for c in range(2 * n_re):
        s_s[:, c * LANES:(c + 1) * LANES] = l_s[c].astype(BF16)
    for tp in range(CHUNK // 2):
        acc = _dot(s_s[...], qm_s[:, tp * MXU:(tp + 1) * MXU])
        for tq in range(tp + 1):
            acc = acc + _dot(z_s[:, tq * MXU:(tq + 1) * MXU], wt_s[tp - tq])
        for e in range(2):
            y_ref[pl.ds(2 * tp + e, n_chunks, stride=CHUNK), :] = acc[:, e * LANES:(e + 1) * LANES]


def _ssm(u, prep, bsz, seq_pad):
    n_chunks = seq_pad // CHUNK
    tile = lambda shape: pl.BlockSpec((1,) + shape, lambda j, b: (j,) + (0,) * len(shape))
    io = pl.BlockSpec((seq_pad, LANES), lambda j, b: (b, j))
    return pl.pallas_call(
        _ssm_kernel,
        grid=(N_SSM_TILES, bsz),
        in_specs=[io] + [tile(p.shape[1:]) for p in prep],
        out_specs=io,
        out_shape=jax.ShapeDtypeStruct(u.shape, F32),
        scratch_shapes=[pltpu.VMEM((n_chunks, CHUNK_W), BF16),
                        pltpu.VMEM((2 * TILE_STATE // LANES, n_chunks, LANES), F32),
                        pltpu.VMEM((2 * TILE_STATE // LANES, n_chunks, LANES), F32),
                        pltpu.VMEM((n_chunks, 2 * TILE_STATE), BF16),
                        pltpu.VMEM((CHUNK // 2, MXU, MXU), BF16),
                        pltpu.VMEM((CHUNK_W, 2 * TILE_STATE), BF16),
                        pltpu.VMEM((2 * TILE_STATE, CHUNK_W), BF16)],
        compiler_params=_params(("arbitrary", "arbitrary")),
        name="ssm",
    )(u, *prep)


def _ssm_prep(a_re, a_im, log_dt, b_re, b_im, c_re, c_im, d_skip, seg):
    hp = lax.Precision.HIGHEST
    G, N, C, T, J, E = N_SSM_GROUPS, SSM_STATE, SSM_GROUP, CHUNK, N_SSM_TILES, TILE_GROUPS
    dt = jnp.exp(log_dt)[:, None]
    lam_re, lam_im = a_re * dt, a_im * dt

    def powers(j):
        j = jnp.asarray(j, F32)[:, None, None]
        mag = jnp.exp(lam_re[None] * j)
        return mag * jnp.cos(lam_im[None] * j), mag * jnp.sin(lam_im[None] * j)

    pw_re, pw_im = powers(np.arange(T + 1))
    den = a_re * a_re + a_im * a_im
    nre, nim = pw_re[1] - 1.0, pw_im[1]
    coef_re = (nre * a_re + nim * a_im) / den
    coef_im = (nim * a_re - nre * a_im) / den
    bb_re = coef_re[..., None] * b_re - coef_im[..., None] * b_im
    bb_im = coef_re[..., None] * b_im + coef_im[..., None] * b_re
    w_re = pw_re[:T, ..., None] * bb_re[None] - pw_im[:T, ..., None] * bb_im[None]
    w_im = pw_re[:T, ..., None] * bb_im[None] + pw_im[:T, ..., None] * bb_re[None]

    taps = (jnp.einsum('gdn,jgnc->gjdc', c_re, w_re, precision=hp)
            - jnp.einsum('gdn,jgnc->gjdc', c_im, w_im, precision=hp))
    taps = taps.at[:, 0].add(d_skip[:, :, None] * jnp.eye(C, dtype=F32)[None])
    kc = taps.reshape(J, E, T, C, C).transpose(0, 2, 1, 4, 3).reshape(J, T * LANES, C)

    state_in = lambda w: w[::-1].reshape(T, J, E, N, C).transpose(1, 0, 2, 4, 3).reshape(J, CHUNK_W, N)
    wc = jnp.stack([state_in(w_re), state_in(w_im)], axis=1)

    p1_re, p1_im = pw_re[1:].transpose(1, 0, 2), pw_im[1:].transpose(1, 0, 2)
    q_re = c_re[:, None] * p1_re[:, :, None] - c_im[:, None] * p1_im[:, :, None]
    q_im = -(c_re[:, None] * p1_im[:, :, None] + c_im[:, None] * p1_re[:, :, None])
    state_out = lambda q: q.reshape(J, E, T, C, N).transpose(0, 4, 2, 1, 3).reshape(J, N, CHUNK_W)
    qc = jnp.stack([state_out(q_re), state_out(q_im)], axis=1)

    tile_row = lambda m: jnp.broadcast_to(m.reshape(-1, J, 1, TILE_STATE), (m.shape[0], J, SUBLANES, TILE_STATE))
    t_re, t_im = powers(T * np.array([1, seg]))
    a = tile_row(jnp.stack([t_re[0], t_im[0], t_re[1], t_im[1]])).transpose(1, 0, 2, 3)
    s_re, s_im = powers(T * np.arange(seg))
    pw = jnp.stack([tile_row(s_re), tile_row(s_im)], axis=2).transpose(1, 0, 2, 3, 4)
    return kc, wc, qc, a, pw


def _attn_kernel(sink_ref, bias_ref, q_ref, km_ref, kp_ref, kc_ref, vm_ref, vp_ref, vc_ref, o_ref):
    bias = bias_ref[0]
    lane = lax.broadcasted_iota(jnp.int32, (3 * BLOCK, LANES), 1)
    first = lane < HEAD_DIM
    olane = lax.broadcasted_iota(jnp.int32, (BLOCK, LANES), 1) < HEAD_DIM
    zero = jnp.zeros((3 * BLOCK, LANES), BF16)
    for kh in range(N_KV_HEADS):
        cols = slice(kh * LANES, (kh + 1) * LANES)
        kd = jnp.concatenate([km_ref[:, cols], kp_ref[:, cols], kc_ref[:, cols]], axis=0)
        vd = jnp.concatenate([vm_ref[:, cols], vp_ref[:, cols], vc_ref[:, cols]], axis=0)
        kab = jnp.concatenate([jnp.where(first, kd, zero), jnp.where(first, zero, kd)], axis=0)
        vab = jnp.concatenate([jnp.where(first, vd, zero), jnp.where(first, zero, vd)], axis=0)
        for pr in range(2):
            hp = kh * 2 + pr
            q2 = q_ref[:, hp * LANES:(hp + 1) * LANES]
            s = lax.dot_general(q2, kab, (((1,), (1,)), ((), ())), preferred_element_type=F32)
            ps, rs = [], []
            for e in range(2):
                se = s[:, e * 3 * BLOCK:(e + 1) * 3 * BLOCK] + bias
                sink = sink_ref[2 * hp + e]
                m = jnp.maximum(jnp.max(se, axis=-1, keepdims=True), sink)
                p = jnp.exp(se - m)
                den = jnp.sum(p, axis=-1, keepdims=True) + jnp.exp(sink - m)
                ps.append(p.astype(BF16))
                rs.append(1.0 / den)
            o2 = _dot(jnp.concatenate(ps, axis=1), vab)
            o_ref[:, hp * LANES:(hp + 1) * LANES] = (o2 * jnp.where(olane, rs[0], rs[1])).astype(BF16)


def _attention(q, kd, vd, sinks, bias, bsz, n_blk):
    cur = lambda b, n: (b * n_blk + n, 0)
    prev = lambda b, n: (b * n_blk + jnp.maximum(n - 1, 0), 0)
    meta = lambda b, n: (b * n_blk, 0)
    kv = lambda imap: pl.BlockSpec((BLOCK, 2 * D_KV), imap)
    return pl.pallas_call(
        _attn_kernel,
        grid=(bsz, n_blk),
        in_specs=[pl.BlockSpec(memory_space=pltpu.SMEM),
                  pl.BlockSpec((1, BLOCK, 3 * BLOCK), lambda b, n: (jnp.minimum(n, 2), 0, 0)),
                  pl.BlockSpec((BLOCK, D_ATTN), cur),
                  kv(meta), kv(prev), kv(cur), kv(meta), kv(prev), kv(cur)],
        out_specs=pl.BlockSpec((BLOCK, D_ATTN), cur),
        out_shape=jax.ShapeDtypeStruct(q.shape, BF16),
        compiler_params=_params(("parallel", "parallel")),
        name="attention",
    )(sinks, bias, q, kd, kd, kd, vd, vd, vd)


def _attn_bias():
    i = np.arange(BLOCK)[:, None]
    j = np.arange(BLOCK)[None, :]
    is_meta = np.broadcast_to(j >= PAD, (BLOCK, BLOCK))
    none = np.zeros((BLOCK, BLOCK), bool)
    causal = j <= i
    blk0 = [none, none, causal & is_meta]
    blk1 = [none, is_meta, causal]
    blk2 = [is_meta, j > i, causal]
    ok = np.stack([np.concatenate(b, axis=1) for b in (blk0, blk1, blk2)])
    return jnp.asarray(np.where(ok, 0.0, NEG_INF), F32)


def _merge_kernel(rows, *refs):
    n = len(rows.specs)
    y_ref, a_ref, gs_ref, ga_ref, wglu_ref, bglu_ref, wos_ref, woa_ref, wout_ref, gain_ref, o_ref = refs[n:]
    z = jax.nn.gelu(y_ref[...])
    z = z * jax.nn.sigmoid(_dot(z.astype(BF16), wglu_ref[...]) + bglu_ref[...])
    merged = (jax.nn.sigmoid(gs_ref[...].astype(F32)) * _dot(z.astype(BF16), wos_ref[...])
              + jax.nn.sigmoid(ga_ref[...].astype(F32)) * _dot(a_ref[...], woa_ref[...]))
    mix = _dot(merged.astype(BF16), wout_ref[...])
    o_ref[...] = rows.load(refs[:n]) + _rms(mix, gain_ref[...])


def _merge(rows, y, a, gs, ga, wglu, bglu, wos, woa, wout, gain, tm):
    n_rows = y.shape[0]
    row = lambda width: pl.BlockSpec((tm, width), lambda i: (i, 0))
    return pl.pallas_call(
        functools.partial(_merge_kernel, rows),
        grid=(n_rows // tm,),
        in_specs=rows.specs + [row(D_SSM), row(D_ATTN), row(D_MODEL), row(D_MODEL),
                               _const_spec((D_SSM, D_SSM)), _const_spec((1, D_SSM)), _const_spec((D_SSM, D_MODEL)),
                               _const_spec((D_ATTN, D_MODEL)), _const_spec((D_MODEL, D_MODEL)),
                               _const_spec((1, D_MODEL))],
        out_specs=row(D_MODEL),
        out_shape=jax.ShapeDtypeStruct((n_rows, D_MODEL), F32),
        compiler_params=_params(("parallel",)),
        name="merge",
    )(*rows.operands, y, a, gs, ga, wglu, bglu, wos, woa, wout, gain)


def _ffn_kernel(rows, *refs):
    n = len(rows.specs)
    gpre_ref, wup_ref, wdown_ref, gpost_ref, o_ref = refs[n:]
    x = rows.load(refs[:n])
    h = _rms(x, gpre_ref[...]).astype(BF16)
    step = 512
    acc = jnp.zeros(x.shape, F32)
    for c in range(0, D_FF, step):
        a = jnp.maximum(_dot(h, wup_ref[:, c:c + step]), 0.0)
        acc = acc + _dot((a * a).astype(BF16), wdown_ref[c:c + step, :])
    o_ref[...] = x + _rms(acc, gpost_ref[...])


def _ffn(rows, n_rows, gpre, wup, wdown, gpost, tm):
    return pl.pallas_call(
        functools.partial(_ffn_kernel, rows),
        grid=(n_rows // tm,),
        in_specs=rows.specs + [_const_spec((1, D_MODEL)), _const_spec((D_MODEL, D_FF)),
                               _const_spec((D_FF, D_MODEL)), _const_spec((1, D_MODEL))],
        out_specs=pl.BlockSpec((tm, D_MODEL), lambda i: (i, 0)),
        out_shape=jax.ShapeDtypeStruct((n_rows, D_MODEL), F32),
        compiler_params=_params(("parallel",)),
        name="ffn",
    )(*rows.operands, gpre, wup, wdown, gpost)


def _rope_tables(seq_pad):
    pos = (jnp.arange(seq_pad, dtype=jnp.int32) - PAD).astype(F32)
    inv_freq = 1.0 / (ROPE_THETA ** (jnp.arange(0, HEAD_DIM, 2, dtype=F32) / HEAD_DIM))
    ang = pos[:, None] * inv_freq[None, :]
    ang = jnp.concatenate([ang, ang, ang, ang], axis=-1)
    first_half = (np.arange(LANES) % HEAD_DIM) < HEAD_DIM // 2
    sin = jnp.sin(ang)
    return jnp.cos(ang), jnp.where(first_half, -sin, 0.0), jnp.where(first_half, 0.0, sin)


def _widen_w_in(w):
    cuts = np.cumsum([D_SSM, D_ATTN, D_KV, D_KV, D_MODEL])
    wu, wq, wk, wv, wgs, wga = jnp.split(w, cuts, axis=-1)
    dup = lambda m: jnp.tile(m.reshape(D_MODEL, N_KV_HEADS, 1, HEAD_DIM), (1, 1, 2, 1)).reshape(D_MODEL, 2 * D_KV)
    return jnp.concatenate([wu, wq, dup(wk), dup(wv), wgs, wga], axis=-1).astype(BF16)


def kernel(x, meta_tokens, norm_mix_pre, norm_mix_post, norm_mlp_pre, norm_mlp_post, w_in, ssm_a_re, ssm_a_im, ssm_log_dt, ssm_b_re, ssm_b_im, ssm_c_re, ssm_c_im, ssm_d, w_glu, b_glu, attn_sinks, w_o_ssm, w_o_attn, w_out, w_up, w_down):
    bsz, seq, _ = x.shape
    depth = w_in.shape[0]
    seq_pad = PAD + N_META + seq
    assert seq_pad % BLOCK == 0 and (seq_pad // CHUNK) % N_SEG == 0
    n_blk, n_chunks = seq_pad // BLOCK, seq_pad // CHUNK
    tm, tm_out = 640, 512
    assert seq_pad % tm == 0 and seq % tm_out == 0
    n_rows = bsz * seq_pad

    meta_block = jnp.concatenate([jnp.zeros((PAD, D_MODEL), x.dtype), meta_tokens.astype(x.dtype)], axis=0)
    stream = _Rows("input", x.reshape(bsz * seq, D_MODEL), tm, seq_pad // tm, seq // BLOCK, meta_block)
    cos, sa, sb = _rope_tables(seq_pad)
    bias = _attn_bias()

    for l in range(depth):
        u, q, kd, vd, gs, ga = _inproj(stream, n_rows, norm_mix_pre[l][None], _widen_w_in(w_in[l]), cos, sa, sb, tm)
        prep = _ssm_prep(ssm_a_re[l], ssm_a_im[l], ssm_log_dt[l], ssm_b_re[l], ssm_b_im[l],
                         ssm_c_re[l], ssm_c_im[l], ssm_d[l], n_chunks // N_SEG)
        y = _ssm(u, prep, bsz, seq_pad)
        att = _attention(q, kd, vd, attn_sinks[l], bias, bsz, n_blk)
        hres = _merge(stream, y, att, gs, ga, w_glu[l].astype(BF16), b_glu[l][None], w_o_ssm[l].astype(BF16),
                      w_o_attn[l].astype(BF16), w_out[l].astype(BF16), norm_mix_post[l][None], tm)
        ffn_w = (norm_mlp_pre[l][None], w_up[l].astype(BF16), w_down[l].astype(BF16), norm_mlp_post[l][None])
        if l + 1 < depth:
            stream = _Rows("padded", _ffn(_Rows("padded", hres, tm, seq_pad // tm, n_blk), n_rows, *ffn_w, tm),
                           tm, seq_pad // tm, n_blk)
        else:
            out = _ffn(_Rows("frames", hres, tm_out, seq // tm_out, n_blk), bsz * seq, *ffn_w, tm_out)
    return out.reshape(bsz, seq, D_MODEL)
```

```python
import functools

import jax
import jax.numpy as jnp
import numpy as np
from jax import lax
from jax.experimental import pallas as pl
from jax.experimental.pallas import tpu as pltpu

D_MODEL = 1024
N_META = 16
HEAD_DIM = 64
N_Q_HEADS = 16
N_KV_HEADS = 4
D_ATTN = N_Q_HEADS * HEAD_DIM
D_KV = N_KV_HEADS * HEAD_DIM
BLOCK = 128
ROPE_THETA = 10000.0
ATTN_SCALE = HEAD_DIM ** -0.5
NEG_INF = -1e30
D_SSM = D_MODEL // 2
SSM_GROUP = 16
N_SSM_GROUPS = D_SSM // SSM_GROUP
SSM_STATE = 64
D_FF = 4 * D_MODEL
RMS_EPS = 1e-6

LANES = 128
SUBLANES = 8
MXU = 256
PAD = BLOCK - N_META
CHUNK = 16
TILE_GROUPS = LANES // SSM_GROUP
N_SSM_TILES = D_SSM // LANES
TILE_STATE = TILE_GROUPS * SSM_STATE
CHUNK_W = CHUNK * LANES
N_SEG = SUBLANES
VMEM_LIMIT = 56 * 1024 * 1024

C_U = 0
C_Q = C_U + D_SSM
C_K = C_Q + D_ATTN
C_GS = C_K + 2 * D_KV
C_GA = C_GS + D_MODEL
C_END = C_GA + D_MODEL

BF16 = jnp.bfloat16
F32 = jnp.float32


def _dot(a, b):
    return jnp.dot(a, b, preferred_element_type=F32)


def _rms(x, gain):
    return x * lax.rsqrt(jnp.mean(x * x, axis=-1, keepdims=True) + RMS_EPS) * gain


def _const_spec(shape):
    return pl.BlockSpec(shape, lambda *_: (0,) * len(shape), pipeline_mode=pl.Buffered(1))


def _params(sem):
    return pltpu.CompilerParams(dimension_semantics=sem, vmem_limit_bytes=VMEM_LIMIT)


class _Rows:
    def __init__(self, kind, array, tm, tiles_per_batch, blocks_per_batch, meta_block=None):
        self.has_meta = kind == "input"
        self.tiles_per_batch = tiles_per_batch
        nb = tm // BLOCK
        if kind == "padded":
            self.specs = [pl.BlockSpec((tm, D_MODEL), lambda i: (i, 0))]
            self.operands = [array]
            return
        shift = -1 if kind == "input" else 1

        def imap(j):
            return lambda i: ((i // tiles_per_batch) * blocks_per_batch
                              + jnp.maximum((i % tiles_per_batch) * nb + j + shift, 0), 0)

        self.specs = [pl.BlockSpec((BLOCK, D_MODEL), imap(j)) for j in range(nb)]
        self.operands = [array] * nb
        if self.has_meta:
            self.specs.append(_const_spec((BLOCK, D_MODEL)))
            self.operands.append(meta_block)

    def load(self, refs):
        n = len(self.specs)
        if n == 1:
            return refs[0][...]
        blocks = [r[...] for r in refs[:n - self.has_meta]]
        if self.has_meta:
            head = pl.program_id(0) % self.tiles_per_batch == 0
            blocks[0] = jnp.where(head, refs[n - 1][...], blocks[0])
        return jnp.concatenate(blocks, axis=0)


def _inproj_kernel(rows, *refs):
    n = len(rows.specs)
    gain_ref, w_ref, wvt_ref, cos_ref, sa_ref, sb_ref, u_ref, q_ref, k_ref, vt_ref, gs_ref, ga_ref = refs[n:]
    h = _rms(rows.load(refs[:n]), gain_ref[...]).astype(BF16)
    cos, sa, sb = cos_ref[...], sa_ref[...], sb_ref[...]

    def rope(t):
        return t * cos + pltpu.roll(t, LANES - HEAD_DIM // 2, 1) * sa + pltpu.roll(t, HEAD_DIM // 2, 1) * sb

    step = 512
    u_ref[...] = _dot(h, w_ref[:, C_U:C_U + D_SSM])
    for c in range(0, D_ATTN, step):
        t = _dot(h, w_ref[:, C_Q + c:C_Q + c + step])
        for j in range(0, step, LANES):
            q_ref[:, c + j:c + j + LANES] = (rope(t[:, j:j + LANES]) * ATTN_SCALE).astype(BF16)
    t = _dot(h, w_ref[:, C_K:C_K + 2 * D_KV])
    for j in range(0, 2 * D_KV, LANES):
        k_ref[:, j:j + LANES] = rope(t[:, j:j + LANES]).astype(BF16)
    vt_ref[...] = lax.dot_general(wvt_ref[...], h, (((1,), (1,)), ((), ())), preferred_element_type=F32).astype(BF16)
    for c in range(0, D_MODEL, step):
        gs_ref[:, c:c + step] = _dot(h, w_ref[:, C_GS + c:C_GS + c + step]).astype(BF16)
        ga_ref[:, c:c + step] = _dot(h, w_ref[:, C_GA + c:C_GA + c + step]).astype(BF16)


def _inproj(rows, n_rows, gain, w, wvt, cos, sa, sb, tm):
    row = lambda width: pl.BlockSpec((tm, width), lambda i: (i, 0))
    tab = pl.BlockSpec((tm, LANES), lambda i: (i % rows.tiles_per_batch, 0))
    out = lambda width, dtype: (row(width), jax.ShapeDtypeStruct((n_rows, width), dtype))
    outs = [out(D_SSM, F32),
            out(D_ATTN, BF16), out(2 * D_KV, BF16),
            (pl.BlockSpec((2 * D_KV, tm), lambda i: (0, i)), jax.ShapeDtypeStruct((2 * D_KV, n_rows), BF16)),
            out(D_MODEL, BF16), out(D_MODEL, BF16)]
    return pl.pallas_call(
        functools.partial(_inproj_kernel, rows),
        grid=(n_rows // tm,),
        in_specs=rows.specs + [_const_spec((1, D_MODEL)), _const_spec((D_MODEL, C_END)),
                               _const_spec((2 * D_KV, D_MODEL)), tab, tab, tab],
        out_specs=[o[0] for o in outs],
        out_shape=[o[1] for o in outs],
        compiler_params=_params(("parallel",)),
        name="inproj",
    )(*rows.operands, gain, w, wvt, cos, sa, sb)


def _ssm_spread(kc_ref, wc_ref, qc_ref, wt_s, pb_s, qm_s):
    def group_of(shape, axis, width):
        return (lax.broadcasted_iota(jnp.int32, shape, axis) // width) % TILE_GROUPS

    def copies(width):
        shape = (width, TILE_GROUPS * width)
        same = lax.broadcasted_iota(jnp.int32, shape, 1) % width == lax.broadcasted_iota(jnp.int32, shape, 0)
        return jnp.where(same, 1.0, 0.0).astype(BF16)

    shape = (CHUNK_W, LANES)
    diag = group_of(shape, 0, SSM_GROUP) == group_of(shape, 1, SSM_GROUP)
    taps = jnp.where(diag, _dot(kc_ref[0].astype(BF16), copies(SSM_GROUP)), 0.0).astype(BF16)
    lag_block = lambda lag: taps[lag * LANES:(lag + 1) * LANES]
    for a in range(CHUNK // 2):
        wt_s[a, :LANES, :LANES] = lag_block(2 * a)
        wt_s[a, :LANES, LANES:] = lag_block(2 * a + 1)
        wt_s[a, LANES:, :LANES] = lag_block(2 * a - 1) if a else jnp.zeros((LANES, LANES), BF16)
        wt_s[a, LANES:, LANES:] = lag_block(2 * a)

    rows = CHUNK_W // 4
    shape = (rows, TILE_STATE)
    diag = group_of(shape, 0, SSM_GROUP) == group_of(shape, 1, SSM_STATE)
    spread = copies(SSM_STATE)
    for part in range(2):
        for r in range(0, CHUNK_W, rows):
            w = _dot(wc_ref[0, part, r:r + rows].astype(BF16), spread)
            pb_s[r:r + rows, part * TILE_STATE:(part + 1) * TILE_STATE] = jnp.where(diag, w, 0.0).astype(BF16)

    col_group = group_of((SSM_STATE, CHUNK_W), 1, SSM_GROUP)
    for part in range(2):
        q = qc_ref[0, part]
        for h in range(TILE_GROUPS):
            r = part * TILE_STATE + h * SSM_STATE
            qm_s[r:r + SSM_STATE, :] = jnp.where(col_group == h, q, 0.0).astype(BF16)


def _ssm_kernel(u_ref, kc_ref, wc_ref, qc_ref, a_ref, pw_ref, y_ref, z_s, v_s, l_s, s_s, wt_s, pb_s, qm_s):
    n_chunks = z_s.shape[0]
    seg = n_chunks // N_SEG
    n_re = TILE_STATE // LANES

    @pl.when(pl.program_id(1) == 0)
    def _():
        _ssm_spread(kc_ref, wc_ref, qc_ref, wt_s, pb_s, qm_s)

    for t in range(CHUNK):
        z_s[:, t * LANES:(t + 1) * LANES] = u_ref[pl.ds(t, n_chunks, stride=CHUNK), :].astype(BF16)
    v = _dot(z_s[...], pb_s[...])
    for c in range(2 * n_re):
        v_s[c] = v[:, c * LANES:(c + 1) * LANES]

    def load(ref, rows, part):
        return jnp.concatenate([ref[part * n_re + c, rows, :] for c in range(n_re)], axis=1)

    def store(ref, rows, part, val):
        for c in range(n_re):
            ref[part * n_re + c, rows, :] = val[:, c * LANES:(c + 1) * LANES]

    are, aim = a_ref[0, 0], a_ref[0, 1]

    def local_scan(i, carry):
        sre, sim = carry
        rows = pl.ds(i, N_SEG, stride=seg)
        store(l_s, rows, 0, sre)
        store(l_s, rows, 1, sim)
        return (are * sre - aim * sim + load(v_s, rows, 0), are * sim + aim * sre + load(v_s, rows, 1))

    zero = jnp.zeros((N_SEG, TILE_STATE), F32)
    ere, eim = lax.fori_loop(0, seg, local_scan, (zero, zero))

    bre, bim = a_ref[0, 2][:1], a_ref[0, 3][:1]
    tre, tim = [zero[:1]], [zero[:1]]
    for m in range(1, N_SEG):
        pre, pim = tre[-1], tim[-1]
        tre.append(bre * pre - bim * pim + ere[m - 1:m])
        tim.append(bre * pim + bim * pre + eim[m - 1:m])
    tre, tim = jnp.concatenate(tre, axis=0), jnp.concatenate(tim, axis=0)

    def add_carry(i, _):
        rows = pl.ds(i, N_SEG, stride=seg)
        pre, pim = pw_ref[0, i, 0], pw_ref[0, i, 1]
        store(l_s, rows, 0, load(l_s, rows, 0) + (pre * tre - pim * tim))
        store(l_s, rows, 1, load(l_s, rows, 1) + (pre * tim + pim * tre))
        return 0

    lax.fori_loop(0, seg, add_carry, 0)

    for c in range(2 * n_re):
        s_s[:, c * LANES:(c + 1) * LANES] = l_s[c].astype(BF16)
    for tp in range(CHUNK // 2):
        acc = _dot(s_s[...], qm_s[:, tp * MXU:(tp + 1) * MXU])
        for tq in range(tp + 1):
            acc = acc + _dot(z_s[:, tq * MXU:(tq + 1) * MXU], wt_s[tp - tq])
        for e in range(2):
            y_ref[pl.ds(2 * tp + e, n_chunks, stride=CHUNK), :] = acc[:, e * LANES:(e + 1) * LANES]


def _ssm(u, prep, bsz, seq_pad):
    n_chunks = seq_pad // CHUNK
    tile = lambda shape: pl.BlockSpec((1,) + shape, lambda j, b: (j,) + (0,) * len(shape))
    io = pl.BlockSpec((seq_pad, LANES), lambda j, b: (b, j))
    return pl.pallas_call(
        _ssm_kernel,
        grid=(N_SSM_TILES, bsz),
        in_specs=[io] + [tile(p.shape[1:]) for p in prep],
        out_specs=io,
        out_shape=jax.ShapeDtypeStruct(u.shape, F32),
        scratch_shapes=[pltpu.VMEM((n_chunks, CHUNK_W), BF16),
                        pltpu.VMEM((2 * TILE_STATE // LANES, n_chunks, LANES), F32),
                        pltpu.VMEM((2 * TILE_STATE // LANES, n_chunks, LANES), F32),
                        pltpu.VMEM((n_chunks, 2 * TILE_STATE), BF16),
                        pltpu.VMEM((CHUNK // 2, MXU, MXU), BF16),
                        pltpu.VMEM((CHUNK_W, 2 * TILE_STATE), BF16),
                        pltpu.VMEM((2 * TILE_STATE, CHUNK_W), BF16)],
        compiler_params=_params(("arbitrary", "arbitrary")),
        name="ssm",
    )(u, *prep)


def _ssm_prep(a_re, a_im, log_dt, b_re, b_im, c_re, c_im, d_skip, seg):
    hp = lax.Precision.HIGHEST
    G, N, C, T, J, E = N_SSM_GROUPS, SSM_STATE, SSM_GROUP, CHUNK, N_SSM_TILES, TILE_GROUPS
    dt = jnp.exp(log_dt)[:, None]
    lam_re, lam_im = a_re * dt, a_im * dt

    def powers(j):
        j = jnp.asarray(j, F32)[:, None, None]
        mag = jnp.exp(lam_re[None] * j)
        return mag * jnp.cos(lam_im[None] * j), mag * jnp.sin(lam_im[None] * j)

    pw_re, pw_im = powers(np.arange(T + 1))
    den = a_re * a_re + a_im * a_im
    nre, nim = pw_re[1] - 1.0, pw_im[1]
    coef_re = (nre * a_re + nim * a_im) / den
    coef_im = (nim * a_re - nre * a_im) / den
    bb_re = coef_re[..., None] * b_re - coef_im[..., None] * b_im
    bb_im = coef_re[..., None] * b_im + coef_im[..., None] * b_re
    w_re = pw_re[:T, ..., None] * bb_re[None] - pw_im[:T, ..., None] * bb_im[None]
    w_im = pw_re[:T, ..., None] * bb_im[None] + pw_im[:T, ..., None] * bb_re[None]

    taps = (jnp.einsum('gdn,jgnc->gjdc', c_re, w_re, precision=hp)
            - jnp.einsum('gdn,jgnc->gjdc', c_im, w_im, precision=hp))
    taps = taps.at[:, 0].add(d_skip[:, :, None] * jnp.eye(C, dtype=F32)[None])
    kc = taps.reshape(J, E, T, C, C).transpose(0, 2, 1, 4, 3).reshape(J, T * LANES, C)

    state_in = lambda w: w[::-1].reshape(T, J, E, N, C).transpose(1, 0, 2, 4, 3).reshape(J, CHUNK_W, N)
    wc = jnp.stack([state_in(w_re), state_in(w_im)], axis=1)

    p1_re, p1_im = pw_re[1:].transpose(1, 0, 2), pw_im[1:].transpose(1, 0, 2)
    q_re = c_re[:, None] * p1_re[:, :, None] - c_im[:, None] * p1_im[:, :, None]
    q_im = -(c_re[:, None] * p1_im[:, :, None] + c_im[:, None] * p1_re[:, :, None])
    state_out = lambda q: q.reshape(J, E, T, C, N).transpose(0, 4, 2, 1, 3).reshape(J, N, CHUNK_W)
    qc = jnp.stack([state_out(q_re), state_out(q_im)], axis=1)

    tile_row = lambda m: jnp.broadcast_to(m.reshape(-1, J, 1, TILE_STATE), (m.shape[0], J, SUBLANES, TILE_STATE))
    t_re, t_im = powers(T * np.array([1, seg]))
    a = tile_row(jnp.stack([t_re[0], t_im[0], t_re[1], t_im[1]])).transpose(1, 0, 2, 3)
    s_re, s_im = powers(T * np.arange(seg))
    pw = jnp.stack([tile_row(s_re), tile_row(s_im)], axis=2).transpose(1, 0, 2, 3, 4)
    return kc, wc, qc, a, pw


def _attn_kernel(sink_ref, bias_ref, q_ref, km_ref, kp_ref, kc_ref, vm_ref, vp_ref, vc_ref, o_ref):
    n_keys = N_META + 2 * BLOCK
    bias = bias_ref[0]
    first_k = lax.broadcasted_iota(jnp.int32, (n_keys, LANES), 1) < HEAD_DIM
    first_v = lax.broadcasted_iota(jnp.int32, (LANES, 3 * BLOCK), 0) < HEAD_DIM
    first_o = lax.broadcasted_iota(jnp.int32, (LANES, 2 * BLOCK), 0) < HEAD_DIM
    zero_k = jnp.zeros((n_keys, LANES), BF16)
    zero_v = jnp.zeros((LANES, 3 * BLOCK), BF16)
    pad_p = jnp.zeros((PAD, 2 * BLOCK), BF16)

    def scores(kh):
        cols = slice(kh * LANES, (kh + 1) * LANES)
        kd = jnp.concatenate([km_ref[:, cols], kp_ref[:, cols], kc_ref[:, cols]], axis=0)
        kab = jnp.concatenate([jnp.where(first_k, kd, zero_k), jnp.where(first_k, zero_k, kd)], axis=0)
        q4 = jnp.concatenate([q_ref[:, (2 * kh + p) * LANES:(2 * kh + p + 1) * LANES] for p in range(2)], axis=0)
        return lax.dot_general(kab, q4, (((1,), (1,)), ((), ())), preferred_element_type=F32)

    s_next = scores(0)
    for kh in range(N_KV_HEADS):
        s = s_next
        if kh + 1 < N_KV_HEADS:
            s_next = scores(kh + 1)
        pt, rden = [], []
        for e in range(2):
            blocks, rd = [], []
            for p in range(2):
                se = s[e * n_keys:(e + 1) * n_keys, p * LANES:(p + 1) * LANES] + bias
                sink = sink_ref[4 * kh + 2 * p + e]
                m = jnp.maximum(jnp.max(se, axis=0, keepdims=True), sink)
                pe = jnp.exp(se - m)
                den = jnp.sum(pe, axis=0, keepdims=True) + jnp.exp(sink - m)
                blocks.append(pe.astype(BF16))
                rd.append(1.0 / den)
            pt += [pad_p, jnp.concatenate(blocks, axis=1)]
            rden.append(jnp.concatenate(rd, axis=1))
        rows = slice(kh * LANES, (kh + 1) * LANES)
        vd = jnp.concatenate([vm_ref[rows, :], vp_ref[rows, :], vc_ref[rows, :]], axis=1)
        vab = jnp.concatenate([jnp.where(first_v, vd, zero_v), jnp.where(first_v, zero_v, vd)], axis=1)
        ot = _dot(vab, jnp.concatenate(pt, axis=0))
        o = (ot * jnp.where(first_o, rden[0], rden[1])).T
        for p in range(2):
            o_ref[:, (2 * kh + p) * LANES:(2 * kh + p + 1) * LANES] = o[p * BLOCK:(p + 1) * BLOCK].astype(BF16)


def _attention(q, kd, vt, sinks, bias, bsz, n_blk):
    cur = lambda b, n: b * n_blk + n
    prev = lambda b, n: b * n_blk + jnp.maximum(n - 1, 0)
    meta = lambda b, n: b * n_blk
    meta_rows = lambda b, n: (b * n_blk + 1) * (BLOCK // N_META) - 1
    k_spec = lambda imap: pl.BlockSpec((BLOCK, 2 * D_KV), lambda b, n: (imap(b, n), 0))
    v_spec = lambda imap: pl.BlockSpec((2 * D_KV, BLOCK), lambda b, n: (0, imap(b, n)))
    return pl.pallas_call(
        _attn_kernel,
        grid=(bsz, n_blk),
        in_specs=[pl.BlockSpec(memory_space=pltpu.SMEM),
                  pl.BlockSpec((1, N_META + 2 * BLOCK, BLOCK), lambda b, n: (jnp.minimum(n, 2), 0, 0)),
                  pl.BlockSpec((BLOCK, D_ATTN), lambda b, n: (cur(b, n), 0)),
                  pl.BlockSpec((N_META, 2 * D_KV), lambda b, n: (meta_rows(b, n), 0)), k_spec(prev), k_spec(cur),
                  v_spec(meta), v_spec(prev), v_spec(cur)],
        out_specs=pl.BlockSpec((BLOCK, D_ATTN), lambda b, n: (cur(b, n), 0)),
        out_shape=jax.ShapeDtypeStruct(q.shape, BF16),
        compiler_params=_params(("parallel", "parallel")),
        name="attention",
    )(sinks, bias, q, kd, kd, kd, vt, vt, vt)


def _attn_bias():
    i = np.arange(BLOCK)[None, :]
    j = np.arange(BLOCK)[:, None]
    none = np.zeros((BLOCK, BLOCK), bool)
    causal = j <= i
    in_meta = np.broadcast_to(j >= PAD, (BLOCK, BLOCK))
    blk0 = [none[:N_META], none, causal & in_meta]
    blk1 = [none[:N_META], in_meta, causal]
    blk2 = [~none[:N_META], j > i, causal]
    ok = np.stack([np.concatenate(b, axis=0) for b in (blk0, blk1, blk2)])
    return jnp.asarray(np.where(ok, 0.0, NEG_INF), F32)


def _merge_kernel(rows, *refs):
    n = len(rows.specs)
    y_ref, a_ref, gs_ref, ga_ref, wglu_ref, bglu_ref, wos_ref, woa_ref, wout_ref, gain_ref, o_ref = refs[n:]
    z = jax.nn.gelu(y_ref[...])
    z = z * jax.nn.sigmoid(_dot(z.astype(BF16), wglu_ref[...]) + bglu_ref[...])
    merged = (jax.nn.sigmoid(gs_ref[...].astype(F32)) * _dot(z.astype(BF16), wos_ref[...])
              + jax.nn.sigmoid(ga_ref[...].astype(F32)) * _dot(a_ref[...], woa_ref[...]))
    mix = _dot(merged.astype(BF16), wout_ref[...])
    o_ref[...] = rows.load(refs[:n]) + _rms(mix, gain_ref[...])


def _merge(rows, y, a, gs, ga, wglu, bglu, wos, woa, wout, gain, tm):
    n_rows = y.shape[0]
    row = lambda width: pl.BlockSpec((tm, width), lambda i: (i, 0))
    return pl.pallas_call(
        functools.partial(_merge_kernel, rows),
        grid=(n_rows // tm,),
        in_specs=rows.specs + [row(D_SSM), row(D_ATTN), row(D_MODEL), row(D_MODEL),
                               _const_spec((D_SSM, D_SSM)), _const_spec((1, D_SSM)), _const_spec((D_SSM, D_MODEL)),
                               _const_spec((D_ATTN, D_MODEL)), _const_spec((D_MODEL, D_MODEL)),
                               _const_spec((1, D_MODEL))],
        out_specs=row(D_MODEL),
        out_shape=jax.ShapeDtypeStruct((n_rows, D_MODEL), F32),
        compiler_params=_params(("parallel",)),
        name="merge",
    )(*rows.operands, y, a, gs, ga, wglu, bglu, wos, woa, wout, gain)


def _ffn_kernel(rows, *refs):
    n = len(rows.specs)
    gpre_ref, wup_ref, wdown_ref, gpost_ref, o_ref = refs[n:]
    x = rows.load(refs[:n])
    h = _rms(x, gpre_ref[...]).astype(BF16)
    step = 512
    acc = jnp.zeros(x.shape, F32)
    for c in range(0, D_FF, step):
        a = jnp.maximum(_dot(h, wup_ref[:, c:c + step]), 0.0)
        acc = acc + _dot((a * a).astype(BF16), wdown_ref[c:c + step, :])
    o_ref[...] = x + _rms(acc, gpost_ref[...])


def _ffn(rows, n_rows, gpre, wup, wdown, gpost, tm):
    return pl.pallas_call(
        functools.partial(_ffn_kernel, rows),
        grid=(n_rows // tm,),
        in_specs=rows.specs + [_const_spec((1, D_MODEL)), _const_spec((D_MODEL, D_FF)),
                               _const_spec((D_FF, D_MODEL)), _const_spec((1, D_MODEL))],
        out_specs=pl.BlockSpec((tm, D_MODEL), lambda i: (i, 0)),
        out_shape=jax.ShapeDtypeStruct((n_rows, D_MODEL), F32),
        compiler_params=_params(("parallel",)),
        name="ffn",
    )(*rows.operands, gpre, wup, wdown, gpost)


def _rope_tables(seq_pad):
    pos = (jnp.arange(seq_pad, dtype=jnp.int32) - PAD).astype(F32)
    inv_freq = 1.0 / (ROPE_THETA ** (jnp.arange(0, HEAD_DIM, 2, dtype=F32) / HEAD_DIM))
    ang = pos[:, None] * inv_freq[None, :]
    ang = jnp.concatenate([ang, ang, ang, ang], axis=-1)
    first_half = (np.arange(LANES) % HEAD_DIM) < HEAD_DIM // 2
    sin = jnp.sin(ang)
    return jnp.cos(ang), jnp.where(first_half, -sin, 0.0), jnp.where(first_half, 0.0, sin)


def _widen_w_in(w):
    cuts = np.cumsum([D_SSM, D_ATTN, D_KV, D_KV, D_MODEL])
    wu, wq, wk, wv, wgs, wga = jnp.split(w, cuts, axis=-1)
    dup = lambda m: jnp.tile(m.reshape(D_MODEL, N_KV_HEADS, 1, HEAD_DIM), (1, 1, 2, 1)).reshape(D_MODEL, 2 * D_KV)
    return jnp.concatenate([wu, wq, dup(wk), wgs, wga], axis=-1).astype(BF16), dup(wv).T.astype(BF16)


def kernel(x, meta_tokens, norm_mix_pre, norm_mix_post, norm_mlp_pre, norm_mlp_post, w_in, ssm_a_re, ssm_a_im, ssm_log_dt, ssm_b_re, ssm_b_im, ssm_c_re, ssm_c_im, ssm_d, w_glu, b_glu, attn_sinks, w_o_ssm, w_o_attn, w_out, w_up, w_down):
    bsz, seq, _ = x.shape
    depth = w_in.shape[0]
    seq_pad = PAD + N_META + seq
    assert seq_pad % BLOCK == 0 and (seq_pad // CHUNK) % N_SEG == 0
    n_blk, n_chunks = seq_pad // BLOCK, seq_pad // CHUNK
    tm, tm_out = 640, 512
    assert seq_pad % tm == 0 and seq % tm_out == 0
    n_rows = bsz * seq_pad

    meta_block = jnp.concatenate([jnp.zeros((PAD, D_MODEL), x.dtype), meta_tokens.astype(x.dtype)], axis=0)
    stream = _Rows("input", x.reshape(bsz * seq, D_MODEL), tm, seq_pad // tm, seq // BLOCK, meta_block)
    cos, sa, sb = _rope_tables(seq_pad)
    bias = _attn_bias()

    for l in range(depth):
        u, q, kd, vt, gs, ga = _inproj(stream, n_rows, norm_mix_pre[l][None], *_widen_w_in(w_in[l]), cos, sa, sb, tm)
        prep = _ssm_prep(ssm_a_re[l], ssm_a_im[l], ssm_log_dt[l], ssm_b_re[l], ssm_b_im[l],
                         ssm_c_re[l], ssm_c_im[l], ssm_d[l], n_chunks // N_SEG)
        y = _ssm(u, prep, bsz, seq_pad)
        att = _attention(q, kd, vt, attn_sinks[l], bias, bsz, n_blk)
        hres = _merge(stream, y, att, gs, ga, w_glu[l].astype(BF16), b_glu[l][None], w_o_ssm[l].astype(BF16),
                      w_o_attn[l].astype(BF16), w_out[l].astype(BF16), norm_mix_post[l][None], tm)
        ffn_w = (norm_mlp_pre[l][None], w_up[l].astype(BF16), w_down[l].astype(BF16), norm_mlp_post[l][None])
        if l + 1 < depth:
            stream = _Rows("padded", _ffn(_Rows("padded", hres, tm, seq_pad // tm, n_blk), n_rows, *ffn_w, tm),
                           tm, seq_pad // tm, n_blk)
        else:
            out = _ffn(_Rows("frames", hres, tm_out, seq // tm_out, n_blk), bsz * seq, *ffn_w, tm_out)
    return out.reshape(bsz, seq, D_MODEL)
```

```python
import functools

import jax
import jax.numpy as jnp
import numpy as np
from jax import lax
from jax.experimental import pallas as pl
from jax.experimental.pallas import tpu as pltpu

D_MODEL = 1024
N_META = 16
HEAD_DIM = 64
N_Q_HEADS = 16
N_KV_HEADS = 4
D_ATTN = N_Q_HEADS * HEAD_DIM
D_KV = N_KV_HEADS * HEAD_DIM
BLOCK = 128
ROPE_THETA = 10000.0
ATTN_SCALE = HEAD_DIM ** -0.5
NEG_INF = -1e30
D_SSM = D_MODEL // 2
SSM_GROUP = 16
N_SSM_GROUPS = D_SSM // SSM_GROUP
SSM_STATE = 64
D_FF = 4 * D_MODEL
RMS_EPS = 1e-6

LANES = 128
SUBLANES = 8
MXU = 256
PAD = BLOCK - N_META
CHUNK = 16
TILE_GROUPS = LANES // SSM_GROUP
N_SSM_TILES = D_SSM // LANES
TILE_STATE = TILE_GROUPS * SSM_STATE
CHUNK_W = CHUNK * LANES
N_SEG = SUBLANES
VMEM_LIMIT = 56 * 1024 * 1024

C_U = 0
C_Q = C_U + D_SSM
C_K = C_Q + D_ATTN
C_V = C_K + D_KV
C_GS = C_V + D_KV
C_GA = C_GS + D_MODEL
C_END = C_GA + D_MODEL

BF16 = jnp.bfloat16
F32 = jnp.float32


def _dot(a, b):
    return jnp.dot(a, b, preferred_element_type=F32)


def _rms(x, gain):
    return x * lax.rsqrt(jnp.mean(x * x, axis=-1, keepdims=True) + RMS_EPS) * gain


def _const_spec(shape):
    return pl.BlockSpec(shape, lambda *_: (0,) * len(shape), pipeline_mode=pl.Buffered(1))


def _layer_spec(stacked, layer):
    shape = stacked.shape[1:]
    return pl.BlockSpec((None,) + shape, lambda *_: (layer,) + (0,) * len(shape), pipeline_mode=pl.Buffered(1))


def _params(sem):
    return pltpu.CompilerParams(dimension_semantics=sem, vmem_limit_bytes=VMEM_LIMIT)


class _Rows:
    def __init__(self, kind, array, tm, tiles_per_batch, blocks_per_batch, meta_block=None):
        self.has_meta = kind == "input"
        self.tiles_per_batch = tiles_per_batch
        nb = tm // BLOCK
        if kind == "padded":
            self.specs = [pl.BlockSpec((tm, D_MODEL), lambda i: (i, 0))]
            self.operands = [array]
            return
        shift = -1 if kind == "input" else 1

        def imap(j):
            return lambda i: ((i // tiles_per_batch) * blocks_per_batch
                              + jnp.maximum((i % tiles_per_batch) * nb + j + shift, 0), 0)

        self.specs = [pl.BlockSpec((BLOCK, D_MODEL), imap(j)) for j in range(nb)]
        self.operands = [array] * nb
        if self.has_meta:
            self.specs.append(_const_spec((BLOCK, D_MODEL)))
            self.operands.append(meta_block)

    def load(self, refs):
        n = len(self.specs)
        if n == 1:
            return refs[0][...]
        blocks = [r[...] for r in refs[:n - self.has_meta]]
        if self.has_meta:
            head = pl.program_id(0) % self.tiles_per_batch == 0
            blocks[0] = jnp.where(head, refs[n - 1][...], blocks[0])
        return jnp.concatenate(blocks, axis=0)


def _inproj_kernel(rows, *refs):
    n = len(rows.specs)
    gain_ref, w_ref, wkd_ref, wvt_ref, cos_ref, sa_ref, sb_ref, u_ref, q_ref, k_ref, vt_ref, gs_ref, ga_ref = refs[n:]
    h = _rms(rows.load(refs[:n]), gain_ref[...]).astype(BF16)
    cos, sa, sb = cos_ref[...], sa_ref[...], sb_ref[...]

    def rope(t):
        return t * cos + pltpu.roll(t, LANES - HEAD_DIM // 2, 1) * sa + pltpu.roll(t, HEAD_DIM // 2, 1) * sb

    step = 512
    u_ref[...] = _dot(h, w_ref[:, C_U:C_U + D_SSM])
    for c in range(0, D_ATTN, step):
        t = _dot(h, w_ref[:, C_Q + c:C_Q + c + step])
        for j in range(0, step, LANES):
            q_ref[:, c + j:c + j + LANES] = (rope(t[:, j:j + LANES]) * ATTN_SCALE).astype(BF16)
    t = _dot(h, wkd_ref[...])
    for j in range(0, 2 * D_KV, LANES):
        k_ref[:, j:j + LANES] = rope(t[:, j:j + LANES]).astype(BF16)
    vt_ref[...] = lax.dot_general(wvt_ref[...], h, (((1,), (1,)), ((), ())), preferred_element_type=F32).astype(BF16)
    for c in range(0, D_MODEL, step):
        gs_ref[:, c:c + step] = _dot(h, w_ref[:, C_GS + c:C_GS + c + step]).astype(BF16)
        ga_ref[:, c:c + step] = _dot(h, w_ref[:, C_GA + c:C_GA + c + step]).astype(BF16)


def _inproj(rows, n_rows, layer, gain, w, wkd, wvt, cos, sa, sb, tm):
    row = lambda width: pl.BlockSpec((tm, width), lambda i: (i, 0))
    tab = pl.BlockSpec((tm, LANES), lambda i: (i % rows.tiles_per_batch, 0))
    out = lambda width, dtype: (row(width), jax.ShapeDtypeStruct((n_rows, width), dtype))
    outs = [out(D_SSM, F32),
            out(D_ATTN, BF16), out(2 * D_KV, BF16),
            (pl.BlockSpec((2 * D_KV, tm), lambda i: (0, i)), jax.ShapeDtypeStruct((2 * D_KV, n_rows), BF16)),
            out(D_MODEL, BF16), out(D_MODEL, BF16)]
    return pl.pallas_call(
        functools.partial(_inproj_kernel, rows),
        grid=(n_rows // tm,),
        in_specs=rows.specs + [_layer_spec(p, layer) for p in (gain, w, wkd, wvt)] + [tab, tab, tab],
        out_specs=[o[0] for o in outs],
        out_shape=[o[1] for o in outs],
        compiler_params=_params(("parallel",)),
        name="inproj",
    )(*rows.operands, gain, w, wkd, wvt, cos, sa, sb)


def _ssm_spread(kc_ref, wc_ref, qc_ref, wt_s, pb_s, qm_s):
    def group_of(shape, axis, width):
        return (lax.broadcasted_iota(jnp.int32, shape, axis) // width) % TILE_GROUPS

    def copies(width):
        shape = (width, TILE_GROUPS * width)
        same = lax.broadcasted_iota(jnp.int32, shape, 1) % width == lax.broadcasted_iota(jnp.int32, shape, 0)
        return jnp.where(same, 1.0, 0.0).astype(BF16)

    shape = (CHUNK_W, LANES)
    diag = group_of(shape, 0, SSM_GROUP) == group_of(shape, 1, SSM_GROUP)
    taps = jnp.where(diag, _dot(kc_ref[0].astype(BF16), copies(SSM_GROUP)), 0.0).astype(BF16)
    lag_block = lambda lag: taps[lag * LANES:(lag + 1) * LANES]
    for a in range(CHUNK // 2):
        wt_s[a, :LANES, :LANES] = lag_block(2 * a)
        wt_s[a, :LANES, LANES:] = lag_block(2 * a + 1)
        wt_s[a, LANES:, :LANES] = lag_block(2 * a - 1) if a else jnp.zeros((LANES, LANES), BF16)
        wt_s[a, LANES:, LANES:] = lag_block(2 * a)

    rows = CHUNK_W // 4
    shape = (rows, TILE_STATE)
    diag = group_of(shape, 0, SSM_GROUP) == group_of(shape, 1, SSM_STATE)
    spread = copies(SSM_STATE)
    for part in range(2):
        for r in range(0, CHUNK_W, rows):
            w = _dot(wc_ref[0, part, r:r + rows].astype(BF16), spread)
            pb_s[r:r + rows, part * TILE_STATE:(part + 1) * TILE_STATE] = jnp.where(diag, w, 0.0).astype(BF16)

    col_group = group_of((SSM_STATE, CHUNK_W), 1, SSM_GROUP)
    for part in range(2):
        q = qc_ref[0, part]
        for h in range(TILE_GROUPS):
            r = part * TILE_STATE + h * SSM_STATE
            qm_s[r:r + SSM_STATE, :] = jnp.where(col_group == h, q, 0.0).astype(BF16)


def _ssm_kernel(u_ref, kc_ref, wc_ref, qc_ref, a_ref, pw_ref, y_ref, z_s, v_s, l_s, s_s, wt_s, pb_s, qm_s):
    n_chunks = z_s.shape[0]
    seg = n_chunks // N_SEG
    n_re = TILE_STATE // LANES

    @pl.when(pl.program_id(1) == 0)
    def _():
        _ssm_spread(kc_ref, wc_ref, qc_ref, wt_s, pb_s, qm_s)

    for t in range(CHUNK):
        z_s[:, t * LANES:(t + 1) * LANES] = u_ref[pl.ds(t, n_chunks, stride=CHUNK), :].astype(BF16)
    v = _dot(z_s[...], pb_s[...])
    for c in range(2 * n_re):
        v_s[c] = v[:, c * LANES:(c + 1) * LANES]

    def load(ref, rows, part):
        return jnp.concatenate([ref[part * n_re + c, rows, :] for c in range(n_re)], axis=1)

    def store(ref, rows, part, val):
        for c in range(n_re):
            ref[part * n_re + c, rows, :] = val[:, c * LANES:(c + 1) * LANES]

    are, aim = a_ref[0, 0], a_ref[0, 1]

    def local_scan(i, carry):
        sre, sim = carry
        rows = pl.ds(i, N_SEG, stride=seg)
        store(l_s, rows, 0, sre)
        store(l_s, rows, 1, sim)
        return (are * sre - aim * sim + load(v_s, rows, 0), are * sim + aim * sre + load(v_s, rows, 1))

    zero = jnp.zeros((N_SEG, TILE_STATE), F32)
    ere, eim = lax.fori_loop(0, seg, local_scan, (zero, zero))

    bre, bim = a_ref[0, 2][:1], a_ref[0, 3][:1]
    tre, tim = [zero[:1]], [zero[:1]]
    for m in range(1, N_SEG):
        pre, pim = tre[-1], tim[-1]
        tre.append(bre * pre - bim * pim + ere[m - 1:m])
        tim.append(bre * pim + bim * pre + eim[m - 1:m])
    tre, tim = jnp.concatenate(tre, axis=0), jnp.concatenate(tim, axis=0)

    def add_carry(i, _):
        rows = pl.ds(i, N_SEG, stride=seg)
        pw = pw_ref[0, i]
        pre, pim = pw[0:1], pw[1:2]
        store(l_s, rows, 0, load(l_s, rows, 0) + (pre * tre - pim * tim))
        store(l_s, rows, 1, load(l_s, rows, 1) + (pre * tim + pim * tre))
        return 0

    lax.fori_loop(0, seg, add_carry, 0)

    for c in range(2 * n_re):
        s_s[:, c * LANES:(c + 1) * LANES] = l_s[c].astype(BF16)
    for tp in range(CHUNK // 2):
        acc = _dot(s_s[...], qm_s[:, tp * MXU:(tp + 1) * MXU])
        for tq in range(tp + 1):
            acc = acc + _dot(z_s[:, tq * MXU:(tq + 1) * MXU], wt_s[tp - tq])
        for e in range(2):
            y_ref[pl.ds(2 * tp + e, n_chunks, stride=CHUNK), :] = acc[:, e * LANES:(e + 1) * LANES]


def _ssm(u, prep, layer, bsz, seq_pad):
    n_chunks = seq_pad // CHUNK
    tile = lambda shape: pl.BlockSpec((None, 1) + shape, lambda j, b: (layer, j) + (0,) * len(shape))
    io = pl.BlockSpec((seq_pad, LANES), lambda j, b: (b, j))
    return pl.pallas_call(
        _ssm_kernel,
        grid=(N_SSM_TILES, bsz),
        in_specs=[io] + [tile(p.shape[2:]) for p in prep],
        out_specs=io,
        out_shape=jax.ShapeDtypeStruct(u.shape, F32),
        scratch_shapes=[pltpu.VMEM((n_chunks, CHUNK_W), BF16),
                        pltpu.VMEM((2 * TILE_STATE // LANES, n_chunks, LANES), F32),
                        pltpu.VMEM((2 * TILE_STATE // LANES, n_chunks, LANES), F32),
                        pltpu.VMEM((n_chunks, 2 * TILE_STATE), BF16),
                        pltpu.VMEM((CHUNK // 2, MXU, MXU), BF16),
                        pltpu.VMEM((CHUNK_W, 2 * TILE_STATE), BF16),
                        pltpu.VMEM((2 * TILE_STATE, CHUNK_W), BF16)],
        compiler_params=_params(("arbitrary", "arbitrary")),
        name="ssm",
    )(u, *prep)


def _ssm_prep(a_re, a_im, log_dt, b_re, b_im, c_re, c_im, d_skip, seg):
    hp = lax.Precision.HIGHEST
    N, C, T, J, E = SSM_STATE, SSM_GROUP, CHUNK, N_SSM_TILES, TILE_GROUPS
    depth = a_re.shape[0]
    tiles = lambda m: m.reshape((depth, J, E) + m.shape[2:])
    a_re, a_im, dt = tiles(a_re), tiles(a_im), tiles(jnp.exp(log_dt))[..., None]
    lam_re, lam_im = (a_re * dt)[:, :, None], (a_im * dt)[:, :, None]

    def powers(p):
        p = jnp.asarray(p, F32)[:, None, None]
        mag = jnp.exp(lam_re * p)
        return mag * jnp.cos(lam_im * p), mag * jnp.sin(lam_im * p)

    pw_re, pw_im = powers(np.arange(T + 1))
    den = a_re * a_re + a_im * a_im
    nre, nim = pw_re[:, :, 1] - 1.0, pw_im[:, :, 1]
    coef_re = ((nre * a_re + nim * a_im) / den)[..., None, :]
    coef_im = ((nim * a_re - nre * a_im) / den)[..., None, :]
    bt_re, bt_im = tiles(b_re).swapaxes(-1, -2), tiles(b_im).swapaxes(-1, -2)
    bb_re = coef_re * bt_re - coef_im * bt_im
    bb_im = coef_re * bt_im + coef_im * bt_re
    p_re, p_im = pw_re[:, :, :T, :, None], pw_im[:, :, :T, :, None]
    w_re = p_re * bb_re[:, :, None] - p_im * bb_im[:, :, None]
    w_im = p_re * bb_im[:, :, None] + p_im * bb_re[:, :, None]

    cr, ci = tiles(c_re), tiles(c_im)
    taps = (jnp.einsum('ljsecn,ljedn->ljsecd', w_re, cr, precision=hp)
            - jnp.einsum('ljsecn,ljedn->ljsecd', w_im, ci, precision=hp))
    taps = taps.at[:, :, 0].add(tiles(d_skip)[..., None, :] * jnp.eye(C, dtype=F32))
    kc = taps.reshape(depth, J, T * LANES, C)

    wc = jnp.stack([w_re[:, :, ::-1], w_im[:, :, ::-1]], axis=2).reshape(depth, J, 2, CHUNK_W, N)

    ct_re, ct_im = cr.transpose(0, 1, 4, 2, 3)[:, :, :, None], ci.transpose(0, 1, 4, 2, 3)[:, :, :, None]
    pt_re = pw_re[:, :, 1:].transpose(0, 1, 4, 2, 3)[..., None]
    pt_im = pw_im[:, :, 1:].transpose(0, 1, 4, 2, 3)[..., None]
    qc = jnp.stack([ct_re * pt_re - ct_im * pt_im, -(ct_re * pt_im + ct_im * pt_re)], axis=2)
    qc = qc.reshape(depth, J, 2, N, CHUNK_W)

    flat = lambda m: m.reshape(depth, J, -1, TILE_STATE)
    t_re, t_im = powers(T * np.array([1, seg]))
    a = jnp.concatenate([flat(t_re), flat(t_im)], axis=2)[:, :, (0, 2, 1, 3), None]
    a = jnp.broadcast_to(a, (depth, J, 4, SUBLANES, TILE_STATE))
    s_re, s_im = powers(T * np.arange(seg))
    pw = jnp.stack([flat(s_re), flat(s_im)], axis=3)
    return kc, wc, qc, a, pw


def _attn_kernel(sink_ref, bias_ref, q_ref, km_ref, kp_ref, kc_ref, vm_ref, vp_ref, vc_ref, o_ref):
    n_keys = N_META + 2 * BLOCK
    bias = bias_ref[0]
    first_k = lax.broadcasted_iota(jnp.int32, (n_keys, LANES), 1) < HEAD_DIM
    first_v = lax.broadcasted_iota(jnp.int32, (LANES, 3 * BLOCK), 0) < HEAD_DIM
    first_o = lax.broadcasted_iota(jnp.int32, (LANES, 2 * BLOCK), 0) < HEAD_DIM
    zero_k = jnp.zeros((n_keys, LANES), BF16)
    zero_v = jnp.zeros((LANES, 3 * BLOCK), BF16)
    pad_p = jnp.zeros((PAD, 2 * BLOCK), BF16)

    def scores(kh):
        cols = slice(kh * LANES, (kh + 1) * LANES)
        kd = jnp.concatenate([km_ref[:, cols], kp_ref[:, cols], kc_ref[:, cols]], axis=0)
        kab = jnp.concatenate([jnp.where(first_k, kd, zero_k), jnp.where(first_k, zero_k, kd)], axis=0)
        q4 = jnp.concatenate([q_ref[:, (2 * kh + p) * LANES:(2 * kh + p + 1) * LANES] for p in range(2)], axis=0)
        return lax.dot_general(kab, q4, (((1,), (1,)), ((), ())), preferred_element_type=F32)

    s_next = scores(0)
    for kh in range(N_KV_HEADS):
        s = s_next
        if kh + 1 < N_KV_HEADS:
            s_next = scores(kh + 1)
        pt, rden = [], []
        for e in range(2):
            blocks, rd = [], []
            for p in range(2):
                se = s[e * n_keys:(e + 1) * n_keys, p * LANES:(p + 1) * LANES] + bias
                sink = sink_ref[4 * kh + 2 * p + e]
                m = jnp.maximum(jnp.max(se, axis=0, keepdims=True), sink)
                pe = jnp.exp(se - m)
                den = jnp.sum(pe, axis=0, keepdims=True) + jnp.exp(sink - m)
                blocks.append(pe.astype(BF16))
                rd.append(1.0 / den)
            pt += [pad_p, jnp.concatenate(blocks, axis=1)]
            rden.append(jnp.concatenate(rd, axis=1))
        rows = slice(kh * LANES, (kh + 1) * LANES)
        vd = jnp.concatenate([vm_ref[rows, :], vp_ref[rows, :], vc_ref[rows, :]], axis=1)
        vab = jnp.concatenate([jnp.where(first_v, vd, zero_v), jnp.where(first_v, zero_v, vd)], axis=1)
        ot = _dot(vab, jnp.concatenate(pt, axis=0))
        o = (ot * jnp.where(first_o, rden[0], rden[1])).T
        for p in range(2):
            o_ref[:, (2 * kh + p) * LANES:(2 * kh + p + 1) * LANES] = o[p * BLOCK:(p + 1) * BLOCK].astype(BF16)


def _attention(q, kd, vt, sinks, bias, bsz, n_blk):
    cur = lambda b, n: b * n_blk + n
    prev = lambda b, n: b * n_blk + jnp.maximum(n - 1, 0)
    meta = lambda b, n: b * n_blk
    meta_rows = lambda b, n: (b * n_blk + 1) * (BLOCK // N_META) - 1
    k_spec = lambda imap: pl.BlockSpec((BLOCK, 2 * D_KV), lambda b, n: (imap(b, n), 0))
    v_spec = lambda imap: pl.BlockSpec((2 * D_KV, BLOCK), lambda b, n: (0, imap(b, n)))
    return pl.pallas_call(
        _attn_kernel,
        grid=(bsz, n_blk),
        in_specs=[pl.BlockSpec(memory_space=pltpu.SMEM),
                  pl.BlockSpec((1, N_META + 2 * BLOCK, BLOCK), lambda b, n: (jnp.minimum(n, 2), 0, 0)),
                  pl.BlockSpec((BLOCK, D_ATTN), lambda b, n: (cur(b, n), 0)),
                  pl.BlockSpec((N_META, 2 * D_KV), lambda b, n: (meta_rows(b, n), 0)), k_spec(prev), k_spec(cur),
                  v_spec(meta), v_spec(prev), v_spec(cur)],
        out_specs=pl.BlockSpec((BLOCK, D_ATTN), lambda b, n: (cur(b, n), 0)),
        out_shape=jax.ShapeDtypeStruct(q.shape, BF16),
        compiler_params=_params(("parallel", "parallel")),
        name="attention",
    )(sinks, bias, q, kd, kd, kd, vt, vt, vt)


def _attn_bias():
    i = np.arange(BLOCK)[None, :]
    j = np.arange(BLOCK)[:, None]
    none = np.zeros((BLOCK, BLOCK), bool)
    causal = j <= i
    in_meta = np.broadcast_to(j >= PAD, (BLOCK, BLOCK))
    blk0 = [none[:N_META], none, causal & in_meta]
    blk1 = [none[:N_META], in_meta, causal]
    blk2 = [~none[:N_META], j > i, causal]
    ok = np.stack([np.concatenate(b, axis=0) for b in (blk0, blk1, blk2)])
    return jnp.asarray(np.where(ok, 0.0, NEG_INF), F32)


def _merge_kernel(rows, *refs):
    n = len(rows.specs)
    y_ref, a_ref, gs_ref, ga_ref, wglu_ref, bglu_ref, wos_ref, woa_ref, wout_ref, gain_ref, o_ref = refs[n:]
    z = jax.nn.gelu(y_ref[...])
    z = z * jax.nn.sigmoid(_dot(z.astype(BF16), wglu_ref[...]) + bglu_ref[...])
    merged = (jax.nn.sigmoid(gs_ref[...].astype(F32)) * _dot(z.astype(BF16), wos_ref[...])
              + jax.nn.sigmoid(ga_ref[...].astype(F32)) * _dot(a_ref[...], woa_ref[...]))
    mix = _dot(merged.astype(BF16), wout_ref[...])
    o_ref[...] = rows.load(refs[:n]) + _rms(mix, gain_ref[...])


def _merge(rows, layer, y, a, gs, ga, wglu, bglu, wos, woa, wout, gain, tm):
    n_rows = y.shape[0]
    row = lambda width: pl.BlockSpec((tm, width), lambda i: (i, 0))
    weights = (wglu, bglu, wos, woa, wout, gain)
    return pl.pallas_call(
        functools.partial(_merge_kernel, rows),
        grid=(n_rows // tm,),
        in_specs=rows.specs + [row(D_SSM), row(D_ATTN), row(D_MODEL), row(D_MODEL)]
                 + [_layer_spec(p, layer) for p in weights],
        out_specs=row(D_MODEL),
        out_shape=jax.ShapeDtypeStruct((n_rows, D_MODEL), F32),
        compiler_params=_params(("parallel",)),
        name="merge",
    )(*rows.operands, y, a, gs, ga, *weights)


def _ffn_kernel(rows, *refs):
    n = len(rows.specs)
    gpre_ref, wup_ref, wdown_ref, gpost_ref, o_ref = refs[n:]
    x = rows.load(refs[:n])
    h = _rms(x, gpre_ref[...]).astype(BF16)
    step = 512
    acc = jnp.zeros(x.shape, F32)
    for c in range(0, D_FF, step):
        a = jnp.maximum(_dot(h, wup_ref[:, c:c + step]), 0.0)
        acc = acc + _dot((a * a).astype(BF16), wdown_ref[c:c + step, :])
    o_ref[...] = x + _rms(acc, gpost_ref[...])


def _ffn(rows, n_rows, layer, gpre, wup, wdown, gpost, tm):
    weights = (gpre, wup, wdown, gpost)
    return pl.pallas_call(
        functools.partial(_ffn_kernel, rows),
        grid=(n_rows // tm,),
        in_specs=rows.specs + [_layer_spec(p, layer) for p in weights],
        out_specs=pl.BlockSpec((tm, D_MODEL), lambda i: (i, 0)),
        out_shape=jax.ShapeDtypeStruct((n_rows, D_MODEL), F32),
        compiler_params=_params(("parallel",)),
        name="ffn",
    )(*rows.operands, *weights)


def _rope_tables(n_blk):
    inv_freq = 1.0 / (ROPE_THETA ** (jnp.arange(0, HEAD_DIM, 2, dtype=F32) / HEAD_DIM))
    blk = (BLOCK * jnp.arange(n_blk, dtype=jnp.int32)).astype(F32)[:, None, None] * inv_freq
    off = (jnp.arange(BLOCK, dtype=jnp.int32) - PAD).astype(F32)[None, :, None] * inv_freq
    cos = jnp.cos(blk) * jnp.cos(off) - jnp.sin(blk) * jnp.sin(off)
    sin = jnp.sin(blk) * jnp.cos(off) + jnp.cos(blk) * jnp.sin(off)
    lanes = lambda m: jnp.tile(m.reshape(n_blk * BLOCK, HEAD_DIM // 2), (1, LANES // (HEAD_DIM // 2)))
    cos, sin = lanes(cos), lanes(sin)
    first_half = (np.arange(LANES) % HEAD_DIM) < HEAD_DIM // 2
    return cos, jnp.where(first_half, -sin, 0.0), jnp.where(first_half, 0.0, sin)


def _kv_weights(w_in):
    depth = w_in.shape[0]
    dup = lambda m: jnp.tile(m.reshape(depth, D_MODEL, N_KV_HEADS, 1, HEAD_DIM),
                             (1, 1, 1, 2, 1)).reshape(depth, D_MODEL, 2 * D_KV)
    wkd = dup(w_in[:, :, C_K:C_K + D_KV])
    wvt = dup(w_in[:, :, C_V:C_V + D_KV]).swapaxes(1, 2)
    return wkd.astype(BF16), wvt.astype(BF16)


def kernel(x, meta_tokens, norm_mix_pre, norm_mix_post, norm_mlp_pre, norm_mlp_post, w_in, ssm_a_re, ssm_a_im, ssm_log_dt, ssm_b_re, ssm_b_im, ssm_c_re, ssm_c_im, ssm_d, w_glu, b_glu, attn_sinks, w_o_ssm, w_o_attn, w_out, w_up, w_down):
    bsz, seq, _ = x.shape
    depth = w_in.shape[0]
    seq_pad = PAD + N_META + seq
    assert seq_pad % BLOCK == 0 and (seq_pad // CHUNK) % N_SEG == 0
    n_blk, n_chunks = seq_pad // BLOCK, seq_pad // CHUNK
    tm, tm_out = 640, 512
    assert seq_pad % tm == 0 and seq % tm_out == 0
    n_rows = bsz * seq_pad

    meta_block = jnp.concatenate([jnp.zeros((PAD, D_MODEL), x.dtype), meta_tokens.astype(x.dtype)], axis=0)
    stream = _Rows("input", x.reshape(bsz * seq, D_MODEL), tm, seq_pad // tm, seq // BLOCK, meta_block)
    cos, sa, sb = _rope_tables(n_blk)
    bias = _attn_bias()

    row = lambda m: m[:, None, :]
    wkd, wvt = _kv_weights(w_in)
    w_in, w_glu, w_o_ssm, w_o_attn, w_out, w_up, w_down = (
        m.astype(BF16) for m in (w_in, w_glu, w_o_ssm, w_o_attn, w_out, w_up, w_down))
    prep = _ssm_prep(ssm_a_re, ssm_a_im, ssm_log_dt, ssm_b_re, ssm_b_im, ssm_c_re, ssm_c_im, ssm_d, n_chunks // N_SEG)

    for l in range(depth):
        u, q, kd, vt, gs, ga = _inproj(stream, n_rows, l, row(norm_mix_pre), w_in, wkd, wvt, cos, sa, sb, tm)
        y = _ssm(u, prep, l, bsz, seq_pad)
        att = _attention(q, kd, vt, attn_sinks[l], bias, bsz, n_blk)
        hres = _merge(stream, l, y, att, gs, ga, w_glu, row(b_glu), w_o_ssm, w_o_attn, w_out, row(norm_mix_post), tm)
        ffn_w = (l, row(norm_mlp_pre), w_up, w_down, row(norm_mlp_post))
        if l + 1 < depth:
            stream = _Rows("padded", _ffn(_Rows("padded", hres, tm, seq_pad // tm, n_blk), n_rows, *ffn_w, tm),
                           tm, seq_pad // tm, n_blk)
        else:
            out = _ffn(_Rows("frames", hres, tm_out, seq // tm_out, n_blk), bsz * seq, *ffn_w, tm_out)
    return out.reshape(bsz, seq, D_MODEL)
```

```python
import functools

import jax
import jax.numpy as jnp
import numpy as np
from jax import lax
from jax.experimental import pallas as pl
from jax.experimental.pallas import tpu as pltpu

D_MODEL = 1024
N_META = 16
HEAD_DIM = 64
N_Q_HEADS = 16
N_KV_HEADS = 4
D_ATTN = N_Q_HEADS * HEAD_DIM
D_KV = N_KV_HEADS * HEAD_DIM
BLOCK = 128
ROPE_THETA = 10000.0
ATTN_SCALE = HEAD_DIM ** -0.5
NEG_INF = -1e30
D_SSM = D_MODEL // 2
SSM_GROUP = 16
N_SSM_GROUPS = D_SSM // SSM_GROUP
SSM_STATE = 64
D_FF = 4 * D_MODEL
RMS_EPS = 1e-6

LANES = 128
SUBLANES = 8
MXU = 256
PAD = BLOCK - N_META
CHUNK = 16
TILE_GROUPS = LANES // SSM_GROUP
N_SSM_TILES = D_SSM // LANES
TILE_STATE = TILE_GROUPS * SSM_STATE
CHUNK_W = CHUNK * LANES
HALF_GROUPS = TILE_GROUPS // 2
HALF_LANES = LANES // 2
HALF_STATE = HALF_GROUPS * SSM_STATE
HALF_W = CHUNK * HALF_LANES
SUB_BLOCKS = MXU // HALF_LANES
N_SEG = SUBLANES
VMEM_LIMIT = 56 * 1024 * 1024

C_U = 0
C_Q = C_U + D_SSM
C_K = C_Q + D_ATTN
C_V = C_K + D_KV
C_GS = C_V + D_KV
C_GA = C_GS + D_MODEL
C_END = C_GA + D_MODEL

BF16 = jnp.bfloat16
F32 = jnp.float32


def _dot(a, b):
    return jnp.dot(a, b, preferred_element_type=F32)


def _rms(x, gain):
    return x * lax.rsqrt(jnp.mean(x * x, axis=-1, keepdims=True) + RMS_EPS) * gain


def _const_spec(shape):
    return pl.BlockSpec(shape, lambda *_: (0,) * len(shape), pipeline_mode=pl.Buffered(1))


def _layer_spec(stacked, layer):
    shape = stacked.shape[1:]
    return pl.BlockSpec((None,) + shape, lambda *_: (layer,) + (0,) * len(shape), pipeline_mode=pl.Buffered(1))


def _params(sem):
    return pltpu.CompilerParams(dimension_semantics=sem, vmem_limit_bytes=VMEM_LIMIT)


class _Rows:
    def __init__(self, kind, array, tm, tiles_per_batch, blocks_per_batch, meta_block=None):
        self.has_meta = kind == "input"
        self.tiles_per_batch = tiles_per_batch
        nb = tm // BLOCK
        if kind == "padded":
            self.specs = [pl.BlockSpec((tm, D_MODEL), lambda i: (i, 0))]
            self.operands = [array]
            return
        shift = -1 if kind == "input" else 1

        def imap(j):
            return lambda i: ((i // tiles_per_batch) * blocks_per_batch
                              + jnp.maximum((i % tiles_per_batch) * nb + j + shift, 0), 0)

        self.specs = [pl.BlockSpec((BLOCK, D_MODEL), imap(j)) for j in range(nb)]
        self.operands = [array] * nb
        if self.has_meta:
            self.specs.append(_const_spec((BLOCK, D_MODEL)))
            self.operands.append(meta_block)

    def load(self, refs):
        n = len(self.specs)
        if n == 1:
            return refs[0][...]
        blocks = [r[...] for r in refs[:n - self.has_meta]]
        if self.has_meta:
            head = pl.program_id(0) % self.tiles_per_batch == 0
            blocks[0] = jnp.where(head, refs[n - 1][...], blocks[0])
        return jnp.concatenate(blocks, axis=0)


def _inproj_kernel(rows, *refs):
    n = len(rows.specs)
    gain_ref, w_ref, wkd_ref, wvt_ref, cos_ref, sa_ref, sb_ref, u_ref, q_ref, k_ref, vt_ref, gs_ref, ga_ref = refs[n:]
    h = _rms(rows.load(refs[:n]), gain_ref[...]).astype(BF16)
    cos, sa, sb = cos_ref[...], sa_ref[...], sb_ref[...]

    def rope(t):
        return t * cos + pltpu.roll(t, LANES - HEAD_DIM // 2, 1) * sa + pltpu.roll(t, HEAD_DIM // 2, 1) * sb

    step = 512
    u_ref[...] = _dot(h, w_ref[:, C_U:C_U + D_SSM])
    for c in range(0, D_ATTN, step):
        t = _dot(h, w_ref[:, C_Q + c:C_Q + c + step])
        for j in range(0, step, LANES):
            q_ref[:, c + j:c + j + LANES] = (rope(t[:, j:j + LANES]) * ATTN_SCALE).astype(BF16)
    t = _dot(h, wkd_ref[...])
    for j in range(0, 2 * D_KV, LANES):
        k_ref[:, j:j + LANES] = rope(t[:, j:j + LANES]).astype(BF16)
    vt_ref[...] = lax.dot_general(wvt_ref[...], h, (((1,), (1,)), ((), ())), preferred_element_type=F32).astype(BF16)
    for c in range(0, D_MODEL, step):
        gs_ref[:, c:c + step] = _dot(h, w_ref[:, C_GS + c:C_GS + c + step]).astype(BF16)
        ga_ref[:, c:c + step] = _dot(h, w_ref[:, C_GA + c:C_GA + c + step]).astype(BF16)


def _inproj(rows, n_rows, layer, gain, w, wkd, wvt, cos, sa, sb, tm):
    row = lambda width: pl.BlockSpec((tm, width), lambda i: (i, 0))
    tab = pl.BlockSpec((tm, LANES), lambda i: (i % rows.tiles_per_batch, 0))
    out = lambda width, dtype: (row(width), jax.ShapeDtypeStruct((n_rows, width), dtype))
    outs = [out(D_SSM, F32),
            out(D_ATTN, BF16), out(2 * D_KV, BF16),
            (pl.BlockSpec((2 * D_KV, tm), lambda i: (0, i)), jax.ShapeDtypeStruct((2 * D_KV, n_rows), BF16)),
            out(D_MODEL, BF16), out(D_MODEL, BF16)]
    return pl.pallas_call(
        functools.partial(_inproj_kernel, rows),
        grid=(n_rows // tm,),
        in_specs=rows.specs + [_layer_spec(p, layer) for p in (gain, w, wkd, wvt)] + [tab, tab, tab],
        out_specs=[o[0] for o in outs],
        out_shape=[o[1] for o in outs],
        compiler_params=_params(("parallel",)),
        name="inproj",
    )(*rows.operands, gain, w, wkd, wvt, cos, sa, sb)


def _ssm_spread(kc_ref, wc_ref, qc_ref, wt_s, pb_s, qm_s):
    def group_of(shape, axis, width):
        return (lax.broadcasted_iota(jnp.int32, shape, axis) // width) % HALF_GROUPS

    def copies(width, n):
        shape = (width, n * width)
        same = lax.broadcasted_iota(jnp.int32, shape, 1) % width == lax.broadcasted_iota(jnp.int32, shape, 0)
        return jnp.where(same, 1.0, 0.0).astype(BF16)

    shape = (CHUNK_W, LANES)
    diag = group_of(shape, 0, SSM_GROUP) == group_of(shape, 1, SSM_GROUP)
    taps = jnp.where(diag, _dot(kc_ref[0].astype(BF16), copies(SSM_GROUP, TILE_GROUPS)), 0.0).astype(BF16)
    even = lax.broadcasted_iota(jnp.int32, (HALF_LANES, LANES), 1) < HALF_LANES
    none = jnp.zeros((HALF_LANES, LANES), BF16)
    for h in range(2):
        block = lambda lag: taps[lag * LANES + h * HALF_LANES:lag * LANES + (h + 1) * HALF_LANES] if lag >= 0 else none
        for a in range(SUB_BLOCKS):
            for i in range(SUB_BLOCKS):
                for kk in range(SUB_BLOCKS // 2):
                    lag = SUB_BLOCKS * a + 2 * kk - i
                    wt_s[h, a, i * HALF_LANES:(i + 1) * HALF_LANES, kk * LANES:(kk + 1) * LANES] = (
                        jnp.where(even, block(lag), block(lag + 1)))

    rows = HALF_W // 2
    shape = (rows, HALF_STATE)
    diag = group_of(shape, 0, SSM_GROUP) == group_of(shape, 1, SSM_STATE)
    spread = copies(SSM_STATE, HALF_GROUPS)
    col_group = group_of((SSM_STATE, HALF_W), 1, SSM_GROUP)
    for part in range(2):
        for h in range(2):
            for r in range(0, HALF_W, rows):
                w = _dot(wc_ref[0, part, h, r:r + rows].astype(BF16), spread)
                pb_s[h, r:r + rows, part * HALF_STATE:(part + 1) * HALF_STATE] = jnp.where(diag, w, 0.0).astype(BF16)
            q = qc_ref[0, part, h]
            for g in range(HALF_GROUPS):
                r = part * HALF_STATE + g * SSM_STATE
                qm_s[h, r:r + SSM_STATE, :] = jnp.where(col_group == g, q, 0.0).astype(BF16)


def _ssm_kernel(u_ref, kc_ref, wc_ref, qc_ref, a_ref, pw_ref, y_ref, z_s, v_s, l_s, s_s, wt_s, pb_s, qm_s):
    n_chunks = z_s.shape[0]
    seg = n_chunks // N_SEG
    half_tiles = HALF_STATE // LANES
    part_tiles = lambda part: [h * 2 * half_tiles + part * half_tiles + c for h in range(2) for c in range(half_tiles)]

    @pl.when(pl.program_id(1) == 0)
    def _():
        _ssm_spread(kc_ref, wc_ref, qc_ref, wt_s, pb_s, qm_s)

    low = lax.broadcasted_iota(jnp.int32, (n_chunks, LANES), 1) < HALF_LANES

    def regroup(a, b):
        return jnp.where(low, a, pltpu.roll(b, HALF_LANES, 1)), jnp.where(low, pltpu.roll(a, HALF_LANES, 1), b)

    for m in range(CHUNK // 2):
        lo, hi = regroup(u_ref[pl.ds(2 * m, n_chunks, stride=CHUNK), :], u_ref[pl.ds(2 * m + 1, n_chunks, stride=CHUNK), :])
        z_s[:, m * LANES:(m + 1) * LANES] = lo.astype(BF16)
        z_s[:, HALF_W + m * LANES:HALF_W + (m + 1) * LANES] = hi.astype(BF16)
    for h in range(2):
        v = _dot(z_s[:, h * HALF_W:(h + 1) * HALF_W], pb_s[h])
        for c in range(2 * half_tiles):
            v_s[h * 2 * half_tiles + c] = v[:, c * LANES:(c + 1) * LANES]

    def load(ref, rows, part):
        return jnp.concatenate([ref[c, rows, :] for c in part_tiles(part)], axis=1)

    def store(ref, rows, part, val):
        for k, c in enumerate(part_tiles(part)):
            ref[c, rows, :] = val[:, k * LANES:(k + 1) * LANES]

    are, aim = a_ref[0, 0], a_ref[0, 1]

    def local_scan(i, carry):
        sre, sim = carry
        rows = pl.ds(i, N_SEG, stride=seg)
        store(l_s, rows, 0, sre)
        store(l_s, rows, 1, sim)
        return (are * sre - aim * sim + load(v_s, rows, 0), are * sim + aim * sre + load(v_s, rows, 1))

    zero = jnp.zeros((N_SEG, TILE_STATE), F32)
    ere, eim = lax.fori_loop(0, seg, local_scan, (zero, zero))

    bre, bim = a_ref[0, 2][:1], a_ref[0, 3][:1]
    tre, tim = [zero[:1]], [zero[:1]]
    for m in range(1, N_SEG):
        pre, pim = tre[-1], tim[-1]
        tre.append(bre * pre - bim * pim + ere[m - 1:m])
        tim.append(bre * pim + bim * pre + eim[m - 1:m])
    tre, tim = jnp.concatenate(tre, axis=0), jnp.concatenate(tim, axis=0)

    def add_carry(i, _):
        rows = pl.ds(i, N_SEG, stride=seg)
        pw = pw_ref[0, i]
        pre, pim = pw[0:1], pw[1:2]
        store(l_s, rows, 0, load(l_s, rows, 0) + (pre * tre - pim * tim))
        store(l_s, rows, 1, load(l_s, rows, 1) + (pre * tim + pim * tre))
        return 0

    lax.fori_loop(0, seg, add_carry, 0)

    for c in range(4 * half_tiles):
        s_s[:, c * LANES:(c + 1) * LANES] = l_s[c].astype(BF16)
    for tp in range(SUB_BLOCKS):
        acc = []
        for h in range(2):
            y = _dot(s_s[:, h * 2 * HALF_STATE:(h + 1) * 2 * HALF_STATE], qm_s[h, :, tp * MXU:(tp + 1) * MXU])
            for tq in range(tp + 1):
                y = y + _dot(z_s[:, h * HALF_W + tq * MXU:h * HALF_W + (tq + 1) * MXU], wt_s[h, tp - tq])
            acc.append(y)
        for kk in range(SUB_BLOCKS // 2):
            ya, yb = regroup(acc[0][:, kk * LANES:(kk + 1) * LANES], acc[1][:, kk * LANES:(kk + 1) * LANES])
            t = SUB_BLOCKS * tp + 2 * kk
            y_ref[pl.ds(t, n_chunks, stride=CHUNK), :] = ya
            y_ref[pl.ds(t + 1, n_chunks, stride=CHUNK), :] = yb


def _ssm(u, prep, layer, bsz, seq_pad):
    n_chunks = seq_pad // CHUNK
    tile = lambda shape: pl.BlockSpec((None, 1) + shape, lambda j, b: (layer, j) + (0,) * len(shape))
    io = pl.BlockSpec((seq_pad, LANES), lambda j, b: (b, j))
    return pl.pallas_call(
        _ssm_kernel,
        grid=(N_SSM_TILES, bsz),
        in_specs=[io] + [tile(p.shape[2:]) for p in prep],
        out_specs=io,
        out_shape=jax.ShapeDtypeStruct(u.shape, F32),
        scratch_shapes=[pltpu.VMEM((n_chunks, CHUNK_W), BF16),
                        pltpu.VMEM((2 * TILE_STATE // LANES, n_chunks, LANES), F32),
                        pltpu.VMEM((2 * TILE_STATE // LANES, n_chunks, LANES), F32),
                        pltpu.VMEM((n_chunks, 2 * TILE_STATE), BF16),
                        pltpu.VMEM((2, SUB_BLOCKS, MXU, MXU), BF16),
                        pltpu.VMEM((2, HALF_W, 2 * HALF_STATE), BF16),
                        pltpu.VMEM((2, 2 * HALF_STATE, HALF_W), BF16)],
        compiler_params=_params(("arbitrary", "arbitrary")),
        name="ssm",
    )(u, *prep)


def _ssm_prep(a_re, a_im, log_dt, b_re, b_im, c_re, c_im, d_skip, seg):
    hp = lax.Precision.HIGHEST
    N, C, T, J, E = SSM_STATE, SSM_GROUP, CHUNK, N_SSM_TILES, TILE_GROUPS
    depth = a_re.shape[0]
    tiles = lambda m: m.reshape((depth, J, E) + m.shape[2:])
    a_re, a_im, dt = tiles(a_re), tiles(a_im), tiles(jnp.exp(log_dt))[..., None]
    lam_re, lam_im = a_re * dt, a_im * dt

    def powers(p, lam_re, lam_im):
        p = jnp.asarray(p, F32).reshape((-1,) + (1,) * (lam_re.ndim - 2))
        mag = jnp.exp(lam_re[:, :, None] * p)
        return mag * jnp.cos(lam_im[:, :, None] * p), mag * jnp.sin(lam_im[:, :, None] * p)

    pw_re, pw_im = powers(np.arange(T + 1), lam_re, lam_im)
    den = a_re * a_re + a_im * a_im
    nre, nim = pw_re[:, :, 1] - 1.0, pw_im[:, :, 1]
    coef_re = ((nre * a_re + nim * a_im) / den)[..., None, :]
    coef_im = ((nim * a_re - nre * a_im) / den)[..., None, :]
    bt_re, bt_im = tiles(b_re).swapaxes(-1, -2), tiles(b_im).swapaxes(-1, -2)
    halves = lambda m, axis: m.reshape(m.shape[:axis] + (2, HALF_GROUPS) + m.shape[axis + 1:])
    bb_re = halves(coef_re * bt_re - coef_im * bt_im, 2)[:, :, None, :, None]
    bb_im = halves(coef_re * bt_im + coef_im * bt_re, 2)[:, :, None, :, None]

    rev = lambda m: halves(m[:, :, T - 1::-1], 3).swapaxes(2, 3)[..., None, :]
    r_re, r_im = rev(pw_re), rev(pw_im)
    w = jnp.stack([r_re, r_im], axis=2) * bb_re + jnp.stack([-r_im, r_re], axis=2) * bb_im
    wc = w.reshape(depth, J, 2, 2, HALF_W, N)

    cr, ci = halves(tiles(c_re), 2), halves(tiles(c_im), 2)
    taps = jnp.einsum('ljphsgcn,ljphgdn->ljshgcd', w, jnp.stack([cr, -ci], axis=2), precision=hp)[:, :, ::-1]
    taps = taps.reshape(depth, J, T, E, C, C).at[:, :, 0].add(tiles(d_skip)[..., None, :] * jnp.eye(C, dtype=F32))
    kc = taps.reshape(depth, J, T * LANES, C)

    ct_re, ct_im = cr.transpose(0, 1, 2, 5, 3, 4)[:, :, :, :, None], ci.transpose(0, 1, 2, 5, 3, 4)[:, :, :, :, None]
    fwd = lambda m: halves(m[:, :, 1:], 3).transpose(0, 1, 3, 5, 2, 4)[:, :, None, ..., None]
    pt_re, pt_im = fwd(pw_re), fwd(pw_im)
    qc = jnp.stack([ct_re, -ct_im], axis=2) * pt_re - jnp.stack([ct_im, ct_re], axis=2) * pt_im
    qc = qc.reshape(depth, J, 2, 2, N, HALF_W)

    flat = lambda m: m.reshape(depth, J, TILE_STATE)
    t_re, t_im = powers(T * np.array([1, seg]), flat(lam_re), flat(lam_im))
    a = jnp.stack([t_re[:, :, 0], t_im[:, :, 0], t_re[:, :, 1], t_im[:, :, 1]], axis=2)[:, :, :, None]
    a = jnp.broadcast_to(a, (depth, J, 4, SUBLANES, TILE_STATE))
    pw = jnp.stack(powers(T * np.arange(seg), flat(lam_re), flat(lam_im)), axis=3)
    return kc, wc, qc, a, pw


def _attn_kernel(sink_ref, bias_ref, q_ref, km_ref, kp_ref, kc_ref, vm_ref, vp_ref, vc_ref, o_ref):
    n_keys = N_META + 2 * BLOCK
    bias = bias_ref[0]
    first_k = lax.broadcasted_iota(jnp.int32, (n_keys, LANES), 1) < HEAD_DIM
    first_v = lax.broadcasted_iota(jnp.int32, (LANES, 3 * BLOCK), 0) < HEAD_DIM
    first_o = lax.broadcasted_iota(jnp.int32, (LANES, 2 * BLOCK), 0) < HEAD_DIM
    zero_k = jnp.zeros((n_keys, LANES), BF16)
    zero_v = jnp.zeros((LANES, 3 * BLOCK), BF16)
    pad_p = jnp.zeros((PAD, 2 * BLOCK), BF16)

    def scores(kh):
        cols = slice(kh * LANES, (kh + 1) * LANES)
        kd = jnp.concatenate([km_ref[:, cols], kp_ref[:, cols], kc_ref[:, cols]], axis=0)
        kab = jnp.concatenate([jnp.where(first_k, kd, zero_k), jnp.where(first_k, zero_k, kd)], axis=0)
        q4 = jnp.concatenate([q_ref[:, (2 * kh + p) * LANES:(2 * kh + p + 1) * LANES] for p in range(2)], axis=0)
        return lax.dot_general(kab, q4, (((1,), (1,)), ((), ())), preferred_element_type=F32)

    s_next = scores(0)
    for kh in range(N_KV_HEADS):
        s = s_next
        if kh + 1 < N_KV_HEADS:
            s_next = scores(kh + 1)
        pt, rden = [], []
        for e in range(2):
            blocks, rd = [], []
            for p in range(2):
                se = s[e * n_keys:(e + 1) * n_keys, p * LANES:(p + 1) * LANES] + bias
                sink = sink_ref[4 * kh + 2 * p + e]
                m = jnp.maximum(jnp.max(se, axis=0, keepdims=True), sink)
                pe = jnp.exp(se - m)
                den = jnp.sum(pe, axis=0, keepdims=True) + jnp.exp(sink - m)
                blocks.append(pe.astype(BF16))
                rd.append(1.0 / den)
            pt += [pad_p, jnp.concatenate(blocks, axis=1)]
            rden.append(jnp.concatenate(rd, axis=1))
        rows = slice(kh * LANES, (kh + 1) * LANES)
        vd = jnp.concatenate([vm_ref[rows, :], vp_ref[rows, :], vc_ref[rows, :]], axis=1)
        vab = jnp.concatenate([jnp.where(first_v, vd, zero_v), jnp.where(first_v, zero_v, vd)], axis=1)
        ot = _dot(vab, jnp.concatenate(pt, axis=0))
        o = (ot * jnp.where(first_o, rden[0], rden[1])).T
        for p in range(2):
            o_ref[:, (2 * kh + p) * LANES:(2 * kh + p + 1) * LANES] = o[p * BLOCK:(p + 1) * BLOCK].astype(BF16)


def _attention(q, kd, vt, sinks, bias, bsz, n_blk):
    cur = lambda b, n: b * n_blk + n
    prev = lambda b, n: b * n_blk + jnp.maximum(n - 1, 0)
    meta = lambda b, n: b * n_blk
    meta_rows = lambda b, n: (b * n_blk + 1) * (BLOCK // N_META) - 1
    k_spec = lambda imap: pl.BlockSpec((BLOCK, 2 * D_KV), lambda b, n: (imap(b, n), 0))
    v_spec = lambda imap: pl.BlockSpec((2 * D_KV, BLOCK), lambda b, n: (0, imap(b, n)))
    return pl.pallas_call(
        _attn_kernel,
        grid=(bsz, n_blk),
        in_specs=[pl.BlockSpec(memory_space=pltpu.SMEM),
                  pl.BlockSpec((1, N_META + 2 * BLOCK, BLOCK), lambda b, n: (jnp.minimum(n, 2), 0, 0)),
                  pl.BlockSpec((BLOCK, D_ATTN), lambda b, n: (cur(b, n), 0)),
                  pl.BlockSpec((N_META, 2 * D_KV), lambda b, n: (meta_rows(b, n), 0)), k_spec(prev), k_spec(cur),
                  v_spec(meta), v_spec(prev), v_spec(cur)],
        out_specs=pl.BlockSpec((BLOCK, D_ATTN), lambda b, n: (cur(b, n), 0)),
        out_shape=jax.ShapeDtypeStruct(q.shape, BF16),
        compiler_params=_params(("parallel", "parallel")),
        name="attention",
    )(sinks, bias, q, kd, kd, kd, vt, vt, vt)


def _attn_bias():
    i = np.arange(BLOCK)[None, :]
    j = np.arange(BLOCK)[:, None]
    none = np.zeros((BLOCK, BLOCK), bool)
    causal = j <= i
    in_meta = np.broadcast_to(j >= PAD, (BLOCK, BLOCK))
    blk0 = [none[:N_META], none, causal & in_meta]
    blk1 = [none[:N_META], in_meta, causal]
    blk2 = [~none[:N_META], j > i, causal]
    ok = np.stack([np.concatenate(b, axis=0) for b in (blk0, blk1, blk2)])
    return jnp.asarray(np.where(ok, 0.0, NEG_INF), F32)


def _merge_kernel(rows, *refs):
    n = len(rows.specs)
    y_ref, a_ref, gs_ref, ga_ref, wglu_ref, bglu_ref, wos_ref, woa_ref, wout_ref, gain_ref, o_ref = refs[n:]
    z = jax.nn.gelu(y_ref[...])
    z = z * jax.nn.sigmoid(_dot(z.astype(BF16), wglu_ref[...]) + bglu_ref[...])
    merged = (jax.nn.sigmoid(gs_ref[...].astype(F32)) * _dot(z.astype(BF16), wos_ref[...])
              + jax.nn.sigmoid(ga_ref[...].astype(F32)) * _dot(a_ref[...], woa_ref[...]))
    mix = _dot(merged.astype(BF16), wout_ref[...])
    o_ref[...] = rows.load(refs[:n]) + _rms(mix, gain_ref[...])


def _merge(rows, layer, y, a, gs, ga, wglu, bglu, wos, woa, wout, gain, tm):
    n_rows = y.shape[0]
    row = lambda width: pl.BlockSpec((tm, width), lambda i: (i, 0))
    weights = (wglu, bglu, wos, woa, wout, gain)
    return pl.pallas_call(
        functools.partial(_merge_kernel, rows),
        grid=(n_rows // tm,),
        in_specs=rows.specs + [row(D_SSM), row(D_ATTN), row(D_MODEL), row(D_MODEL)]
                 + [_layer_spec(p, layer) for p in weights],
        out_specs=row(D_MODEL),
        out_shape=jax.ShapeDtypeStruct((n_rows, D_MODEL), F32),
        compiler_params=_params(("parallel",)),
        name="merge",
    )(*rows.operands, y, a, gs, ga, *weights)


def _ffn_kernel(rows, *refs):
    n = len(rows.specs)
    gpre_ref, wup_ref, wdown_ref, gpost_ref, o_ref = refs[n:]
    x = rows.load(refs[:n])
    h = _rms(x, gpre_ref[...]).astype(BF16)
    step = 512
    acc = jnp.zeros(x.shape, F32)
    for c in range(0, D_FF, step):
        a = jnp.maximum(_dot(h, wup_ref[:, c:c + step]), 0.0)
        acc = acc + _dot((a * a).astype(BF16), wdown_ref[c:c + step, :])
    o_ref[...] = x + _rms(acc, gpost_ref[...])


def _ffn(rows, n_rows, layer, gpre, wup, wdown, gpost, tm):
    weights = (gpre, wup, wdown, gpost)
    return pl.pallas_call(
        functools.partial(_ffn_kernel, rows),
        grid=(n_rows // tm,),
        in_specs=rows.specs + [_layer_spec(p, layer) for p in weights],
        out_specs=pl.BlockSpec((tm, D_MODEL), lambda i: (i, 0)),
        out_shape=jax.ShapeDtypeStruct((n_rows, D_MODEL), F32),
        compiler_params=_params(("parallel",)),
        name="ffn",
    )(*rows.operands, *weights)


def _rope_tables(n_blk):
    inv_freq = 1.0 / (ROPE_THETA ** (jnp.arange(0, HEAD_DIM, 2, dtype=F32) / HEAD_DIM))
    blk = (BLOCK * jnp.arange(n_blk, dtype=jnp.int32)).astype(F32)[:, None, None] * inv_freq
    off = (jnp.arange(BLOCK, dtype=jnp.int32) - PAD).astype(F32)[None, :, None] * inv_freq
    cos = jnp.cos(blk) * jnp.cos(off) - jnp.sin(blk) * jnp.sin(off)
    sin = jnp.sin(blk) * jnp.cos(off) + jnp.cos(blk) * jnp.sin(off)
    lanes = lambda m: jnp.tile(m.reshape(n_blk * BLOCK, HEAD_DIM // 2), (1, LANES // (HEAD_DIM // 2)))
    cos, sin = lanes(cos), lanes(sin)
    first_half = (np.arange(LANES) % HEAD_DIM) < HEAD_DIM // 2
    return cos, jnp.where(first_half, -sin, 0.0), jnp.where(first_half, 0.0, sin)


def _kv_weights(w_in):
    depth = w_in.shape[0]
    dup = lambda m: jnp.tile(m.reshape(depth, D_MODEL, N_KV_HEADS, 1, HEAD_DIM),
                             (1, 1, 1, 2, 1)).reshape(depth, D_MODEL, 2 * D_KV)
    wkd = dup(w_in[:, :, C_K:C_K + D_KV])
    wvt = dup(w_in[:, :, C_V:C_V + D_KV]).swapaxes(1, 2)
    return wkd.astype(BF16), wvt.astype(BF16)


def kernel(x, meta_tokens, norm_mix_pre, norm_mix_post, norm_mlp_pre, norm_mlp_post, w_in, ssm_a_re, ssm_a_im, ssm_log_dt, ssm_b_re, ssm_b_im, ssm_c_re, ssm_c_im, ssm_d, w_glu, b_glu, attn_sinks, w_o_ssm, w_o_attn, w_out, w_up, w_down):
    bsz, seq, _ = x.shape
    depth = w_in.shape[0]
    seq_pad = PAD + N_META + seq
    assert seq_pad % BLOCK == 0 and (seq_pad // CHUNK) % N_SEG == 0
    n_blk, n_chunks = seq_pad // BLOCK, seq_pad // CHUNK
    tm, tm_out = 640, 512
    assert seq_pad % tm == 0 and seq % tm_out == 0
    n_rows = bsz * seq_pad

    meta_block = jnp.concatenate([jnp.zeros((PAD, D_MODEL), x.dtype), meta_tokens.astype(x.dtype)], axis=0)
    stream = _Rows("input", x.reshape(bsz * seq, D_MODEL), tm, seq_pad // tm, seq // BLOCK, meta_block)
    cos, sa, sb = _rope_tables(n_blk)
    bias = _attn_bias()

    row = lambda m: m[:, None, :]
    wkd, wvt = _kv_weights(w_in)
    w_in, w_glu, w_o_ssm, w_o_attn, w_out, w_up, w_down = (
        m.astype(BF16) for m in (w_in, w_glu, w_o_ssm, w_o_attn, w_out, w_up, w_down))
    prep = _ssm_prep(ssm_a_re, ssm_a_im, ssm_log_dt, ssm_b_re, ssm_b_im, ssm_c_re, ssm_c_im, ssm_d, n_chunks // N_SEG)

    for l in range(depth):
        u, q, kd, vt, gs, ga = _inproj(stream, n_rows, l, row(norm_mix_pre), w_in, wkd, wvt, cos, sa, sb, tm)
        y = _ssm(u, prep, l, bsz, seq_pad)
        att = _attention(q, kd, vt, attn_sinks[l], bias, bsz, n_blk)
        hres = _merge(stream, l, y, att, gs, ga, w_glu, row(b_glu), w_o_ssm, w_o_attn, w_out, row(norm_mix_post), tm)
        ffn_w = (l, row(norm_mlp_pre), w_up, w_down, row(norm_mlp_post))
        if l + 1 < depth:
            stream = _Rows("padded", _ffn(_Rows("padded", hres, tm, seq_pad // tm, n_blk), n_rows, *ffn_w, tm),
                           tm, seq_pad // tm, n_blk)
        else:
            out = _ffn(_Rows("frames", hres, tm_out, seq // tm_out, n_blk), bsz * seq, *ffn_w, tm_out)
    return out.reshape(bsz, seq, D_MODEL)
```

```python
import functools

import jax
import jax.numpy as jnp
import numpy as np
from jax import lax
from jax.experimental import pallas as pl
from jax.experimental.pallas import tpu as pltpu

D_MODEL = 1024
N_META = 16
HEAD_DIM = 64
N_Q_HEADS = 16
N_KV_HEADS = 4
D_ATTN = N_Q_HEADS * HEAD_DIM
D_KV = N_KV_HEADS * HEAD_DIM
BLOCK = 128
ROPE_THETA = 10000.0
ATTN_SCALE = HEAD_DIM ** -0.5
NEG_INF = -1e30
D_SSM = D_MODEL // 2
SSM_GROUP = 16
N_SSM_GROUPS = D_SSM // SSM_GROUP
SSM_STATE = 64
D_FF = 4 * D_MODEL
RMS_EPS = 1e-6

LANES = 128
SUBLANES = 8
MXU = 256
PAD = BLOCK - N_META
CHUNK = 16
TILE_GROUPS = LANES // SSM_GROUP
N_SSM_TILES = D_SSM // LANES
TILE_STATE = TILE_GROUPS * SSM_STATE
CHUNK_W = CHUNK * LANES
HALF_GROUPS = TILE_GROUPS // 2
HALF_LANES = LANES // 2
HALF_STATE = HALF_GROUPS * SSM_STATE
HALF_W = CHUNK * HALF_LANES
SUB_BLOCKS = MXU // HALF_LANES
N_SEG = SUBLANES
VMEM_LIMIT = 56 * 1024 * 1024

C_U = 0
C_Q = C_U + D_SSM
C_K = C_Q + D_ATTN
C_V = C_K + D_KV
C_GS = C_V + D_KV
C_GA = C_GS + D_MODEL
C_END = C_GA + D_MODEL

BF16 = jnp.bfloat16
F32 = jnp.float32


def _dot(a, b):
    return jnp.dot(a, b, preferred_element_type=F32)


def _rms(x, gain):
    return x * lax.rsqrt(jnp.mean(x * x, axis=-1, keepdims=True) + RMS_EPS) * gain


def _const_spec(shape):
    return pl.BlockSpec(shape, lambda *_: (0,) * len(shape), pipeline_mode=pl.Buffered(1))


def _layer_spec(stacked, layer):
    shape = stacked.shape[1:]
    return pl.BlockSpec((None,) + shape, lambda *_: (layer,) + (0,) * len(shape), pipeline_mode=pl.Buffered(1))


def _params(sem):
    return pltpu.CompilerParams(dimension_semantics=sem, vmem_limit_bytes=VMEM_LIMIT)


class _Rows:
    def __init__(self, kind, array, tm, tiles_per_batch, blocks_per_batch, meta_block=None):
        self.has_meta = kind == "input"
        self.tiles_per_batch = tiles_per_batch
        nb = tm // BLOCK
        if kind == "padded":
            self.specs = [pl.BlockSpec((tm, D_MODEL), lambda i: (i, 0))]
            self.operands = [array]
            return
        shift = -1 if kind == "input" else 1

        def imap(j):
            return lambda i: ((i // tiles_per_batch) * blocks_per_batch
                              + jnp.maximum((i % tiles_per_batch) * nb + j + shift, 0), 0)

        self.specs = [pl.BlockSpec((BLOCK, D_MODEL), imap(j)) for j in range(nb)]
        self.operands = [array] * nb
        if self.has_meta:
            self.specs.append(_const_spec((BLOCK, D_MODEL)))
            self.operands.append(meta_block)

    def load(self, refs):
        n = len(self.specs)
        if n == 1:
            return refs[0][...]
        blocks = [r[...] for r in refs[:n - self.has_meta]]
        if self.has_meta:
            head = pl.program_id(0) % self.tiles_per_batch == 0
            blocks[0] = jnp.where(head, refs[n - 1][...], blocks[0])
        return jnp.concatenate(blocks, axis=0)


def _inproj_kernel(rows, *refs):
    n = len(rows.specs)
    gain_ref, w_ref, wkd_ref, wvt_ref, cos_ref, sa_ref, sb_ref, u_ref, q_ref, k_ref, vt_ref, gs_ref, ga_ref = refs[n:]
    h = _rms(rows.load(refs[:n]), gain_ref[...]).astype(BF16)
    cos, sa, sb = cos_ref[...], sa_ref[...], sb_ref[...]

    def rope(t):
        return t * cos + pltpu.roll(t, LANES - HEAD_DIM // 2, 1) * sa + pltpu.roll(t, HEAD_DIM // 2, 1) * sb

    step = 512
    u_ref[...] = _dot(h, w_ref[:, C_U:C_U + D_SSM])
    for c in range(0, D_ATTN, step):
        t = _dot(h, w_ref[:, C_Q + c:C_Q + c + step])
        for j in range(0, step, LANES):
            q_ref[:, c + j:c + j + LANES] = (rope(t[:, j:j + LANES]) * ATTN_SCALE).astype(BF16)
    t = _dot(h, wkd_ref[...])
    for j in range(0, 2 * D_KV, LANES):
        k_ref[:, j:j + LANES] = rope(t[:, j:j + LANES]).astype(BF16)
    vt_ref[...] = lax.dot_general(wvt_ref[...], h, (((1,), (1,)), ((), ())), preferred_element_type=F32).astype(BF16)
    for c in range(0, D_MODEL, step):
        gs_ref[:, c:c + step] = _dot(h, w_ref[:, C_GS + c:C_GS + c + step]).astype(BF16)
        ga_ref[:, c:c + step] = _dot(h, w_ref[:, C_GA + c:C_GA + c + step]).astype(BF16)


def _inproj(rows, n_rows, layer, gain, w, wkd, wvt, cos, sa, sb, tm):
    row = lambda width: pl.BlockSpec((tm, width), lambda i: (i, 0))
    tab = pl.BlockSpec((tm, LANES), lambda i: (i % rows.tiles_per_batch, 0))
    out = lambda width, dtype: (row(width), jax.ShapeDtypeStruct((n_rows, width), dtype))
    outs = [out(D_SSM, F32),
            out(D_ATTN, BF16), out(2 * D_KV, BF16),
            (pl.BlockSpec((2 * D_KV, tm), lambda i: (0, i)), jax.ShapeDtypeStruct((2 * D_KV, n_rows), BF16)),
            out(D_MODEL, BF16), out(D_MODEL, BF16)]
    return pl.pallas_call(
        functools.partial(_inproj_kernel, rows),
        grid=(n_rows // tm,),
        in_specs=rows.specs + [_layer_spec(p, layer) for p in (gain, w, wkd, wvt)] + [tab, tab, tab],
        out_specs=[o[0] for o in outs],
        out_shape=[o[1] for o in outs],
        compiler_params=_params(("parallel",)),
        name="inproj",
    )(*rows.operands, gain, w, wkd, wvt, cos, sa, sb)


def _ssm_spread(rp_ref, bbt_ref, cc_ref, dl_ref, pf_ref, ct_ref, wt_s, pb_s, qm_s):
    T = CHUNK
    iota = lambda shape, axis: lax.broadcasted_iota(jnp.int32, shape, axis)
    group_of = lambda shape, axis, width: (iota(shape, axis) // width) % HALF_GROUPS
    one_hot = lambda hit: jnp.where(hit, 1.0, 0.0).astype(BF16)

    even = iota((HALF_LANES, LANES), 1) < HALF_LANES
    none = jnp.zeros((HALF_LANES, LANES), BF16)
    shape = (HALF_W, HALF_LANES)
    row_copy = one_hot(iota(shape, 0) // SSM_GROUP == iota(shape, 1))
    shape = (LANES, 2 * HALF_STATE)
    state_copy = one_hot(iota(shape, 1) % SSM_STATE + iota(shape, 1) // HALF_STATE * SSM_STATE == iota(shape, 0))
    shape = (LANES, 2 * HALF_W)
    lane_copy = one_hot(iota(shape, 1) // SSM_GROUP == iota(shape, 0))
    shape = (HALF_W, LANES)
    tap_diag = group_of(shape, 0, SSM_GROUP) == group_of(shape, 1, SSM_GROUP)
    tap_skip = tap_diag & (iota(shape, 0) // HALF_LANES == T - 1) & (iota(shape, 0) % SSM_GROUP == iota(shape, 1) % SSM_GROUP)
    shape = (HALF_W, 2 * HALF_STATE)
    pb_diag = group_of(shape, 0, SSM_GROUP) == group_of(shape, 1, SSM_STATE)
    col_group = group_of((SSM_STATE, HALF_W), 1, SSM_GROUP)

    for h in range(2):
        x = _dot(row_copy, rp_ref[0, h].astype(BF16))
        y = jnp.concatenate([bbt_ref[0, h]] * T, axis=0)
        w = (x[:, :LANES] * y[:, :LANES] + x[:, LANES:] * y[:, LANES:]).astype(BF16)
        pb_s[h] = jnp.where(pb_diag, _dot(w, state_copy), 0.0).astype(BF16)

        taps = _dot(w, cc_ref[0, h].astype(BF16)) + jnp.where(tap_skip, jnp.concatenate([dl_ref[0, h]] * (HALF_W // SUBLANES), axis=0), 0.0)
        taps = jnp.where(tap_diag, taps, 0.0).astype(BF16)
        block = lambda lag: taps[(T - 1 - lag) * HALF_LANES:(T - lag) * HALF_LANES] if lag >= 0 else none
        for a in range(SUB_BLOCKS):
            for i in range(SUB_BLOCKS):
                for kk in range(SUB_BLOCKS // 2):
                    lag = SUB_BLOCKS * a + 2 * kk - i
                    wt_s[h, a, i * HALF_LANES:(i + 1) * HALF_LANES, kk * LANES:(kk + 1) * LANES] = (
                        jnp.where(even, block(lag), block(lag + 1)))

        pw = _dot(pf_ref[0, h].astype(BF16), lane_copy)
        p_re, p_im = pw[:, :HALF_W], pw[:, HALF_W:]
        c_re = jnp.concatenate([ct_ref[0, h, 0]] * (HALF_W // LANES), axis=1)
        c_im = jnp.concatenate([ct_ref[0, h, 1]] * (HALF_W // LANES), axis=1)
        for part, q in enumerate((c_re * p_re - c_im * p_im, -(c_re * p_im + c_im * p_re))):
            for g in range(HALF_GROUPS):
                r = part * HALF_STATE + g * SSM_STATE
                qm_s[h, r:r + SSM_STATE, :] = jnp.where(col_group == g, q, 0.0).astype(BF16)


def _ssm_kernel(u_ref, rp_ref, bbt_ref, cc_ref, dl_ref, pf_ref, ct_ref, a_ref, pw_ref, y_ref,
                z_s, v_s, l_s, s_s, wt_s, pb_s, qm_s):
    n_chunks = z_s.shape[0]
    seg = n_chunks // N_SEG
    half_tiles = HALF_STATE // LANES
    part_tiles = lambda part: [h * 2 * half_tiles + part * half_tiles + c for h in range(2) for c in range(half_tiles)]

    @pl.when(pl.program_id(1) == 0)
    def _():
        _ssm_spread(rp_ref, bbt_ref, cc_ref, dl_ref, pf_ref, ct_ref, wt_s, pb_s, qm_s)

    low = lax.broadcasted_iota(jnp.int32, (n_chunks, LANES), 1) < HALF_LANES

    def regroup(a, b):
        return jnp.where(low, a, pltpu.roll(b, HALF_LANES, 1)), jnp.where(low, pltpu.roll(a, HALF_LANES, 1), b)

    for m in range(CHUNK // 2):
        lo, hi = regroup(u_ref[pl.ds(2 * m, n_chunks, stride=CHUNK), :], u_ref[pl.ds(2 * m + 1, n_chunks, stride=CHUNK), :])
        z_s[:, m * LANES:(m + 1) * LANES] = lo.astype(BF16)
        z_s[:, HALF_W + m * LANES:HALF_W + (m + 1) * LANES] = hi.astype(BF16)
    for h in range(2):
        v = _dot(z_s[:, h * HALF_W:(h + 1) * HALF_W], pb_s[h])
        for c in range(2 * half_tiles):
            v_s[h * 2 * half_tiles + c] = v[:, c * LANES:(c + 1) * LANES]

    def load(ref, rows, part):
        return jnp.concatenate([ref[c, rows, :] for c in part_tiles(part)], axis=1)

    def store(ref, rows, part, val):
        for k, c in enumerate(part_tiles(part)):
            ref[c, rows, :] = val[:, k * LANES:(k + 1) * LANES]

    are, aim = a_ref[0, 0], a_ref[0, 1]

    def local_scan(i, carry):
        sre, sim = carry
        rows = pl.ds(i, N_SEG, stride=seg)
        store(l_s, rows, 0, sre)
        store(l_s, rows, 1, sim)
        return (are * sre - aim * sim + load(v_s, rows, 0), are * sim + aim * sre + load(v_s, rows, 1))

    zero = jnp.zeros((N_SEG, TILE_STATE), F32)
    ere, eim = lax.fori_loop(0, seg, local_scan, (zero, zero))

    bre, bim = a_ref[0, 2][:1], a_ref[0, 3][:1]
    tre, tim = [zero[:1]], [zero[:1]]
    for m in range(1, N_SEG):
        pre, pim = tre[-1], tim[-1]
        tre.append(bre * pre - bim * pim + ere[m - 1:m])
        tim.append(bre * pim + bim * pre + eim[m - 1:m])
    tre, tim = jnp.concatenate(tre, axis=0), jnp.concatenate(tim, axis=0)

    def add_carry(i, _):
        rows = pl.ds(i, N_SEG, stride=seg)
        pre, pim = pw_ref[0, 0, pl.ds(i, 1), :], pw_ref[0, 1, pl.ds(i, 1), :]
        store(l_s, rows, 0, load(l_s, rows, 0) + (pre * tre - pim * tim))
        store(l_s, rows, 1, load(l_s, rows, 1) + (pre * tim + pim * tre))
        return 0

    lax.fori_loop(0, seg, add_carry, 0)

    for c in range(4 * half_tiles):
        s_s[:, c * LANES:(c + 1) * LANES] = l_s[c].astype(BF16)
    for tp in range(SUB_BLOCKS):
        acc = []
        for h in range(2):
            y = _dot(s_s[:, h * 2 * HALF_STATE:(h + 1) * 2 * HALF_STATE], qm_s[h, :, tp * MXU:(tp + 1) * MXU])
            for tq in range(tp + 1):
                y = y + _dot(z_s[:, h * HALF_W + tq * MXU:h * HALF_W + (tq + 1) * MXU], wt_s[h, tp - tq])
            acc.append(y)
        for kk in range(SUB_BLOCKS // 2):
            ya, yb = regroup(acc[0][:, kk * LANES:(kk + 1) * LANES], acc[1][:, kk * LANES:(kk + 1) * LANES])
            t = SUB_BLOCKS * tp + 2 * kk
            y_ref[pl.ds(t, n_chunks, stride=CHUNK), :] = ya
            y_ref[pl.ds(t + 1, n_chunks, stride=CHUNK), :] = yb


def _ssm(u, prep, layer, bsz, seq_pad):
    n_chunks = seq_pad // CHUNK
    tile = lambda shape: pl.BlockSpec((None, 1) + shape, lambda j, b: (layer, j) + (0,) * len(shape))
    io = pl.BlockSpec((seq_pad, LANES), lambda j, b: (b, j))
    return pl.pallas_call(
        _ssm_kernel,
        grid=(N_SSM_TILES, bsz),
        in_specs=[io] + [tile(p.shape[2:]) for p in prep],
        out_specs=io,
        out_shape=jax.ShapeDtypeStruct(u.shape, F32),
        scratch_shapes=[pltpu.VMEM((n_chunks, CHUNK_W), BF16),
                        pltpu.VMEM((2 * TILE_STATE // LANES, n_chunks, LANES), F32),
                        pltpu.VMEM((2 * TILE_STATE // LANES, n_chunks, LANES), F32),
                        pltpu.VMEM((n_chunks, 2 * TILE_STATE), BF16),
                        pltpu.VMEM((2, SUB_BLOCKS, MXU, MXU), BF16),
                        pltpu.VMEM((2, HALF_W, 2 * HALF_STATE), BF16),
                        pltpu.VMEM((2, 2 * HALF_STATE, HALF_W), BF16)],
        compiler_params=_params(("arbitrary", "arbitrary")),
        name="ssm",
    )(u, *prep)


def _ssm_prep(a_re, a_im, log_dt, b_re, b_im, c_re, c_im, d_skip, seg):
    N, C, T, J, E = SSM_STATE, SSM_GROUP, CHUNK, N_SSM_TILES, TILE_GROUPS
    depth = a_re.shape[0]
    tiles = lambda m: m.reshape((depth, J, E) + m.shape[2:])
    a_re, a_im, dt = tiles(a_re), tiles(a_im), tiles(jnp.exp(log_dt))[..., None]
    lam_re, lam_im = a_re * dt, a_im * dt

    def powers(p, lam_re, lam_im):
        p = jnp.asarray(p, F32).reshape((-1,) + (1,) * (lam_re.ndim - 2))
        mag = jnp.exp(lam_re[:, :, None] * p)
        return mag * jnp.cos(lam_im[:, :, None] * p), mag * jnp.sin(lam_im[:, :, None] * p)

    pw_re, pw_im = powers(np.arange(T + 1), lam_re, lam_im)
    den = a_re * a_re + a_im * a_im
    nre, nim = pw_re[:, :, 1] - 1.0, pw_im[:, :, 1]
    coef_re = ((nre * a_re + nim * a_im) / den)[..., None, :]
    coef_im = ((nim * a_re - nre * a_im) / den)[..., None, :]
    bt_re, bt_im = tiles(b_re).swapaxes(-1, -2), tiles(b_im).swapaxes(-1, -2)
    halves = lambda m, axis: m.reshape(m.shape[:axis] + (2, HALF_GROUPS) + m.shape[axis + 1:])
    slab = lambda m: m.reshape(m.shape[:3] + (-1, m.shape[-1]))
    cat = lambda *ms: jnp.concatenate(ms, axis=-1)
    bb_re = slab(halves(coef_re * bt_re - coef_im * bt_im, 2))
    bb_im = slab(halves(coef_re * bt_im + coef_im * bt_re, 2))
    bbt = cat(bb_re, bb_re, -bb_im, bb_im)

    rev = lambda m: slab(halves(m[:, :, T - 1::-1], 3).swapaxes(2, 3))
    r_re, r_im = rev(pw_re), rev(pw_im)
    rp = cat(r_re, r_im, r_im, r_re)

    cr, ci = halves(tiles(c_re), 2), halves(tiles(c_im), 2)
    by_state = lambda m: m.reshape(depth, J, 2, N, HALF_LANES)
    crt, cit = by_state(cr.transpose(0, 1, 2, 5, 3, 4)), by_state(ci.transpose(0, 1, 2, 5, 3, 4))
    cc = jnp.concatenate([cat(crt, crt), cat(-cit, -cit)], axis=-2)
    skip = halves(tiles(d_skip), 2).reshape(depth, J, 2, 1, HALF_LANES)
    dl = jnp.broadcast_to(cat(skip, skip), (depth, J, 2, SUBLANES, LANES))

    fwd = lambda m: by_state(halves(m[:, :, 1:], 3).transpose(0, 1, 3, 5, 2, 4))
    pf = cat(fwd(pw_re), fwd(pw_im))
    ct = jnp.stack([cat(crt, crt), cat(cit, cit)], axis=3)

    flat = lambda m: m.reshape(depth, J, TILE_STATE)
    t_re, t_im = powers(T * np.array([1, seg]), flat(lam_re), flat(lam_im))
    a = jnp.stack([t_re[:, :, 0], t_im[:, :, 0], t_re[:, :, 1], t_im[:, :, 1]], axis=2)[:, :, :, None]
    a = jnp.broadcast_to(a, (depth, J, 4, SUBLANES, TILE_STATE))
    pw = jnp.stack(powers(T * np.arange(seg), flat(lam_re), flat(lam_im)), axis=2)
    return rp, bbt, cc, dl, pf, ct, a, pw


def _attn_kernel(sink_ref, bias_ref, q_ref, km_ref, kp_ref, kc_ref, vm_ref, vp_ref, vc_ref, o_ref):
    n_keys = N_META + 2 * BLOCK
    bias = bias_ref[0]
    first_k = lax.broadcasted_iota(jnp.int32, (n_keys, LANES), 1) < HEAD_DIM
    first_v = lax.broadcasted_iota(jnp.int32, (LANES, 3 * BLOCK), 0) < HEAD_DIM
    first_o = lax.broadcasted_iota(jnp.int32, (LANES, 2 * BLOCK), 0) < HEAD_DIM
    zero_k = jnp.zeros((n_keys, LANES), BF16)
    zero_v = jnp.zeros((LANES, 3 * BLOCK), BF16)
    pad_p = jnp.zeros((PAD, 2 * BLOCK), BF16)

    def scores(kh):
        cols = slice(kh * LANES, (kh + 1) * LANES)
        kd = jnp.concatenate([km_ref[:, cols], kp_ref[:, cols], kc_ref[:, cols]], axis=0)
        kab = jnp.concatenate([jnp.where(first_k, kd, zero_k), jnp.where(first_k, zero_k, kd)], axis=0)
        q4 = jnp.concatenate([q_ref[:, (2 * kh + p) * LANES:(2 * kh + p + 1) * LANES] for p in range(2)], axis=0)
        return lax.dot_general(kab, q4, (((1,), (1,)), ((), ())), preferred_element_type=F32)

    s_next = scores(0)
    for kh in range(N_KV_HEADS):
        s = s_next
        if kh + 1 < N_KV_HEADS:
            s_next = scores(kh + 1)
        pt, rden = [], []
        for e in range(2):
            blocks, rd = [], []
            for p in range(2):
                se = s[e * n_keys:(e + 1) * n_keys, p * LANES:(p + 1) * LANES] + bias
                sink = sink_ref[4 * kh + 2 * p + e]
                m = jnp.maximum(jnp.max(se, axis=0, keepdims=True), sink)
                pe = jnp.exp(se - m)
                den = jnp.sum(pe, axis=0, keepdims=True) + jnp.exp(sink - m)
                blocks.append(pe.astype(BF16))
                rd.append(1.0 / den)
            pt += [pad_p, jnp.concatenate(blocks, axis=1)]
            rden.append(jnp.concatenate(rd, axis=1))
        rows = slice(kh * LANES, (kh + 1) * LANES)
        vd = jnp.concatenate([vm_ref[rows, :], vp_ref[rows, :], vc_ref[rows, :]], axis=1)
        vab = jnp.concatenate([jnp.where(first_v, vd, zero_v), jnp.where(first_v, zero_v, vd)], axis=1)
        ot = _dot(vab, jnp.concatenate(pt, axis=0))
        o = (ot * jnp.where(first_o, rden[0], rden[1])).T
        for p in range(2):
            o_ref[:, (2 * kh + p) * LANES:(2 * kh + p + 1) * LANES] = o[p * BLOCK:(p + 1) * BLOCK].astype(BF16)


def _attention(q, kd, vt, sinks, bias, bsz, n_blk):
    cur = lambda b, n: b * n_blk + n
    prev = lambda b, n: b * n_blk + jnp.maximum(n - 1, 0)
    meta = lambda b, n: b * n_blk
    meta_rows = lambda b, n: (b * n_blk + 1) * (BLOCK // N_META) - 1
    k_spec = lambda imap: pl.BlockSpec((BLOCK, 2 * D_KV), lambda b, n: (imap(b, n), 0))
    v_spec = lambda imap: pl.BlockSpec((2 * D_KV, BLOCK), lambda b, n: (0, imap(b, n)))
    return pl.pallas_call(
        _attn_kernel,
        grid=(bsz, n_blk),
        in_specs=[pl.BlockSpec(memory_space=pltpu.SMEM),
                  pl.BlockSpec((1, N_META + 2 * BLOCK, BLOCK), lambda b, n: (jnp.minimum(n, 2), 0, 0)),
                  pl.BlockSpec((BLOCK, D_ATTN), lambda b, n: (cur(b, n), 0)),
                  pl.BlockSpec((N_META, 2 * D_KV), lambda b, n: (meta_rows(b, n), 0)), k_spec(prev), k_spec(cur),
                  v_spec(meta), v_spec(prev), v_spec(cur)],
        out_specs=pl.BlockSpec((BLOCK, D_ATTN), lambda b, n: (cur(b, n), 0)),
        out_shape=jax.ShapeDtypeStruct(q.shape, BF16),
        compiler_params=_params(("parallel", "parallel")),
        name="attention",
    )(sinks, bias, q, kd, kd, kd, vt, vt, vt)


def _attn_bias():
    i = np.arange(BLOCK)[None, :]
    j = np.arange(BLOCK)[:, None]
    none = np.zeros((BLOCK, BLOCK), bool)
    causal = j <= i
    in_meta = np.broadcast_to(j >= PAD, (BLOCK, BLOCK))
    blk0 = [none[:N_META], none, causal & in_meta]
    blk1 = [none[:N_META], in_meta, causal]
    blk2 = [~none[:N_META], j > i, causal]
    ok = np.stack([np.concatenate(b, axis=0) for b in (blk0, blk1, blk2)])
    return jnp.asarray(np.where(ok, 0.0, NEG_INF), F32)


def _merge_kernel(rows, *refs):
    n = len(rows.specs)
    y_ref, a_ref, gs_ref, ga_ref, wglu_ref, bglu_ref, wos_ref, woa_ref, wout_ref, gain_ref, o_ref = refs[n:]
    z = jax.nn.gelu(y_ref[...])
    z = z * jax.nn.sigmoid(_dot(z.astype(BF16), wglu_ref[...]) + bglu_ref[...])
    merged = (jax.nn.sigmoid(gs_ref[...].astype(F32)) * _dot(z.astype(BF16), wos_ref[...])
              + jax.nn.sigmoid(ga_ref[...].astype(F32)) * _dot(a_ref[...], woa_ref[...]))
    mix = _dot(merged.astype(BF16), wout_ref[...])
    o_ref[...] = rows.load(refs[:n]) + _rms(mix, gain_ref[...])


def _merge(rows, layer, y, a, gs, ga, wglu, bglu, wos, woa, wout, gain, tm):
    n_rows = y.shape[0]
    row = lambda width: pl.BlockSpec((tm, width), lambda i: (i, 0))
    weights = (wglu, bglu, wos, woa, wout, gain)
    return pl.pallas_call(
        functools.partial(_merge_kernel, rows),
        grid=(n_rows // tm,),
        in_specs=rows.specs + [row(D_SSM), row(D_ATTN), row(D_MODEL), row(D_MODEL)]
                 + [_layer_spec(p, layer) for p in weights],
        out_specs=row(D_MODEL),
        out_shape=jax.ShapeDtypeStruct((n_rows, D_MODEL), F32),
        compiler_params=_params(("parallel",)),
        name="merge",
    )(*rows.operands, y, a, gs, ga, *weights)


def _ffn_kernel(rows, *refs):
    n = len(rows.specs)
    gpre_ref, wup_ref, wdown_ref, gpost_ref, o_ref = refs[n:]
    x = rows.load(refs[:n])
    h = _rms(x, gpre_ref[...]).astype(BF16)
    step = 512
    acc = jnp.zeros(x.shape, F32)
    for c in range(0, D_FF, step):
        a = jnp.maximum(_dot(h, wup_ref[:, c:c + step]), 0.0)
        acc = acc + _dot((a * a).astype(BF16), wdown_ref[c:c + step, :])
    o_ref[...] = x + _rms(acc, gpost_ref[...])


def _ffn(rows, n_rows, layer, gpre, wup, wdown, gpost, tm):
    weights = (gpre, wup, wdown, gpost)
    return pl.pallas_call(
        functools.partial(_ffn_kernel, rows),
        grid=(n_rows // tm,),
        in_specs=rows.specs + [_layer_spec(p, layer) for p in weights],
        out_specs=pl.BlockSpec((tm, D_MODEL), lambda i: (i, 0)),
        out_shape=jax.ShapeDtypeStruct((n_rows, D_MODEL), F32),
        compiler_params=_params(("parallel",)),
        name="ffn",
    )(*rows.operands, *weights)


def _rope_tables(n_blk):
    inv_freq = 1.0 / (ROPE_THETA ** (jnp.arange(0, HEAD_DIM, 2, dtype=F32) / HEAD_DIM))
    blk = (BLOCK * jnp.arange(n_blk, dtype=jnp.int32)).astype(F32)[:, None, None] * inv_freq
    off = (jnp.arange(BLOCK, dtype=jnp.int32) - PAD).astype(F32)[None, :, None] * inv_freq
    cos = jnp.cos(blk) * jnp.cos(off) - jnp.sin(blk) * jnp.sin(off)
    sin = jnp.sin(blk) * jnp.cos(off) + jnp.cos(blk) * jnp.sin(off)
    lanes = lambda m: jnp.tile(m.reshape(n_blk * BLOCK, HEAD_DIM // 2), (1, LANES // (HEAD_DIM // 2)))
    cos, sin = lanes(cos), lanes(sin)
    first_half = (np.arange(LANES) % HEAD_DIM) < HEAD_DIM // 2
    return cos, jnp.where(first_half, -sin, 0.0), jnp.where(first_half, 0.0, sin)


def _kv_weights(w_in):
    depth = w_in.shape[0]
    dup = lambda m: jnp.tile(m.reshape(depth, D_MODEL, N_KV_HEADS, 1, HEAD_DIM),
                             (1, 1, 1, 2, 1)).reshape(depth, D_MODEL, 2 * D_KV)
    wkd = dup(w_in[:, :, C_K:C_K + D_KV])
    wvt = dup(w_in[:, :, C_V:C_V + D_KV]).swapaxes(1, 2)
    return wkd.astype(BF16), wvt.astype(BF16)


def kernel(x, meta_tokens, norm_mix_pre, norm_mix_post, norm_mlp_pre, norm_mlp_post, w_in, ssm_a_re, ssm_a_im, ssm_log_dt, ssm_b_re, ssm_b_im, ssm_c_re, ssm_c_im, ssm_d, w_glu, b_glu, attn_sinks, w_o_ssm, w_o_attn, w_out, w_up, w_down):
    bsz, seq, _ = x.shape
    depth = w_in.shape[0]
    seq_pad = PAD + N_META + seq
    assert seq_pad % BLOCK == 0 and (seq_pad // CHUNK) % N_SEG == 0
    n_blk, n_chunks = seq_pad // BLOCK, seq_pad // CHUNK
    tm, tm_out = 640, 512
    assert seq_pad % tm == 0 and seq % tm_out == 0
    n_rows = bsz * seq_pad

    meta_block = jnp.concatenate([jnp.zeros((PAD, D_MODEL), x.dtype), meta_tokens.astype(x.dtype)], axis=0)
    stream = _Rows("input", x.reshape(bsz * seq, D_MODEL), tm, seq_pad // tm, seq // BLOCK, meta_block)
    cos, sa, sb = _rope_tables(n_blk)
    bias = _attn_bias()

    row = lambda m: m[:, None, :]
    wkd, wvt = _kv_weights(w_in)
    w_in, w_glu, w_o_ssm, w_o_attn, w_out, w_up, w_down = (
        m.astype(BF16) for m in (w_in, w_glu, w_o_ssm, w_o_attn, w_out, w_up, w_down))
    prep = _ssm_prep(ssm_a_re, ssm_a_im, ssm_log_dt, ssm_b_re, ssm_b_im, ssm_c_re, ssm_c_im, ssm_d, n_chunks // N_SEG)

    for l in range(depth):
        u, q, kd, vt, gs, ga = _inproj(stream, n_rows, l, row(norm_mix_pre), w_in, wkd, wvt, cos, sa, sb, tm)
        y = _ssm(u, prep, l, bsz, seq_pad)
        att = _attention(q, kd, vt, attn_sinks[l], bias, bsz, n_blk)
        hres = _merge(stream, l, y, att, gs, ga, w_glu, row(b_glu), w_o_ssm, w_o_attn, w_out, row(norm_mix_post), tm)
        ffn_w = (l, row(norm_mlp_pre), w_up, w_down, row(norm_mlp_post))
        if l + 1 < depth:
            stream = _Rows("padded", _ffn(_Rows("padded", hres, tm, seq_pad // tm, n_blk), n_rows, *ffn_w, tm),
                           tm, seq_pad // tm, n_blk)
        else:
            out = _ffn(_Rows("frames", hres, tm_out, seq // tm_out, n_blk), bsz * seq, *ffn_w, tm_out)
    return out.reshape(bsz, seq, D_MODEL)
```

```python
import functools

import jax
import jax.numpy as jnp
import numpy as np
from jax import lax
from jax.experimental import pallas as pl
from jax.experimental.pallas import tpu as pltpu

D_MODEL = 1024
N_META = 16
HEAD_DIM = 64
N_Q_HEADS = 16
N_KV_HEADS = 4
D_ATTN = N_Q_HEADS * HEAD_DIM
D_KV = N_KV_HEADS * HEAD_DIM
BLOCK = 128
ROPE_THETA = 10000.0
ATTN_SCALE = HEAD_DIM ** -0.5
NEG_INF = -1e30
D_SSM = D_MODEL // 2
SSM_GROUP = 16
N_SSM_GROUPS = D_SSM // SSM_GROUP
SSM_STATE = 64
D_FF = 4 * D_MODEL
RMS_EPS = 1e-6

LANES = 128
SUBLANES = 8
MXU = 256
PAD = BLOCK - N_META
CHUNK = 16
TILE_GROUPS = LANES // SSM_GROUP
N_SSM_TILES = D_SSM // LANES
TILE_STATE = TILE_GROUPS * SSM_STATE
CHUNK_W = CHUNK * LANES
HALF_GROUPS = TILE_GROUPS // 2
HALF_LANES = LANES // 2
HALF_STATE = HALF_GROUPS * SSM_STATE
HALF_W = CHUNK * HALF_LANES
SUB_BLOCKS = MXU // HALF_LANES
N_SEG = SUBLANES
VMEM_LIMIT = 56 * 1024 * 1024

C_U = 0
C_Q = C_U + D_SSM
C_K = C_Q + D_ATTN
C_V = C_K + D_KV
C_GS = C_V + D_KV
C_GA = C_GS + D_MODEL
C_END = C_GA + D_MODEL

BF16 = jnp.bfloat16
F32 = jnp.float32


def _dot(a, b):
    return jnp.dot(a, b, preferred_element_type=F32)


def _rms(x, gain):
    return x * lax.rsqrt(jnp.mean(x * x, axis=-1, keepdims=True) + RMS_EPS) * gain


def _const_spec(shape):
    return pl.BlockSpec(shape, lambda *_: (0,) * len(shape), pipeline_mode=pl.Buffered(1))


def _layer_spec(stacked, layer):
    shape = stacked.shape[1:]
    return pl.BlockSpec((None,) + shape, lambda *_: (layer,) + (0,) * len(shape), pipeline_mode=pl.Buffered(1))


def _params(sem):
    return pltpu.CompilerParams(dimension_semantics=sem, vmem_limit_bytes=VMEM_LIMIT)


class _Rows:
    def __init__(self, kind, array, tm, tiles_per_batch, blocks_per_batch, meta_block=None):
        self.has_meta = kind == "input"
        self.tiles_per_batch = tiles_per_batch
        nb = tm // BLOCK
        if kind == "padded":
            self.specs = [pl.BlockSpec((tm, D_MODEL), lambda i: (i, 0))]
            self.operands = [array]
            return
        shift = -1 if kind == "input" else 1

        def imap(j):
            return lambda i: ((i // tiles_per_batch) * blocks_per_batch
                              + jnp.maximum((i % tiles_per_batch) * nb + j + shift, 0), 0)

        self.specs = [pl.BlockSpec((BLOCK, D_MODEL), imap(j)) for j in range(nb)]
        self.operands = [array] * nb
        if self.has_meta:
            self.specs.append(_const_spec((BLOCK, D_MODEL)))
            self.operands.append(meta_block)

    def load(self, refs):
        n = len(self.specs)
        if n == 1:
            return refs[0][...]
        blocks = [r[...] for r in refs[:n - self.has_meta]]
        if self.has_meta:
            head = pl.program_id(0) % self.tiles_per_batch == 0
            blocks[0] = jnp.where(head, refs[n - 1][...], blocks[0])
        return jnp.concatenate(blocks, axis=0)


def _inproj_kernel(rows, *refs):
    n = len(rows.specs)
    gain_ref, w_ref, wkd_ref, wvt_ref, cos_ref, sa_ref, sb_ref, u_ref, q_ref, k_ref, vt_ref, gs_ref, ga_ref = refs[n:]
    h = _rms(rows.load(refs[:n]), gain_ref[...]).astype(BF16)
    cos, sa, sb = cos_ref[...], sa_ref[...], sb_ref[...]

    def rope(t):
        return t * cos + pltpu.roll(t, LANES - HEAD_DIM // 2, 1) * sa + pltpu.roll(t, HEAD_DIM // 2, 1) * sb

    step = 512
    u_ref[...] = _dot(h, w_ref[:, C_U:C_U + D_SSM])
    for c in range(0, D_ATTN, step):
        t = _dot(h, w_ref[:, C_Q + c:C_Q + c + step])
        for j in range(0, step, LANES):
            q_ref[:, c + j:c + j + LANES] = (rope(t[:, j:j + LANES]) * ATTN_SCALE).astype(BF16)
    t = _dot(h, wkd_ref[...])
    for j in range(0, 2 * D_KV, LANES):
        kj = rope(t[:, j:j + LANES]).astype(BF16)
        for b in range(k_ref.shape[0]):
            k_ref[b, :, j:j + LANES] = kj[b * BLOCK:(b + 1) * BLOCK]
    vt = lax.dot_general(wvt_ref[...], h, (((1,), (1,)), ((), ())), preferred_element_type=F32).astype(BF16)
    for b in range(vt_ref.shape[0]):
        vt_ref[b] = vt[:, b * BLOCK:(b + 1) * BLOCK]
    for c in range(0, D_MODEL, step):
        gs_ref[:, c:c + step] = _dot(h, w_ref[:, C_GS + c:C_GS + c + step]).astype(BF16)
        ga_ref[:, c:c + step] = _dot(h, w_ref[:, C_GA + c:C_GA + c + step]).astype(BF16)


def _inproj(rows, n_rows, layer, gain, w, wkd, wvt, cos, sa, sb, tm):
    row = lambda width: pl.BlockSpec((tm, width), lambda i: (i, 0))
    tab = pl.BlockSpec((tm, LANES), lambda i: (i % rows.tiles_per_batch, 0))
    out = lambda width, dtype: (row(width), jax.ShapeDtypeStruct((n_rows, width), dtype))
    per_block = lambda shape: (pl.BlockSpec((tm // BLOCK,) + shape, lambda i: (i, 0, 0)),
                               jax.ShapeDtypeStruct((n_rows // BLOCK,) + shape, BF16))
    outs = [out(D_SSM, F32),
            out(D_ATTN, BF16), per_block((BLOCK, 2 * D_KV)), per_block((2 * D_KV, BLOCK)),
            out(D_MODEL, BF16), out(D_MODEL, BF16)]
    return pl.pallas_call(
        functools.partial(_inproj_kernel, rows),
        grid=(n_rows // tm,),
        in_specs=rows.specs + [_layer_spec(p, layer) for p in (gain, w, wkd, wvt)] + [tab, tab, tab],
        out_specs=[o[0] for o in outs],
        out_shape=[o[1] for o in outs],
        compiler_params=_params(("parallel",)),
        name="inproj",
    )(*rows.operands, gain, w, wkd, wvt, cos, sa, sb)


def _ssm_spread(rp_ref, bbt_ref, cc_ref, dl_ref, pf_ref, ct_ref, wt_s, pb_s, qm_s):
    T = CHUNK
    iota = lambda shape, axis: lax.broadcasted_iota(jnp.int32, shape, axis)
    group_of = lambda shape, axis, width: (iota(shape, axis) // width) % HALF_GROUPS
    one_hot = lambda hit: jnp.where(hit, 1.0, 0.0).astype(BF16)

    even = iota((HALF_LANES, LANES), 1) < HALF_LANES
    none = jnp.zeros((HALF_LANES, LANES), BF16)
    shape = (HALF_W, HALF_LANES)
    row_copy = one_hot(iota(shape, 0) // SSM_GROUP == iota(shape, 1))
    shape = (LANES, 2 * HALF_STATE)
    state_copy = one_hot(iota(shape, 1) % SSM_STATE + iota(shape, 1) // HALF_STATE * SSM_STATE == iota(shape, 0))
    shape = (LANES, 2 * HALF_W)
    lane_copy = one_hot(iota(shape, 1) // SSM_GROUP == iota(shape, 0))
    shape = (HALF_W, LANES)
    tap_diag = group_of(shape, 0, SSM_GROUP) == group_of(shape, 1, SSM_GROUP)
    tap_skip = tap_diag & (iota(shape, 0) // HALF_LANES == T - 1) & (iota(shape, 0) % SSM_GROUP == iota(shape, 1) % SSM_GROUP)
    shape = (HALF_W, 2 * HALF_STATE)
    pb_diag = group_of(shape, 0, SSM_GROUP) == group_of(shape, 1, SSM_STATE)
    col_group = group_of((SSM_STATE, HALF_W), 1, SSM_GROUP)

    for h in range(2):
        x = _dot(row_copy, rp_ref[0, h].astype(BF16))
        y = jnp.concatenate([bbt_ref[0, h]] * T, axis=0)
        w = (x[:, :LANES] * y[:, :LANES] + x[:, LANES:] * y[:, LANES:]).astype(BF16)
        pb_s[h] = jnp.where(pb_diag, _dot(w, state_copy), 0.0).astype(BF16)

        taps = _dot(w, cc_ref[0, h].astype(BF16)) + jnp.where(tap_skip, jnp.concatenate([dl_ref[0, h]] * (HALF_W // SUBLANES), axis=0), 0.0)
        taps = jnp.where(tap_diag, taps, 0.0).astype(BF16)
        block = lambda lag: taps[(T - 1 - lag) * HALF_LANES:(T - lag) * HALF_LANES] if lag >= 0 else none
        for a in range(SUB_BLOCKS):
            for i in range(SUB_BLOCKS):
                for kk in range(SUB_BLOCKS // 2):
                    lag = SUB_BLOCKS * a + 2 * kk - i
                    wt_s[h, a, i * HALF_LANES:(i + 1) * HALF_LANES, kk * LANES:(kk + 1) * LANES] = (
                        jnp.where(even, block(lag), block(lag + 1)))

        pw = _dot(pf_ref[0, h].astype(BF16), lane_copy)
        p_re, p_im = pw[:, :HALF_W], pw[:, HALF_W:]
        c_re = jnp.concatenate([ct_ref[0, h, 0]] * (HALF_W // LANES), axis=1)
        c_im = jnp.concatenate([ct_ref[0, h, 1]] * (HALF_W // LANES), axis=1)
        for part, q in enumerate((c_re * p_re - c_im * p_im, -(c_re * p_im + c_im * p_re))):
            for g in range(HALF_GROUPS):
                r = part * HALF_STATE + g * SSM_STATE
                qm_s[h, r:r + SSM_STATE, :] = jnp.where(col_group == g, q, 0.0).astype(BF16)


def _ssm_kernel(u_ref, rp_ref, bbt_ref, cc_ref, dl_ref, pf_ref, ct_ref, a_ref, pw_ref, y_ref,
                z_s, v_s, l_s, s_s, wt_s, pb_s, qm_s):
    n_chunks = z_s.shape[0]
    seg = n_chunks // N_SEG
    half_tiles = HALF_STATE // LANES
    part_tiles = lambda part: [h * 2 * half_tiles + part * half_tiles + c for h in range(2) for c in range(half_tiles)]

    @pl.when(pl.program_id(1) == 0)
    def _():
        _ssm_spread(rp_ref, bbt_ref, cc_ref, dl_ref, pf_ref, ct_ref, wt_s, pb_s, qm_s)

    low = lax.broadcasted_iota(jnp.int32, (n_chunks, LANES), 1) < HALF_LANES

    def regroup(a, b):
        return jnp.where(low, a, pltpu.roll(b, HALF_LANES, 1)), jnp.where(low, pltpu.roll(a, HALF_LANES, 1), b)

    for m in range(CHUNK // 2):
        lo, hi = regroup(u_ref[pl.ds(2 * m, n_chunks, stride=CHUNK), :], u_ref[pl.ds(2 * m + 1, n_chunks, stride=CHUNK), :])
        z_s[:, m * LANES:(m + 1) * LANES] = lo.astype(BF16)
        z_s[:, HALF_W + m * LANES:HALF_W + (m + 1) * LANES] = hi.astype(BF16)
    for h in range(2):
        v = _dot(z_s[:, h * HALF_W:(h + 1) * HALF_W], pb_s[h])
        for c in range(2 * half_tiles):
            v_s[h * 2 * half_tiles + c] = v[:, c * LANES:(c + 1) * LANES]

    def load(ref, rows, part):
        return jnp.concatenate([ref[c, rows, :] for c in part_tiles(part)], axis=1)

    def store(ref, rows, part, val):
        for k, c in enumerate(part_tiles(part)):
            ref[c, rows, :] = val[:, k * LANES:(k + 1) * LANES]

    are, aim = a_ref[0, 0], a_ref[0, 1]

    def local_scan(i, carry):
        sre, sim = carry
        rows = pl.ds(i, N_SEG, stride=seg)
        store(l_s, rows, 0, sre)
        store(l_s, rows, 1, sim)
        return (are * sre - aim * sim + load(v_s, rows, 0), are * sim + aim * sre + load(v_s, rows, 1))

    zero = jnp.zeros((N_SEG, TILE_STATE), F32)
    ere, eim = lax.fori_loop(0, seg, local_scan, (zero, zero))

    bre, bim = a_ref[0, 2][:1], a_ref[0, 3][:1]
    tre, tim = [zero[:1]], [zero[:1]]
    for m in range(1, N_SEG):
        pre, pim = tre[-1], tim[-1]
        tre.append(bre * pre - bim * pim + ere[m - 1:m])
        tim.append(bre * pim + bim * pre + eim[m - 1:m])
    tre, tim = jnp.concatenate(tre, axis=0), jnp.concatenate(tim, axis=0)

    def add_carry(i, _):
        rows = pl.ds(i, N_SEG, stride=seg)
        pre, pim = pw_ref[0, 0, pl.ds(i, 1), :], pw_ref[0, 1, pl.ds(i, 1), :]
        store(l_s, rows, 0, load(l_s, rows, 0) + (pre * tre - pim * tim))
        store(l_s, rows, 1, load(l_s, rows, 1) + (pre * tim + pim * tre))
        return 0

    lax.fori_loop(0, seg, add_carry, 0)

    for c in range(4 * half_tiles):
        s_s[:, c * LANES:(c + 1) * LANES] = l_s[c].astype(BF16)
    for tp in range(SUB_BLOCKS):
        acc = []
        for h in range(2):
            y = _dot(s_s[:, h * 2 * HALF_STATE:(h + 1) * 2 * HALF_STATE], qm_s[h, :, tp * MXU:(tp + 1) * MXU])
            for tq in range(tp + 1):
                y = y + _dot(z_s[:, h * HALF_W + tq * MXU:h * HALF_W + (tq + 1) * MXU], wt_s[h, tp - tq])
            acc.append(y)
        for kk in range(SUB_BLOCKS // 2):
            ya, yb = regroup(acc[0][:, kk * LANES:(kk + 1) * LANES], acc[1][:, kk * LANES:(kk + 1) * LANES])
            t = SUB_BLOCKS * tp + 2 * kk
            y_ref[pl.ds(t, n_chunks, stride=CHUNK), :] = ya
            y_ref[pl.ds(t + 1, n_chunks, stride=CHUNK), :] = yb


def _ssm(u, prep, layer, bsz, seq_pad):
    n_chunks = seq_pad // CHUNK
    tile = lambda shape: pl.BlockSpec((None, 1) + shape, lambda j, b: (layer, j) + (0,) * len(shape))
    io = pl.BlockSpec((seq_pad, LANES), lambda j, b: (b, j))
    return pl.pallas_call(
        _ssm_kernel,
        grid=(N_SSM_TILES, bsz),
        in_specs=[io] + [tile(p.shape[2:]) for p in prep],
        out_specs=io,
        out_shape=jax.ShapeDtypeStruct(u.shape, F32),
        scratch_shapes=[pltpu.VMEM((n_chunks, CHUNK_W), BF16),
                        pltpu.VMEM((2 * TILE_STATE // LANES, n_chunks, LANES), F32),
                        pltpu.VMEM((2 * TILE_STATE // LANES, n_chunks, LANES), F32),
                        pltpu.VMEM((n_chunks, 2 * TILE_STATE), BF16),
                        pltpu.VMEM((2, SUB_BLOCKS, MXU, MXU), BF16),
                        pltpu.VMEM((2, HALF_W, 2 * HALF_STATE), BF16),
                        pltpu.VMEM((2, 2 * HALF_STATE, HALF_W), BF16)],
        compiler_params=_params(("arbitrary", "arbitrary")),
        name="ssm",
    )(u, *prep)


def _ssm_prep(a_re, a_im, log_dt, b_re, b_im, c_re, c_im, d_skip, seg):
    N, C, T, J, E = SSM_STATE, SSM_GROUP, CHUNK, N_SSM_TILES, TILE_GROUPS
    depth = a_re.shape[0]
    tiles = lambda m: m.reshape((depth, J, E) + m.shape[2:])
    a_re, a_im, dt = tiles(a_re), tiles(a_im), tiles(jnp.exp(log_dt))[..., None]
    lam_re, lam_im = a_re * dt, a_im * dt

    def powers(p, lam_re, lam_im):
        p = jnp.asarray(p, F32).reshape((-1,) + (1,) * (lam_re.ndim - 2))
        mag = jnp.exp(lam_re[:, :, None] * p)
        return mag * jnp.cos(lam_im[:, :, None] * p), mag * jnp.sin(lam_im[:, :, None] * p)

    pw_re, pw_im = powers(np.arange(T + 1), lam_re, lam_im)
    den = a_re * a_re + a_im * a_im
    nre, nim = pw_re[:, :, 1] - 1.0, pw_im[:, :, 1]
    coef_re = ((nre * a_re + nim * a_im) / den)[..., None, :]
    coef_im = ((nim * a_re - nre * a_im) / den)[..., None, :]
    bt_re, bt_im = tiles(b_re).swapaxes(-1, -2), tiles(b_im).swapaxes(-1, -2)
    halves = lambda m, axis: m.reshape(m.shape[:axis] + (2, HALF_GROUPS) + m.shape[axis + 1:])
    slab = lambda m: m.reshape(m.shape[:3] + (-1, m.shape[-1]))
    cat = lambda *ms: jnp.concatenate(ms, axis=-1)
    bb_re = slab(halves(coef_re * bt_re - coef_im * bt_im, 2))
    bb_im = slab(halves(coef_re * bt_im + coef_im * bt_re, 2))
    bbt = cat(bb_re, bb_re, -bb_im, bb_im)

    rev = lambda m: slab(halves(m[:, :, T - 1::-1], 3).swapaxes(2, 3))
    r_re, r_im = rev(pw_re), rev(pw_im)
    rp = cat(r_re, r_im, r_im, r_re)

    cr, ci = halves(tiles(c_re), 2), halves(tiles(c_im), 2)
    by_state = lambda m: m.reshape(depth, J, 2, N, HALF_LANES)
    crt, cit = by_state(cr.transpose(0, 1, 2, 5, 3, 4)), by_state(ci.transpose(0, 1, 2, 5, 3, 4))
    cc = jnp.concatenate([cat(crt, crt), cat(-cit, -cit)], axis=-2)
    skip = halves(tiles(d_skip), 2).reshape(depth, J, 2, 1, HALF_LANES)
    dl = jnp.broadcast_to(cat(skip, skip), (depth, J, 2, SUBLANES, LANES))

    fwd = lambda m: by_state(halves(m[:, :, 1:], 3).transpose(0, 1, 3, 5, 2, 4))
    pf = cat(fwd(pw_re), fwd(pw_im))
    ct = jnp.stack([cat(crt, crt), cat(cit, cit)], axis=3)

    flat = lambda m: m.reshape(depth, J, TILE_STATE)
    t_re, t_im = powers(T * np.array([1, seg]), flat(lam_re), flat(lam_im))
    a = jnp.stack([t_re[:, :, 0], t_im[:, :, 0], t_re[:, :, 1], t_im[:, :, 1]], axis=2)[:, :, :, None]
    a = jnp.broadcast_to(a, (depth, J, 4, SUBLANES, TILE_STATE))
    pw = jnp.stack(powers(T * np.arange(seg), flat(lam_re), flat(lam_im)), axis=2)
    return rp, bbt, cc, dl, pf, ct, a, pw


def _attn_kernel(sink_ref, bias_ref, q_ref, kd_ref, vt_ref, o_ref):
    n_keys = N_META + 2 * BLOCK
    n_local = q_ref.shape[0] // BLOCK
    first_k = lax.broadcasted_iota(jnp.int32, (n_keys, LANES), 1) < HEAD_DIM
    first_v = lax.broadcasted_iota(jnp.int32, (LANES, 3 * BLOCK), 0) < HEAD_DIM
    first_o = lax.broadcasted_iota(jnp.int32, (LANES, 2 * BLOCK), 0) < HEAD_DIM
    zero_k = jnp.zeros((n_keys, LANES), BF16)
    zero_v = jnp.zeros((LANES, 3 * BLOCK), BF16)
    pad_p = jnp.zeros((PAD, 2 * BLOCK), BF16)
    shape = (2 * SUBLANES, 6 * BLOCK)
    key_rows = jnp.where(lax.broadcasted_iota(jnp.int32, shape, 0) == lax.broadcasted_iota(jnp.int32, shape, 1) // (3 * BLOCK),
                         1.0, 0.0).astype(BF16)

    def block_of(i):
        n = pl.program_id(1) * n_local + i
        return n, jnp.maximum(n - 1, 0)

    def scores(i, kh):
        n, prev = block_of(i)
        cols = slice(kh * LANES, (kh + 1) * LANES)
        kd = jnp.concatenate([kd_ref[0, PAD:, cols], kd_ref[prev, :, cols], kd_ref[n, :, cols]], axis=0)
        kab = jnp.concatenate([jnp.where(first_k, kd, zero_k), jnp.where(first_k, zero_k, kd)], axis=0)
        q4 = jnp.concatenate([q_ref[i * BLOCK:(i + 1) * BLOCK, (2 * kh + p) * LANES:(2 * kh + p + 1) * LANES]
                              for p in range(2)], axis=0)
        return lax.dot_general(kab, q4, (((1,), (1,)), ((), ())), preferred_element_type=F32)

    chains = [(i, kh) for i in range(n_local) for kh in range(N_KV_HEADS)]
    s_next = scores(*chains[0])
    for c, (i, kh) in enumerate(chains):
        s = s_next
        if c + 1 < len(chains):
            s_next = scores(*chains[c + 1])
        n, prev = block_of(i)
        bias = bias_ref[jnp.minimum(n, 2)]
        pt, tail = [], []
        for e in range(2):
            blocks, tl = [], []
            for p in range(2):
                se = s[e * n_keys:(e + 1) * n_keys, p * LANES:(p + 1) * LANES] + bias
                sink = sink_ref[4 * kh + 2 * p + e]
                m = jnp.maximum(jnp.max(se, axis=0, keepdims=True), sink)
                blocks.append(jnp.exp(se - m).astype(BF16))
                tl.append(jnp.exp(sink - m))
            pt += [pad_p, jnp.concatenate(blocks, axis=1)]
            tail.append(jnp.concatenate(tl, axis=1))
        rows = slice(kh * LANES, (kh + 1) * LANES)
        vd = jnp.concatenate([vt_ref[0, rows, :], vt_ref[prev, rows, :], vt_ref[n, rows, :]], axis=1)
        vab = jnp.concatenate([jnp.where(first_v, vd, zero_v), jnp.where(first_v, zero_v, vd)], axis=1)
        vab = jnp.concatenate([vab, key_rows], axis=0)
        ot = _dot(vab, jnp.concatenate(pt, axis=0))
        rden = [1.0 / (ot[LANES + e:LANES + e + 1] + tail[e]) for e in range(2)]
        o = (ot[:LANES] * jnp.where(first_o, rden[0], rden[1])).T
        for p in range(2):
            o_ref[i * BLOCK:(i + 1) * BLOCK, (2 * kh + p) * LANES:(2 * kh + p + 1) * LANES] = (
                o[p * BLOCK:(p + 1) * BLOCK].astype(BF16))


def _attention(q, kd, vt, sinks, bias, bsz, n_blk, tq):
    steps = n_blk * BLOCK // tq
    rows = pl.BlockSpec((tq, D_ATTN), lambda b, s: (b * steps + s, 0))
    return pl.pallas_call(
        _attn_kernel,
        grid=(bsz, steps),
        in_specs=[pl.BlockSpec(memory_space=pltpu.SMEM), _const_spec(bias.shape), rows,
                  pl.BlockSpec((n_blk, BLOCK, 2 * D_KV), lambda b, s: (b, 0, 0)),
                  pl.BlockSpec((n_blk, 2 * D_KV, BLOCK), lambda b, s: (b, 0, 0))],
        out_specs=rows,
        out_shape=jax.ShapeDtypeStruct(q.shape, BF16),
        compiler_params=_params(("parallel", "parallel")),
        name="attention",
    )(sinks, bias, q, kd, vt)


def _attn_bias():
    i = np.arange(BLOCK)[None, :]
    j = np.arange(BLOCK)[:, None]
    none = np.zeros((BLOCK, BLOCK), bool)
    causal = j <= i
    in_meta = np.broadcast_to(j >= PAD, (BLOCK, BLOCK))
    blk0 = [none[:N_META], none, causal & in_meta]
    blk1 = [none[:N_META], in_meta, causal]
    blk2 = [~none[:N_META], j > i, causal]
    ok = np.stack([np.concatenate(b, axis=0) for b in (blk0, blk1, blk2)])
    return jnp.asarray(np.where(ok, 0.0, NEG_INF), F32)


def _merge_kernel(rows, *refs):
    n = len(rows.specs)
    y_ref, a_ref, gs_ref, ga_ref, wglu_ref, bglu_ref, wos_ref, woa_ref, wout_ref, gain_ref, o_ref = refs[n:]
    z = jax.nn.gelu(y_ref[...])
    z = z * jax.nn.sigmoid(_dot(z.astype(BF16), wglu_ref[...]) + bglu_ref[...])
    merged = (jax.nn.sigmoid(gs_ref[...].astype(F32)) * _dot(z.astype(BF16), wos_ref[...])
              + jax.nn.sigmoid(ga_ref[...].astype(F32)) * _dot(a_ref[...], woa_ref[...]))
    mix = _dot(merged.astype(BF16), wout_ref[...])
    o_ref[...] = rows.load(refs[:n]) + _rms(mix, gain_ref[...])


def _merge(rows, layer, y, a, gs, ga, wglu, bglu, wos, woa, wout, gain, tm):
    n_rows = y.shape[0]
    row = lambda width: pl.BlockSpec((tm, width), lambda i: (i, 0))
    weights = (wglu, bglu, wos, woa, wout, gain)
    return pl.pallas_call(
        functools.partial(_merge_kernel, rows),
        grid=(n_rows // tm,),
        in_specs=rows.specs + [row(D_SSM), row(D_ATTN), row(D_MODEL), row(D_MODEL)]
                 + [_layer_spec(p, layer) for p in weights],
        out_specs=row(D_MODEL),
        out_shape=jax.ShapeDtypeStruct((n_rows, D_MODEL), F32),
        compiler_params=_params(("parallel",)),
        name="merge",
    )(*rows.operands, y, a, gs, ga, *weights)


def _ffn_kernel(rows, *refs):
    n = len(rows.specs)
    gpre_ref, wup_ref, wdown_ref, gpost_ref, o_ref = refs[n:]
    x = rows.load(refs[:n])
    h = _rms(x, gpre_ref[...]).astype(BF16)
    step = 512
    acc = jnp.zeros(x.shape, F32)
    for c in range(0, D_FF, step):
        a = jnp.maximum(_dot(h, wup_ref[:, c:c + step]), 0.0)
        acc = acc + _dot((a * a).astype(BF16), wdown_ref[c:c + step, :])
    o_ref[...] = x + _rms(acc, gpost_ref[...])


def _ffn(rows, n_rows, layer, gpre, wup, wdown, gpost, tm):
    weights = (gpre, wup, wdown, gpost)
    return pl.pallas_call(
        functools.partial(_ffn_kernel, rows),
        grid=(n_rows // tm,),
        in_specs=rows.specs + [_layer_spec(p, layer) for p in weights],
        out_specs=pl.BlockSpec((tm, D_MODEL), lambda i: (i, 0)),
        out_shape=jax.ShapeDtypeStruct((n_rows, D_MODEL), F32),
        compiler_params=_params(("parallel",)),
        name="ffn",
    )(*rows.operands, *weights)


def _rope_tables(n_blk):
    inv_freq = 1.0 / (ROPE_THETA ** (jnp.arange(0, HEAD_DIM, 2, dtype=F32) / HEAD_DIM))
    blk = (BLOCK * jnp.arange(n_blk, dtype=jnp.int32)).astype(F32)[:, None, None] * inv_freq
    off = (jnp.arange(BLOCK, dtype=jnp.int32) - PAD).astype(F32)[None, :, None] * inv_freq
    cos = jnp.cos(blk) * jnp.cos(off) - jnp.sin(blk) * jnp.sin(off)
    sin = jnp.sin(blk) * jnp.cos(off) + jnp.cos(blk) * jnp.sin(off)
    lanes = lambda m: jnp.tile(m.reshape(n_blk * BLOCK, HEAD_DIM // 2), (1, LANES // (HEAD_DIM // 2)))
    cos, sin = lanes(cos), lanes(sin)
    first_half = (np.arange(LANES) % HEAD_DIM) < HEAD_DIM // 2
    return cos, jnp.where(first_half, -sin, 0.0), jnp.where(first_half, 0.0, sin)


def _kv_weights(w_in):
    depth = w_in.shape[0]
    dup = lambda m: jnp.tile(m.reshape(depth, D_MODEL, N_KV_HEADS, 1, HEAD_DIM),
                             (1, 1, 1, 2, 1)).reshape(depth, D_MODEL, 2 * D_KV)
    wkd = dup(w_in[:, :, C_K:C_K + D_KV])
    wvt = dup(w_in[:, :, C_V:C_V + D_KV]).swapaxes(1, 2)
    return wkd.astype(BF16), wvt.astype(BF16)


def kernel(x, meta_tokens, norm_mix_pre, norm_mix_post, norm_mlp_pre, norm_mlp_post, w_in, ssm_a_re, ssm_a_im, ssm_log_dt, ssm_b_re, ssm_b_im, ssm_c_re, ssm_c_im, ssm_d, w_glu, b_glu, attn_sinks, w_o_ssm, w_o_attn, w_out, w_up, w_down):
    bsz, seq, _ = x.shape
    depth = w_in.shape[0]
    seq_pad = PAD + N_META + seq
    assert seq_pad % BLOCK == 0 and (seq_pad // CHUNK) % N_SEG == 0
    n_blk, n_chunks = seq_pad // BLOCK, seq_pad // CHUNK
    tm, tm_out = 640, 512
    assert seq_pad % tm == 0 and seq % tm_out == 0
    n_rows = bsz * seq_pad

    meta_block = jnp.concatenate([jnp.zeros((PAD, D_MODEL), x.dtype), meta_tokens.astype(x.dtype)], axis=0)
    stream = _Rows("input", x.reshape(bsz * seq, D_MODEL), tm, seq_pad // tm, seq // BLOCK, meta_block)
    cos, sa, sb = _rope_tables(n_blk)
    bias = _attn_bias()

    row = lambda m: m[:, None, :]
    wkd, wvt = _kv_weights(w_in)
    w_in, w_glu, w_o_ssm, w_o_attn, w_out, w_up, w_down = (
        m.astype(BF16) for m in (w_in, w_glu, w_o_ssm, w_o_attn, w_out, w_up, w_down))
    prep = _ssm_prep(ssm_a_re, ssm_a_im, ssm_log_dt, ssm_b_re, ssm_b_im, ssm_c_re, ssm_c_im, ssm_d, n_chunks // N_SEG)

    for l in range(depth):
        u, q, kd, vt, gs, ga = _inproj(stream, n_rows, l, row(norm_mix_pre), w_in, wkd, wvt, cos, sa, sb, tm)
        y = _ssm(u, prep, l, bsz, seq_pad)
        att = _attention(q, kd, vt, attn_sinks[l], bias, bsz, n_blk, tm)
        hres = _merge(stream, l, y, att, gs, ga, w_glu, row(b_glu), w_o_ssm, w_o_attn, w_out, row(norm_mix_post), tm)
        ffn_w = (l, row(norm_mlp_pre), w_up, w_down, row(norm_mlp_post))
        if l + 1 < depth:
            stream = _Rows("padded", _ffn(_Rows("padded", hres, tm, seq_pad // tm, n_blk), n_rows, *ffn_w, tm),
                           tm, seq_pad // tm, n_blk)
        else:
            out = _ffn(_Rows("frames", hres, tm_out, seq // tm_out, n_blk), bsz * seq, *ffn_w, tm_out)
    return out.reshape(bsz, seq, D_MODEL)
```

```python
import functools
import math

import jax
import jax.numpy as jnp
import numpy as np
from jax import lax
from jax.experimental import pallas as pl
from jax.experimental.pallas import tpu as pltpu

D_MODEL = 1024
N_META = 16
HEAD_DIM = 64
N_Q_HEADS = 16
N_KV_HEADS = 4
D_ATTN = N_Q_HEADS * HEAD_DIM
D_KV = N_KV_HEADS * HEAD_DIM
BLOCK = 128
ROPE_THETA = 10000.0
ATTN_SCALE = HEAD_DIM ** -0.5
LOG2_E = math.log2(math.e)
NEG_INF = -1e30
D_SSM = D_MODEL // 2
SSM_GROUP = 16
N_SSM_GROUPS = D_SSM // SSM_GROUP
SSM_STATE = 64
D_FF = 4 * D_MODEL
RMS_EPS = 1e-6

LANES = 128
SUBLANES = 8
MXU = 256
PAD = BLOCK - N_META
CHUNK = 16
TILE_GROUPS = LANES // SSM_GROUP
N_SSM_TILES = D_SSM // LANES
TILE_STATE = TILE_GROUPS * SSM_STATE
CHUNK_W = CHUNK * LANES
HALF_GROUPS = TILE_GROUPS // 2
HALF_LANES = LANES // 2
HALF_STATE = HALF_GROUPS * SSM_STATE
HALF_W = CHUNK * HALF_LANES
SUB_BLOCKS = MXU // HALF_LANES
N_SEG = SUBLANES
VMEM_LIMIT = 56 * 1024 * 1024

C_U = 0
C_Q = C_U + D_SSM
C_K = C_Q + D_ATTN
C_V = C_K + D_KV
C_GS = C_V + D_KV
C_GA = C_GS + D_MODEL
C_END = C_GA + D_MODEL

BF16 = jnp.bfloat16
F32 = jnp.float32


def _dot(a, b):
    return jnp.dot(a, b, preferred_element_type=F32)


def _rms(x, gain):
    return x * lax.rsqrt(jnp.mean(x * x, axis=-1, keepdims=True) + RMS_EPS) * gain


def _const_spec(shape):
    return pl.BlockSpec(shape, lambda *_: (0,) * len(shape), pipeline_mode=pl.Buffered(1))


def _layer_spec(stacked, layer):
    shape = stacked.shape[1:]
    return pl.BlockSpec((None,) + shape, lambda *_: (layer,) + (0,) * len(shape), pipeline_mode=pl.Buffered(1))


def _params(sem):
    return pltpu.CompilerParams(dimension_semantics=sem, vmem_limit_bytes=VMEM_LIMIT)


class _Rows:
    def __init__(self, kind, array, tm, tiles_per_batch, blocks_per_batch, meta_block=None):
        self.has_meta = kind == "input"
        self.tiles_per_batch = tiles_per_batch
        nb = tm // BLOCK
        if kind == "padded":
            self.specs = [pl.BlockSpec((tm, D_MODEL), lambda i: (i, 0))]
            self.operands = [array]
            return
        shift = -1 if kind == "input" else 1

        def imap(j):
            return lambda i: ((i // tiles_per_batch) * blocks_per_batch
                              + jnp.maximum((i % tiles_per_batch) * nb + j + shift, 0), 0)

        self.specs = [pl.BlockSpec((BLOCK, D_MODEL), imap(j)) for j in range(nb)]
        self.operands = [array] * nb
        if self.has_meta:
            self.specs.append(_const_spec((BLOCK, D_MODEL)))
            self.operands.append(meta_block)

    def load(self, refs):
        n = len(self.specs)
        if n == 1:
            return refs[0][...]
        blocks = [r[...] for r in refs[:n - self.has_meta]]
        if self.has_meta:
            head = pl.program_id(0) % self.tiles_per_batch == 0
            blocks[0] = jnp.where(head, refs[n - 1][...], blocks[0])
        return jnp.concatenate(blocks, axis=0)


def _inproj_kernel(rows, *refs):
    n = len(rows.specs)
    gain_ref, w_ref, wkd_ref, wvt_ref, cos_ref, sa_ref, sb_ref, u_ref, q_ref, k_ref, vt_ref, gs_ref, ga_ref = refs[n:]
    h = _rms(rows.load(refs[:n]), gain_ref[...]).astype(BF16)
    cos, sa, sb = cos_ref[...], sa_ref[...], sb_ref[...]

    def rope(t):
        return t * cos + pltpu.roll(t, LANES - HEAD_DIM // 2, 1) * sa + pltpu.roll(t, HEAD_DIM // 2, 1) * sb

    step = 512
    u_ref[...] = _dot(h, w_ref[:, C_U:C_U + D_SSM])
    for c in range(0, D_ATTN, step):
        t = _dot(h, w_ref[:, C_Q + c:C_Q + c + step])
        for j in range(0, step, LANES):
            q_ref[:, c + j:c + j + LANES] = (rope(t[:, j:j + LANES]) * (ATTN_SCALE * LOG2_E)).astype(BF16)
    t = _dot(h, wkd_ref[...])
    for j in range(0, 2 * D_KV, LANES):
        kj = rope(t[:, j:j + LANES]).astype(BF16)
        for b in range(k_ref.shape[0]):
            k_ref[b, :, j:j + LANES] = kj[b * BLOCK:(b + 1) * BLOCK]
    vt = lax.dot_general(wvt_ref[...], h, (((1,), (1,)), ((), ())), preferred_element_type=F32).astype(BF16)
    for b in range(vt_ref.shape[0]):
        vt_ref[b] = vt[:, b * BLOCK:(b + 1) * BLOCK]
    for c in range(0, D_MODEL, step):
        gs_ref[:, c:c + step] = _dot(h, w_ref[:, C_GS + c:C_GS + c + step]).astype(BF16)
        ga_ref[:, c:c + step] = _dot(h, w_ref[:, C_GA + c:C_GA + c + step]).astype(BF16)


def _inproj(rows, n_rows, layer, gain, w, wkd, wvt, cos, sa, sb, tm):
    row = lambda width: pl.BlockSpec((tm, width), lambda i: (i, 0))
    tab = pl.BlockSpec((tm, LANES), lambda i: (i % rows.tiles_per_batch, 0))
    out = lambda width, dtype: (row(width), jax.ShapeDtypeStruct((n_rows, width), dtype))
    per_block = lambda shape: (pl.BlockSpec((tm // BLOCK,) + shape, lambda i: (i, 0, 0)),
                               jax.ShapeDtypeStruct((n_rows // BLOCK,) + shape, BF16))
    outs = [out(D_SSM, F32),
            out(D_ATTN, BF16), per_block((BLOCK, 2 * D_KV)), per_block((2 * D_KV, BLOCK)),
            out(D_MODEL, BF16), out(D_MODEL, BF16)]
    return pl.pallas_call(
        functools.partial(_inproj_kernel, rows),
        grid=(n_rows // tm,),
        in_specs=rows.specs + [_layer_spec(p, layer) for p in (gain, w, wkd, wvt)] + [tab, tab, tab],
        out_specs=[o[0] for o in outs],
        out_shape=[o[1] for o in outs],
        compiler_params=_params(("parallel",)),
        name="inproj",
    )(*rows.operands, gain, w, wkd, wvt, cos, sa, sb)


def _ssm_spread(rp_ref, bbt_ref, cc_ref, dl_ref, pf_ref, ct_ref, wt_s, pb_s, qm_s):
    T = CHUNK
    iota = lambda shape, axis: lax.broadcasted_iota(jnp.int32, shape, axis)
    group_of = lambda shape, axis, width: (iota(shape, axis) // width) % HALF_GROUPS
    one_hot = lambda hit: jnp.where(hit, 1.0, 0.0).astype(BF16)

    even = iota((HALF_LANES, LANES), 1) < HALF_LANES
    none = jnp.zeros((HALF_LANES, LANES), BF16)
    shape = (HALF_W, HALF_LANES)
    row_copy = one_hot(iota(shape, 0) // SSM_GROUP == iota(shape, 1))
    shape = (LANES, 2 * HALF_STATE)
    state_copy = one_hot(iota(shape, 1) % SSM_STATE + iota(shape, 1) // HALF_STATE * SSM_STATE == iota(shape, 0))
    shape = (LANES, 2 * HALF_W)
    lane_copy = one_hot(iota(shape, 1) // SSM_GROUP == iota(shape, 0))
    shape = (HALF_W, LANES)
    tap_diag = group_of(shape, 0, SSM_GROUP) == group_of(shape, 1, SSM_GROUP)
    tap_skip = tap_diag & (iota(shape, 0) // HALF_LANES == T - 1) & (iota(shape, 0) % SSM_GROUP == iota(shape, 1) % SSM_GROUP)
    shape = (HALF_W, 2 * HALF_STATE)
    pb_diag = group_of(shape, 0, SSM_GROUP) == group_of(shape, 1, SSM_STATE)
    col_group = group_of((SSM_STATE, HALF_W), 1, SSM_GROUP)

    for h in range(2):
        x = _dot(row_copy, rp_ref[0, h].astype(BF16))
        y = jnp.concatenate([bbt_ref[0, h]] * T, axis=0)
        w = (x[:, :LANES] * y[:, :LANES] + x[:, LANES:] * y[:, LANES:]).astype(BF16)
        pb_s[h] = jnp.where(pb_diag, _dot(w, state_copy), 0.0).astype(BF16)

        taps = _dot(w, cc_ref[0, h].astype(BF16)) + jnp.where(tap_skip, jnp.concatenate([dl_ref[0, h]] * (HALF_W // SUBLANES), axis=0), 0.0)
        taps = jnp.where(tap_diag, taps, 0.0).astype(BF16)
        block = lambda lag: taps[(T - 1 - lag) * HALF_LANES:(T - lag) * HALF_LANES] if lag >= 0 else none
        for a in range(SUB_BLOCKS):
            for i in range(SUB_BLOCKS):
                for kk in range(SUB_BLOCKS // 2):
                    lag = SUB_BLOCKS * a + 2 * kk - i
                    wt_s[h, a, i * HALF_LANES:(i + 1) * HALF_LANES, kk * LANES:(kk + 1) * LANES] = (
                        jnp.where(even, block(lag), block(lag + 1)))

        pw = _dot(pf_ref[0, h].astype(BF16), lane_copy)
        p_re, p_im = pw[:, :HALF_W], pw[:, HALF_W:]
        c_re = jnp.concatenate([ct_ref[0, h, 0]] * (HALF_W // LANES), axis=1)
        c_im = jnp.concatenate([ct_ref[0, h, 1]] * (HALF_W // LANES), axis=1)
        for part, q in enumerate((c_re * p_re - c_im * p_im, -(c_re * p_im + c_im * p_re))):
            for g in range(HALF_GROUPS):
                r = part * HALF_STATE + g * SSM_STATE
                qm_s[h, r:r + SSM_STATE, :] = jnp.where(col_group == g, q, 0.0).astype(BF16)


def _ssm_kernel(u_ref, rp_ref, bbt_ref, cc_ref, dl_ref, pf_ref, ct_ref, a_ref, pw_ref, y_ref,
                z_s, v_s, l_s, s_s, wt_s, pb_s, qm_s):
    n_chunks = z_s.shape[0]
    seg = n_chunks // N_SEG
    half_tiles = HALF_STATE // LANES
    part_tiles = lambda part: [h * 2 * half_tiles + part * half_tiles + c for h in range(2) for c in range(half_tiles)]

    @pl.when(pl.program_id(1) == 0)
    def _():
        _ssm_spread(rp_ref, bbt_ref, cc_ref, dl_ref, pf_ref, ct_ref, wt_s, pb_s, qm_s)

    low = lax.broadcasted_iota(jnp.int32, (n_chunks, LANES), 1) < HALF_LANES

    def regroup(a, b):
        return jnp.where(low, a, pltpu.roll(b, HALF_LANES, 1)), jnp.where(low, pltpu.roll(a, HALF_LANES, 1), b)

    for m in range(CHUNK // 2):
        lo, hi = regroup(u_ref[pl.ds(2 * m, n_chunks, stride=CHUNK), :], u_ref[pl.ds(2 * m + 1, n_chunks, stride=CHUNK), :])
        z_s[:, m * LANES:(m + 1) * LANES] = lo.astype(BF16)
        z_s[:, HALF_W + m * LANES:HALF_W + (m + 1) * LANES] = hi.astype(BF16)
    for h in range(2):
        v = _dot(z_s[:, h * HALF_W:(h + 1) * HALF_W], pb_s[h])
        for c in range(2 * half_tiles):
            v_s[h * 2 * half_tiles + c] = v[:, c * LANES:(c + 1) * LANES]

    def load(ref, rows, part):
        return jnp.concatenate([ref[c, rows, :] for c in part_tiles(part)], axis=1)

    def store(ref, rows, part, val):
        for k, c in enumerate(part_tiles(part)):
            ref[c, rows, :] = val[:, k * LANES:(k + 1) * LANES]

    are, aim = a_ref[0, 0], a_ref[0, 1]

    def local_scan(i, carry):
        sre, sim = carry
        rows = pl.ds(i, N_SEG, stride=seg)
        store(l_s, rows, 0, sre)
        store(l_s, rows, 1, sim)
        return (are * sre - aim * sim + load(v_s, rows, 0), are * sim + aim * sre + load(v_s, rows, 1))

    zero = jnp.zeros((N_SEG, TILE_STATE), F32)
    ere, eim = lax.fori_loop(0, seg, local_scan, (zero, zero), unroll=5)

    bre, bim = a_ref[0, 2][:1], a_ref[0, 3][:1]
    tre, tim = [zero[:1]], [zero[:1]]
    for m in range(1, N_SEG):
        pre, pim = tre[-1], tim[-1]
        tre.append(bre * pre - bim * pim + ere[m - 1:m])
        tim.append(bre * pim + bim * pre + eim[m - 1:m])
    tre, tim = jnp.concatenate(tre, axis=0), jnp.concatenate(tim, axis=0)

    def add_carry(i, _):
        rows = pl.ds(i, N_SEG, stride=seg)
        pre, pim = pw_ref[0, 0, pl.ds(i, 1), :], pw_ref[0, 1, pl.ds(i, 1), :]
        store(l_s, rows, 0, load(l_s, rows, 0) + (pre * tre - pim * tim))
        store(l_s, rows, 1, load(l_s, rows, 1) + (pre * tim + pim * tre))
        return 0

    lax.fori_loop(0, seg, add_carry, 0, unroll=5)

    for c in range(4 * half_tiles):
        s_s[:, c * LANES:(c + 1) * LANES] = l_s[c].astype(BF16)
    for tp in range(SUB_BLOCKS):
        acc = []
        for h in range(2):
            y = _dot(s_s[:, h * 2 * HALF_STATE:(h + 1) * 2 * HALF_STATE], qm_s[h, :, tp * MXU:(tp + 1) * MXU])
            for tq in range(tp + 1):
                y = y + _dot(z_s[:, h * HALF_W + tq * MXU:h * HALF_W + (tq + 1) * MXU], wt_s[h, tp - tq])
            acc.append(y)
        for kk in range(SUB_BLOCKS // 2):
            ya, yb = regroup(acc[0][:, kk * LANES:(kk + 1) * LANES], acc[1][:, kk * LANES:(kk + 1) * LANES])
            t = SUB_BLOCKS * tp + 2 * kk
            y_ref[pl.ds(t, n_chunks, stride=CHUNK), :] = ya
            y_ref[pl.ds(t + 1, n_chunks, stride=CHUNK), :] = yb


def _ssm(u, prep, layer, bsz, seq_pad):
    n_chunks = seq_pad // CHUNK
    tile = lambda shape: pl.BlockSpec((None, 1) + shape, lambda j, b: (layer, j) + (0,) * len(shape))
    io = pl.BlockSpec((seq_pad, LANES), lambda j, b: (b, j))
    return pl.pallas_call(
        _ssm_kernel,
        grid=(N_SSM_TILES, bsz),
        in_specs=[io] + [tile(p.shape[2:]) for p in prep],
        out_specs=io,
        out_shape=jax.ShapeDtypeStruct(u.shape, F32),
        scratch_shapes=[pltpu.VMEM((n_chunks, CHUNK_W), BF16),
                        pltpu.VMEM((2 * TILE_STATE // LANES, n_chunks, LANES), F32),
                        pltpu.VMEM((2 * TILE_STATE // LANES, n_chunks, LANES), F32),
                        pltpu.VMEM((n_chunks, 2 * TILE_STATE), BF16),
                        pltpu.VMEM((2, SUB_BLOCKS, MXU, MXU), BF16),
                        pltpu.VMEM((2, HALF_W, 2 * HALF_STATE), BF16),
                        pltpu.VMEM((2, 2 * HALF_STATE, HALF_W), BF16)],
        compiler_params=_params(("arbitrary", "arbitrary")),
        name="ssm",
    )(u, *prep)


def _ssm_prep(a_re, a_im, log_dt, b_re, b_im, c_re, c_im, d_skip, seg):
    N, C, T, J, E = SSM_STATE, SSM_GROUP, CHUNK, N_SSM_TILES, TILE_GROUPS
    depth = a_re.shape[0]
    tiles = lambda m: m.reshape((depth, J, E) + m.shape[2:])
    a_re, a_im, dt = tiles(a_re), tiles(a_im), tiles(jnp.exp(log_dt))[..., None]
    lam_re, lam_im = a_re * dt, a_im * dt

    def powers(p, lam_re, lam_im):
        p = jnp.asarray(p, F32).reshape((-1,) + (1,) * (lam_re.ndim - 2))
        mag = jnp.exp(lam_re[:, :, None] * p)
        return mag * jnp.cos(lam_im[:, :, None] * p), mag * jnp.sin(lam_im[:, :, None] * p)

    pw_re, pw_im = powers(np.arange(T + 1), lam_re, lam_im)
    den = a_re * a_re + a_im * a_im
    nre, nim = pw_re[:, :, 1] - 1.0, pw_im[:, :, 1]
    coef_re = ((nre * a_re + nim * a_im) / den)[..., None, :]
    coef_im = ((nim * a_re - nre * a_im) / den)[..., None, :]
    bt_re, bt_im = tiles(b_re).swapaxes(-1, -2), tiles(b_im).swapaxes(-1, -2)
    halves = lambda m, axis: m.reshape(m.shape[:axis] + (2, HALF_GROUPS) + m.shape[axis + 1:])
    slab = lambda m: m.reshape(m.shape[:3] + (-1, m.shape[-1]))
    cat = lambda *ms: jnp.concatenate(ms, axis=-1)
    bb_re = slab(halves(coef_re * bt_re - coef_im * bt_im, 2))
    bb_im = slab(halves(coef_re * bt_im + coef_im * bt_re, 2))
    bbt = cat(bb_re, bb_re, -bb_im, bb_im)

    rev = lambda m: slab(halves(m[:, :, T - 1::-1], 3).swapaxes(2, 3))
    r_re, r_im = rev(pw_re), rev(pw_im)
    rp = cat(r_re, r_im, r_im, r_re)

    cr, ci = halves(tiles(c_re), 2), halves(tiles(c_im), 2)
    by_state = lambda m: m.reshape(depth, J, 2, N, HALF_LANES)
    crt, cit = by_state(cr.transpose(0, 1, 2, 5, 3, 4)), by_state(ci.transpose(0, 1, 2, 5, 3, 4))
    cc = jnp.concatenate([cat(crt, crt), cat(-cit, -cit)], axis=-2)
    skip = halves(tiles(d_skip), 2).reshape(depth, J, 2, 1, HALF_LANES)
    dl = jnp.broadcast_to(cat(skip, skip), (depth, J, 2, SUBLANES, LANES))

    fwd = lambda m: by_state(halves(m[:, :, 1:], 3).transpose(0, 1, 3, 5, 2, 4))
    pf = cat(fwd(pw_re), fwd(pw_im))
    ct = jnp.stack([cat(crt, crt), cat(cit, cit)], axis=3)

    flat = lambda m: m.reshape(depth, J, TILE_STATE)
    t_re, t_im = powers(T * np.array([1, seg]), flat(lam_re), flat(lam_im))
    a = jnp.stack([t_re[:, :, 0], t_im[:, :, 0], t_re[:, :, 1], t_im[:, :, 1]], axis=2)[:, :, :, None]
    a = jnp.broadcast_to(a, (depth, J, 4, SUBLANES, TILE_STATE))
    pw = jnp.stack(powers(T * np.arange(seg), flat(lam_re), flat(lam_im)), axis=2)
    return rp, bbt, cc, dl, pf, ct, a, pw


def _attn_kernel(sink_ref, bias_ref, q_ref, kd_ref, vt_ref, o_ref):
    n_keys = N_META + 2 * BLOCK
    n_local = q_ref.shape[0] // BLOCK
    first_k = lax.broadcasted_iota(jnp.int32, (n_keys, LANES), 1) < HEAD_DIM
    first_v = lax.broadcasted_iota(jnp.int32, (LANES, 3 * BLOCK), 0) < HEAD_DIM
    first_o = lax.broadcasted_iota(jnp.int32, (LANES, 2 * BLOCK), 0) < HEAD_DIM
    zero_k = jnp.zeros((n_keys, LANES), BF16)
    zero_v = jnp.zeros((LANES, 3 * BLOCK), BF16)
    pad_p = jnp.zeros((PAD, 2 * BLOCK), BF16)
    shape = (2 * SUBLANES, 6 * BLOCK)
    key_rows = jnp.where(lax.broadcasted_iota(jnp.int32, shape, 0) == lax.broadcasted_iota(jnp.int32, shape, 1) // (3 * BLOCK),
                         1.0, 0.0).astype(BF16)

    def block_of(i):
        n = pl.program_id(1) * n_local + i
        return n, jnp.maximum(n - 1, 0)

    def scores(i, kh):
        n, prev = block_of(i)
        cols = slice(kh * LANES, (kh + 1) * LANES)
        kd = jnp.concatenate([kd_ref[0, PAD:, cols], kd_ref[prev, :, cols], kd_ref[n, :, cols]], axis=0)
        kab = jnp.concatenate([jnp.where(first_k, kd, zero_k), jnp.where(first_k, zero_k, kd)], axis=0)
        q4 = jnp.concatenate([q_ref[i * BLOCK:(i + 1) * BLOCK, (2 * kh + p) * LANES:(2 * kh + p + 1) * LANES]
                              for p in range(2)], axis=0)
        return lax.dot_general(kab, q4, (((1,), (1,)), ((), ())), preferred_element_type=F32)

    chains = [(i, kh) for i in range(n_local) for kh in range(N_KV_HEADS)]
    s_next = scores(*chains[0])
    for c, (i, kh) in enumerate(chains):
        s = s_next
        if c + 1 < len(chains):
            s_next = scores(*chains[c + 1])
        n, prev = block_of(i)
        bias = bias_ref[jnp.minimum(n, 2)]
        pt, tail = [], []
        for e in range(2):
            blocks, tl = [], []
            for p in range(2):
                se = s[e * n_keys:(e + 1) * n_keys, p * LANES:(p + 1) * LANES] + bias
                sink = sink_ref[4 * kh + 2 * p + e] * LOG2_E
                m = jnp.maximum(jnp.max(se, axis=0, keepdims=True), sink)
                blocks.append(jnp.exp2(se - m).astype(BF16))
                tl.append(jnp.exp2(sink - m))
            pt += [pad_p, jnp.concatenate(blocks, axis=1)]
            tail.append(jnp.concatenate(tl, axis=1))
        rows = slice(kh * LANES, (kh + 1) * LANES)
        vd = jnp.concatenate([vt_ref[0, rows, :], vt_ref[prev, rows, :], vt_ref[n, rows, :]], axis=1)
        vab = jnp.concatenate([jnp.where(first_v, vd, zero_v), jnp.where(first_v, zero_v, vd)], axis=1)
        vab = jnp.concatenate([vab, key_rows], axis=0)
        ot = _dot(vab, jnp.concatenate(pt, axis=0))
        rden = [1.0 / (ot[LANES + e:LANES + e + 1] + tail[e]) for e in range(2)]
        o = (ot[:LANES] * jnp.where(first_o, rden[0], rden[1])).T
        for p in range(2):
            o_ref[i * BLOCK:(i + 1) * BLOCK, (2 * kh + p) * LANES:(2 * kh + p + 1) * LANES] = (
                o[p * BLOCK:(p + 1) * BLOCK].astype(BF16))


def _attention(q, kd, vt, sinks, bias, bsz, n_blk, tq):
    steps = n_blk * BLOCK // tq
    rows = pl.BlockSpec((tq, D_ATTN), lambda b, s: (b * steps + s, 0))
    return pl.pallas_call(
        _attn_kernel,
        grid=(bsz, steps),
        in_specs=[pl.BlockSpec(memory_space=pltpu.SMEM), _const_spec(bias.shape), rows,
                  pl.BlockSpec((n_blk, BLOCK, 2 * D_KV), lambda b, s: (b, 0, 0)),
                  pl.BlockSpec((n_blk, 2 * D_KV, BLOCK), lambda b, s: (b, 0, 0))],
        out_specs=rows,
        out_shape=jax.ShapeDtypeStruct(q.shape, BF16),
        compiler_params=_params(("parallel", "parallel")),
        name="attention",
    )(sinks, bias, q, kd, vt)


def _attn_bias():
    i = np.arange(BLOCK)[None, :]
    j = np.arange(BLOCK)[:, None]
    none = np.zeros((BLOCK, BLOCK), bool)
    causal = j <= i
    in_meta = np.broadcast_to(j >= PAD, (BLOCK, BLOCK))
    blk0 = [none[:N_META], none, causal & in_meta]
    blk1 = [none[:N_META], in_meta, causal]
    blk2 = [~none[:N_META], j > i, causal]
    ok = np.stack([np.concatenate(b, axis=0) for b in (blk0, blk1, blk2)])
    return jnp.asarray(np.where(ok, 0.0, NEG_INF), F32)


def _merge_kernel(rows, *refs):
    n = len(rows.specs)
    y_ref, a_ref, gs_ref, ga_ref, wglu_ref, bglu_ref, wos_ref, woa_ref, wout_ref, gain_ref, o_ref = refs[n:]
    z = jax.nn.gelu(y_ref[...])
    z = z * jax.nn.sigmoid(_dot(z.astype(BF16), wglu_ref[...]) + bglu_ref[...])
    merged = (jax.nn.sigmoid(gs_ref[...].astype(F32)) * _dot(z.astype(BF16), wos_ref[...])
              + jax.nn.sigmoid(ga_ref[...].astype(F32)) * _dot(a_ref[...], woa_ref[...]))
    mix = _dot(merged.astype(BF16), wout_ref[...])
    o_ref[...] = rows.load(refs[:n]) + _rms(mix, gain_ref[...])


def _merge(rows, layer, y, a, gs, ga, wglu, bglu, wos, woa, wout, gain, tm):
    n_rows = y.shape[0]
    row = lambda width: pl.BlockSpec((tm, width), lambda i: (i, 0))
    weights = (wglu, bglu, wos, woa, wout, gain)
    return pl.pallas_call(
        functools.partial(_merge_kernel, rows),
        grid=(n_rows // tm,),
        in_specs=rows.specs + [row(D_SSM), row(D_ATTN), row(D_MODEL), row(D_MODEL)]
                 + [_layer_spec(p, layer) for p in weights],
        out_specs=row(D_MODEL),
        out_shape=jax.ShapeDtypeStruct((n_rows, D_MODEL), F32),
        compiler_params=_params(("parallel",)),
        name="merge",
    )(*rows.operands, y, a, gs, ga, *weights)


def _ffn_kernel(rows, *refs):
    n = len(rows.specs)
    gpre_ref, wup_ref, wdown_ref, gpost_ref, o_ref = refs[n:]
    x = rows.load(refs[:n])
    h = _rms(x, gpre_ref[...]).astype(BF16)
    step = 512
    acc = jnp.zeros(x.shape, F32)
    for c in range(0, D_FF, step):
        a = jnp.maximum(_dot(h, wup_ref[:, c:c + step]), 0.0)
        acc = acc + _dot((a * a).astype(BF16), wdown_ref[c:c + step, :])
    o_ref[...] = x + _rms(acc, gpost_ref[...])


def _ffn(rows, n_rows, layer, gpre, wup, wdown, gpost, tm):
    weights = (gpre, wup, wdown, gpost)
    return pl.pallas_call(
        functools.partial(_ffn_kernel, rows),
        grid=(n_rows // tm,),
        in_specs=rows.specs + [_layer_spec(p, layer) for p in weights],
        out_specs=pl.BlockSpec((tm, D_MODEL), lambda i: (i, 0)),
        out_shape=jax.ShapeDtypeStruct((n_rows, D_MODEL), F32),
        compiler_params=_params(("parallel",)),
        name="ffn",
    )(*rows.operands, *weights)


def _rope_tables(n_blk):
    inv_freq = 1.0 / (ROPE_THETA ** (jnp.arange(0, HEAD_DIM, 2, dtype=F32) / HEAD_DIM))
    blk = (BLOCK * jnp.arange(n_blk, dtype=jnp.int32)).astype(F32)[:, None, None] * inv_freq
    off = (jnp.arange(BLOCK, dtype=jnp.int32) - PAD).astype(F32)[None, :, None] * inv_freq
    cos = jnp.cos(blk) * jnp.cos(off) - jnp.sin(blk) * jnp.sin(off)
    sin = jnp.sin(blk) * jnp.cos(off) + jnp.cos(blk) * jnp.sin(off)
    lanes = lambda m: jnp.tile(m.reshape(n_blk * BLOCK, HEAD_DIM // 2), (1, LANES // (HEAD_DIM // 2)))
    cos, sin = lanes(cos), lanes(sin)
    first_half = (np.arange(LANES) % HEAD_DIM) < HEAD_DIM // 2
    return cos, jnp.where(first_half, -sin, 0.0), jnp.where(first_half, 0.0, sin)


def _kv_weights(w_in):
    depth = w_in.shape[0]
    dup = lambda m: jnp.tile(m.reshape(depth, D_MODEL, N_KV_HEADS, 1, HEAD_DIM),
                             (1, 1, 1, 2, 1)).reshape(depth, D_MODEL, 2 * D_KV)
    wkd = dup(w_in[:, :, C_K:C_K + D_KV])
    wvt = dup(w_in[:, :, C_V:C_V + D_KV]).swapaxes(1, 2)
    return wkd.astype(BF16), wvt.astype(BF16)


def kernel(x, meta_tokens, norm_mix_pre, norm_mix_post, norm_mlp_pre, norm_mlp_post, w_in, ssm_a_re, ssm_a_im, ssm_log_dt, ssm_b_re, ssm_b_im, ssm_c_re, ssm_c_im, ssm_d, w_glu, b_glu, attn_sinks, w_o_ssm, w_o_attn, w_out, w_up, w_down):
    bsz, seq, _ = x.shape
    depth = w_in.shape[0]
    seq_pad = PAD + N_META + seq
    assert seq_pad % BLOCK == 0 and (seq_pad // CHUNK) % N_SEG == 0
    n_blk, n_chunks = seq_pad // BLOCK, seq_pad // CHUNK
    tm, tm_out = 640, 1024
    assert seq_pad % tm == 0 and seq % tm_out == 0
    n_rows = bsz * seq_pad

    meta_block = jnp.concatenate([jnp.zeros((PAD, D_MODEL), x.dtype), meta_tokens.astype(x.dtype)], axis=0)
    stream = _Rows("input", x.reshape(bsz * seq, D_MODEL), tm, seq_pad // tm, seq // BLOCK, meta_block)
    cos, sa, sb = _rope_tables(n_blk)
    bias = _attn_bias()

    row = lambda m: m[:, None, :]
    wkd, wvt = _kv_weights(w_in)
    w_in, w_glu, w_o_ssm, w_o_attn, w_out, w_up, w_down = (
        m.astype(BF16) for m in (w_in, w_glu, w_o_ssm, w_o_attn, w_out, w_up, w_down))
    prep = _ssm_prep(ssm_a_re, ssm_a_im, ssm_log_dt, ssm_b_re, ssm_b_im, ssm_c_re, ssm_c_im, ssm_d, n_chunks // N_SEG)

    for l in range(depth):
        u, q, kd, vt, gs, ga = _inproj(stream, n_rows, l, row(norm_mix_pre), w_in, wkd, wvt, cos, sa, sb, tm)
        y = _ssm(u, prep, l, bsz, seq_pad)
        att = _attention(q, kd, vt, attn_sinks[l], bias, bsz, n_blk, tm)
        hres = _merge(stream, l, y, att, gs, ga, w_glu, row(b_glu), w_o_ssm, w_o_attn, w_out, row(norm_mix_post), tm)
        ffn_w = (l, row(norm_mlp_pre), w_up, w_down, row(norm_mlp_post))
        if l + 1 < depth:
            stream = _Rows("padded", _ffn(_Rows("padded", hres, tm, seq_pad // tm, n_blk), n_rows, *ffn_w, tm),
                           tm, seq_pad // tm, n_blk)
        else:
            out = _ffn(_Rows("frames", hres, tm_out, seq // tm_out, n_blk), bsz * seq, *ffn_w, tm_out)
    return out.reshape(bsz, seq, D_MODEL)
```

```python
import functools
import math

import jax
import jax.numpy as jnp
import numpy as np
from jax import lax
from jax.experimental import pallas as pl
from jax.experimental.pallas import tpu as pltpu

D_MODEL = 1024
N_META = 16
HEAD_DIM = 64
N_Q_HEADS = 16
N_KV_HEADS = 4
D_ATTN = N_Q_HEADS * HEAD_DIM
D_KV = N_KV_HEADS * HEAD_DIM
BLOCK = 128
ROPE_THETA = 10000.0
ATTN_SCALE = HEAD_DIM ** -0.5
LOG2_E = math.log2(math.e)
NEG_INF = -1e30
D_SSM = D_MODEL // 2
SSM_GROUP = 16
N_SSM_GROUPS = D_SSM // SSM_GROUP
SSM_STATE = 64
D_FF = 4 * D_MODEL
RMS_EPS = 1e-6

LANES = 128
SUBLANES = 8
MXU = 256
PAD = BLOCK - N_META
CHUNK = 16
TILE_GROUPS = LANES // SSM_GROUP
N_SSM_TILES = D_SSM // LANES
TILE_STATE = TILE_GROUPS * SSM_STATE
CHUNK_W = CHUNK * LANES
HALF_GROUPS = TILE_GROUPS // 2
HALF_LANES = LANES // 2
HALF_STATE = HALF_GROUPS * SSM_STATE
HALF_W = CHUNK * HALF_LANES
SUB_BLOCKS = MXU // HALF_LANES
N_SEG = SUBLANES
VMEM_LIMIT = 56 * 1024 * 1024

C_U = 0
C_Q = C_U + D_SSM
C_K = C_Q + D_ATTN
C_V = C_K + D_KV
C_GS = C_V + D_KV
C_GA = C_GS + D_MODEL
C_END = C_GA + D_MODEL

BF16 = jnp.bfloat16
F32 = jnp.float32


def _dot(a, b):
    return jnp.dot(a, b, preferred_element_type=F32)


def _rms(x, gain):
    return x * lax.rsqrt(jnp.mean(x * x, axis=-1, keepdims=True) + RMS_EPS) * gain


def _const_spec(shape):
    return pl.BlockSpec(shape, lambda *_: (0,) * len(shape), pipeline_mode=pl.Buffered(1))


def _layer_spec(stacked, layer):
    shape = stacked.shape[1:]
    return pl.BlockSpec((None,) + shape, lambda *_: (layer,) + (0,) * len(shape), pipeline_mode=pl.Buffered(1))


def _params(sem):
    return pltpu.CompilerParams(dimension_semantics=sem, vmem_limit_bytes=VMEM_LIMIT)


class _Rows:
    def __init__(self, kind, array, tm, tiles_per_batch, blocks_per_batch, meta_block=None):
        self.has_meta = kind == "input"
        self.tiles_per_batch = tiles_per_batch
        nb = tm // BLOCK
        if kind == "padded":
            self.specs = [pl.BlockSpec((tm, D_MODEL), lambda i: (i, 0))]
            self.operands = [array]
            return
        shift = -1 if kind == "input" else 1

        def imap(j):
            return lambda i: ((i // tiles_per_batch) * blocks_per_batch
                              + jnp.maximum((i % tiles_per_batch) * nb + j + shift, 0), 0)

        self.specs = [pl.BlockSpec((BLOCK, D_MODEL), imap(j)) for j in range(nb)]
        self.operands = [array] * nb
        if self.has_meta:
            self.specs.append(_const_spec((BLOCK, D_MODEL)))
            self.operands.append(meta_block)

    def load(self, refs):
        n = len(self.specs)
        if n == 1:
            return refs[0][...]
        blocks = [r[...] for r in refs[:n - self.has_meta]]
        if self.has_meta:
            head = pl.program_id(0) % self.tiles_per_batch == 0
            blocks[0] = jnp.where(head, refs[n - 1][...], blocks[0])
        return jnp.concatenate(blocks, axis=0)


def _inproj_kernel(rows, *refs):
    n = len(rows.specs)
    gain_ref, w_ref, wkd_ref, wvt_ref, cos_ref, sa_ref, sb_ref, u_ref, q_ref, k_ref, vt_ref, gs_ref, ga_ref = refs[n:]
    h = _rms(rows.load(refs[:n]), gain_ref[...]).astype(BF16)
    cos, sa, sb = cos_ref[...], sa_ref[...], sb_ref[...]

    def rope(t):
        return t * cos + pltpu.roll(t, LANES - HEAD_DIM // 2, 1) * sa + pltpu.roll(t, HEAD_DIM // 2, 1) * sb

    step = 512
    u_ref[...] = _dot(h, w_ref[:, C_U:C_U + D_SSM].astype(BF16))
    for c in range(0, D_ATTN, step):
        t = _dot(h, w_ref[:, C_Q + c:C_Q + c + step].astype(BF16))
        for j in range(0, step, LANES):
            q_ref[:, c + j:c + j + LANES] = (rope(t[:, j:j + LANES]) * (ATTN_SCALE * LOG2_E)).astype(BF16)
    t = _dot(h, wkd_ref[...])
    for j in range(0, 2 * D_KV, LANES):
        kj = rope(t[:, j:j + LANES]).astype(BF16)
        for b in range(k_ref.shape[0]):
            k_ref[b, :, j:j + LANES] = kj[b * BLOCK:(b + 1) * BLOCK]
    vt = lax.dot_general(wvt_ref[...], h, (((1,), (1,)), ((), ())), preferred_element_type=F32).astype(BF16)
    for b in range(vt_ref.shape[0]):
        vt_ref[b] = vt[:, b * BLOCK:(b + 1) * BLOCK]
    for c in range(0, D_MODEL, step):
        gs_ref[:, c:c + step] = _dot(h, w_ref[:, C_GS + c:C_GS + c + step].astype(BF16)).astype(BF16)
        ga_ref[:, c:c + step] = _dot(h, w_ref[:, C_GA + c:C_GA + c + step].astype(BF16)).astype(BF16)


def _inproj(rows, n_rows, layer, gain, w, wkd, wvt, cos, sa, sb, tm):
    row = lambda width: pl.BlockSpec((tm, width), lambda i: (i, 0))
    tab = pl.BlockSpec((tm, LANES), lambda i: (i % rows.tiles_per_batch, 0))
    out = lambda width, dtype: (row(width), jax.ShapeDtypeStruct((n_rows, width), dtype))
    per_block = lambda shape: (pl.BlockSpec((tm // BLOCK,) + shape, lambda i: (i, 0, 0)),
                               jax.ShapeDtypeStruct((n_rows // BLOCK,) + shape, BF16))
    outs = [out(D_SSM, F32),
            out(D_ATTN, BF16), per_block((BLOCK, 2 * D_KV)), per_block((2 * D_KV, BLOCK)),
            out(D_MODEL, BF16), out(D_MODEL, BF16)]
    return pl.pallas_call(
        functools.partial(_inproj_kernel, rows),
        grid=(n_rows // tm,),
        in_specs=rows.specs + [_layer_spec(p, layer) for p in (gain, w, wkd, wvt)] + [tab, tab, tab],
        out_specs=[o[0] for o in outs],
        out_shape=[o[1] for o in outs],
        compiler_params=_params(("parallel",)),
        name="inproj",
    )(*rows.operands, gain, w, wkd, wvt, cos, sa, sb)


def _ssm_spread(rp_ref, bbt_ref, cc_ref, dl_ref, pf_ref, ct_ref, wt_s, pb_s, qm_s):
    T = CHUNK
    iota = lambda shape, axis: lax.broadcasted_iota(jnp.int32, shape, axis)
    group_of = lambda shape, axis, width: (iota(shape, axis) // width) % HALF_GROUPS
    one_hot = lambda hit: jnp.where(hit, 1.0, 0.0).astype(BF16)

    even = iota((HALF_LANES, LANES), 1) < HALF_LANES
    none = jnp.zeros((HALF_LANES, LANES), BF16)
    shape = (HALF_W, HALF_LANES)
    row_copy = one_hot(iota(shape, 0) // SSM_GROUP == iota(shape, 1))
    shape = (LANES, 2 * HALF_STATE)
    state_copy = one_hot(iota(shape, 1) % SSM_STATE + iota(shape, 1) // HALF_STATE * SSM_STATE == iota(shape, 0))
    shape = (LANES, 2 * HALF_W)
    lane_copy = one_hot(iota(shape, 1) // SSM_GROUP == iota(shape, 0))
    shape = (HALF_W, LANES)
    tap_diag = group_of(shape, 0, SSM_GROUP) == group_of(shape, 1, SSM_GROUP)
    tap_skip = tap_diag & (iota(shape, 0) // HALF_LANES == T - 1) & (iota(shape, 0) % SSM_GROUP == iota(shape, 1) % SSM_GROUP)
    shape = (HALF_W, 2 * HALF_STATE)
    pb_diag = group_of(shape, 0, SSM_GROUP) == group_of(shape, 1, SSM_STATE)
    col_group = group_of((SSM_STATE, HALF_W), 1, SSM_GROUP)

    for h in range(2):
        x = _dot(row_copy, rp_ref[0, h].astype(BF16))
        y = jnp.concatenate([bbt_ref[0, h]] * T, axis=0)
        w = (x[:, :LANES] * y[:, :LANES] + x[:, LANES:] * y[:, LANES:]).astype(BF16)
        pb_s[h] = jnp.where(pb_diag, _dot(w, state_copy), 0.0).astype(BF16)

        taps = _dot(w, cc_ref[0, h].astype(BF16)) + jnp.where(tap_skip, jnp.concatenate([dl_ref[0, h]] * (HALF_W // SUBLANES), axis=0), 0.0)
        taps = jnp.where(tap_diag, taps, 0.0).astype(BF16)
        block = lambda lag: taps[(T - 1 - lag) * HALF_LANES:(T - lag) * HALF_LANES] if lag >= 0 else none
        for a in range(SUB_BLOCKS):
            for i in range(SUB_BLOCKS):
                for kk in range(SUB_BLOCKS // 2):
                    lag = SUB_BLOCKS * a + 2 * kk - i
                    wt_s[h, a, i * HALF_LANES:(i + 1) * HALF_LANES, kk * LANES:(kk + 1) * LANES] = (
                        jnp.where(even, block(lag), block(lag + 1)))

        pw = _dot(pf_ref[0, h].astype(BF16), lane_copy)
        p_re, p_im = pw[:, :HALF_W], pw[:, HALF_W:]
        c_re = jnp.concatenate([ct_ref[0, h, 0]] * (HALF_W // LANES), axis=1)
        c_im = jnp.concatenate([ct_ref[0, h, 1]] * (HALF_W // LANES), axis=1)
        for part, q in enumerate((c_re * p_re - c_im * p_im, -(c_re * p_im + c_im * p_re))):
            for g in range(HALF_GROUPS):
                r = part * HALF_STATE + g * SSM_STATE
                qm_s[h, r:r + SSM_STATE, :] = jnp.where(col_group == g, q, 0.0).astype(BF16)


def _ssm_kernel(u_ref, rp_ref, bbt_ref, cc_ref, dl_ref, pf_ref, ct_ref, a_ref, pw_ref, y_ref,
                z_s, v_s, l_s, s_s, wt_s, pb_s, qm_s):
    n_chunks = z_s.shape[0]
    seg = n_chunks // N_SEG
    half_tiles = HALF_STATE // LANES
    part_tiles = lambda part: [h * 2 * half_tiles + part * half_tiles + c for h in range(2) for c in range(half_tiles)]

    @pl.when(pl.program_id(1) == 0)
    def _():
        _ssm_spread(rp_ref, bbt_ref, cc_ref, dl_ref, pf_ref, ct_ref, wt_s, pb_s, qm_s)

    low = lax.broadcasted_iota(jnp.int32, (n_chunks, LANES), 1) < HALF_LANES

    def regroup(a, b):
        return jnp.where(low, a, pltpu.roll(b, HALF_LANES, 1)), jnp.where(low, pltpu.roll(a, HALF_LANES, 1), b)

    for m in range(CHUNK // 2):
        lo, hi = regroup(u_ref[pl.ds(2 * m, n_chunks, stride=CHUNK), :], u_ref[pl.ds(2 * m + 1, n_chunks, stride=CHUNK), :])
        z_s[:, m * LANES:(m + 1) * LANES] = lo.astype(BF16)
        z_s[:, HALF_W + m * LANES:HALF_W + (m + 1) * LANES] = hi.astype(BF16)
    for h in range(2):
        v = _dot(z_s[:, h * HALF_W:(h + 1) * HALF_W], pb_s[h])
        for c in range(2 * half_tiles):
            v_s[h * 2 * half_tiles + c] = v[:, c * LANES:(c + 1) * LANES]

    def load(ref, rows, part):
        return jnp.concatenate([ref[c, rows, :] for c in part_tiles(part)], axis=1)

    def store(ref, rows, part, val):
        for k, c in enumerate(part_tiles(part)):
            ref[c, rows, :] = val[:, k * LANES:(k + 1) * LANES]

    are, aim = a_ref[0, 0], a_ref[0, 1]

    def local_scan(i, carry):
        sre, sim = carry
        rows = pl.ds(i, N_SEG, stride=seg)
        store(l_s, rows, 0, sre)
        store(l_s, rows, 1, sim)
        return (are * sre - aim * sim + load(v_s, rows, 0), are * sim + aim * sre + load(v_s, rows, 1))

    zero = jnp.zeros((N_SEG, TILE_STATE), F32)
    ere, eim = lax.fori_loop(0, seg, local_scan, (zero, zero), unroll=5)

    bre, bim = a_ref[0, 2][:1], a_ref[0, 3][:1]
    tre, tim = [zero[:1]], [zero[:1]]
    for m in range(1, N_SEG):
        pre, pim = tre[-1], tim[-1]
        tre.append(bre * pre - bim * pim + ere[m - 1:m])
        tim.append(bre * pim + bim * pre + eim[m - 1:m])
    tre, tim = jnp.concatenate(tre, axis=0), jnp.concatenate(tim, axis=0)

    def add_carry(i, _):
        rows = pl.ds(i, N_SEG, stride=seg)
        pre, pim = pw_ref[0, 0, pl.ds(i, 1), :], pw_ref[0, 1, pl.ds(i, 1), :]
        store(l_s, rows, 0, load(l_s, rows, 0) + (pre * tre - pim * tim))
        store(l_s, rows, 1, load(l_s, rows, 1) + (pre * tim + pim * tre))
        return 0

    lax.fori_loop(0, seg, add_carry, 0, unroll=5)

    for c in range(4 * half_tiles):
        s_s[:, c * LANES:(c + 1) * LANES] = l_s[c].astype(BF16)
    for tp in range(SUB_BLOCKS):
        acc = []
        for h in range(2):
            y = _dot(s_s[:, h * 2 * HALF_STATE:(h + 1) * 2 * HALF_STATE], qm_s[h, :, tp * MXU:(tp + 1) * MXU])
            for tq in range(tp + 1):
                y = y + _dot(z_s[:, h * HALF_W + tq * MXU:h * HALF_W + (tq + 1) * MXU], wt_s[h, tp - tq])
            acc.append(y)
        for kk in range(SUB_BLOCKS // 2):
            ya, yb = regroup(acc[0][:, kk * LANES:(kk + 1) * LANES], acc[1][:, kk * LANES:(kk + 1) * LANES])
            t = SUB_BLOCKS * tp + 2 * kk
            y_ref[pl.ds(t, n_chunks, stride=CHUNK), :] = ya
            y_ref[pl.ds(t + 1, n_chunks, stride=CHUNK), :] = yb


def _ssm(u, prep, layer, bsz, seq_pad):
    n_chunks = seq_pad // CHUNK
    tile = lambda shape: pl.BlockSpec((None, 1) + shape, lambda j, b: (layer, j) + (0,) * len(shape))
    io = pl.BlockSpec((seq_pad, LANES), lambda j, b: (b, j))
    return pl.pallas_call(
        _ssm_kernel,
        grid=(N_SSM_TILES, bsz),
        in_specs=[io] + [tile(p.shape[2:]) for p in prep],
        out_specs=io,
        out_shape=jax.ShapeDtypeStruct(u.shape, F32),
        scratch_shapes=[pltpu.VMEM((n_chunks, CHUNK_W), BF16),
                        pltpu.VMEM((2 * TILE_STATE // LANES, n_chunks, LANES), F32),
                        pltpu.VMEM((2 * TILE_STATE // LANES, n_chunks, LANES), F32),
                        pltpu.VMEM((n_chunks, 2 * TILE_STATE), BF16),
                        pltpu.VMEM((2, SUB_BLOCKS, MXU, MXU), BF16),
                        pltpu.VMEM((2, HALF_W, 2 * HALF_STATE), BF16),
                        pltpu.VMEM((2, 2 * HALF_STATE, HALF_W), BF16)],
        compiler_params=_params(("arbitrary", "arbitrary")),
        name="ssm",
    )(u, *prep)


def _ssm_prep(a_re, a_im, log_dt, b_re, b_im, c_re, c_im, d_skip, seg):
    N, C, T, J, E = SSM_STATE, SSM_GROUP, CHUNK, N_SSM_TILES, TILE_GROUPS
    depth = a_re.shape[0]
    tiles = lambda m: m.reshape((depth, J, E) + m.shape[2:])
    a_re, a_im, dt = tiles(a_re), tiles(a_im), tiles(jnp.exp(log_dt))[..., None]
    lam_re, lam_im = a_re * dt, a_im * dt

    def powers(p, lam_re, lam_im):
        p = jnp.asarray(p, F32).reshape((-1,) + (1,) * (lam_re.ndim - 2))
        mag = jnp.exp(lam_re[:, :, None] * p)
        return mag * jnp.cos(lam_im[:, :, None] * p), mag * jnp.sin(lam_im[:, :, None] * p)

    pw_re, pw_im = powers(np.arange(T + 1), lam_re, lam_im)
    den = a_re * a_re + a_im * a_im
    nre, nim = pw_re[:, :, 1] - 1.0, pw_im[:, :, 1]
    coef_re = ((nre * a_re + nim * a_im) / den)[..., None, :]
    coef_im = ((nim * a_re - nre * a_im) / den)[..., None, :]
    bt_re, bt_im = tiles(b_re).swapaxes(-1, -2), tiles(b_im).swapaxes(-1, -2)
    halves = lambda m, axis: m.reshape(m.shape[:axis] + (2, HALF_GROUPS) + m.shape[axis + 1:])
    slab = lambda m: m.reshape(m.shape[:3] + (-1, m.shape[-1]))
    cat = lambda *ms: jnp.concatenate(ms, axis=-1)
    bb_re = slab(halves(coef_re * bt_re - coef_im * bt_im, 2))
    bb_im = slab(halves(coef_re * bt_im + coef_im * bt_re, 2))
    bbt = cat(bb_re, bb_re, -bb_im, bb_im)

    rev = lambda m: slab(halves(m[:, :, T - 1::-1], 3).swapaxes(2, 3))
    r_re, r_im = rev(pw_re), rev(pw_im)
    rp = cat(r_re, r_im, r_im, r_re)

    cr, ci = halves(tiles(c_re), 2), halves(tiles(c_im), 2)
    by_state = lambda m: m.reshape(depth, J, 2, N, HALF_LANES)
    crt, cit = by_state(cr.transpose(0, 1, 2, 5, 3, 4)), by_state(ci.transpose(0, 1, 2, 5, 3, 4))
    cc = jnp.concatenate([cat(crt, crt), cat(-cit, -cit)], axis=-2)
    skip = halves(tiles(d_skip), 2).reshape(depth, J, 2, 1, HALF_LANES)
    dl = jnp.broadcast_to(cat(skip, skip), (depth, J, 2, SUBLANES, LANES))

    fwd = lambda m: by_state(halves(m[:, :, 1:], 3).transpose(0, 1, 3, 5, 2, 4))
    pf = cat(fwd(pw_re), fwd(pw_im))
    ct = jnp.stack([cat(crt, crt), cat(cit, cit)], axis=3)

    flat = lambda m: m.reshape(depth, J, TILE_STATE)
    t_re, t_im = powers(T * np.array([1, seg]), flat(lam_re), flat(lam_im))
    a = jnp.stack([t_re[:, :, 0], t_im[:, :, 0], t_re[:, :, 1], t_im[:, :, 1]], axis=2)[:, :, :, None]
    a = jnp.broadcast_to(a, (depth, J, 4, SUBLANES, TILE_STATE))
    pw = jnp.stack(powers(T * np.arange(seg), flat(lam_re), flat(lam_im)), axis=2)
    return rp, bbt, cc, dl, pf, ct, a, pw


def _attn_kernel(sink_ref, bias_ref, q_ref, kd_ref, vt_ref, o_ref):
    n_keys = N_META + 2 * BLOCK
    n_local = q_ref.shape[0] // BLOCK
    first_k = lax.broadcasted_iota(jnp.int32, (n_keys, LANES), 1) < HEAD_DIM
    first_v = lax.broadcasted_iota(jnp.int32, (LANES, 3 * BLOCK), 0) < HEAD_DIM
    first_o = lax.broadcasted_iota(jnp.int32, (LANES, 2 * BLOCK), 0) < HEAD_DIM
    zero_k = jnp.zeros((n_keys, LANES), BF16)
    zero_v = jnp.zeros((LANES, 3 * BLOCK), BF16)
    pad_p = jnp.zeros((PAD, 2 * BLOCK), BF16)
    shape = (2 * SUBLANES, 6 * BLOCK)
    key_rows = jnp.where(lax.broadcasted_iota(jnp.int32, shape, 0) == lax.broadcasted_iota(jnp.int32, shape, 1) // (3 * BLOCK),
                         1.0, 0.0).astype(BF16)

    def block_of(i):
        n = pl.program_id(1) * n_local + i
        return n, jnp.maximum(n - 1, 0)

    def scores(i, kh):
        n, prev = block_of(i)
        cols = slice(kh * LANES, (kh + 1) * LANES)
        kd = jnp.concatenate([kd_ref[0, PAD:, cols], kd_ref[prev, :, cols], kd_ref[n, :, cols]], axis=0)
        kab = jnp.concatenate([jnp.where(first_k, kd, zero_k), jnp.where(first_k, zero_k, kd)], axis=0)
        q4 = jnp.concatenate([q_ref[i * BLOCK:(i + 1) * BLOCK, (2 * kh + p) * LANES:(2 * kh + p + 1) * LANES]
                              for p in range(2)], axis=0)
        return lax.dot_general(kab, q4, (((1,), (1,)), ((), ())), preferred_element_type=F32)

    chains = [(i, kh) for i in range(n_local) for kh in range(N_KV_HEADS)]
    s_next = scores(*chains[0])
    for c, (i, kh) in enumerate(chains):
        s = s_next
        if c + 1 < len(chains):
            s_next = scores(*chains[c + 1])
        n, prev = block_of(i)
        bias = bias_ref[jnp.minimum(n, 2)]
        pt, tail = [], []
        for e in range(2):
            blocks, tl = [], []
            for p in range(2):
                se = s[e * n_keys:(e + 1) * n_keys, p * LANES:(p + 1) * LANES] + bias
                sink = sink_ref[4 * kh + 2 * p + e] * LOG2_E
                m = jnp.maximum(jnp.max(se, axis=0, keepdims=True), sink)
                blocks.append(jnp.exp2(se - m).astype(BF16))
                tl.append(jnp.exp2(sink - m))
            pt += [pad_p, jnp.concatenate(blocks, axis=1)]
            tail.append(jnp.concatenate(tl, axis=1))
        rows = slice(kh * LANES, (kh + 1) * LANES)
        vd = jnp.concatenate([vt_ref[0, rows, :], vt_ref[prev, rows, :], vt_ref[n, rows, :]], axis=1)
        vab = jnp.concatenate([jnp.where(first_v, vd, zero_v), jnp.where(first_v, zero_v, vd)], axis=1)
        vab = jnp.concatenate([vab, key_rows], axis=0)
        ot = _dot(vab, jnp.concatenate(pt, axis=0))
        rden = [1.0 / (ot[LANES + e:LANES + e + 1] + tail[e]) for e in range(2)]
        o = (ot[:LANES] * jnp.where(first_o, rden[0], rden[1])).T
        for p in range(2):
            o_ref[i * BLOCK:(i + 1) * BLOCK, (2 * kh + p) * LANES:(2 * kh + p + 1) * LANES] = (
                o[p * BLOCK:(p + 1) * BLOCK].astype(BF16))


def _attention(q, kd, vt, sinks, bias, bsz, n_blk, tq):
    steps = n_blk * BLOCK // tq
    rows = pl.BlockSpec((tq, D_ATTN), lambda b, s: (b * steps + s, 0))
    return pl.pallas_call(
        _attn_kernel,
        grid=(bsz, steps),
        in_specs=[pl.BlockSpec(memory_space=pltpu.SMEM), _const_spec(bias.shape), rows,
                  pl.BlockSpec((n_blk, BLOCK, 2 * D_KV), lambda b, s: (b, 0, 0)),
                  pl.BlockSpec((n_blk, 2 * D_KV, BLOCK), lambda b, s: (b, 0, 0))],
        out_specs=rows,
        out_shape=jax.ShapeDtypeStruct(q.shape, BF16),
        compiler_params=_params(("parallel", "parallel")),
        name="attention",
    )(sinks, bias, q, kd, vt)


def _attn_bias():
    i = np.arange(BLOCK)[None, :]
    j = np.arange(BLOCK)[:, None]
    none = np.zeros((BLOCK, BLOCK), bool)
    causal = j <= i
    in_meta = np.broadcast_to(j >= PAD, (BLOCK, BLOCK))
    blk0 = [none[:N_META], none, causal & in_meta]
    blk1 = [none[:N_META], in_meta, causal]
    blk2 = [~none[:N_META], j > i, causal]
    ok = np.stack([np.concatenate(b, axis=0) for b in (blk0, blk1, blk2)])
    return jnp.asarray(np.where(ok, 0.0, NEG_INF), F32)


def _merge_kernel(rows, *refs):
    n = len(rows.specs)
    y_ref, a_ref, gs_ref, ga_ref, wglu_ref, bglu_ref, wos_ref, woa_ref, wout_ref, gain_ref, o_ref = refs[n:]
    z = jax.nn.gelu(y_ref[...])
    z = z * jax.nn.sigmoid(_dot(z.astype(BF16), wglu_ref[...].astype(BF16)) + bglu_ref[...])
    merged = (jax.nn.sigmoid(gs_ref[...].astype(F32)) * _dot(z.astype(BF16), wos_ref[...].astype(BF16))
              + jax.nn.sigmoid(ga_ref[...].astype(F32)) * _dot(a_ref[...], woa_ref[...].astype(BF16)))
    mix = _dot(merged.astype(BF16), wout_ref[...].astype(BF16))
    o_ref[...] = rows.load(refs[:n]) + _rms(mix, gain_ref[...])


def _merge(rows, layer, y, a, gs, ga, wglu, bglu, wos, woa, wout, gain, tm):
    n_rows = y.shape[0]
    row = lambda width: pl.BlockSpec((tm, width), lambda i: (i, 0))
    weights = (wglu, bglu, wos, woa, wout, gain)
    return pl.pallas_call(
        functools.partial(_merge_kernel, rows),
        grid=(n_rows // tm,),
        in_specs=rows.specs + [row(D_SSM), row(D_ATTN), row(D_MODEL), row(D_MODEL)]
                 + [_layer_spec(p, layer) for p in weights],
        out_specs=row(D_MODEL),
        out_shape=jax.ShapeDtypeStruct((n_rows, D_MODEL), F32),
        compiler_params=_params(("parallel",)),
        name="merge",
    )(*rows.operands, y, a, gs, ga, *weights)


def _ffn_kernel(rows, *refs):
    n = len(rows.specs)
    gpre_ref, wup_ref, wdown_ref, gpost_ref, o_ref = refs[n:]
    x = rows.load(refs[:n])
    h = _rms(x, gpre_ref[...]).astype(BF16)
    step = 512
    acc = jnp.zeros(x.shape, F32)
    for c in range(0, D_FF, step):
        a = jnp.maximum(_dot(h, wup_ref[:, c:c + step].astype(BF16)), 0.0)
        acc = acc + _dot((a * a).astype(BF16), wdown_ref[c:c + step, :].astype(BF16))
    o_ref[...] = x + _rms(acc, gpost_ref[...])


def _ffn(rows, n_rows, layer, gpre, wup, wdown, gpost, tm):
    weights = (gpre, wup, wdown, gpost)
    return pl.pallas_call(
        functools.partial(_ffn_kernel, rows),
        grid=(n_rows // tm,),
        in_specs=rows.specs + [_layer_spec(p, layer) for p in weights],
        out_specs=pl.BlockSpec((tm, D_MODEL), lambda i: (i, 0)),
        out_shape=jax.ShapeDtypeStruct((n_rows, D_MODEL), F32),
        compiler_params=_params(("parallel",)),
        name="ffn",
    )(*rows.operands, *weights)


def _rope_tables(n_blk):
    inv_freq = 1.0 / (ROPE_THETA ** (jnp.arange(0, HEAD_DIM, 2, dtype=F32) / HEAD_DIM))
    blk = (BLOCK * jnp.arange(n_blk, dtype=jnp.int32)).astype(F32)[:, None, None] * inv_freq
    off = (jnp.arange(BLOCK, dtype=jnp.int32) - PAD).astype(F32)[None, :, None] * inv_freq
    cos = jnp.cos(blk) * jnp.cos(off) - jnp.sin(blk) * jnp.sin(off)
    sin = jnp.sin(blk) * jnp.cos(off) + jnp.cos(blk) * jnp.sin(off)
    lanes = lambda m: jnp.tile(m.reshape(n_blk * BLOCK, HEAD_DIM // 2), (1, LANES // (HEAD_DIM // 2)))
    cos, sin = lanes(cos), lanes(sin)
    first_half = (np.arange(LANES) % HEAD_DIM) < HEAD_DIM // 2
    return cos, jnp.where(first_half, -sin, 0.0), jnp.where(first_half, 0.0, sin)


def _kv_weights(w_in):
    depth = w_in.shape[0]
    dup = lambda m: jnp.tile(m.reshape(depth, D_MODEL, N_KV_HEADS, 1, HEAD_DIM),
                             (1, 1, 1, 2, 1)).reshape(depth, D_MODEL, 2 * D_KV)
    wkd = dup(w_in[:, :, C_K:C_K + D_KV])
    wvt = dup(w_in[:, :, C_V:C_V + D_KV]).swapaxes(1, 2)
    return wkd.astype(BF16), wvt.astype(BF16)


def kernel(x, meta_tokens, norm_mix_pre, norm_mix_post, norm_mlp_pre, norm_mlp_post, w_in, ssm_a_re, ssm_a_im, ssm_log_dt, ssm_b_re, ssm_b_im, ssm_c_re, ssm_c_im, ssm_d, w_glu, b_glu, attn_sinks, w_o_ssm, w_o_attn, w_out, w_up, w_down):
    bsz, seq, _ = x.shape
    depth = w_in.shape[0]
    seq_pad = PAD + N_META + seq
    assert seq_pad % BLOCK == 0 and (seq_pad // CHUNK) % N_SEG == 0
    n_blk, n_chunks = seq_pad // BLOCK, seq_pad // CHUNK
    tm, tm_out = 640, 512
    assert seq_pad % tm == 0 and seq % tm_out == 0
    n_rows = bsz * seq_pad

    meta_block = jnp.concatenate([jnp.zeros((PAD, D_MODEL), x.dtype), meta_tokens.astype(x.dtype)], axis=0)
    stream = _Rows("input", x.reshape(bsz * seq, D_MODEL), tm, seq_pad // tm, seq // BLOCK, meta_block)
    cos, sa, sb = _rope_tables(n_blk)
    bias = _attn_bias()

    row = lambda m: m[:, None, :]
    wkd, wvt = _kv_weights(w_in)
    prep = _ssm_prep(ssm_a_re, ssm_a_im, ssm_log_dt, ssm_b_re, ssm_b_im, ssm_c_re, ssm_c_im, ssm_d, n_chunks // N_SEG)

    for l in range(depth):
        u, q, kd, vt, gs, ga = _inproj(stream, n_rows, l, row(norm_mix_pre), w_in, wkd, wvt, cos, sa, sb, tm)
        y = _ssm(u, prep, l, bsz, seq_pad)
        att = _attention(q, kd, vt, attn_sinks[l], bias, bsz, n_blk, tm)
        hres = _merge(stream, l, y, att, gs, ga, w_glu, row(b_glu), w_o_ssm, w_o_attn, w_out, row(norm_mix_post), tm)
        ffn_w = (l, row(norm_mlp_pre), w_up, w_down, row(norm_mlp_post))
        if l + 1 < depth:
            stream = _Rows("padded", _ffn(_Rows("padded", hres, tm, seq_pad // tm, n_blk), n_rows, *ffn_w, tm),
                           tm, seq_pad // tm, n_blk)
        else:
            out = _ffn(_Rows("frames", hres, tm_out, seq // tm_out, n_blk), bsz * seq, *ffn_w, tm_out)
    return out.reshape(bsz, seq, D_MODEL)
```

```python
import functools
import math

import jax
import jax.numpy as jnp
import numpy as np
from jax import lax
from jax.experimental import pallas as pl
from jax.experimental.pallas import tpu as pltpu

D_MODEL = 1024
N_META = 16
HEAD_DIM = 64
N_Q_HEADS = 16
N_KV_HEADS = 4
D_ATTN = N_Q_HEADS * HEAD_DIM
D_KV = N_KV_HEADS * HEAD_DIM
BLOCK = 128
ROPE_THETA = 10000.0
ATTN_SCALE = HEAD_DIM ** -0.5
LOG2_E = math.log2(math.e)
NEG_INF = -1e30
D_SSM = D_MODEL // 2
SSM_GROUP = 16
N_SSM_GROUPS = D_SSM // SSM_GROUP
SSM_STATE = 64
D_FF = 4 * D_MODEL
RMS_EPS = 1e-6

LANES = 128
SUBLANES = 8
MXU = 256
PAD = BLOCK - N_META
CHUNK = 16
TILE_GROUPS = LANES // SSM_GROUP
N_SSM_TILES = D_SSM // LANES
TILE_STATE = TILE_GROUPS * SSM_STATE
CHUNK_W = CHUNK * LANES
HALF_GROUPS = TILE_GROUPS // 2
HALF_LANES = LANES // 2
HALF_STATE = HALF_GROUPS * SSM_STATE
HALF_W = CHUNK * HALF_LANES
SUB_BLOCKS = MXU // HALF_LANES
N_SEG = SUBLANES
VMEM_LIMIT = 56 * 1024 * 1024

C_U = 0
C_Q = C_U + D_SSM
C_K = C_Q + D_ATTN
C_V = C_K + D_KV
C_GS = C_V + D_KV
C_GA = C_GS + D_MODEL
C_END = C_GA + D_MODEL

BF16 = jnp.bfloat16
F32 = jnp.float32


def _dot(a, b):
    return jnp.dot(a, b, preferred_element_type=F32)


def _rms(x, gain):
    return x * lax.rsqrt(jnp.mean(x * x, axis=-1, keepdims=True) + RMS_EPS) * gain


def _const_spec(shape):
    return pl.BlockSpec(shape, lambda *_: (0,) * len(shape), pipeline_mode=pl.Buffered(1))


def _layer_spec(stacked, layer):
    shape = stacked.shape[1:]
    return pl.BlockSpec((None,) + shape, lambda *_: (layer,) + (0,) * len(shape), pipeline_mode=pl.Buffered(1))


def _params(sem):
    return pltpu.CompilerParams(dimension_semantics=sem, vmem_limit_bytes=VMEM_LIMIT)


class _Rows:
    def __init__(self, kind, array, tm, tiles_per_batch, blocks_per_batch, meta_block=None):
        self.has_meta = kind == "input"
        self.tiles_per_batch = tiles_per_batch
        nb = tm // BLOCK
        if kind == "padded":
            self.specs = [pl.BlockSpec((tm, D_MODEL), lambda i: (i, 0))]
            self.operands = [array]
            return
        shift = -1 if kind == "input" else 1

        def imap(j):
            return lambda i: ((i // tiles_per_batch) * blocks_per_batch
                              + jnp.maximum((i % tiles_per_batch) * nb + j + shift, 0), 0)

        self.specs = [pl.BlockSpec((BLOCK, D_MODEL), imap(j)) for j in range(nb)]
        self.operands = [array] * nb
        if self.has_meta:
            self.specs.append(_const_spec((BLOCK, D_MODEL)))
            self.operands.append(meta_block)

    def load(self, refs):
        n = len(self.specs)
        if n == 1:
            return refs[0][...]
        blocks = [r[...] for r in refs[:n - self.has_meta]]
        if self.has_meta:
            head = pl.program_id(0) % self.tiles_per_batch == 0
            blocks[0] = jnp.where(head, refs[n - 1][...], blocks[0])
        return jnp.concatenate(blocks, axis=0)


def _inproj_kernel(rows, *refs):
    n = len(rows.specs)
    gain_ref, w_ref, wq_ref, wvt_ref, cos_ref, sa_ref, sb_ref, u_ref, q_ref, k_ref, vt_ref, gs_ref, ga_ref = refs[n:]
    h = _rms(rows.load(refs[:n]), gain_ref[...]).astype(BF16)
    cos, sa, sb = cos_ref[...], sa_ref[...], sb_ref[...]

    def rope(t):
        return t * cos + pltpu.roll(t, LANES - HEAD_DIM // 2, 1) * sa + pltpu.roll(t, HEAD_DIM // 2, 1) * sb

    step = 512
    u_ref[...] = _dot(h, w_ref[:, C_U:C_U + D_SSM].astype(BF16))
    for c in range(0, D_ATTN, step):
        t = _dot(h, wq_ref[:, c:c + step])
        for j in range(0, step, LANES):
            q_ref[:, c + j:c + j + LANES] = (rope(t[:, j:j + LANES]) * (ATTN_SCALE * LOG2_E)).astype(BF16)
    t = _dot(h, w_ref[:, C_K:C_K + D_KV].astype(BF16))
    for j in range(0, D_KV, LANES):
        kj = rope(t[:, j:j + LANES]).astype(BF16)
        for b in range(k_ref.shape[0]):
            k_ref[b, :, j:j + LANES] = kj[b * BLOCK:(b + 1) * BLOCK]
    vt = lax.dot_general(wvt_ref[...], h, (((1,), (1,)), ((), ())), preferred_element_type=F32).astype(BF16)
    for b in range(vt_ref.shape[0]):
        vt_ref[b] = vt[:, b * BLOCK:(b + 1) * BLOCK]
    for c in range(0, D_MODEL, step):
        gs_ref[:, c:c + step] = _dot(h, w_ref[:, C_GS + c:C_GS + c + step].astype(BF16)).astype(BF16)
        ga_ref[:, c:c + step] = _dot(h, w_ref[:, C_GA + c:C_GA + c + step].astype(BF16)).astype(BF16)


def _inproj(rows, n_rows, layer, gain, w, wq, wvt, cos, sa, sb, tm):
    row = lambda width: pl.BlockSpec((tm, width), lambda i: (i, 0))
    tab = pl.BlockSpec((tm, LANES), lambda i: (i % rows.tiles_per_batch, 0))
    out = lambda width, dtype: (row(width), jax.ShapeDtypeStruct((n_rows, width), dtype))
    per_block = lambda shape: (pl.BlockSpec((tm // BLOCK,) + shape, lambda i: (i, 0, 0)),
                               jax.ShapeDtypeStruct((n_rows // BLOCK,) + shape, BF16))
    outs = [out(D_SSM, F32),
            out(D_ATTN, BF16), per_block((BLOCK, D_KV)), per_block((D_KV, BLOCK)),
            out(D_MODEL, BF16), out(D_MODEL, BF16)]
    return pl.pallas_call(
        functools.partial(_inproj_kernel, rows),
        grid=(n_rows // tm,),
        in_specs=rows.specs + [_layer_spec(p, layer) for p in (gain, w, wq, wvt)] + [tab, tab, tab],
        out_specs=[o[0] for o in outs],
        out_shape=[o[1] for o in outs],
        compiler_params=_params(("parallel",)),
        name="inproj",
    )(*rows.operands, gain, w, wq, wvt, cos, sa, sb)


def _ssm_spread(rp_ref, bbt_ref, cc_ref, dl_ref, pf_ref, ct_ref, wt_s, pb_s, qm_s):
    T = CHUNK
    iota = lambda shape, axis: lax.broadcasted_iota(jnp.int32, shape, axis)
    group_of = lambda shape, axis, width: (iota(shape, axis) // width) % HALF_GROUPS
    one_hot = lambda hit: jnp.where(hit, 1.0, 0.0).astype(BF16)

    even = iota((HALF_LANES, LANES), 1) < HALF_LANES
    none = jnp.zeros((HALF_LANES, LANES), BF16)
    shape = (HALF_W, HALF_LANES)
    row_copy = one_hot(iota(shape, 0) // SSM_GROUP == iota(shape, 1))
    shape = (LANES, 2 * HALF_STATE)
    state_copy = one_hot(iota(shape, 1) % SSM_STATE + iota(shape, 1) // HALF_STATE * SSM_STATE == iota(shape, 0))
    shape = (LANES, 2 * HALF_W)
    lane_copy = one_hot(iota(shape, 1) // SSM_GROUP == iota(shape, 0))
    shape = (HALF_W, LANES)
    tap_diag = group_of(shape, 0, SSM_GROUP) == group_of(shape, 1, SSM_GROUP)
    tap_skip = tap_diag & (iota(shape, 0) // HALF_LANES == T - 1) & (iota(shape, 0) % SSM_GROUP == iota(shape, 1) % SSM_GROUP)
    shape = (HALF_W, 2 * HALF_STATE)
    pb_diag = group_of(shape, 0, SSM_GROUP) == group_of(shape, 1, SSM_STATE)
    col_group = group_of((SSM_STATE, HALF_W), 1, SSM_GROUP)

    for h in range(2):
        x = _dot(row_copy, rp_ref[0, h].astype(BF16))
        y = jnp.concatenate([bbt_ref[0, h]] * T, axis=0)
        w = (x[:, :LANES] * y[:, :LANES] + x[:, LANES:] * y[:, LANES:]).astype(BF16)
        pb_s[h] = jnp.where(pb_diag, _dot(w, state_copy), 0.0).astype(BF16)

        taps = _dot(w, cc_ref[0, h].astype(BF16)) + jnp.where(tap_skip, jnp.concatenate([dl_ref[0, h]] * (HALF_W // SUBLANES), axis=0), 0.0)
        taps = jnp.where(tap_diag, taps, 0.0).astype(BF16)
        block = lambda lag: taps[(T - 1 - lag) * HALF_LANES:(T - lag) * HALF_LANES] if lag >= 0 else none
        for a in range(SUB_BLOCKS):
            for i in range(SUB_BLOCKS):
                for kk in range(SUB_BLOCKS // 2):
                    lag = SUB_BLOCKS * a + 2 * kk - i
                    wt_s[h, a, i * HALF_LANES:(i + 1) * HALF_LANES, kk * LANES:(kk + 1) * LANES] = (
                        jnp.where(even, block(lag), block(lag + 1)))

        pw = _dot(pf_ref[0, h].astype(BF16), lane_copy)
        p_re, p_im = pw[:, :HALF_W], pw[:, HALF_W:]
        c_re = jnp.concatenate([ct_ref[0, h, 0]] * (HALF_W // LANES), axis=1)
        c_im = jnp.concatenate([ct_ref[0, h, 1]] * (HALF_W // LANES), axis=1)
        for part, q in enumerate((c_re * p_re - c_im * p_im, -(c_re * p_im + c_im * p_re))):
            for g in range(HALF_GROUPS):
                r = part * HALF_STATE + g * SSM_STATE
                qm_s[h, r:r + SSM_STATE, :] = jnp.where(col_group == g, q, 0.0).astype(BF16)


def _ssm_kernel(u_ref, rp_ref, bbt_ref, cc_ref, dl_ref, pf_ref, ct_ref, a_ref, pw_ref, y_ref,
                z_s, v_s, l_s, s_s, wt_s, pb_s, qm_s):
    n_chunks = z_s.shape[0]
    seg = n_chunks // N_SEG
    half_tiles = HALF_STATE // LANES
    part_tiles = lambda part: [h * 2 * half_tiles + part * half_tiles + c for h in range(2) for c in range(half_tiles)]

    @pl.when(pl.program_id(1) == 0)
    def _():
        _ssm_spread(rp_ref, bbt_ref, cc_ref, dl_ref, pf_ref, ct_ref, wt_s, pb_s, qm_s)

    low = lax.broadcasted_iota(jnp.int32, (n_chunks, LANES), 1) < HALF_LANES

    def regroup(a, b):
        return jnp.where(low, a, pltpu.roll(b, HALF_LANES, 1)), jnp.where(low, pltpu.roll(a, HALF_LANES, 1), b)

    for m in range(CHUNK // 2):
        lo, hi = regroup(u_ref[pl.ds(2 * m, n_chunks, stride=CHUNK), :], u_ref[pl.ds(2 * m + 1, n_chunks, stride=CHUNK), :])
        z_s[:, m * LANES:(m + 1) * LANES] = lo.astype(BF16)
        z_s[:, HALF_W + m * LANES:HALF_W + (m + 1) * LANES] = hi.astype(BF16)
    for h in range(2):
        v = _dot(z_s[:, h * HALF_W:(h + 1) * HALF_W], pb_s[h])
        for c in range(2 * half_tiles):
            v_s[h * 2 * half_tiles + c] = v[:, c * LANES:(c + 1) * LANES]

    def load(ref, rows, part):
        return jnp.concatenate([ref[c, rows, :] for c in part_tiles(part)], axis=1)

    def store(ref, rows, part, val):
        for k, c in enumerate(part_tiles(part)):
            ref[c, rows, :] = val[:, k * LANES:(k + 1) * LANES]

    are, aim = a_ref[0, 0], a_ref[0, 1]

    def local_scan(i, carry):
        sre, sim = carry
        rows = pl.ds(i, N_SEG, stride=seg)
        store(l_s, rows, 0, sre)
        store(l_s, rows, 1, sim)
        return (are * sre - aim * sim + load(v_s, rows, 0), are * sim + aim * sre + load(v_s, rows, 1))

    zero = jnp.zeros((N_SEG, TILE_STATE), F32)
    ere, eim = lax.fori_loop(0, seg, local_scan, (zero, zero), unroll=5)

    bre, bim = a_ref[0, 2][:1], a_ref[0, 3][:1]
    tre, tim = [zero[:1]], [zero[:1]]
    for m in range(1, N_SEG):
        pre, pim = tre[-1], tim[-1]
        tre.append(bre * pre - bim * pim + ere[m - 1:m])
        tim.append(bre * pim + bim * pre + eim[m - 1:m])
    tre, tim = jnp.concatenate(tre, axis=0), jnp.concatenate(tim, axis=0)

    def add_carry(i, _):
        rows = pl.ds(i, N_SEG, stride=seg)
        pre, pim = pw_ref[0, 0, pl.ds(i, 1), :], pw_ref[0, 1, pl.ds(i, 1), :]
        store(l_s, rows, 0, load(l_s, rows, 0) + (pre * tre - pim * tim))
        store(l_s, rows, 1, load(l_s, rows, 1) + (pre * tim + pim * tre))
        return 0

    lax.fori_loop(0, seg, add_carry, 0, unroll=5)

    for c in range(4 * half_tiles):
        s_s[:, c * LANES:(c + 1) * LANES] = l_s[c].astype(BF16)
    for tp in range(SUB_BLOCKS):
        acc = []
        for h in range(2):
            y = _dot(s_s[:, h * 2 * HALF_STATE:(h + 1) * 2 * HALF_STATE], qm_s[h, :, tp * MXU:(tp + 1) * MXU])
            for tq in range(tp + 1):
                y = y + _dot(z_s[:, h * HALF_W + tq * MXU:h * HALF_W + (tq + 1) * MXU], wt_s[h, tp - tq])
            acc.append(y)
        for kk in range(SUB_BLOCKS // 2):
            ya, yb = regroup(acc[0][:, kk * LANES:(kk + 1) * LANES], acc[1][:, kk * LANES:(kk + 1) * LANES])
            t = SUB_BLOCKS * tp + 2 * kk
            y_ref[pl.ds(t, n_chunks, stride=CHUNK), :] = ya
            y_ref[pl.ds(t + 1, n_chunks, stride=CHUNK), :] = yb


def _ssm(u, prep, layer, bsz, seq_pad):
    n_chunks = seq_pad // CHUNK
    tile = lambda shape: pl.BlockSpec((None, 1) + shape, lambda j, b: (layer, j) + (0,) * len(shape))
    io = pl.BlockSpec((seq_pad, LANES), lambda j, b: (b, j))
    return pl.pallas_call(
        _ssm_kernel,
        grid=(N_SSM_TILES, bsz),
        in_specs=[io] + [tile(p.shape[2:]) for p in prep],
        out_specs=io,
        out_shape=jax.ShapeDtypeStruct(u.shape, F32),
        scratch_shapes=[pltpu.VMEM((n_chunks, CHUNK_W), BF16),
                        pltpu.VMEM((2 * TILE_STATE // LANES, n_chunks, LANES), F32),
                        pltpu.VMEM((2 * TILE_STATE // LANES, n_chunks, LANES), F32),
                        pltpu.VMEM((n_chunks, 2 * TILE_STATE), BF16),
                        pltpu.VMEM((2, SUB_BLOCKS, MXU, MXU), BF16),
                        pltpu.VMEM((2, HALF_W, 2 * HALF_STATE), BF16),
                        pltpu.VMEM((2, 2 * HALF_STATE, HALF_W), BF16)],
        compiler_params=_params(("arbitrary", "arbitrary")),
        name="ssm",
    )(u, *prep)


def _ssm_prep(a_re, a_im, log_dt, b_re, b_im, c_re, c_im, d_skip, seg):
    N, C, T, J, E = SSM_STATE, SSM_GROUP, CHUNK, N_SSM_TILES, TILE_GROUPS
    depth = a_re.shape[0]
    tiles = lambda m: m.reshape((depth, J, E) + m.shape[2:])
    a_re, a_im, dt = tiles(a_re), tiles(a_im), tiles(jnp.exp(log_dt))[..., None]
    lam_re, lam_im = a_re * dt, a_im * dt

    def powers(p, lam_re, lam_im):
        p = jnp.asarray(p, F32).reshape((-1,) + (1,) * (lam_re.ndim - 2))
        mag = jnp.exp(lam_re[:, :, None] * p)
        return mag * jnp.cos(lam_im[:, :, None] * p), mag * jnp.sin(lam_im[:, :, None] * p)

    pw_re, pw_im = powers(np.arange(T + 1), lam_re, lam_im)
    den = a_re * a_re + a_im * a_im
    nre, nim = pw_re[:, :, 1] - 1.0, pw_im[:, :, 1]
    coef_re = ((nre * a_re + nim * a_im) / den)[..., None, :]
    coef_im = ((nim * a_re - nre * a_im) / den)[..., None, :]
    bt_re, bt_im = tiles(b_re).swapaxes(-1, -2), tiles(b_im).swapaxes(-1, -2)
    halves = lambda m, axis: m.reshape(m.shape[:axis] + (2, HALF_GROUPS) + m.shape[axis + 1:])
    slab = lambda m: m.reshape(m.shape[:3] + (-1, m.shape[-1]))
    cat = lambda *ms: jnp.concatenate(ms, axis=-1)
    bb_re = slab(halves(coef_re * bt_re - coef_im * bt_im, 2))
    bb_im = slab(halves(coef_re * bt_im + coef_im * bt_re, 2))
    bbt = cat(bb_re, bb_re, -bb_im, bb_im)

    rev = lambda m: slab(halves(m[:, :, T - 1::-1], 3).swapaxes(2, 3))
    r_re, r_im = rev(pw_re), rev(pw_im)
    rp = cat(r_re, r_im, r_im, r_re)

    cr, ci = halves(tiles(c_re), 2), halves(tiles(c_im), 2)
    by_state = lambda m: m.reshape(depth, J, 2, N, HALF_LANES)
    crt, cit = by_state(cr.transpose(0, 1, 2, 5, 3, 4)), by_state(ci.transpose(0, 1, 2, 5, 3, 4))
    cc = jnp.concatenate([cat(crt, crt), cat(-cit, -cit)], axis=-2)
    skip = halves(tiles(d_skip), 2).reshape(depth, J, 2, 1, HALF_LANES)
    dl = jnp.broadcast_to(cat(skip, skip), (depth, J, 2, SUBLANES, LANES))

    fwd = lambda m: by_state(halves(m[:, :, 1:], 3).transpose(0, 1, 3, 5, 2, 4))
    pf = cat(fwd(pw_re), fwd(pw_im))
    ct = jnp.stack([cat(crt, crt), cat(cit, cit)], axis=3)

    flat = lambda m: m.reshape(depth, J, TILE_STATE)
    t_re, t_im = powers(T * np.array([1, seg]), flat(lam_re), flat(lam_im))
    a = jnp.stack([t_re[:, :, 0], t_im[:, :, 0], t_re[:, :, 1], t_im[:, :, 1]], axis=2)[:, :, :, None]
    a = jnp.broadcast_to(a, (depth, J, 4, SUBLANES, TILE_STATE))
    pw = jnp.stack(powers(T * np.arange(seg), flat(lam_re), flat(lam_im)), axis=2)
    return rp, bbt, cc, dl, pf, ct, a, pw


def _attn_head_order():
    per_kv = N_Q_HEADS // N_KV_HEADS
    return [(2 * j + e) * per_kv + g for j in range(N_KV_HEADS // 2) for g in range(per_kv) for e in range(2)]


def _attn_kernel(sink_ref, bias_ref, q_ref, k_ref, vt_ref, o_ref):
    n_keys = N_META + 2 * BLOCK
    n_local = q_ref.shape[0] // BLOCK
    first_k = lax.broadcasted_iota(jnp.int32, (n_keys, LANES), 1) < HEAD_DIM
    first_v = lax.broadcasted_iota(jnp.int32, (LANES, 3 * BLOCK), 0) < HEAD_DIM
    first_o = lax.broadcasted_iota(jnp.int32, (LANES, 2 * BLOCK), 0) < HEAD_DIM
    zero_k = jnp.zeros((n_keys, LANES), BF16)
    zero_v = jnp.zeros((LANES, 3 * BLOCK), BF16)
    pad_p = jnp.zeros((PAD, 2 * BLOCK), BF16)
    shape = (2 * SUBLANES, 6 * BLOCK)
    key_rows = jnp.where(lax.broadcasted_iota(jnp.int32, shape, 0) == lax.broadcasted_iota(jnp.int32, shape, 1) // (3 * BLOCK),
                         1.0, 0.0).astype(BF16)

    def block_of(i):
        n = pl.program_id(1) * n_local + i
        return n, jnp.maximum(n - 1, 0)

    def scores(i, tp):
        n, prev = block_of(i)
        cols = slice(tp // 2 * LANES, (tp // 2 + 1) * LANES)
        k2 = jnp.concatenate([k_ref[0, PAD:, cols], k_ref[prev, :, cols], k_ref[n, :, cols]], axis=0)
        kab = jnp.concatenate([jnp.where(first_k, k2, zero_k), jnp.where(first_k, zero_k, k2)], axis=0)
        q4 = jnp.concatenate([q_ref[i * BLOCK:(i + 1) * BLOCK, (2 * tp + p) * LANES:(2 * tp + p + 1) * LANES]
                              for p in range(2)], axis=0)
        return lax.dot_general(kab, q4, (((1,), (1,)), ((), ())), preferred_element_type=F32)

    heads = _attn_head_order()
    chains = [(i, tp) for i in range(n_local) for tp in range(N_Q_HEADS // 4)]
    s_next = scores(*chains[0])
    for c, (i, tp) in enumerate(chains):
        s = s_next
        if c + 1 < len(chains):
            s_next = scores(*chains[c + 1])
        n, prev = block_of(i)
        bias = bias_ref[jnp.minimum(n, 2)]
        pt, tail = [], []
        for e in range(2):
            blocks, tl = [], []
            for p in range(2):
                se = s[e * n_keys:(e + 1) * n_keys, p * LANES:(p + 1) * LANES] + bias
                sink = sink_ref[heads[2 * (2 * tp + p) + e]] * LOG2_E
                m = jnp.maximum(jnp.max(se, axis=0, keepdims=True), sink)
                blocks.append(jnp.exp2(se - m).astype(BF16))
                tl.append(jnp.exp2(sink - m))
            pt += [pad_p, jnp.concatenate(blocks, axis=1)]
            tail.append(jnp.concatenate(tl, axis=1))
        rows = slice(tp // 2 * LANES, (tp // 2 + 1) * LANES)
        v2 = jnp.concatenate([vt_ref[0, rows, :], vt_ref[prev, rows, :], vt_ref[n, rows, :]], axis=1)
        vab = jnp.concatenate([jnp.where(first_v, v2, zero_v), jnp.where(first_v, zero_v, v2)], axis=1)
        vab = jnp.concatenate([vab, key_rows], axis=0)
        ot = _dot(vab, jnp.concatenate(pt, axis=0))
        rden = [1.0 / (ot[LANES + e:LANES + e + 1] + tail[e]) for e in range(2)]
        o = (ot[:LANES] * jnp.where(first_o, rden[0], rden[1])).T
        for p in range(2):
            o_ref[i * BLOCK:(i + 1) * BLOCK, (2 * tp + p) * LANES:(2 * tp + p + 1) * LANES] = (
                o[p * BLOCK:(p + 1) * BLOCK].astype(BF16))


def _attention(q, k, vt, sinks, bias, bsz, n_blk, tq):
    steps = n_blk * BLOCK // tq
    rows = pl.BlockSpec((tq, D_ATTN), lambda b, s: (b * steps + s, 0))
    return pl.pallas_call(
        _attn_kernel,
        grid=(bsz, steps),
        in_specs=[pl.BlockSpec(memory_space=pltpu.SMEM), _const_spec(bias.shape), rows,
                  pl.BlockSpec((n_blk, BLOCK, D_KV), lambda b, s: (b, 0, 0)),
                  pl.BlockSpec((n_blk, D_KV, BLOCK), lambda b, s: (b, 0, 0))],
        out_specs=rows,
        out_shape=jax.ShapeDtypeStruct(q.shape, BF16),
        compiler_params=_params(("parallel", "parallel")),
        name="attention",
    )(sinks, bias, q, k, vt)


def _attn_bias():
    i = np.arange(BLOCK)[None, :]
    j = np.arange(BLOCK)[:, None]
    none = np.zeros((BLOCK, BLOCK), bool)
    causal = j <= i
    in_meta = np.broadcast_to(j >= PAD, (BLOCK, BLOCK))
    blk0 = [none[:N_META], none, causal & in_meta]
    blk1 = [none[:N_META], in_meta, causal]
    blk2 = [~none[:N_META], j > i, causal]
    ok = np.stack([np.concatenate(b, axis=0) for b in (blk0, blk1, blk2)])
    return jnp.asarray(np.where(ok, 0.0, NEG_INF), F32)


def _merge_kernel(rows, *refs):
    n = len(rows.specs)
    y_ref, a_ref, gs_ref, ga_ref, wglu_ref, bglu_ref, wos_ref, woa_ref, wout_ref, gain_ref, o_ref = refs[n:]
    z = jax.nn.gelu(y_ref[...])
    z = z * jax.nn.sigmoid(_dot(z.astype(BF16), wglu_ref[...].astype(BF16)) + bglu_ref[...])
    merged = (jax.nn.sigmoid(gs_ref[...].astype(F32)) * _dot(z.astype(BF16), wos_ref[...].astype(BF16))
              + jax.nn.sigmoid(ga_ref[...].astype(F32)) * _dot(a_ref[...], woa_ref[...].astype(BF16)))
    mix = _dot(merged.astype(BF16), wout_ref[...].astype(BF16))
    o_ref[...] = rows.load(refs[:n]) + _rms(mix, gain_ref[...])


def _merge(rows, layer, y, a, gs, ga, wglu, bglu, wos, woa, wout, gain, tm):
    n_rows = y.shape[0]
    row = lambda width: pl.BlockSpec((tm, width), lambda i: (i, 0))
    weights = (wglu, bglu, wos, woa, wout, gain)
    return pl.pallas_call(
        functools.partial(_merge_kernel, rows),
        grid=(n_rows // tm,),
        in_specs=rows.specs + [row(D_SSM), row(D_ATTN), row(D_MODEL), row(D_MODEL)]
                 + [_layer_spec(p, layer) for p in weights],
        out_specs=row(D_MODEL),
        out_shape=jax.ShapeDtypeStruct((n_rows, D_MODEL), F32),
        compiler_params=_params(("parallel",)),
        name="merge",
    )(*rows.operands, y, a, gs, ga, *weights)


def _ffn_kernel(rows, *refs):
    n = len(rows.specs)
    gpre_ref, wup_ref, wdown_ref, gpost_ref, o_ref = refs[n:]
    x = rows.load(refs[:n])
    h = _rms(x, gpre_ref[...]).astype(BF16)
    step = 512
    acc = jnp.zeros(x.shape, F32)
    for c in range(0, D_FF, step):
        a = jnp.maximum(_dot(h, wup_ref[:, c:c + step].astype(BF16)), 0.0)
        acc = acc + _dot((a * a).astype(BF16), wdown_ref[c:c + step, :].astype(BF16))
    o_ref[...] = x + _rms(acc, gpost_ref[...])


def _ffn(rows, n_rows, layer, gpre, wup, wdown, gpost, tm):
    weights = (gpre, wup, wdown, gpost)
    return pl.pallas_call(
        functools.partial(_ffn_kernel, rows),
        grid=(n_rows // tm,),
        in_specs=rows.specs + [_layer_spec(p, layer) for p in weights],
        out_specs=pl.BlockSpec((tm, D_MODEL), lambda i: (i, 0)),
        out_shape=jax.ShapeDtypeStruct((n_rows, D_MODEL), F32),
        compiler_params=_params(("parallel",)),
        name="ffn",
    )(*rows.operands, *weights)


def _rope_tables(n_blk):
    inv_freq = 1.0 / (ROPE_THETA ** (jnp.arange(0, HEAD_DIM, 2, dtype=F32) / HEAD_DIM))
    blk = (BLOCK * jnp.arange(n_blk, dtype=jnp.int32)).astype(F32)[:, None, None] * inv_freq
    off = (jnp.arange(BLOCK, dtype=jnp.int32) - PAD).astype(F32)[None, :, None] * inv_freq
    cos = jnp.cos(blk) * jnp.cos(off) - jnp.sin(blk) * jnp.sin(off)
    sin = jnp.sin(blk) * jnp.cos(off) + jnp.cos(blk) * jnp.sin(off)
    lanes = lambda m: jnp.tile(m.reshape(n_blk * BLOCK, HEAD_DIM // 2), (1, LANES // (HEAD_DIM // 2)))
    cos, sin = lanes(cos), lanes(sin)
    first_half = (np.arange(LANES) % HEAD_DIM) < HEAD_DIM // 2
    return cos, jnp.where(first_half, -sin, 0.0), jnp.where(first_half, 0.0, sin)


def _attn_weights(w_in, w_o_attn):
    depth = w_in.shape[0]
    order = np.asarray(_attn_head_order())
    wq, wv = lax.optimization_barrier((w_in[:, :, C_Q:C_Q + D_ATTN].astype(BF16), w_in[:, :, C_V:C_V + D_KV].astype(BF16)))
    wq = wq.reshape(depth, D_MODEL, N_Q_HEADS, HEAD_DIM)[:, :, order].reshape(depth, D_MODEL, D_ATTN)
    woa = w_o_attn.astype(BF16).reshape(depth, N_Q_HEADS, HEAD_DIM, D_MODEL)[:, order].reshape(depth, D_ATTN, D_MODEL)
    return wq, wv.swapaxes(1, 2), woa


def kernel(x, meta_tokens, norm_mix_pre, norm_mix_post, norm_mlp_pre, norm_mlp_post, w_in, ssm_a_re, ssm_a_im, ssm_log_dt, ssm_b_re, ssm_b_im, ssm_c_re, ssm_c_im, ssm_d, w_glu, b_glu, attn_sinks, w_o_ssm, w_o_attn, w_out, w_up, w_down):
    bsz, seq, _ = x.shape
    depth = w_in.shape[0]
    seq_pad = PAD + N_META + seq
    assert seq_pad % BLOCK == 0 and (seq_pad // CHUNK) % N_SEG == 0
    n_blk, n_chunks = seq_pad // BLOCK, seq_pad // CHUNK
    tm, tm_out = 640, 512
    assert seq_pad % tm == 0 and seq % tm_out == 0
    n_rows = bsz * seq_pad

    meta_block = jnp.concatenate([jnp.zeros((PAD, D_MODEL), x.dtype), meta_tokens.astype(x.dtype)], axis=0)
    stream = _Rows("input", x.reshape(bsz * seq, D_MODEL), tm, seq_pad // tm, seq // BLOCK, meta_block)
    cos, sa, sb = _rope_tables(n_blk)
    bias = _attn_bias()

    row = lambda m: m[:, None, :]
    wq, wvt, woa = _attn_weights(w_in, w_o_attn)
    prep = _ssm_prep(ssm_a_re, ssm_a_im, ssm_log_dt, ssm_b_re, ssm_b_im, ssm_c_re, ssm_c_im, ssm_d, n_chunks // N_SEG)

    for l in range(depth):
        u, q, k, vt, gs, ga = _inproj(stream, n_rows, l, row(norm_mix_pre), w_in, wq, wvt, cos, sa, sb, tm)
        y = _ssm(u, prep, l, bsz, seq_pad)
        att = _attention(q, k, vt, attn_sinks[l], bias, bsz, n_blk, tm)
        hres = _merge(stream, l, y, att, gs, ga, w_glu, row(b_glu), w_o_ssm, woa, w_out, row(norm_mix_post), tm)
        ffn_w = (l, row(norm_mlp_pre), w_up, w_down, row(norm_mlp_post))
        if l + 1 < depth:
            stream = _Rows("padded", _ffn(_Rows("padded", hres, tm, seq_pad // tm, n_blk), n_rows, *ffn_w, tm),
                           tm, seq_pad // tm, n_blk)
        else:
            out = _ffn(_Rows("frames", hres, tm_out, seq // tm_out, n_blk), bsz * seq, *ffn_w, tm_out)
    return out.reshape(bsz, seq, D_MODEL)
```

```python
import functools
import math

import jax
import jax.numpy as jnp
import numpy as np
from jax import lax
from jax.experimental import pallas as pl
from jax.experimental.pallas import tpu as pltpu

D_MODEL = 1024
N_META = 16
HEAD_DIM = 64
N_Q_HEADS = 16
N_KV_HEADS = 4
D_ATTN = N_Q_HEADS * HEAD_DIM
D_KV = N_KV_HEADS * HEAD_DIM
BLOCK = 128
ROPE_THETA = 10000.0
ATTN_SCALE = HEAD_DIM ** -0.5
LOG2_E = math.log2(math.e)
NEG_INF = -1e30
D_SSM = D_MODEL // 2
SSM_GROUP = 16
N_SSM_GROUPS = D_SSM // SSM_GROUP
SSM_STATE = 64
D_FF = 4 * D_MODEL
RMS_EPS = 1e-6

LANES = 128
SUBLANES = 8
MXU = 256
PAD = BLOCK - N_META
CHUNK = 16
TILE_GROUPS = LANES // SSM_GROUP
N_SSM_TILES = D_SSM // LANES
TILE_STATE = TILE_GROUPS * SSM_STATE
CHUNK_W = CHUNK * LANES
HALF_GROUPS = TILE_GROUPS // 2
HALF_LANES = LANES // 2
HALF_STATE = HALF_GROUPS * SSM_STATE
HALF_W = CHUNK * HALF_LANES
SUB_BLOCKS = MXU // HALF_LANES
N_SEG = SUBLANES
VMEM_LIMIT = 56 * 1024 * 1024

C_U = 0
C_Q = C_U + D_SSM
C_K = C_Q + D_ATTN
C_V = C_K + D_KV
C_GS = C_V + D_KV
C_GA = C_GS + D_MODEL
C_END = C_GA + D_MODEL

BF16 = jnp.bfloat16
F32 = jnp.float32


def _dot(a, b):
    return jnp.dot(a, b, preferred_element_type=F32)


def _rms(x, gain):
    return x * lax.rsqrt(jnp.mean(x * x, axis=-1, keepdims=True) + RMS_EPS) * gain


def _const_spec(shape):
    return pl.BlockSpec(shape, lambda *_: (0,) * len(shape), pipeline_mode=pl.Buffered(1))


def _layer_spec(stacked, layer):
    shape = stacked.shape[1:]
    return pl.BlockSpec((None,) + shape, lambda *_: (layer,) + (0,) * len(shape), pipeline_mode=pl.Buffered(1))


def _params(sem):
    return pltpu.CompilerParams(dimension_semantics=sem, vmem_limit_bytes=VMEM_LIMIT)


class _Rows:
    def __init__(self, kind, array, tm, tiles_per_batch, blocks_per_batch, meta_block=None):
        self.has_meta = kind == "input"
        self.tiles_per_batch = tiles_per_batch
        nb = tm // BLOCK
        if kind == "padded":
            self.specs = [pl.BlockSpec((tm, D_MODEL), lambda i: (i, 0))]
            self.operands = [array]
            return
        shift = -1 if kind == "input" else 1

        def imap(j):
            return lambda i: ((i // tiles_per_batch) * blocks_per_batch
                              + jnp.maximum((i % tiles_per_batch) * nb + j + shift, 0), 0)

        self.specs = [pl.BlockSpec((BLOCK, D_MODEL), imap(j)) for j in range(nb)]
        self.operands = [array] * nb
        if self.has_meta:
            self.specs.append(_const_spec((BLOCK, D_MODEL)))
            self.operands.append(meta_block)

    def load(self, refs):
        n = len(self.specs)
        if n == 1:
            return refs[0][...]
        blocks = [r[...] for r in refs[:n - self.has_meta]]
        if self.has_meta:
            head = pl.program_id(0) % self.tiles_per_batch == 0
            blocks[0] = jnp.where(head, refs[n - 1][...], blocks[0])
        return jnp.concatenate(blocks, axis=0)


def _inproj_kernel(rows, *refs):
    n = len(rows.specs)
    gain_ref, w_ref, wq_ref, wvt_ref, blk_ref, off_ref, u_ref, q_ref, k_ref, vt_ref, gs_ref, ga_ref = refs[n:]
    h = _rms(rows.load(refs[:n]), gain_ref[...]).astype(BF16)

    cos_o, sin_o = off_ref[0], off_ref[1]
    cos, sin = [], []
    for b in range(blk_ref.shape[0]):
        cos_b, sin_b = (jnp.concatenate([blk_ref[b, p]] * (BLOCK // SUBLANES), axis=0) for p in range(2))
        cos.append(cos_b * cos_o - sin_b * sin_o)
        sin.append(sin_b * cos_o + cos_b * sin_o)
    cos, sin = jnp.concatenate(cos, axis=0), jnp.concatenate(sin, axis=0)
    first_half = lax.broadcasted_iota(jnp.int32, sin.shape, 1) % HEAD_DIM < HEAD_DIM // 2
    sa, sb = jnp.where(first_half, -sin, 0.0), jnp.where(first_half, 0.0, sin)

    def rope(t):
        return t * cos + pltpu.roll(t, LANES - HEAD_DIM // 2, 1) * sa + pltpu.roll(t, HEAD_DIM // 2, 1) * sb

    step = 512
    u_ref[...] = _dot(h, w_ref[:, C_U:C_U + D_SSM].astype(BF16))
    for c in range(0, D_ATTN, step):
        t = _dot(h, wq_ref[:, c:c + step])
        for j in range(0, step, LANES):
            q_ref[:, c + j:c + j + LANES] = (rope(t[:, j:j + LANES]) * (ATTN_SCALE * LOG2_E)).astype(BF16)
    t = _dot(h, w_ref[:, C_K:C_K + D_KV].astype(BF16))
    for j in range(0, D_KV, LANES):
        kj = rope(t[:, j:j + LANES]).astype(BF16)
        for b in range(k_ref.shape[0]):
            k_ref[b, :, j:j + LANES] = kj[b * BLOCK:(b + 1) * BLOCK]
    vt = lax.dot_general(wvt_ref[...], h, (((1,), (1,)), ((), ())), preferred_element_type=F32).astype(BF16)
    for b in range(vt_ref.shape[0]):
        vt_ref[b] = vt[:, b * BLOCK:(b + 1) * BLOCK]
    for c in range(0, D_MODEL, step):
        gs_ref[:, c:c + step] = _dot(h, w_ref[:, C_GS + c:C_GS + c + step].astype(BF16)).astype(BF16)
        ga_ref[:, c:c + step] = _dot(h, w_ref[:, C_GA + c:C_GA + c + step].astype(BF16)).astype(BF16)


def _inproj(rows, n_rows, layer, gain, w, wq, wvt, rope_blk, rope_off, tm):
    row = lambda width: pl.BlockSpec((tm, width), lambda i: (i, 0))
    blk = pl.BlockSpec((tm // BLOCK,) + rope_blk.shape[1:], lambda i: (i % rows.tiles_per_batch, 0, 0, 0))
    out = lambda width, dtype: (row(width), jax.ShapeDtypeStruct((n_rows, width), dtype))
    per_block = lambda shape: (pl.BlockSpec((tm // BLOCK,) + shape, lambda i: (i, 0, 0)),
                               jax.ShapeDtypeStruct((n_rows // BLOCK,) + shape, BF16))
    outs = [out(D_SSM, F32),
            out(D_ATTN, BF16), per_block((BLOCK, D_KV)), per_block((D_KV, BLOCK)),
            out(D_MODEL, BF16), out(D_MODEL, BF16)]
    return pl.pallas_call(
        functools.partial(_inproj_kernel, rows),
        grid=(n_rows // tm,),
        in_specs=rows.specs + [_layer_spec(p, layer) for p in (gain, w, wq, wvt)] + [blk, _const_spec(rope_off.shape)],
        out_specs=[o[0] for o in outs],
        out_shape=[o[1] for o in outs],
        compiler_params=_params(("parallel",)),
        name="inproj",
    )(*rows.operands, gain, w, wq, wvt, rope_blk, rope_off)


def _ssm_spread(rp_ref, bbt_ref, cc_ref, dl_ref, pf_ref, ct_ref, wt_s, pb_s, qm_s):
    T = CHUNK
    iota = lambda shape, axis: lax.broadcasted_iota(jnp.int32, shape, axis)
    group_of = lambda shape, axis, width: (iota(shape, axis) // width) % HALF_GROUPS
    one_hot = lambda hit: jnp.where(hit, 1.0, 0.0).astype(BF16)

    even = iota((HALF_LANES, LANES), 1) < HALF_LANES
    none = jnp.zeros((HALF_LANES, LANES), BF16)
    shape = (HALF_W, HALF_LANES)
    row_copy = one_hot(iota(shape, 0) // SSM_GROUP == iota(shape, 1))
    shape = (LANES, 2 * HALF_STATE)
    state_copy = one_hot(iota(shape, 1) % SSM_STATE + iota(shape, 1) // HALF_STATE * SSM_STATE == iota(shape, 0))
    shape = (LANES, 2 * HALF_W)
    lane_copy = one_hot(iota(shape, 1) // SSM_GROUP == iota(shape, 0))
    shape = (HALF_W, LANES)
    tap_diag = group_of(shape, 0, SSM_GROUP) == group_of(shape, 1, SSM_GROUP)
    tap_skip = tap_diag & (iota(shape, 0) // HALF_LANES == T - 1) & (iota(shape, 0) % SSM_GROUP == iota(shape, 1) % SSM_GROUP)
    shape = (HALF_W, 2 * HALF_STATE)
    pb_diag = group_of(shape, 0, SSM_GROUP) == group_of(shape, 1, SSM_STATE)
    col_group = group_of((SSM_STATE, HALF_W), 1, SSM_GROUP)

    for h in range(2):
        x = _dot(row_copy, rp_ref[0, h].astype(BF16))
        y = jnp.concatenate([bbt_ref[0, h]] * T, axis=0)
        w = (x[:, :LANES] * y[:, :LANES] + x[:, LANES:] * y[:, LANES:]).astype(BF16)
        pb_s[h] = jnp.where(pb_diag, _dot(w, state_copy), 0.0).astype(BF16)

        taps = _dot(w, cc_ref[0, h].astype(BF16)) + jnp.where(tap_skip, jnp.concatenate([dl_ref[0, h]] * (HALF_W // SUBLANES), axis=0), 0.0)
        taps = jnp.where(tap_diag, taps, 0.0).astype(BF16)
        block = lambda lag: taps[(T - 1 - lag) * HALF_LANES:(T - lag) * HALF_LANES] if lag >= 0 else none
        for a in range(SUB_BLOCKS):
            for i in range(SUB_BLOCKS):
                for kk in range(SUB_BLOCKS // 2):
                    lag = SUB_BLOCKS * a + 2 * kk - i
                    wt_s[h, a, i * HALF_LANES:(i + 1) * HALF_LANES, kk * LANES:(kk + 1) * LANES] = (
                        jnp.where(even, block(lag), block(lag + 1)))

        pw = _dot(pf_ref[0, h].astype(BF16), lane_copy)
        p_re, p_im = pw[:, :HALF_W], pw[:, HALF_W:]
        c_re = jnp.concatenate([ct_ref[0, h, 0]] * (HALF_W // LANES), axis=1)
        c_im = jnp.concatenate([ct_ref[0, h, 1]] * (HALF_W // LANES), axis=1)
        for part, q in enumerate((c_re * p_re - c_im * p_im, -(c_re * p_im + c_im * p_re))):
            for g in range(HALF_GROUPS):
                r = part * HALF_STATE + g * SSM_STATE
                qm_s[h, r:r + SSM_STATE, :] = jnp.where(col_group == g, q, 0.0).astype(BF16)


def _ssm_kernel(u_ref, rp_ref, bbt_ref, cc_ref, dl_ref, pf_ref, ct_ref, a_ref, pw_ref, y_ref,
                z_s, v_s, l_s, s_s, wt_s, pb_s, qm_s):
    n_chunks = z_s.shape[0]
    seg = n_chunks // N_SEG
    half_tiles = HALF_STATE // LANES
    part_tiles = lambda part: [h * 2 * half_tiles + part * half_tiles + c for h in range(2) for c in range(half_tiles)]

    @pl.when(pl.program_id(1) == 0)
    def _():
        _ssm_spread(rp_ref, bbt_ref, cc_ref, dl_ref, pf_ref, ct_ref, wt_s, pb_s, qm_s)

    low = lax.broadcasted_iota(jnp.int32, (n_chunks, LANES), 1) < HALF_LANES

    def regroup(a, b):
        return jnp.where(low, a, pltpu.roll(b, HALF_LANES, 1)), jnp.where(low, pltpu.roll(a, HALF_LANES, 1), b)

    for m in range(CHUNK // 2):
        lo, hi = regroup(u_ref[pl.ds(2 * m, n_chunks, stride=CHUNK), :], u_ref[pl.ds(2 * m + 1, n_chunks, stride=CHUNK), :])
        z_s[:, m * LANES:(m + 1) * LANES] = lo.astype(BF16)
        z_s[:, HALF_W + m * LANES:HALF_W + (m + 1) * LANES] = hi.astype(BF16)
    for h in range(2):
        v = _dot(z_s[:, h * HALF_W:(h + 1) * HALF_W], pb_s[h])
        for c in range(2 * half_tiles):
            v_s[h * 2 * half_tiles + c] = v[:, c * LANES:(c + 1) * LANES]

    def load(ref, rows, part):
        return jnp.concatenate([ref[c, rows, :] for c in part_tiles(part)], axis=1)

    def store(ref, rows, part, val):
        for k, c in enumerate(part_tiles(part)):
            ref[c, rows, :] = val[:, k * LANES:(k + 1) * LANES]

    are, aim = a_ref[0, 0], a_ref[0, 1]

    def local_scan(i, carry):
        sre, sim = carry
        rows = pl.ds(i, N_SEG, stride=seg)
        store(l_s, rows, 0, sre)
        store(l_s, rows, 1, sim)
        return (are * sre - aim * sim + load(v_s, rows, 0), are * sim + aim * sre + load(v_s, rows, 1))

    zero = jnp.zeros((N_SEG, TILE_STATE), F32)
    ere, eim = lax.fori_loop(0, seg, local_scan, (zero, zero), unroll=5)

    bre, bim = a_ref[0, 2][:1], a_ref[0, 3][:1]
    tre, tim = [zero[:1]], [zero[:1]]
    for m in range(1, N_SEG):
        pre, pim = tre[-1], tim[-1]
        tre.append(bre * pre - bim * pim + ere[m - 1:m])
        tim.append(bre * pim + bim * pre + eim[m - 1:m])
    tre, tim = jnp.concatenate(tre, axis=0), jnp.concatenate(tim, axis=0)

    def add_carry(i, _):
        rows = pl.ds(i, N_SEG, stride=seg)
        pre, pim = pw_ref[0, 0, pl.ds(i, 1), :], pw_ref[0, 1, pl.ds(i, 1), :]
        store(l_s, rows, 0, load(l_s, rows, 0) + (pre * tre - pim * tim))
        store(l_s, rows, 1, load(l_s, rows, 1) + (pre * tim + pim * tre))
        return 0

    lax.fori_loop(0, seg, add_carry, 0, unroll=5)

    for c in range(4 * half_tiles):
        s_s[:, c * LANES:(c + 1) * LANES] = l_s[c].astype(BF16)
    for tp in range(SUB_BLOCKS):
        acc = []
        for h in range(2):
            y = _dot(s_s[:, h * 2 * HALF_STATE:(h + 1) * 2 * HALF_STATE], qm_s[h, :, tp * MXU:(tp + 1) * MXU])
            for tq in range(tp + 1):
                y = y + _dot(z_s[:, h * HALF_W + tq * MXU:h * HALF_W + (tq + 1) * MXU], wt_s[h, tp - tq])
            acc.append(y)
        for kk in range(SUB_BLOCKS // 2):
            ya, yb = regroup(acc[0][:, kk * LANES:(kk + 1) * LANES], acc[1][:, kk * LANES:(kk + 1) * LANES])
            t = SUB_BLOCKS * tp + 2 * kk
            y_ref[pl.ds(t, n_chunks, stride=CHUNK), :] = ya
            y_ref[pl.ds(t + 1, n_chunks, stride=CHUNK), :] = yb


def _ssm(u, prep, layer, bsz, seq_pad):
    n_chunks = seq_pad // CHUNK
    tile = lambda shape: pl.BlockSpec((None, 1) + shape, lambda j, b: (layer, j) + (0,) * len(shape))
    io = pl.BlockSpec((seq_pad, LANES), lambda j, b: (b, j))
    return pl.pallas_call(
        _ssm_kernel,
        grid=(N_SSM_TILES, bsz),
        in_specs=[io] + [tile(p.shape[2:]) for p in prep],
        out_specs=io,
        out_shape=jax.ShapeDtypeStruct(u.shape, F32),
        scratch_shapes=[pltpu.VMEM((n_chunks, CHUNK_W), BF16),
                        pltpu.VMEM((2 * TILE_STATE // LANES, n_chunks, LANES), F32),
                        pltpu.VMEM((2 * TILE_STATE // LANES, n_chunks, LANES), F32),
                        pltpu.VMEM((n_chunks, 2 * TILE_STATE), BF16),
                        pltpu.VMEM((2, SUB_BLOCKS, MXU, MXU), BF16),
                        pltpu.VMEM((2, HALF_W, 2 * HALF_STATE), BF16),
                        pltpu.VMEM((2, 2 * HALF_STATE, HALF_W), BF16)],
        compiler_params=_params(("arbitrary", "arbitrary")),
        name="ssm",
    )(u, *prep)


def _ssm_prep(a_re, a_im, log_dt, b_re, b_im, c_re, c_im, d_skip, seg):
    N, C, T, J, E = SSM_STATE, SSM_GROUP, CHUNK, N_SSM_TILES, TILE_GROUPS
    depth = a_re.shape[0]
    tiles = lambda m: m.reshape((depth, J, E) + m.shape[2:])
    a_re, a_im, dt = tiles(a_re), tiles(a_im), tiles(jnp.exp(log_dt))[..., None]
    lam_re, lam_im = a_re * dt, a_im * dt

    def powers(p, lam_re, lam_im):
        p = jnp.asarray(p, F32).reshape((-1,) + (1,) * (lam_re.ndim - 2))
        mag = jnp.exp(lam_re[:, :, None] * p)
        return mag * jnp.cos(lam_im[:, :, None] * p), mag * jnp.sin(lam_im[:, :, None] * p)

    pw_re, pw_im = powers(np.arange(T + 1), lam_re, lam_im)
    den = a_re * a_re + a_im * a_im
    nre, nim = pw_re[:, :, 1] - 1.0, pw_im[:, :, 1]
    coef_re = ((nre * a_re + nim * a_im) / den)[..., None, :]
    coef_im = ((nim * a_re - nre * a_im) / den)[..., None, :]
    bt_re, bt_im = tiles(b_re).swapaxes(-1, -2), tiles(b_im).swapaxes(-1, -2)
    halves = lambda m, axis: m.reshape(m.shape[:axis] + (2, HALF_GROUPS) + m.shape[axis + 1:])
    slab = lambda m: m.reshape(m.shape[:3] + (-1, m.shape[-1]))
    cat = lambda *ms: jnp.concatenate(ms, axis=-1)
    bb_re = slab(halves(coef_re * bt_re - coef_im * bt_im, 2))
    bb_im = slab(halves(coef_re * bt_im + coef_im * bt_re, 2))
    bbt = cat(bb_re, bb_re, -bb_im, bb_im)

    rev = lambda m: slab(halves(m[:, :, T - 1::-1], 3).swapaxes(2, 3))
    r_re, r_im = rev(pw_re), rev(pw_im)
    rp = cat(r_re, r_im, r_im, r_re)

    cr, ci = halves(tiles(c_re), 2), halves(tiles(c_im), 2)
    by_state = lambda m: m.reshape(depth, J, 2, N, HALF_LANES)
    crt, cit = by_state(cr.transpose(0, 1, 2, 5, 3, 4)), by_state(ci.transpose(0, 1, 2, 5, 3, 4))
    cc = jnp.concatenate([cat(crt, crt), cat(-cit, -cit)], axis=-2)
    skip = halves(tiles(d_skip), 2).reshape(depth, J, 2, 1, HALF_LANES)
    dl = jnp.broadcast_to(cat(skip, skip), (depth, J, 2, SUBLANES, LANES))

    fwd = lambda m: by_state(halves(m[:, :, 1:], 3).transpose(0, 1, 3, 5, 2, 4))
    pf = cat(fwd(pw_re), fwd(pw_im))
    ct = jnp.stack([cat(crt, crt), cat(cit, cit)], axis=3)

    flat = lambda m: m.reshape(depth, J, TILE_STATE)
    t_re, t_im = powers(T * np.array([1, seg]), flat(lam_re), flat(lam_im))
    a = jnp.stack([t_re[:, :, 0], t_im[:, :, 0], t_re[:, :, 1], t_im[:, :, 1]], axis=2)[:, :, :, None]
    a = jnp.broadcast_to(a, (depth, J, 4, SUBLANES, TILE_STATE))
    pw = jnp.stack(powers(T * np.arange(seg), flat(lam_re), flat(lam_im)), axis=2)
    return rp, bbt, cc, dl, pf, ct, a, pw


def _attn_head_order():
    per_kv = N_Q_HEADS // N_KV_HEADS
    return [(2 * j + e) * per_kv + g for j in range(N_KV_HEADS // 2) for g in range(per_kv) for e in range(2)]


def _attn_kernel(sink_ref, bias_ref, q_ref, k_ref, vt_ref, o_ref):
    n_keys = N_META + 2 * BLOCK
    n_local = q_ref.shape[0] // BLOCK
    first_k = lax.broadcasted_iota(jnp.int32, (n_keys, LANES), 1) < HEAD_DIM
    first_v = lax.broadcasted_iota(jnp.int32, (LANES, 3 * BLOCK), 0) < HEAD_DIM
    first_o = lax.broadcasted_iota(jnp.int32, (LANES, 2 * BLOCK), 0) < HEAD_DIM
    zero_k = jnp.zeros((n_keys, LANES), BF16)
    zero_v = jnp.zeros((LANES, 3 * BLOCK), BF16)
    pad_p = jnp.zeros((PAD, 2 * BLOCK), BF16)
    shape = (2 * SUBLANES, 6 * BLOCK)
    key_rows = jnp.where(lax.broadcasted_iota(jnp.int32, shape, 0) == lax.broadcasted_iota(jnp.int32, shape, 1) // (3 * BLOCK),
                         1.0, 0.0).astype(BF16)

    def block_of(i):
        n = pl.program_id(1) * n_local + i
        return n, jnp.maximum(n - 1, 0)

    def scores(i, tp):
        n, prev = block_of(i)
        cols = slice(tp // 2 * LANES, (tp // 2 + 1) * LANES)
        k2 = jnp.concatenate([k_ref[0, PAD:, cols], k_ref[prev, :, cols], k_ref[n, :, cols]], axis=0)
        kab = jnp.concatenate([jnp.where(first_k, k2, zero_k), jnp.where(first_k, zero_k, k2)], axis=0)
        q4 = jnp.concatenate([q_ref[i * BLOCK:(i + 1) * BLOCK, (2 * tp + p) * LANES:(2 * tp + p + 1) * LANES]
                              for p in range(2)], axis=0)
        return lax.dot_general(kab, q4, (((1,), (1,)), ((), ())), preferred_element_type=F32)

    heads = _attn_head_order()
    chains = [(i, tp) for i in range(n_local) for tp in range(N_Q_HEADS // 4)]
    s_next = scores(*chains[0])
    for c, (i, tp) in enumerate(chains):
        s = s_next
        if c + 1 < len(chains):
            s_next = scores(*chains[c + 1])
        n, prev = block_of(i)
        bias = bias_ref[jnp.minimum(n, 2)]
        pt, tail = [], []
        for e in range(2):
            blocks, tl = [], []
            for p in range(2):
                se = s[e * n_keys:(e + 1) * n_keys, p * LANES:(p + 1) * LANES] + bias
                sink = sink_ref[heads[2 * (2 * tp + p) + e]] * LOG2_E
                m = jnp.maximum(jnp.max(se, axis=0, keepdims=True), sink)
                blocks.append(jnp.exp2(se - m).astype(BF16))
                tl.append(jnp.exp2(sink - m))
            pt += [pad_p, jnp.concatenate(blocks, axis=1)]
            tail.append(jnp.concatenate(tl, axis=1))
        rows = slice(tp // 2 * LANES, (tp // 2 + 1) * LANES)
        v2 = jnp.concatenate([vt_ref[0, rows, :], vt_ref[prev, rows, :], vt_ref[n, rows, :]], axis=1)
        vab = jnp.concatenate([jnp.where(first_v, v2, zero_v), jnp.where(first_v, zero_v, v2)], axis=1)
        vab = jnp.concatenate([vab, key_rows], axis=0)
        ot = _dot(vab, jnp.concatenate(pt, axis=0))
        rden = [1.0 / (ot[LANES + e:LANES + e + 1] + tail[e]) for e in range(2)]
        o = (ot[:LANES] * jnp.where(first_o, rden[0], rden[1])).T
        for p in range(2):
            o_ref[i * BLOCK:(i + 1) * BLOCK, (2 * tp + p) * LANES:(2 * tp + p + 1) * LANES] = (
                o[p * BLOCK:(p + 1) * BLOCK].astype(BF16))


def _attention(q, k, vt, sinks, bias, bsz, n_blk, tq):
    steps = n_blk * BLOCK // tq
    rows = pl.BlockSpec((tq, D_ATTN), lambda b, s: (b * steps + s, 0))
    return pl.pallas_call(
        _attn_kernel,
        grid=(bsz, steps),
        in_specs=[pl.BlockSpec(memory_space=pltpu.SMEM), _const_spec(bias.shape), rows,
                  pl.BlockSpec((n_blk, BLOCK, D_KV), lambda b, s: (b, 0, 0)),
                  pl.BlockSpec((n_blk, D_KV, BLOCK), lambda b, s: (b, 0, 0))],
        out_specs=rows,
        out_shape=jax.ShapeDtypeStruct(q.shape, BF16),
        compiler_params=_params(("parallel", "parallel")),
        name="attention",
    )(sinks, bias, q, k, vt)


def _attn_bias():
    i = np.arange(BLOCK)[None, :]
    j = np.arange(BLOCK)[:, None]
    none = np.zeros((BLOCK, BLOCK), bool)
    causal = j <= i
    in_meta = np.broadcast_to(j >= PAD, (BLOCK, BLOCK))
    blk0 = [none[:N_META], none, causal & in_meta]
    blk1 = [none[:N_META], in_meta, causal]
    blk2 = [~none[:N_META], j > i, causal]
    ok = np.stack([np.concatenate(b, axis=0) for b in (blk0, blk1, blk2)])
    return jnp.asarray(np.where(ok, 0.0, NEG_INF), F32)


def _merge_kernel(rows, *refs):
    n = len(rows.specs)
    y_ref, a_ref, gs_ref, ga_ref, wglu_ref, bglu_ref, wos_ref, woa_ref, wout_ref, gain_ref, o_ref = refs[n:]
    z = jax.nn.gelu(y_ref[...])
    z = z * jax.nn.sigmoid(_dot(z.astype(BF16), wglu_ref[...].astype(BF16)) + bglu_ref[...])
    merged = (jax.nn.sigmoid(gs_ref[...].astype(F32)) * _dot(z.astype(BF16), wos_ref[...].astype(BF16))
              + jax.nn.sigmoid(ga_ref[...].astype(F32)) * _dot(a_ref[...], woa_ref[...].astype(BF16)))
    mix = _dot(merged.astype(BF16), wout_ref[...].astype(BF16))
    o_ref[...] = rows.load(refs[:n]) + _rms(mix, gain_ref[...])


def _merge(rows, layer, y, a, gs, ga, wglu, bglu, wos, woa, wout, gain, tm):
    n_rows = y.shape[0]
    row = lambda width: pl.BlockSpec((tm, width), lambda i: (i, 0))
    weights = (wglu, bglu, wos, woa, wout, gain)
    return pl.pallas_call(
        functools.partial(_merge_kernel, rows),
        grid=(n_rows // tm,),
        in_specs=rows.specs + [row(D_SSM), row(D_ATTN), row(D_MODEL), row(D_MODEL)]
                 + [_layer_spec(p, layer) for p in weights],
        out_specs=row(D_MODEL),
        out_shape=jax.ShapeDtypeStruct((n_rows, D_MODEL), F32),
        compiler_params=_params(("parallel",)),
        name="merge",
    )(*rows.operands, y, a, gs, ga, *weights)


def _ffn_kernel(rows, *refs):
    n = len(rows.specs)
    gpre_ref, wup_ref, wdown_ref, gpost_ref, o_ref = refs[n:]
    x = rows.load(refs[:n])
    h = _rms(x, gpre_ref[...]).astype(BF16)
    step = 512
    acc = jnp.zeros(x.shape, F32)
    for c in range(0, D_FF, step):
        a = jnp.maximum(_dot(h, wup_ref[:, c:c + step].astype(BF16)), 0.0)
        acc = acc + _dot((a * a).astype(BF16), wdown_ref[c:c + step, :].astype(BF16))
    o_ref[...] = x + _rms(acc, gpost_ref[...])


def _ffn(rows, n_rows, layer, gpre, wup, wdown, gpost, tm):
    weights = (gpre, wup, wdown, gpost)
    return pl.pallas_call(
        functools.partial(_ffn_kernel, rows),
        grid=(n_rows // tm,),
        in_specs=rows.specs + [_layer_spec(p, layer) for p in weights],
        out_specs=pl.BlockSpec((tm, D_MODEL), lambda i: (i, 0)),
        out_shape=jax.ShapeDtypeStruct((n_rows, D_MODEL), F32),
        compiler_params=_params(("parallel",)),
        name="ffn",
    )(*rows.operands, *weights)


def _rope_tables(n_blk):
    inv_freq = 1.0 / (ROPE_THETA ** (jnp.arange(0, HEAD_DIM, 2, dtype=F32) / HEAD_DIM))
    inv_freq = jnp.tile(inv_freq, LANES // (HEAD_DIM // 2))
    blk = (BLOCK * jnp.arange(n_blk, dtype=jnp.int32)).astype(F32)[:, None] * inv_freq
    off = (jnp.arange(BLOCK, dtype=jnp.int32) - PAD).astype(F32)[:, None] * inv_freq
    blk = jnp.stack([jnp.cos(blk), jnp.sin(blk)], axis=1)[:, :, None]
    return jnp.broadcast_to(blk, (n_blk, 2, SUBLANES, LANES)), jnp.stack([jnp.cos(off), jnp.sin(off)])


def _attn_weights(w_in, w_o_attn):
    depth = w_in.shape[0]
    per_kv = N_Q_HEADS // N_KV_HEADS
    assert _attn_head_order() == [int(h) for h in np.arange(N_Q_HEADS).reshape(-1, 2, per_kv).swapaxes(1, 2).ravel()]
    by_head = (N_KV_HEADS // 2, 2, per_kv, HEAD_DIM)
    wq = w_in[:, :, C_Q:C_Q + D_ATTN].reshape((depth, D_MODEL) + by_head).swapaxes(3, 4)
    woa = w_o_attn.reshape((depth,) + by_head + (D_MODEL,)).swapaxes(2, 3)
    wv = lax.optimization_barrier(w_in[:, :, C_V:C_V + D_KV].astype(BF16))
    return (wq.reshape(depth, D_MODEL, D_ATTN).astype(BF16), wv.swapaxes(1, 2),
            woa.reshape(depth, D_ATTN, D_MODEL).astype(BF16))


def kernel(x, meta_tokens, norm_mix_pre, norm_mix_post, norm_mlp_pre, norm_mlp_post, w_in, ssm_a_re, ssm_a_im, ssm_log_dt, ssm_b_re, ssm_b_im, ssm_c_re, ssm_c_im, ssm_d, w_glu, b_glu, attn_sinks, w_o_ssm, w_o_attn, w_out, w_up, w_down):
    bsz, seq, _ = x.shape
    depth = w_in.shape[0]
    seq_pad = PAD + N_META + seq
    assert seq_pad % BLOCK == 0 and (seq_pad // CHUNK) % N_SEG == 0
    n_blk, n_chunks = seq_pad // BLOCK, seq_pad // CHUNK
    tm, tm_out = 640, 512
    assert seq_pad % tm == 0 and seq % tm_out == 0
    n_rows = bsz * seq_pad

    meta_block = jnp.concatenate([jnp.zeros((PAD, D_MODEL), x.dtype), meta_tokens.astype(x.dtype)], axis=0)
    stream = _Rows("input", x.reshape(bsz * seq, D_MODEL), tm, seq_pad // tm, seq // BLOCK, meta_block)
    rope_blk, rope_off = _rope_tables(n_blk)
    bias = _attn_bias()

    row = lambda m: m[:, None, :]
    wq, wvt, woa = _attn_weights(w_in, w_o_attn)
    prep = _ssm_prep(ssm_a_re, ssm_a_im, ssm_log_dt, ssm_b_re, ssm_b_im, ssm_c_re, ssm_c_im, ssm_d, n_chunks // N_SEG)

    for l in range(depth):
        u, q, k, vt, gs, ga = _inproj(stream, n_rows, l, row(norm_mix_pre), w_in, wq, wvt, rope_blk, rope_off, tm)
        y = _ssm(u, prep, l, bsz, seq_pad)
        att = _attention(q, k, vt, attn_sinks[l], bias, bsz, n_blk, tm)
        hres = _merge(stream, l, y, att, gs, ga, w_glu, row(b_glu), w_o_ssm, woa, w_out, row(norm_mix_post), tm)
        ffn_w = (l, row(norm_mlp_pre), w_up, w_down, row(norm_mlp_post))
        if l + 1 < depth:
            stream = _Rows("padded", _ffn(_Rows("padded", hres, tm, seq_pad // tm, n_blk), n_rows, *ffn_w, tm),
                           tm, seq_pad // tm, n_blk)
        else:
            out = _ffn(_Rows("frames", hres, tm_out, seq // tm_out, n_blk), bsz * seq, *ffn_w, tm_out)
    return out.reshape(bsz, seq, D_MODEL)
```

```python
import functools
import math

import jax
import jax.numpy as jnp
import numpy as np
from jax import lax
from jax.experimental import pallas as pl
from jax.experimental.pallas import tpu as pltpu

D_MODEL = 1024
N_META = 16
HEAD_DIM = 64
N_Q_HEADS = 16
N_KV_HEADS = 4
D_ATTN = N_Q_HEADS * HEAD_DIM
D_KV = N_KV_HEADS * HEAD_DIM
BLOCK = 128
ROPE_THETA = 10000.0
ATTN_SCALE = HEAD_DIM ** -0.5
LOG2_E = math.log2(math.e)
NEG_INF = -1e30
D_SSM = D_MODEL // 2
SSM_GROUP = 16
N_SSM_GROUPS = D_SSM // SSM_GROUP
SSM_STATE = 64
D_FF = 4 * D_MODEL
RMS_EPS = 1e-6

LANES = 128
SUBLANES = 8
MXU = 256
PAD = BLOCK - N_META
CHUNK = 16
TILE_GROUPS = LANES // SSM_GROUP
N_SSM_TILES = D_SSM // LANES
TILE_STATE = TILE_GROUPS * SSM_STATE
CHUNK_W = CHUNK * LANES
HALF_GROUPS = TILE_GROUPS // 2
HALF_LANES = LANES // 2
HALF_STATE = HALF_GROUPS * SSM_STATE
HALF_W = CHUNK * HALF_LANES
SUB_BLOCKS = MXU // HALF_LANES
N_SEG = SUBLANES
VMEM_LIMIT = 56 * 1024 * 1024

C_U = 0
C_Q = C_U + D_SSM
C_K = C_Q + D_ATTN
C_V = C_K + D_KV
C_GS = C_V + D_KV
C_GA = C_GS + D_MODEL
C_END = C_GA + D_MODEL

BF16 = jnp.bfloat16
F32 = jnp.float32


def _dot(a, b):
    return jnp.dot(a, b, preferred_element_type=F32)


def _rms(x, gain):
    return x * lax.rsqrt(jnp.mean(x * x, axis=-1, keepdims=True) + RMS_EPS) * gain


def _const_spec(shape):
    return pl.BlockSpec(shape, lambda *_: (0,) * len(shape), pipeline_mode=pl.Buffered(1))


def _layer_spec(stacked, layer):
    shape = stacked.shape[1:]
    return pl.BlockSpec((None,) + shape, lambda *_: (layer,) + (0,) * len(shape), pipeline_mode=pl.Buffered(1))


def _params(sem):
    return pltpu.CompilerParams(dimension_semantics=sem, vmem_limit_bytes=VMEM_LIMIT)


class _Rows:
    def __init__(self, kind, array, tm, tiles_per_batch, blocks_per_batch, meta_block=None):
        self.has_meta = kind == "input"
        self.tiles_per_batch = tiles_per_batch
        nb = tm // BLOCK
        if kind == "padded":
            self.specs = [pl.BlockSpec((tm, D_MODEL), lambda i: (i, 0))]
            self.operands = [array]
            return
        shift = -1 if kind == "input" else 1

        def imap(j):
            return lambda i: ((i // tiles_per_batch) * blocks_per_batch
                              + jnp.maximum((i % tiles_per_batch) * nb + j + shift, 0), 0)

        self.specs = [pl.BlockSpec((BLOCK, D_MODEL), imap(j)) for j in range(nb)]
        self.operands = [array] * nb
        if self.has_meta:
            self.specs.append(_const_spec((BLOCK, D_MODEL)))
            self.operands.append(meta_block)

    def load(self, refs):
        n = len(self.specs)
        if n == 1:
            return refs[0][...]
        blocks = [r[...] for r in refs[:n - self.has_meta]]
        if self.has_meta:
            head = pl.program_id(0) % self.tiles_per_batch == 0
            blocks[0] = jnp.where(head, refs[n - 1][...], blocks[0])
        return jnp.concatenate(blocks, axis=0)


def _inproj_kernel(rows, *refs):
    n = len(rows.specs)
    gain_ref, w_ref, wq_ref, wvt_ref, blk_ref, off_ref, u_ref, q_ref, k_ref, vt_ref, gs_ref, ga_ref = refs[n:]
    h = _rms(rows.load(refs[:n]), gain_ref[...]).astype(BF16)

    cos_o, sin_o = off_ref[0], off_ref[1]
    cos, sin = [], []
    for b in range(blk_ref.shape[0]):
        cos_b, sin_b = (jnp.concatenate([blk_ref[b, p]] * (BLOCK // SUBLANES), axis=0) for p in range(2))
        cos.append(cos_b * cos_o - sin_b * sin_o)
        sin.append(sin_b * cos_o + cos_b * sin_o)
    cos, sin = jnp.concatenate(cos, axis=0), jnp.concatenate(sin, axis=0)
    first_half = lax.broadcasted_iota(jnp.int32, sin.shape, 1) % HEAD_DIM < HEAD_DIM // 2
    sa, sb = jnp.where(first_half, -sin, 0.0), jnp.where(first_half, 0.0, sin)

    def rope(t):
        return t * cos + pltpu.roll(t, LANES - HEAD_DIM // 2, 1) * sa + pltpu.roll(t, HEAD_DIM // 2, 1) * sb

    step = 512
    u_ref[...] = _dot(h, w_ref[:, C_U:C_U + D_SSM].astype(BF16))
    for c in range(0, D_ATTN, step):
        t = _dot(h, wq_ref[:, c:c + step])
        for j in range(0, step, LANES):
            q_ref[:, c + j:c + j + LANES] = (rope(t[:, j:j + LANES]) * (ATTN_SCALE * LOG2_E)).astype(BF16)
    t = _dot(h, w_ref[:, C_K:C_K + D_KV].astype(BF16))
    for j in range(0, D_KV, LANES):
        kj = rope(t[:, j:j + LANES]).astype(BF16)
        for b in range(k_ref.shape[0]):
            k_ref[b, :, j:j + LANES] = kj[b * BLOCK:(b + 1) * BLOCK]
    vt = lax.dot_general(wvt_ref[...], h, (((1,), (1,)), ((), ())), preferred_element_type=F32).astype(BF16)
    for b in range(vt_ref.shape[0]):
        vt_ref[b] = vt[:, b * BLOCK:(b + 1) * BLOCK]
    for c in range(0, D_MODEL, step):
        gs_ref[:, c:c + step] = _dot(h, w_ref[:, C_GS + c:C_GS + c + step].astype(BF16)).astype(BF16)
        ga_ref[:, c:c + step] = _dot(h, w_ref[:, C_GA + c:C_GA + c + step].astype(BF16)).astype(BF16)


def _inproj(rows, n_rows, layer, gain, w, wq, wvt, rope_blk, rope_off, tm):
    row = lambda width: pl.BlockSpec((tm, width), lambda i: (i, 0))
    blk = pl.BlockSpec((tm // BLOCK,) + rope_blk.shape[1:], lambda i: (i % rows.tiles_per_batch, 0, 0, 0))
    out = lambda width, dtype: (row(width), jax.ShapeDtypeStruct((n_rows, width), dtype))
    per_block = lambda shape: (pl.BlockSpec((tm // BLOCK,) + shape, lambda i: (i, 0, 0)),
                               jax.ShapeDtypeStruct((n_rows // BLOCK,) + shape, BF16))
    outs = [out(D_SSM, F32),
            out(D_ATTN, BF16), per_block((BLOCK, D_KV)), per_block((D_KV, BLOCK)),
            out(D_MODEL, BF16), out(D_MODEL, BF16)]
    return pl.pallas_call(
        functools.partial(_inproj_kernel, rows),
        grid=(n_rows // tm,),
        in_specs=rows.specs + [_layer_spec(p, layer) for p in (gain, w, wq, wvt)] + [blk, _const_spec(rope_off.shape)],
        out_specs=[o[0] for o in outs],
        out_shape=[o[1] for o in outs],
        compiler_params=_params(("parallel",)),
        name="inproj",
    )(*rows.operands, gain, w, wq, wvt, rope_blk, rope_off)


def _ssm_spread(rp_ref, bbt_ref, cc_ref, dl_ref, pf_ref, ct_ref, wt_s, pb_s, qm_s):
    T = CHUNK
    iota = lambda shape, axis: lax.broadcasted_iota(jnp.int32, shape, axis)
    group_of = lambda shape, axis, width: (iota(shape, axis) // width) % HALF_GROUPS
    one_hot = lambda hit: jnp.where(hit, 1.0, 0.0).astype(BF16)

    even = iota((HALF_LANES, LANES), 1) < HALF_LANES
    none = jnp.zeros((HALF_LANES, LANES), BF16)
    shape = (HALF_W, HALF_LANES)
    row_copy = one_hot(iota(shape, 0) // SSM_GROUP == iota(shape, 1))
    shape = (LANES, 2 * HALF_STATE)
    state_copy = one_hot(iota(shape, 1) % SSM_STATE + iota(shape, 1) // HALF_STATE * SSM_STATE == iota(shape, 0))
    shape = (LANES, 2 * HALF_W)
    lane_copy = one_hot(iota(shape, 1) // SSM_GROUP == iota(shape, 0))
    shape = (HALF_W, LANES)
    tap_diag = group_of(shape, 0, SSM_GROUP) == group_of(shape, 1, SSM_GROUP)
    tap_skip = tap_diag & (iota(shape, 0) // HALF_LANES == T - 1) & (iota(shape, 0) % SSM_GROUP == iota(shape, 1) % SSM_GROUP)
    shape = (HALF_W, 2 * HALF_STATE)
    pb_diag = group_of(shape, 0, SSM_GROUP) == group_of(shape, 1, SSM_STATE)
    col_group = group_of((SSM_STATE, HALF_W), 1, SSM_GROUP)

    for h in range(2):
        x = _dot(row_copy, rp_ref[0, h].astype(BF16))
        y = jnp.concatenate([bbt_ref[0, h]] * T, axis=0)
        w = (x[:, :LANES] * y[:, :LANES] + x[:, LANES:] * y[:, LANES:]).astype(BF16)
        pb_s[h] = jnp.where(pb_diag, _dot(w, state_copy), 0.0).astype(BF16)

        taps = _dot(w, cc_ref[0, h].astype(BF16)) + jnp.where(tap_skip, jnp.concatenate([dl_ref[0, h]] * (HALF_W // SUBLANES), axis=0), 0.0)
        taps = jnp.where(tap_diag, taps, 0.0).astype(BF16)
        block = lambda lag: taps[(T - 1 - lag) * HALF_LANES:(T - lag) * HALF_LANES] if lag >= 0 else none
        for a in range(SUB_BLOCKS):
            for i in range(SUB_BLOCKS):
                for kk in range(SUB_BLOCKS // 2):
                    lag = SUB_BLOCKS * a + 2 * kk - i
                    wt_s[h, a, i * HALF_LANES:(i + 1) * HALF_LANES, kk * LANES:(kk + 1) * LANES] = (
                        jnp.where(even, block(lag), block(lag + 1)))

        pw = _dot(pf_ref[0, h].astype(BF16), lane_copy)
        p_re, p_im = pw[:, :HALF_W], pw[:, HALF_W:]
        c_re = jnp.concatenate([ct_ref[0, h, 0]] * (HALF_W // LANES), axis=1)
        c_im = jnp.concatenate([ct_ref[0, h, 1]] * (HALF_W // LANES), axis=1)
        for part, q in enumerate((c_re * p_re - c_im * p_im, -(c_re * p_im + c_im * p_re))):
            for g in range(HALF_GROUPS):
                r = part * HALF_STATE + g * SSM_STATE
                qm_s[h, r:r + SSM_STATE, :] = jnp.where(col_group == g, q, 0.0).astype(BF16)


def _ssm_kernel(u_ref, rp_ref, bbt_ref, cc_ref, dl_ref, pf_ref, ct_ref, a_ref, pw_ref, y_ref,
                z_s, v_s, l_s, s_s, wt_s, pb_s, qm_s):
    n_chunks = z_s.shape[0]
    seg = n_chunks // N_SEG
    half_tiles = HALF_STATE // LANES
    part_tiles = lambda part: [h * 2 * half_tiles + part * half_tiles + c for h in range(2) for c in range(half_tiles)]

    @pl.when(pl.program_id(1) == 0)
    def _():
        _ssm_spread(rp_ref, bbt_ref, cc_ref, dl_ref, pf_ref, ct_ref, wt_s, pb_s, qm_s)

    low = lax.broadcasted_iota(jnp.int32, (n_chunks, LANES), 1) < HALF_LANES

    def regroup(a, b):
        return jnp.where(low, a, pltpu.roll(b, HALF_LANES, 1)), jnp.where(low, pltpu.roll(a, HALF_LANES, 1), b)

    for m in range(CHUNK // 2):
        lo, hi = regroup(u_ref[pl.ds(2 * m, n_chunks, stride=CHUNK), :], u_ref[pl.ds(2 * m + 1, n_chunks, stride=CHUNK), :])
        z_s[:, m * LANES:(m + 1) * LANES] = lo.astype(BF16)
        z_s[:, HALF_W + m * LANES:HALF_W + (m + 1) * LANES] = hi.astype(BF16)
    for h in range(2):
        v = _dot(z_s[:, h * HALF_W:(h + 1) * HALF_W], pb_s[h])
        for c in range(2 * half_tiles):
            v_s[h * 2 * half_tiles + c] = v[:, c * LANES:(c + 1) * LANES]

    def load(ref, rows, part):
        return jnp.concatenate([ref[c, rows, :] for c in part_tiles(part)], axis=1)

    def store(ref, rows, part, val):
        for k, c in enumerate(part_tiles(part)):
            ref[c, rows, :] = val[:, k * LANES:(k + 1) * LANES]

    are, aim = a_ref[0, 0], a_ref[0, 1]

    def local_scan(i, carry):
        sre, sim = carry
        rows = pl.ds(i, N_SEG, stride=seg)
        store(l_s, rows, 0, sre)
        store(l_s, rows, 1, sim)
        return (are * sre - aim * sim + load(v_s, rows, 0), are * sim + aim * sre + load(v_s, rows, 1))

    zero = jnp.zeros((N_SEG, TILE_STATE), F32)
    ere, eim = lax.fori_loop(0, seg, local_scan, (zero, zero), unroll=5)

    bre, bim = a_ref[0, 2][:1], a_ref[0, 3][:1]
    tre, tim = [zero[:1]], [zero[:1]]
    for m in range(1, N_SEG):
        pre, pim = tre[-1], tim[-1]
        tre.append(bre * pre - bim * pim + ere[m - 1:m])
        tim.append(bre * pim + bim * pre + eim[m - 1:m])
    tre, tim = jnp.concatenate(tre, axis=0), jnp.concatenate(tim, axis=0)

    def add_carry(i, _):
        rows = pl.ds(i, N_SEG, stride=seg)
        pre, pim = pw_ref[0, 0, pl.ds(i, 1), :], pw_ref[0, 1, pl.ds(i, 1), :]
        store(l_s, rows, 0, load(l_s, rows, 0) + (pre * tre - pim * tim))
        store(l_s, rows, 1, load(l_s, rows, 1) + (pre * tim + pim * tre))
        return 0

    lax.fori_loop(0, seg, add_carry, 0, unroll=5)

    for c in range(4 * half_tiles):
        s_s[:, c * LANES:(c + 1) * LANES] = l_s[c].astype(BF16)
    for tp in range(SUB_BLOCKS):
        acc = []
        for h in range(2):
            y = _dot(s_s[:, h * 2 * HALF_STATE:(h + 1) * 2 * HALF_STATE], qm_s[h, :, tp * MXU:(tp + 1) * MXU])
            for tq in range(tp + 1):
                y = y + _dot(z_s[:, h * HALF_W + tq * MXU:h * HALF_W + (tq + 1) * MXU], wt_s[h, tp - tq])
            acc.append(y)
        for kk in range(SUB_BLOCKS // 2):
            ya, yb = regroup(acc[0][:, kk * LANES:(kk + 1) * LANES], acc[1][:, kk * LANES:(kk + 1) * LANES])
            t = SUB_BLOCKS * tp + 2 * kk
            y_ref[pl.ds(t, n_chunks, stride=CHUNK), :] = ya
            y_ref[pl.ds(t + 1, n_chunks, stride=CHUNK), :] = yb


def _ssm(u, prep, layer, bsz, seq_pad):
    n_chunks = seq_pad // CHUNK
    tile = lambda shape: pl.BlockSpec((None, 1) + shape, lambda j, b: (layer, j) + (0,) * len(shape))
    io = pl.BlockSpec((seq_pad, LANES), lambda j, b: (b, j))
    return pl.pallas_call(
        _ssm_kernel,
        grid=(N_SSM_TILES, bsz),
        in_specs=[io] + [tile(p.shape[2:]) for p in prep],
        out_specs=io,
        out_shape=jax.ShapeDtypeStruct(u.shape, F32),
        scratch_shapes=[pltpu.VMEM((n_chunks, CHUNK_W), BF16),
                        pltpu.VMEM((2 * TILE_STATE // LANES, n_chunks, LANES), F32),
                        pltpu.VMEM((2 * TILE_STATE // LANES, n_chunks, LANES), F32),
                        pltpu.VMEM((n_chunks, 2 * TILE_STATE), BF16),
                        pltpu.VMEM((2, SUB_BLOCKS, MXU, MXU), BF16),
                        pltpu.VMEM((2, HALF_W, 2 * HALF_STATE), BF16),
                        pltpu.VMEM((2, 2 * HALF_STATE, HALF_W), BF16)],
        compiler_params=_params(("arbitrary", "arbitrary")),
        name="ssm",
    )(u, *prep)


def _ssm_prep(a_re, a_im, log_dt, b_re, b_im, c_re, c_im, d_skip, seg):
    N, C, T, J, E = SSM_STATE, SSM_GROUP, CHUNK, N_SSM_TILES, TILE_GROUPS
    depth = a_re.shape[0]
    tiles = lambda m: m.reshape((depth, J, E) + m.shape[2:])
    a_re, a_im, dt = tiles(a_re), tiles(a_im), tiles(jnp.exp(log_dt))[..., None]
    lam_re, lam_im = a_re * dt, a_im * dt

    def powers(p, lam_re, lam_im):
        p = jnp.asarray(p, F32).reshape((-1,) + (1,) * (lam_re.ndim - 2))
        mag = jnp.exp(lam_re[:, :, None] * p)
        return mag * jnp.cos(lam_im[:, :, None] * p), mag * jnp.sin(lam_im[:, :, None] * p)

    pw_re, pw_im = powers(np.arange(T + 1), lam_re, lam_im)
    den = a_re * a_re + a_im * a_im
    nre, nim = pw_re[:, :, 1] - 1.0, pw_im[:, :, 1]
    coef_re = ((nre * a_re + nim * a_im) / den)[..., None, :]
    coef_im = ((nim * a_re - nre * a_im) / den)[..., None, :]
    bt_re, bt_im = tiles(b_re).swapaxes(-1, -2), tiles(b_im).swapaxes(-1, -2)
    halves = lambda m, axis: m.reshape(m.shape[:axis] + (2, HALF_GROUPS) + m.shape[axis + 1:])
    slab = lambda m: m.reshape(m.shape[:3] + (-1, m.shape[-1]))
    cat = lambda *ms: jnp.concatenate(ms, axis=-1)
    bb_re = slab(halves(coef_re * bt_re - coef_im * bt_im, 2))
    bb_im = slab(halves(coef_re * bt_im + coef_im * bt_re, 2))
    bbt = cat(bb_re, bb_re, -bb_im, bb_im)

    rev = lambda m: slab(halves(m[:, :, T - 1::-1], 3).swapaxes(2, 3))
    r_re, r_im = rev(pw_re), rev(pw_im)
    rp = cat(r_re, r_im, r_im, r_re)

    cr, ci = halves(tiles(c_re), 2), halves(tiles(c_im), 2)
    by_state = lambda m: m.reshape(depth, J, 2, N, HALF_LANES)
    crt, cit = by_state(cr.transpose(0, 1, 2, 5, 3, 4)), by_state(ci.transpose(0, 1, 2, 5, 3, 4))
    cc = jnp.concatenate([cat(crt, crt), cat(-cit, -cit)], axis=-2)
    skip = halves(tiles(d_skip), 2).reshape(depth, J, 2, 1, HALF_LANES)
    dl = jnp.broadcast_to(cat(skip, skip), (depth, J, 2, SUBLANES, LANES))

    fwd = lambda m: by_state(halves(m[:, :, 1:], 3).transpose(0, 1, 3, 5, 2, 4))
    pf = cat(fwd(pw_re), fwd(pw_im))
    ct = jnp.stack([cat(crt, crt), cat(cit, cit)], axis=3)

    flat = lambda m: m.reshape(depth, J, TILE_STATE)
    t_re, t_im = powers(T * np.array([1, seg]), flat(lam_re), flat(lam_im))
    a = jnp.stack([t_re[:, :, 0], t_im[:, :, 0], t_re[:, :, 1], t_im[:, :, 1]], axis=2)[:, :, :, None]
    a = jnp.broadcast_to(a, (depth, J, 4, SUBLANES, TILE_STATE))
    pw = jnp.stack(powers(T * np.arange(seg), flat(lam_re), flat(lam_im)), axis=2)
    return rp, bbt, cc, dl, pf, ct, a, pw


def _attn_head_order():
    per_kv = N_Q_HEADS // N_KV_HEADS
    return [(2 * j + e) * per_kv + g for j in range(N_KV_HEADS // 2) for g in range(per_kv) for e in range(2)]


def _attn_kernel(sink_ref, bias_ref, q_ref, k_ref, vt_ref, o_ref):
    n_keys = N_META + 2 * BLOCK
    n_local = q_ref.shape[0] // BLOCK
    first_k = lax.broadcasted_iota(jnp.int32, (n_keys, LANES), 1) < HEAD_DIM
    first_v = lax.broadcasted_iota(jnp.int32, (LANES, 3 * BLOCK), 0) < HEAD_DIM
    first_o = lax.broadcasted_iota(jnp.int32, (LANES, 2 * BLOCK), 0) < HEAD_DIM
    zero_k = jnp.zeros((n_keys, LANES), BF16)
    zero_v = jnp.zeros((LANES, 3 * BLOCK), BF16)
    pad_p = jnp.zeros((PAD, 2 * BLOCK), BF16)
    shape = (2 * SUBLANES, 6 * BLOCK)
    key_rows = jnp.where(lax.broadcasted_iota(jnp.int32, shape, 0) == lax.broadcasted_iota(jnp.int32, shape, 1) // (3 * BLOCK),
                         1.0, 0.0).astype(BF16)

    def block_of(i):
        n = pl.program_id(1) * n_local + i
        return n, jnp.maximum(n - 1, 0)

    def scores(i, tp):
        n, prev = block_of(i)
        cols = slice(tp // 2 * LANES, (tp // 2 + 1) * LANES)
        k2 = jnp.concatenate([k_ref[0, PAD:, cols], k_ref[prev, :, cols], k_ref[n, :, cols]], axis=0)
        kab = jnp.concatenate([jnp.where(first_k, k2, zero_k), jnp.where(first_k, zero_k, k2)], axis=0)
        q4 = jnp.concatenate([q_ref[i * BLOCK:(i + 1) * BLOCK, (2 * tp + p) * LANES:(2 * tp + p + 1) * LANES]
                              for p in range(2)], axis=0)
        return lax.dot_general(kab, q4, (((1,), (1,)), ((), ())), preferred_element_type=F32)

    heads = _attn_head_order()
    chains = [(i, tp) for i in range(n_local) for tp in range(N_Q_HEADS // 4)]
    s_next = scores(*chains[0])
    for c, (i, tp) in enumerate(chains):
        s = s_next
        if c + 1 < len(chains):
            s_next = scores(*chains[c + 1])
        n, prev = block_of(i)
        bias = bias_ref[jnp.minimum(n, 2)]
        pt, tail = [], []
        for e in range(2):
            blocks, tl = [], []
            for p in range(2):
                se = s[e * n_keys:(e + 1) * n_keys, p * LANES:(p + 1) * LANES] + bias
                sink = sink_ref[heads[2 * (2 * tp + p) + e]] * LOG2_E
                m = jnp.maximum(jnp.max(se, axis=0, keepdims=True), sink)
                blocks.append(jnp.exp2(se - m).astype(BF16))
                tl.append(jnp.exp2(sink - m))
            pt += [pad_p, jnp.concatenate(blocks, axis=1)]
            tail.append(jnp.concatenate(tl, axis=1))
        rows = slice(tp // 2 * LANES, (tp // 2 + 1) * LANES)
        v2 = jnp.concatenate([vt_ref[0, rows, :], vt_ref[prev, rows, :], vt_ref[n, rows, :]], axis=1)
        vab = jnp.concatenate([jnp.where(first_v, v2, zero_v), jnp.where(first_v, zero_v, v2)], axis=1)
        vab = jnp.concatenate([vab, key_rows], axis=0)
        ot = _dot(vab, jnp.concatenate(pt, axis=0))
        rden = [1.0 / (ot[LANES + e:LANES + e + 1] + tail[e]) for e in range(2)]
        o = (ot[:LANES] * jnp.where(first_o, rden[0], rden[1])).T
        for p in range(2):
            o_ref[i * BLOCK:(i + 1) * BLOCK, (2 * tp + p) * LANES:(2 * tp + p + 1) * LANES] = (
                o[p * BLOCK:(p + 1) * BLOCK].astype(BF16))


def _attention(q, k, vt, sinks, bias, bsz, n_blk, tq):
    steps = n_blk * BLOCK // tq
    rows = pl.BlockSpec((tq, D_ATTN), lambda b, s: (b * steps + s, 0))
    return pl.pallas_call(
        _attn_kernel,
        grid=(bsz, steps),
        in_specs=[pl.BlockSpec(memory_space=pltpu.SMEM), _const_spec(bias.shape), rows,
                  pl.BlockSpec((n_blk, BLOCK, D_KV), lambda b, s: (b, 0, 0)),
                  pl.BlockSpec((n_blk, D_KV, BLOCK), lambda b, s: (b, 0, 0))],
        out_specs=rows,
        out_shape=jax.ShapeDtypeStruct(q.shape, BF16),
        compiler_params=_params(("parallel", "parallel")),
        name="attention",
    )(sinks, bias, q, k, vt)


def _attn_bias():
    i = np.arange(BLOCK)[None, :]
    j = np.arange(BLOCK)[:, None]
    none = np.zeros((BLOCK, BLOCK), bool)
    causal = j <= i
    in_meta = np.broadcast_to(j >= PAD, (BLOCK, BLOCK))
    blk0 = [none[:N_META], none, causal & in_meta]
    blk1 = [none[:N_META], in_meta, causal]
    blk2 = [~none[:N_META], j > i, causal]
    ok = np.stack([np.concatenate(b, axis=0) for b in (blk0, blk1, blk2)])
    return jnp.asarray(np.where(ok, 0.0, NEG_INF), F32)


def _merge_kernel(rows, *refs):
    n = len(rows.specs)
    y_ref, a_ref, gs_ref, ga_ref, wglu_ref, bglu_ref, wos_ref, woa_ref, wout_ref, gain_ref, o_ref = refs[n:]
    z = jax.nn.gelu(y_ref[...])
    z = z * jax.nn.sigmoid(_dot(z.astype(BF16), wglu_ref[...].astype(BF16)) + bglu_ref[...])
    merged = (jax.nn.sigmoid(gs_ref[...].astype(F32)) * _dot(z.astype(BF16), wos_ref[...].astype(BF16))
              + jax.nn.sigmoid(ga_ref[...].astype(F32)) * _dot(a_ref[...], woa_ref[...].astype(BF16)))
    mix = _dot(merged.astype(BF16), wout_ref[...].astype(BF16))
    o_ref[...] = rows.load(refs[:n]) + _rms(mix, gain_ref[...])


def _merge(rows, layer, y, a, gs, ga, wglu, bglu, wos, woa, wout, gain, tm):
    n_rows = y.shape[0]
    row = lambda width: pl.BlockSpec((tm, width), lambda i: (i, 0))
    weights = (wglu, bglu, wos, woa, wout, gain)
    return pl.pallas_call(
        functools.partial(_merge_kernel, rows),
        grid=(n_rows // tm,),
        in_specs=rows.specs + [row(D_SSM), row(D_ATTN), row(D_MODEL), row(D_MODEL)]
                 + [_layer_spec(p, layer) for p in weights],
        out_specs=row(D_MODEL),
        out_shape=jax.ShapeDtypeStruct((n_rows, D_MODEL), F32),
        compiler_params=_params(("parallel",)),
        name="merge",
    )(*rows.operands, y, a, gs, ga, *weights)


def _ffn_kernel(rows, *refs):
    n = len(rows.specs)
    gpre_ref, wup_ref, wdown_ref, gpost_ref, o_ref = refs[n:]
    x = rows.load(refs[:n])
    h = _rms(x, gpre_ref[...]).astype(BF16)
    step = 512
    acc = jnp.zeros(x.shape, F32)
    for c in range(0, D_FF, step):
        a = jnp.maximum(_dot(h, wup_ref[:, c:c + step].astype(BF16)), 0.0)
        acc = acc + _dot((a * a).astype(BF16), wdown_ref[c:c + step, :].astype(BF16))
    o_ref[...] = x + _rms(acc, gpost_ref[...])


def _ffn(rows, n_rows, layer, gpre, wup, wdown, gpost, tm):
    weights = (gpre, wup, wdown, gpost)
    return pl.pallas_call(
        functools.partial(_ffn_kernel, rows),
        grid=(n_rows // tm,),
        in_specs=rows.specs + [_layer_spec(p, layer) for p in weights],
        out_specs=pl.BlockSpec((tm, D_MODEL), lambda i: (i, 0)),
        out_shape=jax.ShapeDtypeStruct((n_rows, D_MODEL), F32),
        compiler_params=_params(("parallel",)),
        name="ffn",
    )(*rows.operands, *weights)


def _rope_tables(n_blk):
    inv_freq = 1.0 / (ROPE_THETA ** (jnp.arange(0, HEAD_DIM, 2, dtype=F32) / HEAD_DIM))
    inv_freq = jnp.tile(inv_freq, LANES // (HEAD_DIM // 2))
    blk = (BLOCK * jnp.arange(n_blk, dtype=jnp.int32)).astype(F32)[:, None] * inv_freq
    off = (jnp.arange(BLOCK, dtype=jnp.int32) - PAD).astype(F32)[:, None] * inv_freq
    blk = jnp.stack([jnp.cos(blk), jnp.sin(blk)], axis=1)[:, :, None]
    return jnp.broadcast_to(blk, (n_blk, 2, SUBLANES, LANES)), jnp.stack([jnp.cos(off), jnp.sin(off)])


def _attn_weights(w_in, w_o_attn):
    depth = w_in.shape[0]
    per_kv = N_Q_HEADS // N_KV_HEADS
    assert _attn_head_order() == [int(h) for h in np.arange(N_Q_HEADS).reshape(-1, 2, per_kv).swapaxes(1, 2).ravel()]
    by_head = (N_KV_HEADS // 2, 2, per_kv, HEAD_DIM)
    wq = w_in[:, :, C_Q:C_Q + D_ATTN].reshape((depth, D_MODEL) + by_head).swapaxes(3, 4)
    woa = w_o_attn.reshape((depth,) + by_head + (D_MODEL,)).swapaxes(2, 3)
    wv = lax.optimization_barrier(w_in[:, :, C_V:C_V + D_KV].astype(BF16))
    return (wq.reshape(depth, D_MODEL, D_ATTN).astype(BF16), wv.swapaxes(1, 2),
            woa.reshape(depth, D_ATTN, D_MODEL).astype(BF16))


def kernel(x, meta_tokens, norm_mix_pre, norm_mix_post, norm_mlp_pre, norm_mlp_post, w_in, ssm_a_re, ssm_a_im, ssm_log_dt, ssm_b_re, ssm_b_im, ssm_c_re, ssm_c_im, ssm_d, w_glu, b_glu, attn_sinks, w_o_ssm, w_o_attn, w_out, w_up, w_down):
    bsz, seq, _ = x.shape
    depth = w_in.shape[0]
    seq_pad = PAD + N_META + seq
    assert seq_pad % BLOCK == 0 and (seq_pad // CHUNK) % N_SEG == 0
    n_blk, n_chunks = seq_pad // BLOCK, seq_pad // CHUNK
    tm, tm_out = 640, 512
    tq = 13 * BLOCK if n_blk % 13 == 0 else tm
    assert seq_pad % tm == 0 and seq % tm_out == 0 and seq_pad % tq == 0
    n_rows = bsz * seq_pad

    meta_block = jnp.concatenate([jnp.zeros((PAD, D_MODEL), x.dtype), meta_tokens.astype(x.dtype)], axis=0)
    stream = _Rows("input", x.reshape(bsz * seq, D_MODEL), tm, seq_pad // tm, seq // BLOCK, meta_block)
    rope_blk, rope_off = _rope_tables(n_blk)
    bias = _attn_bias()

    row = lambda m: m[:, None, :]
    wq, wvt, woa = _attn_weights(w_in, w_o_attn)
    prep = _ssm_prep(ssm_a_re, ssm_a_im, ssm_log_dt, ssm_b_re, ssm_b_im, ssm_c_re, ssm_c_im, ssm_d, n_chunks // N_SEG)

    for l in range(depth):
        u, q, k, vt, gs, ga = _inproj(stream, n_rows, l, row(norm_mix_pre), w_in, wq, wvt, rope_blk, rope_off, tm)
        y = _ssm(u, prep, l, bsz, seq_pad)
        att = _attention(q, k, vt, attn_sinks[l], bias, bsz, n_blk, tq)
        hres = _merge(stream, l, y, att, gs, ga, w_glu, row(b_glu), w_o_ssm, woa, w_out, row(norm_mix_post), tm)
        ffn_w = (l, row(norm_mlp_pre), w_up, w_down, row(norm_mlp_post))
        if l + 1 < depth:
            stream = _Rows("padded", _ffn(_Rows("padded", hres, tm, seq_pad // tm, n_blk), n_rows, *ffn_w, tm),
                           tm, seq_pad // tm, n_blk)
        else:
            out = _ffn(_Rows("frames", hres, tm_out, seq // tm_out, n_blk), bsz * seq, *ffn_w, tm_out)
    return out.reshape(bsz, seq, D_MODEL)
```

```python
import functools
import math

import jax
import jax.numpy as jnp
import numpy as np
from jax import lax
from jax.experimental import pallas as pl
from jax.experimental.pallas import tpu as pltpu

D_MODEL = 1024
N_META = 16
HEAD_DIM = 64
N_Q_HEADS = 16
N_KV_HEADS = 4
D_ATTN = N_Q_HEADS * HEAD_DIM
D_KV = N_KV_HEADS * HEAD_DIM
BLOCK = 128
ROPE_THETA = 10000.0
ATTN_SCALE = HEAD_DIM ** -0.5
LOG2_E = math.log2(math.e)
NEG_INF = -1e30
D_SSM = D_MODEL // 2
SSM_GROUP = 16
SSM_STATE = 64
D_FF = 4 * D_MODEL
RMS_EPS = 1e-6

LANES = 128
SUBLANES = 8
MXU = 256
PAD = BLOCK - N_META
CHUNK = 16
TILE_GROUPS = LANES // SSM_GROUP
N_SSM_TILES = D_SSM // LANES
TILE_STATE = TILE_GROUPS * SSM_STATE
CHUNK_W = CHUNK * LANES
HALF_GROUPS = TILE_GROUPS // 2
HALF_LANES = LANES // 2
HALF_STATE = HALF_GROUPS * SSM_STATE
HALF_W = CHUNK * HALF_LANES
SUB_BLOCKS = MXU // HALF_LANES
N_SEG = SUBLANES
COL_STEP = 2 * MXU
VMEM_LIMIT = 56 * 1024 * 1024

C_U = 0
C_Q = C_U + D_SSM
C_K = C_Q + D_ATTN
C_V = C_K + D_KV
C_GS = C_V + D_KV
C_GA = C_GS + D_MODEL

BF16 = jnp.bfloat16
F32 = jnp.float32


def _dot(a, b):
    return jnp.dot(a, b, preferred_element_type=F32)


def _rms(x, gain):
    return x * lax.rsqrt(jnp.mean(x * x, axis=-1, keepdims=True) + RMS_EPS) * gain


def _const_spec(shape):
    return pl.BlockSpec(shape, lambda *_: (0,) * len(shape), pipeline_mode=pl.Buffered(1))


def _layer_spec(stacked, layer):
    shape = stacked.shape[1:]
    return pl.BlockSpec((None,) + shape, lambda *_: (layer,) + (0,) * len(shape), pipeline_mode=pl.Buffered(1))


def _params(sem):
    return pltpu.CompilerParams(dimension_semantics=sem, vmem_limit_bytes=VMEM_LIMIT)


class _Rows:
    def __init__(self, kind, array, tm, tiles_per_batch, blocks_per_batch, meta_block=None):
        self.has_meta = kind == "input"
        self.tiles_per_batch = tiles_per_batch
        nb = tm // BLOCK
        if kind == "padded":
            self.specs = [pl.BlockSpec((tm, D_MODEL), lambda i: (i, 0))]
            self.operands = [array]
            return
        shift = -1 if kind == "input" else 1

        def imap(j):
            return lambda i: ((i // tiles_per_batch) * blocks_per_batch
                              + jnp.maximum((i % tiles_per_batch) * nb + j + shift, 0), 0)

        self.specs = [pl.BlockSpec((BLOCK, D_MODEL), imap(j)) for j in range(nb)]
        self.operands = [array] * nb
        if self.has_meta:
            self.specs.append(_const_spec((BLOCK, D_MODEL)))
            self.operands.append(meta_block)

    def load(self, refs):
        n = len(self.specs)
        if n == 1:
            return refs[0][...]
        blocks = [r[...] for r in refs[:n - self.has_meta]]
        if self.has_meta:
            head = pl.program_id(0) % self.tiles_per_batch == 0
            blocks[0] = jnp.where(head, refs[n - 1][...], blocks[0])
        return jnp.concatenate(blocks, axis=0)


def _inproj_kernel(rows, *refs):
    n = len(rows.specs)
    gain_ref, w_ref, wq_ref, wvt_ref, blk_ref, off_ref, u_ref, q_ref, k_ref, vt_ref, gs_ref, ga_ref = refs[n:]
    h = _rms(rows.load(refs[:n]), gain_ref[...]).astype(BF16)

    cos_o, sin_o = off_ref[0], off_ref[1]
    cos, sin = [], []
    for b in range(blk_ref.shape[0]):
        cos_b, sin_b = (jnp.concatenate([blk_ref[b, p]] * (BLOCK // SUBLANES), axis=0) for p in range(2))
        cos.append(cos_b * cos_o - sin_b * sin_o)
        sin.append(sin_b * cos_o + cos_b * sin_o)
    cos, sin = jnp.concatenate(cos, axis=0), jnp.concatenate(sin, axis=0)
    first_half = lax.broadcasted_iota(jnp.int32, sin.shape, 1) % HEAD_DIM < HEAD_DIM // 2
    sa, sb = jnp.where(first_half, -sin, 0.0), jnp.where(first_half, 0.0, sin)

    def rope(t):
        return t * cos + pltpu.roll(t, LANES - HEAD_DIM // 2, 1) * sa + pltpu.roll(t, HEAD_DIM // 2, 1) * sb

    step = COL_STEP
    u_ref[...] = _dot(h, w_ref[:, C_U:C_U + D_SSM].astype(BF16))
    for c in range(0, D_ATTN, step):
        t = _dot(h, wq_ref[:, c:c + step])
        for j in range(0, step, LANES):
            q_ref[:, c + j:c + j + LANES] = (rope(t[:, j:j + LANES]) * (ATTN_SCALE * LOG2_E)).astype(BF16)
    t = _dot(h, w_ref[:, C_K:C_K + D_KV].astype(BF16))
    for j in range(0, D_KV, LANES):
        kj = rope(t[:, j:j + LANES]).astype(BF16)
        for b in range(k_ref.shape[0]):
            k_ref[b, :, j:j + LANES] = kj[b * BLOCK:(b + 1) * BLOCK]
    vt = lax.dot_general(wvt_ref[...], h, (((1,), (1,)), ((), ())), preferred_element_type=F32).astype(BF16)
    for b in range(vt_ref.shape[0]):
        vt_ref[b] = vt[:, b * BLOCK:(b + 1) * BLOCK]
    for c in range(0, D_MODEL, step):
        gs_ref[:, c:c + step] = _dot(h, w_ref[:, C_GS + c:C_GS + c + step].astype(BF16)).astype(BF16)
        ga_ref[:, c:c + step] = _dot(h, w_ref[:, C_GA + c:C_GA + c + step].astype(BF16)).astype(BF16)


def _inproj(rows, n_rows, layer, gain, w, wq, wvt, rope_blk, rope_off, tm):
    row = lambda width: pl.BlockSpec((tm, width), lambda i: (i, 0))
    blk = pl.BlockSpec((tm // BLOCK,) + rope_blk.shape[1:], lambda i: (i % rows.tiles_per_batch, 0, 0, 0))
    out = lambda width, dtype: (row(width), jax.ShapeDtypeStruct((n_rows, width), dtype))
    per_block = lambda shape: (pl.BlockSpec((tm // BLOCK,) + shape, lambda i: (i, 0, 0)),
                               jax.ShapeDtypeStruct((n_rows // BLOCK,) + shape, BF16))
    outs = [out(D_SSM, F32),
            out(D_ATTN, BF16), per_block((BLOCK, D_KV)), per_block((D_KV, BLOCK)),
            out(D_MODEL, BF16), out(D_MODEL, BF16)]
    return pl.pallas_call(
        functools.partial(_inproj_kernel, rows),
        grid=(n_rows // tm,),
        in_specs=rows.specs + [_layer_spec(p, layer) for p in (gain, w, wq, wvt)] + [blk, _const_spec(rope_off.shape)],
        out_specs=[o[0] for o in outs],
        out_shape=[o[1] for o in outs],
        compiler_params=_params(("parallel",)),
        name="inproj",
    )(*rows.operands, gain, w, wq, wvt, rope_blk, rope_off)


def _ssm_spread(rp_ref, bbt_ref, cc_ref, dl_ref, pf_ref, ct_ref, wt_s, pb_s, qm_s):
    T = CHUNK
    iota = lambda shape, axis: lax.broadcasted_iota(jnp.int32, shape, axis)
    group_of = lambda shape, axis, width: (iota(shape, axis) // width) % HALF_GROUPS
    one_hot = lambda hit: jnp.where(hit, 1.0, 0.0).astype(BF16)

    even = iota((HALF_LANES, LANES), 1) < HALF_LANES
    none = jnp.zeros((HALF_LANES, LANES), BF16)
    shape = (HALF_W, HALF_LANES)
    row_copy = one_hot(iota(shape, 0) // SSM_GROUP == iota(shape, 1))
    shape = (LANES, 2 * HALF_STATE)
    state_copy = one_hot(iota(shape, 1) % SSM_STATE + iota(shape, 1) // HALF_STATE * SSM_STATE == iota(shape, 0))
    shape = (LANES, 2 * HALF_W)
    lane_copy = one_hot(iota(shape, 1) // SSM_GROUP == iota(shape, 0))
    shape = (HALF_W, LANES)
    tap_diag = group_of(shape, 0, SSM_GROUP) == group_of(shape, 1, SSM_GROUP)
    tap_skip = tap_diag & (iota(shape, 0) // HALF_LANES == T - 1) & (iota(shape, 0) % SSM_GROUP == iota(shape, 1) % SSM_GROUP)
    shape = (HALF_W, 2 * HALF_STATE)
    pb_diag = group_of(shape, 0, SSM_GROUP) == group_of(shape, 1, SSM_STATE)
    col_group = group_of((SSM_STATE, HALF_W), 1, SSM_GROUP)

    for h in range(2):
        x = _dot(row_copy, rp_ref[0, h].astype(BF16))
        y = jnp.concatenate([bbt_ref[0, h]] * T, axis=0)
        w = (x[:, :LANES] * y[:, :LANES] + x[:, LANES:] * y[:, LANES:]).astype(BF16)
        pb_s[h] = jnp.where(pb_diag, _dot(w, state_copy), 0.0).astype(BF16)

        taps = _dot(w, cc_ref[0, h].astype(BF16)) + jnp.where(tap_skip, jnp.concatenate([dl_ref[0, h]] * (HALF_W // SUBLANES), axis=0), 0.0)
        taps = jnp.where(tap_diag, taps, 0.0).astype(BF16)
        block = lambda lag: taps[(T - 1 - lag) * HALF_LANES:(T - lag) * HALF_LANES] if lag >= 0 else none
        for a in range(SUB_BLOCKS):
            for i in range(SUB_BLOCKS):
                for kk in range(SUB_BLOCKS // 2):
                    lag = SUB_BLOCKS * a + 2 * kk - i
                    wt_s[h, a, i * HALF_LANES:(i + 1) * HALF_LANES, kk * LANES:(kk + 1) * LANES] = (
                        jnp.where(even, block(lag), block(lag + 1)))

        pw = _dot(pf_ref[0, h].astype(BF16), lane_copy)
        p_re, p_im = pw[:, :HALF_W], pw[:, HALF_W:]
        c_re = jnp.concatenate([ct_ref[0, h, 0]] * (HALF_W // LANES), axis=1)
        c_im = jnp.concatenate([ct_ref[0, h, 1]] * (HALF_W // LANES), axis=1)
        for part, q in enumerate((c_re * p_re - c_im * p_im, -(c_re * p_im + c_im * p_re))):
            for g in range(HALF_GROUPS):
                r = part * HALF_STATE + g * SSM_STATE
                qm_s[h, r:r + SSM_STATE, :] = jnp.where(col_group == g, q, 0.0).astype(BF16)


def _ssm_kernel(u_ref, rp_ref, bbt_ref, cc_ref, dl_ref, pf_ref, ct_ref, a_ref, pw_ref, y_ref,
                z_s, v_s, l_s, s_s, wt_s, pb_s, qm_s):
    n_chunks = z_s.shape[0]
    seg = n_chunks // N_SEG
    half_tiles = HALF_STATE // LANES
    part_tiles = lambda part: [h * 2 * half_tiles + part * half_tiles + c for h in range(2) for c in range(half_tiles)]

    @pl.when(pl.program_id(1) == 0)
    def _():
        _ssm_spread(rp_ref, bbt_ref, cc_ref, dl_ref, pf_ref, ct_ref, wt_s, pb_s, qm_s)

    low = lax.broadcasted_iota(jnp.int32, (n_chunks, LANES), 1) < HALF_LANES

    def regroup(a, b):
        return jnp.where(low, a, pltpu.roll(b, HALF_LANES, 1)), jnp.where(low, pltpu.roll(a, HALF_LANES, 1), b)

    for m in range(CHUNK // 2):
        lo, hi = regroup(u_ref[pl.ds(2 * m, n_chunks, stride=CHUNK), :], u_ref[pl.ds(2 * m + 1, n_chunks, stride=CHUNK), :])
        z_s[:, m * LANES:(m + 1) * LANES] = lo.astype(BF16)
        z_s[:, HALF_W + m * LANES:HALF_W + (m + 1) * LANES] = hi.astype(BF16)
    for h in range(2):
        v = _dot(z_s[:, h * HALF_W:(h + 1) * HALF_W], pb_s[h])
        for c in range(2 * half_tiles):
            v_s[h * 2 * half_tiles + c] = v[:, c * LANES:(c + 1) * LANES]

    def load(ref, rows, part):
        return jnp.concatenate([ref[c, rows, :] for c in part_tiles(part)], axis=1)

    def store(ref, rows, part, val):
        for k, c in enumerate(part_tiles(part)):
            ref[c, rows, :] = val[:, k * LANES:(k + 1) * LANES]

    are, aim = a_ref[0, 0], a_ref[0, 1]

    def local_scan(i, carry):
        sre, sim = carry
        rows = pl.ds(i, N_SEG, stride=seg)
        store(l_s, rows, 0, sre)
        store(l_s, rows, 1, sim)
        return (are * sre - aim * sim + load(v_s, rows, 0), are * sim + aim * sre + load(v_s, rows, 1))

    zero = jnp.zeros((N_SEG, TILE_STATE), F32)
    ere, eim = lax.fori_loop(0, seg, local_scan, (zero, zero), unroll=5)

    bre, bim = a_ref[0, 2][:1], a_ref[0, 3][:1]
    tre, tim = [zero[:1]], [zero[:1]]
    for m in range(1, N_SEG):
        pre, pim = tre[-1], tim[-1]
        tre.append(bre * pre - bim * pim + ere[m - 1:m])
        tim.append(bre * pim + bim * pre + eim[m - 1:m])
    tre, tim = jnp.concatenate(tre, axis=0), jnp.concatenate(tim, axis=0)

    def add_carry(i, _):
        rows = pl.ds(i, N_SEG, stride=seg)
        pre, pim = pw_ref[0, 0, pl.ds(i, 1), :], pw_ref[0, 1, pl.ds(i, 1), :]
        store(l_s, rows, 0, load(l_s, rows, 0) + (pre * tre - pim * tim))
        store(l_s, rows, 1, load(l_s, rows, 1) + (pre * tim + pim * tre))
        return 0

    lax.fori_loop(0, seg, add_carry, 0, unroll=5)

    for c in range(4 * half_tiles):
        s_s[:, c * LANES:(c + 1) * LANES] = l_s[c].astype(BF16)
    for tp in range(SUB_BLOCKS):
        acc = []
        for h in range(2):
            y = _dot(s_s[:, h * 2 * HALF_STATE:(h + 1) * 2 * HALF_STATE], qm_s[h, :, tp * MXU:(tp + 1) * MXU])
            for tq in range(tp + 1):
                y = y + _dot(z_s[:, h * HALF_W + tq * MXU:h * HALF_W + (tq + 1) * MXU], wt_s[h, tp - tq])
            acc.append(y)
        for kk in range(SUB_BLOCKS // 2):
            ya, yb = regroup(acc[0][:, kk * LANES:(kk + 1) * LANES], acc[1][:, kk * LANES:(kk + 1) * LANES])
            t = SUB_BLOCKS * tp + 2 * kk
            y_ref[pl.ds(t, n_chunks, stride=CHUNK), :] = ya
            y_ref[pl.ds(t + 1, n_chunks, stride=CHUNK), :] = yb


def _ssm(u, prep, layer, bsz, seq_pad):
    n_chunks = seq_pad // CHUNK
    tile = lambda shape: pl.BlockSpec((None, 1) + shape, lambda j, b: (layer, j) + (0,) * len(shape))
    io = pl.BlockSpec((seq_pad, LANES), lambda j, b: (b, j))
    return pl.pallas_call(
        _ssm_kernel,
        grid=(N_SSM_TILES, bsz),
        in_specs=[io] + [tile(p.shape[2:]) for p in prep],
        out_specs=io,
        out_shape=jax.ShapeDtypeStruct(u.shape, F32),
        scratch_shapes=[pltpu.VMEM((n_chunks, CHUNK_W), BF16),
                        pltpu.VMEM((2 * TILE_STATE // LANES, n_chunks, LANES), F32),
                        pltpu.VMEM((2 * TILE_STATE // LANES, n_chunks, LANES), F32),
                        pltpu.VMEM((n_chunks, 2 * TILE_STATE), BF16),
                        pltpu.VMEM((2, SUB_BLOCKS, MXU, MXU), BF16),
                        pltpu.VMEM((2, HALF_W, 2 * HALF_STATE), BF16),
                        pltpu.VMEM((2, 2 * HALF_STATE, HALF_W), BF16)],
        compiler_params=_params(("arbitrary", "arbitrary")),
        name="ssm",
    )(u, *prep)


def _ssm_prep(a_re, a_im, log_dt, b_re, b_im, c_re, c_im, d_skip, seg):
    N, C, T, J, E = SSM_STATE, SSM_GROUP, CHUNK, N_SSM_TILES, TILE_GROUPS
    depth = a_re.shape[0]
    tiles = lambda m: m.reshape((depth, J, E) + m.shape[2:])
    a_re, a_im, dt = tiles(a_re), tiles(a_im), tiles(jnp.exp(log_dt))[..., None]
    lam_re, lam_im = a_re * dt, a_im * dt

    def powers(p, lam_re, lam_im):
        p = jnp.asarray(p, F32).reshape((-1,) + (1,) * (lam_re.ndim - 2))
        mag = jnp.exp(lam_re[:, :, None] * p)
        return mag * jnp.cos(lam_im[:, :, None] * p), mag * jnp.sin(lam_im[:, :, None] * p)

    pw_re, pw_im = powers(np.arange(T + 1), lam_re, lam_im)
    den = a_re * a_re + a_im * a_im
    nre, nim = pw_re[:, :, 1] - 1.0, pw_im[:, :, 1]
    coef_re = ((nre * a_re + nim * a_im) / den)[..., None, :]
    coef_im = ((nim * a_re - nre * a_im) / den)[..., None, :]
    bt_re, bt_im = tiles(b_re).swapaxes(-1, -2), tiles(b_im).swapaxes(-1, -2)
    halves = lambda m, axis: m.reshape(m.shape[:axis] + (2, HALF_GROUPS) + m.shape[axis + 1:])
    slab = lambda m: m.reshape(m.shape[:3] + (-1, m.shape[-1]))
    cat = lambda *ms: jnp.concatenate(ms, axis=-1)
    bb_re = slab(halves(coef_re * bt_re - coef_im * bt_im, 2))
    bb_im = slab(halves(coef_re * bt_im + coef_im * bt_re, 2))
    bbt = cat(bb_re, bb_re, -bb_im, bb_im)

    rev = lambda m: slab(halves(m[:, :, T - 1::-1], 3).swapaxes(2, 3))
    r_re, r_im = rev(pw_re), rev(pw_im)
    rp = cat(r_re, r_im, r_im, r_re)

    cr, ci = halves(tiles(c_re), 2), halves(tiles(c_im), 2)
    by_state = lambda m: m.reshape(depth, J, 2, N, HALF_LANES)
    crt, cit = by_state(cr.transpose(0, 1, 2, 5, 3, 4)), by_state(ci.transpose(0, 1, 2, 5, 3, 4))
    cc = jnp.concatenate([cat(crt, crt), cat(-cit, -cit)], axis=-2)
    skip = halves(tiles(d_skip), 2).reshape(depth, J, 2, 1, HALF_LANES)
    dl = jnp.broadcast_to(cat(skip, skip), (depth, J, 2, SUBLANES, LANES))

    fwd = lambda m: by_state(halves(m[:, :, 1:], 3).transpose(0, 1, 3, 5, 2, 4))
    pf = cat(fwd(pw_re), fwd(pw_im))
    ct = jnp.stack([cat(crt, crt), cat(cit, cit)], axis=3)

    flat = lambda m: m.reshape(depth, J, TILE_STATE)
    t_re, t_im = powers(T * np.array([1, seg]), flat(lam_re), flat(lam_im))
    a = jnp.stack([t_re[:, :, 0], t_im[:, :, 0], t_re[:, :, 1], t_im[:, :, 1]], axis=2)[:, :, :, None]
    a = jnp.broadcast_to(a, (depth, J, 4, SUBLANES, TILE_STATE))
    pw = jnp.stack(powers(T * np.arange(seg), flat(lam_re), flat(lam_im)), axis=2)
    return rp, bbt, cc, dl, pf, ct, a, pw


def _attn_head_order():
    per_kv = N_Q_HEADS // N_KV_HEADS
    return [(2 * j + e) * per_kv + g for j in range(N_KV_HEADS // 2) for g in range(per_kv) for e in range(2)]


def _attn_kernel(sink_ref, bias_ref, q_ref, k_ref, vt_ref, o_ref):
    n_keys = N_META + 2 * BLOCK
    n_local = q_ref.shape[0] // BLOCK
    first_k = lax.broadcasted_iota(jnp.int32, (n_keys, LANES), 1) < HEAD_DIM
    first_v = lax.broadcasted_iota(jnp.int32, (LANES, 3 * BLOCK), 0) < HEAD_DIM
    first_o = lax.broadcasted_iota(jnp.int32, (LANES, 2 * BLOCK), 0) < HEAD_DIM
    zero_k = jnp.zeros((n_keys, LANES), BF16)
    zero_v = jnp.zeros((LANES, 3 * BLOCK), BF16)
    pad_p = jnp.zeros((PAD, 2 * BLOCK), BF16)
    shape = (2 * SUBLANES, 6 * BLOCK)
    key_rows = jnp.where(lax.broadcasted_iota(jnp.int32, shape, 0) == lax.broadcasted_iota(jnp.int32, shape, 1) // (3 * BLOCK),
                         1.0, 0.0).astype(BF16)

    def block_of(i):
        n = pl.program_id(1) * n_local + i
        return n, jnp.maximum(n - 1, 0)

    def scores(i, tp):
        n, prev = block_of(i)
        cols = slice(tp // 2 * LANES, (tp // 2 + 1) * LANES)
        k2 = jnp.concatenate([k_ref[0, PAD:, cols], k_ref[prev, :, cols], k_ref[n, :, cols]], axis=0)
        kab = jnp.concatenate([jnp.where(first_k, k2, zero_k), jnp.where(first_k, zero_k, k2)], axis=0)
        q4 = jnp.concatenate([q_ref[i * BLOCK:(i + 1) * BLOCK, (2 * tp + p) * LANES:(2 * tp + p + 1) * LANES]
                              for p in range(2)], axis=0)
        return lax.dot_general(kab, q4, (((1,), (1,)), ((), ())), preferred_element_type=F32)

    heads = _attn_head_order()
    chains = [(i, tp) for i in range(n_local) for tp in range(N_Q_HEADS // 4)]
    s_next = scores(*chains[0])
    for c, (i, tp) in enumerate(chains):
        s = s_next
        if c + 1 < len(chains):
            s_next = scores(*chains[c + 1])
        n, prev = block_of(i)
        bias = bias_ref[jnp.minimum(n, 2)]
        pt, tail = [], []
        for e in range(2):
            blocks, tl = [], []
            for p in range(2):
                se = s[e * n_keys:(e + 1) * n_keys, p * LANES:(p + 1) * LANES] + bias
                sink = sink_ref[heads[2 * (2 * tp + p) + e]] * LOG2_E
                m = jnp.maximum(jnp.max(se, axis=0, keepdims=True), sink)
                blocks.append(jnp.exp2(se - m).astype(BF16))
                tl.append(jnp.exp2(sink - m))
            pt += [pad_p, jnp.concatenate(blocks, axis=1)]
            tail.append(jnp.concatenate(tl, axis=1))
        rows = slice(tp // 2 * LANES, (tp // 2 + 1) * LANES)
        v2 = jnp.concatenate([vt_ref[0, rows, :], vt_ref[prev, rows, :], vt_ref[n, rows, :]], axis=1)
        vab = jnp.concatenate([jnp.where(first_v, v2, zero_v), jnp.where(first_v, zero_v, v2)], axis=1)
        vab = jnp.concatenate([vab, key_rows], axis=0)
        ot = _dot(vab, jnp.concatenate(pt, axis=0))
        rden = [1.0 / (ot[LANES + e:LANES + e + 1] + tail[e]) for e in range(2)]
        o = (ot[:LANES] * jnp.where(first_o, rden[0], rden[1])).T
        for p in range(2):
            o_ref[i * BLOCK:(i + 1) * BLOCK, (2 * tp + p) * LANES:(2 * tp + p + 1) * LANES] = (
                o[p * BLOCK:(p + 1) * BLOCK].astype(BF16))


def _attention(q, k, vt, sinks, bias, bsz, n_blk, tq):
    steps = n_blk * BLOCK // tq
    rows = pl.BlockSpec((tq, D_ATTN), lambda b, s: (b * steps + s, 0))
    return pl.pallas_call(
        _attn_kernel,
        grid=(bsz, steps),
        in_specs=[pl.BlockSpec(memory_space=pltpu.SMEM), _const_spec(bias.shape), rows,
                  pl.BlockSpec((n_blk, BLOCK, D_KV), lambda b, s: (b, 0, 0)),
                  pl.BlockSpec((n_blk, D_KV, BLOCK), lambda b, s: (b, 0, 0))],
        out_specs=rows,
        out_shape=jax.ShapeDtypeStruct(q.shape, BF16),
        compiler_params=_params(("parallel", "parallel")),
        name="attention",
    )(sinks, bias, q, k, vt)


def _attn_bias():
    i = np.arange(BLOCK)[None, :]
    j = np.arange(BLOCK)[:, None]
    none = np.zeros((BLOCK, BLOCK), bool)
    causal = j <= i
    in_meta = np.broadcast_to(j >= PAD, (BLOCK, BLOCK))
    blk0 = [none[:N_META], none, causal & in_meta]
    blk1 = [none[:N_META], in_meta, causal]
    blk2 = [~none[:N_META], j > i, causal]
    ok = np.stack([np.concatenate(b, axis=0) for b in (blk0, blk1, blk2)])
    return jnp.asarray(np.where(ok, 0.0, NEG_INF), F32)


def _merge_kernel(rows, *refs):
    n = len(rows.specs)
    y_ref, a_ref, gs_ref, ga_ref, wglu_ref, bglu_ref, wos_ref, woa_ref, wout_ref, gain_ref, o_ref = refs[n:]
    z = jax.nn.gelu(y_ref[...])
    z = z * jax.nn.sigmoid(_dot(z.astype(BF16), wglu_ref[...].astype(BF16)) + bglu_ref[...])
    merged = (jax.nn.sigmoid(gs_ref[...].astype(F32)) * _dot(z.astype(BF16), wos_ref[...].astype(BF16))
              + jax.nn.sigmoid(ga_ref[...].astype(F32)) * _dot(a_ref[...], woa_ref[...].astype(BF16)))
    mix = _dot(merged.astype(BF16), wout_ref[...].astype(BF16))
    o_ref[...] = rows.load(refs[:n]) + _rms(mix, gain_ref[...])


def _merge(rows, layer, y, a, gs, ga, wglu, bglu, wos, woa, wout, gain, tm):
    n_rows = y.shape[0]
    row = lambda width: pl.BlockSpec((tm, width), lambda i: (i, 0))
    weights = (wglu, bglu, wos, woa, wout, gain)
    return pl.pallas_call(
        functools.partial(_merge_kernel, rows),
        grid=(n_rows // tm,),
        in_specs=rows.specs + [row(D_SSM), row(D_ATTN), row(D_MODEL), row(D_MODEL)]
                 + [_layer_spec(p, layer) for p in weights],
        out_specs=row(D_MODEL),
        out_shape=jax.ShapeDtypeStruct((n_rows, D_MODEL), F32),
        compiler_params=_params(("parallel",)),
        name="merge",
    )(*rows.operands, y, a, gs, ga, *weights)


def _ffn_kernel(rows, *refs):
    n = len(rows.specs)
    gpre_ref, wup_ref, wdown_ref, gpost_ref, o_ref = refs[n:]
    x = rows.load(refs[:n])
    h = _rms(x, gpre_ref[...]).astype(BF16)
    step = COL_STEP
    acc = jnp.zeros(x.shape, F32)
    for c in range(0, D_FF, step):
        a = jnp.maximum(_dot(h, wup_ref[:, c:c + step].astype(BF16)), 0.0)
        acc = acc + _dot((a * a).astype(BF16), wdown_ref[c:c + step, :].astype(BF16))
    o_ref[...] = x + _rms(acc, gpost_ref[...])


def _ffn(rows, n_rows, layer, gpre, wup, wdown, gpost, tm):
    weights = (gpre, wup, wdown, gpost)
    return pl.pallas_call(
        functools.partial(_ffn_kernel, rows),
        grid=(n_rows // tm,),
        in_specs=rows.specs + [_layer_spec(p, layer) for p in weights],
        out_specs=pl.BlockSpec((tm, D_MODEL), lambda i: (i, 0)),
        out_shape=jax.ShapeDtypeStruct((n_rows, D_MODEL), F32),
        compiler_params=_params(("parallel",)),
        name="ffn",
    )(*rows.operands, *weights)


def _rope_tables(n_blk):
    inv_freq = 1.0 / (ROPE_THETA ** (jnp.arange(0, HEAD_DIM, 2, dtype=F32) / HEAD_DIM))
    inv_freq = jnp.tile(inv_freq, LANES // (HEAD_DIM // 2))
    blk = (BLOCK * jnp.arange(n_blk, dtype=jnp.int32)).astype(F32)[:, None] * inv_freq
    off = (jnp.arange(BLOCK, dtype=jnp.int32) - PAD).astype(F32)[:, None] * inv_freq
    blk = jnp.stack([jnp.cos(blk), jnp.sin(blk)], axis=1)[:, :, None]
    return jnp.broadcast_to(blk, (n_blk, 2, SUBLANES, LANES)), jnp.stack([jnp.cos(off), jnp.sin(off)])


def _attn_weights(w_in, w_o_attn):
    depth = w_in.shape[0]
    per_kv = N_Q_HEADS // N_KV_HEADS
    assert _attn_head_order() == [int(h) for h in np.arange(N_Q_HEADS).reshape(-1, 2, per_kv).swapaxes(1, 2).ravel()]
    by_head = (N_KV_HEADS // 2, 2, per_kv, HEAD_DIM)
    wq = w_in[:, :, C_Q:C_Q + D_ATTN].reshape((depth, D_MODEL) + by_head).swapaxes(3, 4)
    woa = w_o_attn.reshape((depth,) + by_head + (D_MODEL,)).swapaxes(2, 3)
    wv = lax.optimization_barrier(w_in[:, :, C_V:C_V + D_KV].astype(BF16))
    return (wq.reshape(depth, D_MODEL, D_ATTN).astype(BF16), wv.swapaxes(1, 2),
            woa.reshape(depth, D_ATTN, D_MODEL).astype(BF16))


def kernel(x, meta_tokens, norm_mix_pre, norm_mix_post, norm_mlp_pre, norm_mlp_post, w_in, ssm_a_re, ssm_a_im, ssm_log_dt, ssm_b_re, ssm_b_im, ssm_c_re, ssm_c_im, ssm_d, w_glu, b_glu, attn_sinks, w_o_ssm, w_o_attn, w_out, w_up, w_down):
    bsz, seq, _ = x.shape
    depth = w_in.shape[0]
    seq_pad = PAD + N_META + seq
    assert seq_pad % BLOCK == 0 and (seq_pad // CHUNK) % N_SEG == 0
    n_blk, n_chunks = seq_pad // BLOCK, seq_pad // CHUNK
    tm, tm_out = 5 * BLOCK, 4 * BLOCK
    tq = 13 * BLOCK if n_blk % 13 == 0 else tm
    assert seq_pad % tm == 0 and seq % tm_out == 0 and seq_pad % tq == 0
    n_rows = bsz * seq_pad

    meta_block = jnp.concatenate([jnp.zeros((PAD, D_MODEL), x.dtype), meta_tokens.astype(x.dtype)], axis=0)
    stream = _Rows("input", x.reshape(bsz * seq, D_MODEL), tm, seq_pad // tm, seq // BLOCK, meta_block)
    rope_blk, rope_off = _rope_tables(n_blk)
    bias = _attn_bias()

    row = lambda m: m[:, None, :]
    wq, wvt, woa = _attn_weights(w_in, w_o_attn)
    prep = _ssm_prep(ssm_a_re, ssm_a_im, ssm_log_dt, ssm_b_re, ssm_b_im, ssm_c_re, ssm_c_im, ssm_d, n_chunks // N_SEG)

    for l in range(depth):
        u, q, k, vt, gs, ga = _inproj(stream, n_rows, l, row(norm_mix_pre), w_in, wq, wvt, rope_blk, rope_off, tm)
        y = _ssm(u, prep, l, bsz, seq_pad)
        att = _attention(q, k, vt, attn_sinks[l], bias, bsz, n_blk, tq)
        hres = _merge(stream, l, y, att, gs, ga, w_glu, row(b_glu), w_o_ssm, woa, w_out, row(norm_mix_post), tm)
        ffn_w = (l, row(norm_mlp_pre), w_up, w_down, row(norm_mlp_post))
        if l + 1 < depth:
            stream = _Rows("padded", _ffn(_Rows("padded", hres, tm, seq_pad // tm, n_blk), n_rows, *ffn_w, tm),
                           tm, seq_pad // tm, n_blk)
        else:
            out = _ffn(_Rows("frames", hres, tm_out, seq // tm_out, n_blk), bsz * seq, *ffn_w, tm_out)
    return out.reshape(bsz, seq, D_MODEL)
```

```python
import functools
import math

import jax
import jax.numpy as jnp
import numpy as np
from jax import lax
from jax.experimental import pallas as pl
from jax.experimental.pallas import tpu as pltpu

D_MODEL = 1024
N_META = 16
HEAD_DIM = 64
N_Q_HEADS = 16
N_KV_HEADS = 4
D_ATTN = N_Q_HEADS * HEAD_DIM
D_KV = N_KV_HEADS * HEAD_DIM
BLOCK = 128
ROPE_THETA = 10000.0
ATTN_SCALE = HEAD_DIM ** -0.5
LOG2_E = math.log2(math.e)
NEG_INF = -1e30
D_SSM = D_MODEL // 2
SSM_GROUP = 16
SSM_STATE = 64
D_FF = 4 * D_MODEL
RMS_EPS = 1e-6

LANES = 128
SUBLANES = 8
MXU = 256
PAD = BLOCK - N_META
CHUNK = 16
TILE_GROUPS = LANES // SSM_GROUP
N_SSM_TILES = D_SSM // LANES
TILE_STATE = TILE_GROUPS * SSM_STATE
CHUNK_W = CHUNK * LANES
HALF_GROUPS = TILE_GROUPS // 2
HALF_LANES = LANES // 2
HALF_STATE = HALF_GROUPS * SSM_STATE
HALF_W = CHUNK * HALF_LANES
SUB_BLOCKS = MXU // HALF_LANES
N_SEG = SUBLANES
COL_STEP = 2 * MXU
VMEM_LIMIT = 56 * 1024 * 1024

C_U = 0
C_Q = C_U + D_SSM
C_K = C_Q + D_ATTN
C_V = C_K + D_KV
C_GS = C_V + D_KV
C_GA = C_GS + D_MODEL

BF16 = jnp.bfloat16
F32 = jnp.float32


def _dot(a, b):
    return jnp.dot(a, b, preferred_element_type=F32)


def _rms(x, gain):
    return x * lax.rsqrt(jnp.mean(x * x, axis=-1, keepdims=True) + RMS_EPS) * gain


def _const_spec(shape):
    return pl.BlockSpec(shape, lambda *_: (0,) * len(shape), pipeline_mode=pl.Buffered(1))


def _layer_spec(stacked, layer):
    shape = stacked.shape[1:]
    return pl.BlockSpec((None,) + shape, lambda *_: (layer,) + (0,) * len(shape), pipeline_mode=pl.Buffered(1))


def _params(sem):
    return pltpu.CompilerParams(dimension_semantics=sem, vmem_limit_bytes=VMEM_LIMIT)


class _Rows:
    def __init__(self, kind, array, tm, tiles_per_batch, blocks_per_batch, meta_block=None):
        self.has_meta = kind == "input"
        self.tiles_per_batch = tiles_per_batch
        nb = tm // BLOCK
        if kind == "padded":
            self.specs = [pl.BlockSpec((tm, D_MODEL), lambda i: (i, 0))]
            self.operands = [array]
            return
        shift = -1 if kind == "input" else 1

        def imap(j):
            return lambda i: ((i // tiles_per_batch) * blocks_per_batch
                              + jnp.maximum((i % tiles_per_batch) * nb + j + shift, 0), 0)

        self.specs = [pl.BlockSpec((BLOCK, D_MODEL), imap(j)) for j in range(nb)]
        self.operands = [array] * nb
        if self.has_meta:
            self.specs.append(_const_spec((BLOCK, D_MODEL)))
            self.operands.append(meta_block)

    def load(self, refs):
        n = len(self.specs)
        if n == 1:
            return refs[0][...]
        blocks = [r[...] for r in refs[:n - self.has_meta]]
        if self.has_meta:
            head = pl.program_id(0) % self.tiles_per_batch == 0
            blocks[0] = jnp.where(head, refs[n - 1][...], blocks[0])
        return jnp.concatenate(blocks, axis=0)


def _inproj_kernel(rows, *refs):
    n = len(rows.specs)
    gain_ref, w_ref, wvt_ref, blk_ref, off_ref, u_ref, q_ref, k_ref, vt_ref, gs_ref, ga_ref = refs[n:]
    h = _rms(rows.load(refs[:n]), gain_ref[...]).astype(BF16)

    cos_o, sin_o = off_ref[0], off_ref[1]
    cos, sin = [], []
    for b in range(blk_ref.shape[0]):
        cos_b, sin_b = (jnp.concatenate([blk_ref[b, p]] * (BLOCK // SUBLANES), axis=0) for p in range(2))
        cos.append(cos_b * cos_o - sin_b * sin_o)
        sin.append(sin_b * cos_o + cos_b * sin_o)
    cos, sin = jnp.concatenate(cos, axis=0), jnp.concatenate(sin, axis=0)
    first_half = lax.broadcasted_iota(jnp.int32, sin.shape, 1) % HEAD_DIM < HEAD_DIM // 2
    sa, sb = jnp.where(first_half, -sin, 0.0), jnp.where(first_half, 0.0, sin)

    def rope(t):
        return t * cos + pltpu.roll(t, LANES - HEAD_DIM // 2, 1) * sa + pltpu.roll(t, HEAD_DIM // 2, 1) * sb

    step = COL_STEP
    u_ref[...] = _dot(h, w_ref[:, C_U:C_U + D_SSM].astype(BF16))
    low = lax.broadcasted_iota(jnp.int32, sin.shape, 1) < HEAD_DIM
    for c in range(0, D_ATTN, step):
        t = _dot(h, w_ref[:, C_Q + c:C_Q + c + step].astype(BF16))
        tiles = [rope(t[:, j:j + LANES]) * (ATTN_SCALE * LOG2_E) for j in range(0, step, LANES)]
        for m in range(2):
            a, b = tiles[m], tiles[m + 2]
            pair = (jnp.where(low, a, pltpu.roll(b, HEAD_DIM, 1)), jnp.where(low, pltpu.roll(a, HEAD_DIM, 1), b))
            for e in range(2):
                j = c + (2 * m + e) * LANES
                q_ref[:, j:j + LANES] = pair[e].astype(BF16)
    t = _dot(h, w_ref[:, C_K:C_K + D_KV].astype(BF16))
    for j in range(0, D_KV, LANES):
        kj = rope(t[:, j:j + LANES]).astype(BF16)
        for b in range(k_ref.shape[0]):
            k_ref[b, :, j:j + LANES] = kj[b * BLOCK:(b + 1) * BLOCK]
    vt = lax.dot_general(wvt_ref[...], h, (((1,), (1,)), ((), ())), preferred_element_type=F32).astype(BF16)
    for b in range(vt_ref.shape[0]):
        vt_ref[b] = vt[:, b * BLOCK:(b + 1) * BLOCK]
    for c in range(0, D_MODEL, step):
        gs_ref[:, c:c + step] = _dot(h, w_ref[:, C_GS + c:C_GS + c + step].astype(BF16)).astype(BF16)
        ga_ref[:, c:c + step] = _dot(h, w_ref[:, C_GA + c:C_GA + c + step].astype(BF16)).astype(BF16)


def _inproj(rows, n_rows, layer, gain, w, wvt, rope_blk, rope_off, tm):
    row = lambda width: pl.BlockSpec((tm, width), lambda i: (i, 0))
    blk = pl.BlockSpec((tm // BLOCK,) + rope_blk.shape[1:], lambda i: (i % rows.tiles_per_batch, 0, 0, 0))
    out = lambda width, dtype: (row(width), jax.ShapeDtypeStruct((n_rows, width), dtype))
    per_block = lambda shape: (pl.BlockSpec((tm // BLOCK,) + shape, lambda i: (i, 0, 0)),
                               jax.ShapeDtypeStruct((n_rows // BLOCK,) + shape, BF16))
    outs = [out(D_SSM, F32),
            out(D_ATTN, BF16), per_block((BLOCK, D_KV)), per_block((D_KV, BLOCK)),
            out(D_MODEL, BF16), out(D_MODEL, BF16)]
    return pl.pallas_call(
        functools.partial(_inproj_kernel, rows),
        grid=(n_rows // tm,),
        in_specs=rows.specs + [_layer_spec(p, layer) for p in (gain, w, wvt)] + [blk, _const_spec(rope_off.shape)],
        out_specs=[o[0] for o in outs],
        out_shape=[o[1] for o in outs],
        compiler_params=_params(("parallel",)),
        name="inproj",
    )(*rows.operands, gain, w, wvt, rope_blk, rope_off)


def _ssm_spread(rp_ref, bbt_ref, cc_ref, dl_ref, pf_ref, ct_ref, wt_s, pb_s, qm_s):
    T = CHUNK
    iota = lambda shape, axis: lax.broadcasted_iota(jnp.int32, shape, axis)
    group_of = lambda shape, axis, width: (iota(shape, axis) // width) % HALF_GROUPS
    one_hot = lambda hit: jnp.where(hit, 1.0, 0.0).astype(BF16)

    even = iota((HALF_LANES, LANES), 1) < HALF_LANES
    none = jnp.zeros((HALF_LANES, LANES), BF16)
    shape = (HALF_W, HALF_LANES)
    row_copy = one_hot(iota(shape, 0) // SSM_GROUP == iota(shape, 1))
    shape = (LANES, 2 * HALF_STATE)
    state_copy = one_hot(iota(shape, 1) % SSM_STATE + iota(shape, 1) // HALF_STATE * SSM_STATE == iota(shape, 0))
    shape = (LANES, 2 * HALF_W)
    lane_copy = one_hot(iota(shape, 1) // SSM_GROUP == iota(shape, 0))
    shape = (HALF_W, LANES)
    tap_diag = group_of(shape, 0, SSM_GROUP) == group_of(shape, 1, SSM_GROUP)
    tap_skip = tap_diag & (iota(shape, 0) // HALF_LANES == T - 1) & (iota(shape, 0) % SSM_GROUP == iota(shape, 1) % SSM_GROUP)
    shape = (HALF_W, 2 * HALF_STATE)
    pb_diag = group_of(shape, 0, SSM_GROUP) == group_of(shape, 1, SSM_STATE)
    col_group = group_of((SSM_STATE, HALF_W), 1, SSM_GROUP)

    for h in range(2):
        x = _dot(row_copy, rp_ref[0, h].astype(BF16))
        y = jnp.concatenate([bbt_ref[0, h]] * T, axis=0)
        w = (x[:, :LANES] * y[:, :LANES] + x[:, LANES:] * y[:, LANES:]).astype(BF16)
        pb_s[h] = jnp.where(pb_diag, _dot(w, state_copy), 0.0).astype(BF16)

        taps = _dot(w, cc_ref[0, h].astype(BF16)) + jnp.where(tap_skip, jnp.concatenate([dl_ref[0, h]] * (HALF_W // SUBLANES), axis=0), 0.0)
        taps = jnp.where(tap_diag, taps, 0.0).astype(BF16)
        block = lambda lag: taps[(T - 1 - lag) * HALF_LANES:(T - lag) * HALF_LANES] if lag >= 0 else none
        for a in range(SUB_BLOCKS):
            for i in range(SUB_BLOCKS):
                for kk in range(SUB_BLOCKS // 2):
                    lag = SUB_BLOCKS * a + 2 * kk - i
                    wt_s[h, a, i * HALF_LANES:(i + 1) * HALF_LANES, kk * LANES:(kk + 1) * LANES] = (
                        jnp.where(even, block(lag), block(lag + 1)))

        pw = _dot(pf_ref[0, h].astype(BF16), lane_copy)
        p_re, p_im = pw[:, :HALF_W], pw[:, HALF_W:]
        c_re = jnp.concatenate([ct_ref[0, h, 0]] * (HALF_W // LANES), axis=1)
        c_im = jnp.concatenate([ct_ref[0, h, 1]] * (HALF_W // LANES), axis=1)
        for part, q in enumerate((c_re * p_re - c_im * p_im, -(c_re * p_im + c_im * p_re))):
            for g in range(HALF_GROUPS):
                r = part * HALF_STATE + g * SSM_STATE
                qm_s[h, r:r + SSM_STATE, :] = jnp.where(col_group == g, q, 0.0).astype(BF16)


def _ssm_kernel(u_ref, rp_ref, bbt_ref, cc_ref, dl_ref, pf_ref, ct_ref, a_ref, pw_ref, y_ref,
                z_s, v_s, l_s, s_s, wt_s, pb_s, qm_s):
    n_chunks = z_s.shape[0]
    seg = n_chunks // N_SEG
    half_tiles = HALF_STATE // LANES
    part_tiles = lambda part: [h * 2 * half_tiles + part * half_tiles + c for h in range(2) for c in range(half_tiles)]

    @pl.when(pl.program_id(1) == 0)
    def _():
        _ssm_spread(rp_ref, bbt_ref, cc_ref, dl_ref, pf_ref, ct_ref, wt_s, pb_s, qm_s)

    low = lax.broadcasted_iota(jnp.int32, (n_chunks, LANES), 1) < HALF_LANES

    def regroup(a, b):
        return jnp.where(low, a, pltpu.roll(b, HALF_LANES, 1)), jnp.where(low, pltpu.roll(a, HALF_LANES, 1), b)

    for m in range(CHUNK // 2):
        lo, hi = regroup(u_ref[pl.ds(2 * m, n_chunks, stride=CHUNK), :], u_ref[pl.ds(2 * m + 1, n_chunks, stride=CHUNK), :])
        z_s[:, m * LANES:(m + 1) * LANES] = lo.astype(BF16)
        z_s[:, HALF_W + m * LANES:HALF_W + (m + 1) * LANES] = hi.astype(BF16)
    for h in range(2):
        v = _dot(z_s[:, h * HALF_W:(h + 1) * HALF_W], pb_s[h])
        for c in range(2 * half_tiles):
            v_s[h * 2 * half_tiles + c] = v[:, c * LANES:(c + 1) * LANES]

    def load(ref, rows, part):
        return jnp.concatenate([ref[c, rows, :] for c in part_tiles(part)], axis=1)

    def store(ref, rows, part, val):
        for k, c in enumerate(part_tiles(part)):
            ref[c, rows, :] = val[:, k * LANES:(k + 1) * LANES]

    are, aim = a_ref[0, 0], a_ref[0, 1]

    def local_scan(i, carry):
        sre, sim = carry
        rows = pl.ds(i, N_SEG, stride=seg)
        store(l_s, rows, 0, sre)
        store(l_s, rows, 1, sim)
        return (are * sre - aim * sim + load(v_s, rows, 0), are * sim + aim * sre + load(v_s, rows, 1))

    zero = jnp.zeros((N_SEG, TILE_STATE), F32)
    ere, eim = lax.fori_loop(0, seg, local_scan, (zero, zero), unroll=5)

    bre, bim = a_ref[0, 2][:1], a_ref[0, 3][:1]
    tre, tim = [zero[:1]], [zero[:1]]
    for m in range(1, N_SEG):
        pre, pim = tre[-1], tim[-1]
        tre.append(bre * pre - bim * pim + ere[m - 1:m])
        tim.append(bre * pim + bim * pre + eim[m - 1:m])
    tre, tim = jnp.concatenate(tre, axis=0), jnp.concatenate(tim, axis=0)

    def add_carry(i, _):
        rows = pl.ds(i, N_SEG, stride=seg)
        pre, pim = pw_ref[0, 0, pl.ds(i, 1), :], pw_ref[0, 1, pl.ds(i, 1), :]
        store(l_s, rows, 0, load(l_s, rows, 0) + (pre * tre - pim * tim))
        store(l_s, rows, 1, load(l_s, rows, 1) + (pre * tim + pim * tre))
        return 0

    lax.fori_loop(0, seg, add_carry, 0, unroll=5)

    for c in range(4 * half_tiles):
        s_s[:, c * LANES:(c + 1) * LANES] = l_s[c].astype(BF16)
    for tp in range(SUB_BLOCKS):
        acc = []
        for h in range(2):
            y = _dot(s_s[:, h * 2 * HALF_STATE:(h + 1) * 2 * HALF_STATE], qm_s[h, :, tp * MXU:(tp + 1) * MXU])
            for tq in range(tp + 1):
                y = y + _dot(z_s[:, h * HALF_W + tq * MXU:h * HALF_W + (tq + 1) * MXU], wt_s[h, tp - tq])
            acc.append(y)
        for kk in range(SUB_BLOCKS // 2):
            ya, yb = regroup(acc[0][:, kk * LANES:(kk + 1) * LANES], acc[1][:, kk * LANES:(kk + 1) * LANES])
            t = SUB_BLOCKS * tp + 2 * kk
            y_ref[pl.ds(t, n_chunks, stride=CHUNK), :] = ya
            y_ref[pl.ds(t + 1, n_chunks, stride=CHUNK), :] = yb


def _ssm(u, prep, layer, bsz, seq_pad):
    n_chunks = seq_pad // CHUNK
    tile = lambda shape: pl.BlockSpec((None, 1) + shape, lambda j, b: (layer, j) + (0,) * len(shape))
    io = pl.BlockSpec((seq_pad, LANES), lambda j, b: (b, j))
    return pl.pallas_call(
        _ssm_kernel,
        grid=(N_SSM_TILES, bsz),
        in_specs=[io] + [tile(p.shape[2:]) for p in prep],
        out_specs=io,
        out_shape=jax.ShapeDtypeStruct(u.shape, F32),
        scratch_shapes=[pltpu.VMEM((n_chunks, CHUNK_W), BF16),
                        pltpu.VMEM((2 * TILE_STATE // LANES, n_chunks, LANES), F32),
                        pltpu.VMEM((2 * TILE_STATE // LANES, n_chunks, LANES), F32),
                        pltpu.VMEM((n_chunks, 2 * TILE_STATE), BF16),
                        pltpu.VMEM((2, SUB_BLOCKS, MXU, MXU), BF16),
                        pltpu.VMEM((2, HALF_W, 2 * HALF_STATE), BF16),
                        pltpu.VMEM((2, 2 * HALF_STATE, HALF_W), BF16)],
        compiler_params=_params(("arbitrary", "arbitrary")),
        name="ssm",
    )(u, *prep)


def _ssm_prep(a_re, a_im, log_dt, b_re, b_im, c_re, c_im, d_skip, seg):
    N, C, T, J, E = SSM_STATE, SSM_GROUP, CHUNK, N_SSM_TILES, TILE_GROUPS
    depth = a_re.shape[0]
    tiles = lambda m: m.reshape((depth, J, E) + m.shape[2:])
    a_re, a_im, dt = tiles(a_re), tiles(a_im), tiles(jnp.exp(log_dt))[..., None]
    lam_re, lam_im = a_re * dt, a_im * dt

    def powers(p, lam_re, lam_im):
        p = jnp.asarray(p, F32).reshape((-1,) + (1,) * (lam_re.ndim - 2))
        mag = jnp.exp(lam_re[:, :, None] * p)
        return mag * jnp.cos(lam_im[:, :, None] * p), mag * jnp.sin(lam_im[:, :, None] * p)

    pw_re, pw_im = powers(np.arange(T + 1), lam_re, lam_im)
    den = a_re * a_re + a_im * a_im
    nre, nim = pw_re[:, :, 1] - 1.0, pw_im[:, :, 1]
    coef_re = ((nre * a_re + nim * a_im) / den)[..., None, :]
    coef_im = ((nim * a_re - nre * a_im) / den)[..., None, :]
    bt_re, bt_im = tiles(b_re).swapaxes(-1, -2), tiles(b_im).swapaxes(-1, -2)
    halves = lambda m, axis: m.reshape(m.shape[:axis] + (2, HALF_GROUPS) + m.shape[axis + 1:])
    slab = lambda m: m.reshape(m.shape[:3] + (-1, m.shape[-1]))
    cat = lambda *ms: jnp.concatenate(ms, axis=-1)
    bb_re = slab(halves(coef_re * bt_re - coef_im * bt_im, 2))
    bb_im = slab(halves(coef_re * bt_im + coef_im * bt_re, 2))
    bbt = cat(bb_re, bb_re, -bb_im, bb_im)

    rev = lambda m: slab(halves(m[:, :, T - 1::-1], 3).swapaxes(2, 3))
    r_re, r_im = rev(pw_re), rev(pw_im)
    rp = cat(r_re, r_im, r_im, r_re)

    cr, ci = halves(tiles(c_re), 2), halves(tiles(c_im), 2)
    by_state = lambda m: m.reshape(depth, J, 2, N, HALF_LANES)
    crt, cit = by_state(cr.transpose(0, 1, 2, 5, 3, 4)), by_state(ci.transpose(0, 1, 2, 5, 3, 4))
    cc = jnp.concatenate([cat(crt, crt), cat(-cit, -cit)], axis=-2)
    skip = halves(tiles(d_skip), 2).reshape(depth, J, 2, 1, HALF_LANES)
    dl = jnp.broadcast_to(cat(skip, skip), (depth, J, 2, SUBLANES, LANES))

    fwd = lambda m: by_state(halves(m[:, :, 1:], 3).transpose(0, 1, 3, 5, 2, 4))
    pf = cat(fwd(pw_re), fwd(pw_im))
    ct = jnp.stack([cat(crt, crt), cat(cit, cit)], axis=3)

    flat = lambda m: m.reshape(depth, J, TILE_STATE)
    t_re, t_im = powers(T * np.array([1, seg]), flat(lam_re), flat(lam_im))
    a = jnp.stack([t_re[:, :, 0], t_im[:, :, 0], t_re[:, :, 1], t_im[:, :, 1]], axis=2)[:, :, :, None]
    a = jnp.broadcast_to(a, (depth, J, 4, SUBLANES, TILE_STATE))
    pw = jnp.stack(powers(T * np.arange(seg), flat(lam_re), flat(lam_im)), axis=2)
    return rp, bbt, cc, dl, pf, ct, a, pw


def _attn_head_order():
    per_kv = N_Q_HEADS // N_KV_HEADS
    return [(2 * j + e) * per_kv + g for j in range(N_KV_HEADS // 2) for g in range(per_kv) for e in range(2)]


def _attn_kernel(sink_ref, bias_ref, q_ref, k_ref, vt_ref, o_ref):
    n_keys = N_META + 2 * BLOCK
    n_local = q_ref.shape[0] // BLOCK
    first_k = lax.broadcasted_iota(jnp.int32, (n_keys, LANES), 1) < HEAD_DIM
    first_v = lax.broadcasted_iota(jnp.int32, (LANES, 3 * BLOCK), 0) < HEAD_DIM
    first_o = lax.broadcasted_iota(jnp.int32, (LANES, 2 * BLOCK), 0) < HEAD_DIM
    low_o = lax.broadcasted_iota(jnp.int32, (BLOCK, LANES), 1) < HEAD_DIM
    zero_k = jnp.zeros((n_keys, LANES), BF16)
    zero_v = jnp.zeros((LANES, 3 * BLOCK), BF16)
    pad_p = jnp.zeros((PAD, 2 * BLOCK), BF16)
    shape = (2 * SUBLANES, 6 * BLOCK)
    key_rows = jnp.where(lax.broadcasted_iota(jnp.int32, shape, 0) == lax.broadcasted_iota(jnp.int32, shape, 1) // (3 * BLOCK),
                         1.0, 0.0).astype(BF16)

    def block_of(i):
        n = pl.program_id(1) * n_local + i
        return n, jnp.maximum(n - 1, 0)

    def scores(i, tp):
        n, prev = block_of(i)
        cols = slice(tp // 2 * LANES, (tp // 2 + 1) * LANES)
        k2 = jnp.concatenate([k_ref[0, PAD:, cols], k_ref[prev, :, cols], k_ref[n, :, cols]], axis=0)
        kab = jnp.concatenate([jnp.where(first_k, k2, zero_k), jnp.where(first_k, zero_k, k2)], axis=0)
        q4 = jnp.concatenate([q_ref[i * BLOCK:(i + 1) * BLOCK, (2 * tp + p) * LANES:(2 * tp + p + 1) * LANES]
                              for p in range(2)], axis=0)
        return lax.dot_general(kab, q4, (((1,), (1,)), ((), ())), preferred_element_type=F32)

    heads = _attn_head_order()
    chains = [(i, tp) for i in range(n_local) for tp in range(N_Q_HEADS // 4)]
    s_next = scores(*chains[0])
    for c, (i, tp) in enumerate(chains):
        s = s_next
        if c + 1 < len(chains):
            s_next = scores(*chains[c + 1])
        n, prev = block_of(i)
        bias = bias_ref[jnp.minimum(n, 2)]
        pt, tail = [], []
        for e in range(2):
            blocks, tl = [], []
            for p in range(2):
                se = s[e * n_keys:(e + 1) * n_keys, p * LANES:(p + 1) * LANES] + bias
                sink = sink_ref[heads[2 * (2 * tp + p) + e]] * LOG2_E
                m = jnp.maximum(jnp.max(se, axis=0, keepdims=True), sink)
                blocks.append(jnp.exp2(se - m).astype(BF16))
                tl.append(jnp.exp2(sink - m))
            pt += [pad_p, jnp.concatenate(blocks, axis=1)]
            tail.append(jnp.concatenate(tl, axis=1))
        rows = slice(tp // 2 * LANES, (tp // 2 + 1) * LANES)
        v2 = jnp.concatenate([vt_ref[0, rows, :], vt_ref[prev, rows, :], vt_ref[n, rows, :]], axis=1)
        vab = jnp.concatenate([jnp.where(first_v, v2, zero_v), jnp.where(first_v, zero_v, v2)], axis=1)
        vab = jnp.concatenate([vab, key_rows], axis=0)
        ot = _dot(vab, jnp.concatenate(pt, axis=0))
        rden = [1.0 / (ot[LANES + e:LANES + e + 1] + tail[e]) for e in range(2)]
        o = (ot[:LANES] * jnp.where(first_o, rden[0], rden[1])).T
        a, b = o[:BLOCK], o[BLOCK:]
        pair = (jnp.where(low_o, a, pltpu.roll(b, HEAD_DIM, 1)), jnp.where(low_o, pltpu.roll(a, HEAD_DIM, 1), b))
        for e in range(2):
            t = 4 * (tp // 2) + 2 * e + tp % 2
            o_ref[i * BLOCK:(i + 1) * BLOCK, t * LANES:(t + 1) * LANES] = pair[e].astype(BF16)


def _attention(q, k, vt, sinks, bias, bsz, n_blk, tq):
    steps = n_blk * BLOCK // tq
    rows = pl.BlockSpec((tq, D_ATTN), lambda b, s: (b * steps + s, 0))
    return pl.pallas_call(
        _attn_kernel,
        grid=(bsz, steps),
        in_specs=[pl.BlockSpec(memory_space=pltpu.SMEM), _const_spec(bias.shape), rows,
                  pl.BlockSpec((n_blk, BLOCK, D_KV), lambda b, s: (b, 0, 0)),
                  pl.BlockSpec((n_blk, D_KV, BLOCK), lambda b, s: (b, 0, 0))],
        out_specs=rows,
        out_shape=jax.ShapeDtypeStruct(q.shape, BF16),
        compiler_params=_params(("parallel", "parallel")),
        name="attention",
    )(sinks, bias, q, k, vt)


def _attn_bias():
    i = np.arange(BLOCK)[None, :]
    j = np.arange(BLOCK)[:, None]
    none = np.zeros((BLOCK, BLOCK), bool)
    causal = j <= i
    in_meta = np.broadcast_to(j >= PAD, (BLOCK, BLOCK))
    blk0 = [none[:N_META], none, causal & in_meta]
    blk1 = [none[:N_META], in_meta, causal]
    blk2 = [~none[:N_META], j > i, causal]
    ok = np.stack([np.concatenate(b, axis=0) for b in (blk0, blk1, blk2)])
    return jnp.asarray(np.where(ok, 0.0, NEG_INF), F32)


def _merge_kernel(rows, *refs):
    n = len(rows.specs)
    y_ref, a_ref, gs_ref, ga_ref, wglu_ref, bglu_ref, wos_ref, woa_ref, wout_ref, gain_ref, o_ref = refs[n:]
    z = jax.nn.gelu(y_ref[...])
    z = z * jax.nn.sigmoid(_dot(z.astype(BF16), wglu_ref[...].astype(BF16)) + bglu_ref[...])
    merged = (jax.nn.sigmoid(gs_ref[...].astype(F32)) * _dot(z.astype(BF16), wos_ref[...].astype(BF16))
              + jax.nn.sigmoid(ga_ref[...].astype(F32)) * _dot(a_ref[...], woa_ref[...].astype(BF16)))
    mix = _dot(merged.astype(BF16), wout_ref[...].astype(BF16))
    o_ref[...] = rows.load(refs[:n]) + _rms(mix, gain_ref[...])


def _merge(rows, layer, y, a, gs, ga, wglu, bglu, wos, woa, wout, gain, tm):
    n_rows = y.shape[0]
    row = lambda width: pl.BlockSpec((tm, width), lambda i: (i, 0))
    weights = (wglu, bglu, wos, woa, wout, gain)
    return pl.pallas_call(
        functools.partial(_merge_kernel, rows),
        grid=(n_rows // tm,),
        in_specs=rows.specs + [row(D_SSM), row(D_ATTN), row(D_MODEL), row(D_MODEL)]
                 + [_layer_spec(p, layer) for p in weights],
        out_specs=row(D_MODEL),
        out_shape=jax.ShapeDtypeStruct((n_rows, D_MODEL), F32),
        compiler_params=_params(("parallel",)),
        name="merge",
    )(*rows.operands, y, a, gs, ga, *weights)


def _ffn_kernel(rows, *refs):
    n = len(rows.specs)
    gpre_ref, wup_ref, wdown_ref, gpost_ref, o_ref = refs[n:]
    x = rows.load(refs[:n])
    h = _rms(x, gpre_ref[...]).astype(BF16)
    step = COL_STEP
    acc = jnp.zeros(x.shape, F32)
    for c in range(0, D_FF, step):
        a = jnp.maximum(_dot(h, wup_ref[:, c:c + step].astype(BF16)), 0.0)
        acc = acc + _dot((a * a).astype(BF16), wdown_ref[c:c + step, :].astype(BF16))
    o_ref[...] = x + _rms(acc, gpost_ref[...])


def _ffn(rows, n_rows, layer, gpre, wup, wdown, gpost, tm):
    weights = (gpre, wup, wdown, gpost)
    return pl.pallas_call(
        functools.partial(_ffn_kernel, rows),
        grid=(n_rows // tm,),
        in_specs=rows.specs + [_layer_spec(p, layer) for p in weights],
        out_specs=pl.BlockSpec((tm, D_MODEL), lambda i: (i, 0)),
        out_shape=jax.ShapeDtypeStruct((n_rows, D_MODEL), F32),
        compiler_params=_params(("parallel",)),
        name="ffn",
    )(*rows.operands, *weights)


def _rope_tables(n_blk):
    inv_freq = 1.0 / (ROPE_THETA ** (jnp.arange(0, HEAD_DIM, 2, dtype=F32) / HEAD_DIM))
    inv_freq = jnp.tile(inv_freq, LANES // (HEAD_DIM // 2))
    blk = (BLOCK * jnp.arange(n_blk, dtype=jnp.int32)).astype(F32)[:, None] * inv_freq
    off = (jnp.arange(BLOCK, dtype=jnp.int32) - PAD).astype(F32)[:, None] * inv_freq
    blk = jnp.stack([jnp.cos(blk), jnp.sin(blk)], axis=1)[:, :, None]
    return jnp.broadcast_to(blk, (n_blk, 2, SUBLANES, LANES)), jnp.stack([jnp.cos(off), jnp.sin(off)])


def _v_weights_t(w_in):
    return lax.optimization_barrier(w_in[:, :, C_V:C_V + D_KV].astype(BF16)).swapaxes(1, 2)


def kernel(x, meta_tokens, norm_mix_pre, norm_mix_post, norm_mlp_pre, norm_mlp_post, w_in, ssm_a_re, ssm_a_im, ssm_log_dt, ssm_b_re, ssm_b_im, ssm_c_re, ssm_c_im, ssm_d, w_glu, b_glu, attn_sinks, w_o_ssm, w_o_attn, w_out, w_up, w_down):
    bsz, seq, _ = x.shape
    depth = w_in.shape[0]
    seq_pad = PAD + N_META + seq
    assert seq_pad % BLOCK == 0 and (seq_pad // CHUNK) % N_SEG == 0
    n_blk, n_chunks = seq_pad // BLOCK, seq_pad // CHUNK
    tm, tm_out = 5 * BLOCK, 4 * BLOCK
    tq = 13 * BLOCK if n_blk % 13 == 0 else tm
    assert seq_pad % tm == 0 and seq % tm_out == 0 and seq_pad % tq == 0
    n_rows = bsz * seq_pad

    meta_block = jnp.concatenate([jnp.zeros((PAD, D_MODEL), x.dtype), meta_tokens.astype(x.dtype)], axis=0)
    stream = _Rows("input", x.reshape(bsz * seq, D_MODEL), tm, seq_pad // tm, seq // BLOCK, meta_block)
    rope_blk, rope_off = _rope_tables(n_blk)
    bias = _attn_bias()

    row = lambda m: m[:, None, :]
    wvt = _v_weights_t(w_in)
    prep = _ssm_prep(ssm_a_re, ssm_a_im, ssm_log_dt, ssm_b_re, ssm_b_im, ssm_c_re, ssm_c_im, ssm_d, n_chunks // N_SEG)

    for l in range(depth):
        u, q, k, vt, gs, ga = _inproj(stream, n_rows, l, row(norm_mix_pre), w_in, wvt, rope_blk, rope_off, tm)
        y = _ssm(u, prep, l, bsz, seq_pad)
        att = _attention(q, k, vt, attn_sinks[l], bias, bsz, n_blk, tq)
        hres = _merge(stream, l, y, att, gs, ga, w_glu, row(b_glu), w_o_ssm, w_o_attn, w_out, row(norm_mix_post), tm)
        ffn_w = (l, row(norm_mlp_pre), w_up, w_down, row(norm_mlp_post))
        if l + 1 < depth:
            stream = _Rows("padded", _ffn(_Rows("padded", hres, tm, seq_pad // tm, n_blk), n_rows, *ffn_w, tm),
                           tm, seq_pad // tm, n_blk)
        else:
            out = _ffn(_Rows("frames", hres, tm_out, seq // tm_out, n_blk), bsz * seq, *ffn_w, tm_out)
    return out.reshape(bsz, seq, D_MODEL)
```

```python
import functools
import math

import jax
import jax.numpy as jnp
import numpy as np
from jax import lax
from jax.experimental import pallas as pl
from jax.experimental.pallas import tpu as pltpu

D_MODEL = 1024
N_META = 16
HEAD_DIM = 64
N_Q_HEADS = 16
N_KV_HEADS = 4
D_ATTN = N_Q_HEADS * HEAD_DIM
D_KV = N_KV_HEADS * HEAD_DIM
BLOCK = 128
ROPE_THETA = 10000.0
ATTN_SCALE = HEAD_DIM ** -0.5
LOG2_E = math.log2(math.e)
NEG_INF = -1e30
D_SSM = D_MODEL // 2
SSM_GROUP = 16
SSM_STATE = 64
D_FF = 4 * D_MODEL
RMS_EPS = 1e-6

LANES = 128
SUBLANES = 8
MXU = 256
PAD = BLOCK - N_META
CHUNK = 16
TILE_GROUPS = LANES // SSM_GROUP
N_SSM_TILES = D_SSM // LANES
TILE_STATE = TILE_GROUPS * SSM_STATE
CHUNK_W = CHUNK * LANES
HALF_GROUPS = TILE_GROUPS // 2
HALF_LANES = LANES // 2
HALF_STATE = HALF_GROUPS * SSM_STATE
HALF_W = CHUNK * HALF_LANES
SUB_BLOCKS = MXU // HALF_LANES
N_SEG = SUBLANES
COL_STEP = 2 * MXU
VMEM_LIMIT = 56 * 1024 * 1024

C_U = 0
C_Q = C_U + D_SSM
C_K = C_Q + D_ATTN
C_V = C_K + D_KV
C_GS = C_V + D_KV
C_GA = C_GS + D_MODEL

BF16 = jnp.bfloat16
F32 = jnp.float32


def _dot(a, b):
    return jnp.dot(a, b, preferred_element_type=F32)


def _rms(x, gain):
    return x * lax.rsqrt(jnp.mean(x * x, axis=-1, keepdims=True) + RMS_EPS) * gain


def _const_spec(shape):
    return pl.BlockSpec(shape, lambda *_: (0,) * len(shape), pipeline_mode=pl.Buffered(1))


def _layer_spec(stacked, layer):
    shape = stacked.shape[1:]
    return pl.BlockSpec((None,) + shape, lambda *_: (layer,) + (0,) * len(shape), pipeline_mode=pl.Buffered(1))


def _params(sem):
    return pltpu.CompilerParams(dimension_semantics=sem, vmem_limit_bytes=VMEM_LIMIT)


class _Rows:
    def __init__(self, kind, array, tm, tiles_per_batch, blocks_per_batch, meta_block=None):
        self.has_meta = kind == "input"
        self.tiles_per_batch = tiles_per_batch
        nb = tm // BLOCK
        if kind == "padded":
            self.specs = [pl.BlockSpec((tm, D_MODEL), lambda i: (i, 0))]
            self.operands = [array]
            return
        shift = -1 if kind == "input" else 1

        def imap(j):
            return lambda i: ((i // tiles_per_batch) * blocks_per_batch
                              + jnp.maximum((i % tiles_per_batch) * nb + j + shift, 0), 0)

        self.specs = [pl.BlockSpec((BLOCK, D_MODEL), imap(j)) for j in range(nb)]
        self.operands = [array] * nb
        if self.has_meta:
            self.specs.append(_const_spec((BLOCK, D_MODEL)))
            self.operands.append(meta_block)

    def load(self, refs):
        n = len(self.specs)
        if n == 1:
            return refs[0][...]
        blocks = [r[...] for r in refs[:n - self.has_meta]]
        if self.has_meta:
            head = pl.program_id(0) % self.tiles_per_batch == 0
            blocks[0] = jnp.where(head, refs[n - 1][...], blocks[0])
        return jnp.concatenate(blocks, axis=0)


def _inproj_kernel(rows, *refs):
    n = len(rows.specs)
    gain_ref, w_ref, wvt_ref, blk_ref, off_ref, u_ref, q_ref, k_ref, vt_ref, gs_ref, ga_ref = refs[n:]
    h = _rms(rows.load(refs[:n]), gain_ref[...]).astype(BF16)

    cos_o, sin_o = off_ref[0], off_ref[1]
    cos, sin = [], []
    for b in range(blk_ref.shape[0]):
        cos_b, sin_b = (jnp.concatenate([blk_ref[b, p]] * (BLOCK // SUBLANES), axis=0) for p in range(2))
        cos.append(cos_b * cos_o - sin_b * sin_o)
        sin.append(sin_b * cos_o + cos_b * sin_o)
    cos, sin = jnp.concatenate(cos, axis=0), jnp.concatenate(sin, axis=0)
    first_half = lax.broadcasted_iota(jnp.int32, sin.shape, 1) % HEAD_DIM < HEAD_DIM // 2
    sa, sb = jnp.where(first_half, -sin, 0.0), jnp.where(first_half, 0.0, sin)

    def rope(t):
        return t * cos + pltpu.roll(t, LANES - HEAD_DIM // 2, 1) * sa + pltpu.roll(t, HEAD_DIM // 2, 1) * sb

    step = COL_STEP
    u_ref[...] = _dot(h, w_ref[:, C_U:C_U + D_SSM].astype(BF16))
    low = lax.broadcasted_iota(jnp.int32, sin.shape, 1) < HEAD_DIM
    for c in range(0, D_ATTN, step):
        t = _dot(h, w_ref[:, C_Q + c:C_Q + c + step].astype(BF16))
        tiles = [rope(t[:, j:j + LANES]) * (ATTN_SCALE * LOG2_E) for j in range(0, step, LANES)]
        for m in range(2):
            a, b = tiles[m], tiles[m + 2]
            pair = (jnp.where(low, a, pltpu.roll(b, HEAD_DIM, 1)), jnp.where(low, pltpu.roll(a, HEAD_DIM, 1), b))
            for e in range(2):
                j = c + (2 * m + e) * LANES
                q_ref[:, j:j + LANES] = pair[e].astype(BF16)
    t = _dot(h, w_ref[:, C_K:C_K + D_KV].astype(BF16))
    for j in range(0, D_KV, LANES):
        kj = rope(t[:, j:j + LANES]).astype(BF16)
        for b in range(k_ref.shape[0]):
            k_ref[b, :, j:j + LANES] = kj[b * BLOCK:(b + 1) * BLOCK]
    vt = lax.dot_general(wvt_ref[...], h, (((1,), (1,)), ((), ())), preferred_element_type=F32).astype(BF16)
    for b in range(vt_ref.shape[0]):
        vt_ref[b] = vt[:, b * BLOCK:(b + 1) * BLOCK]
    for c in range(0, D_MODEL, step):
        gs_ref[:, c:c + step] = _dot(h, w_ref[:, C_GS + c:C_GS + c + step].astype(BF16)).astype(BF16)
        ga_ref[:, c:c + step] = _dot(h, w_ref[:, C_GA + c:C_GA + c + step].astype(BF16)).astype(BF16)


def _inproj(rows, n_rows, layer, gain, w, wvt, rope_blk, rope_off, tm):
    row = lambda width: pl.BlockSpec((tm, width), lambda i: (i, 0))
    blk = pl.BlockSpec((tm // BLOCK,) + rope_blk.shape[1:], lambda i: (i % rows.tiles_per_batch, 0, 0, 0))
    out = lambda width, dtype: (row(width), jax.ShapeDtypeStruct((n_rows, width), dtype))
    per_block = lambda shape: (pl.BlockSpec((tm // BLOCK,) + shape, lambda i: (i, 0, 0)),
                               jax.ShapeDtypeStruct((n_rows // BLOCK,) + shape, BF16))
    outs = [out(D_SSM, F32),
            out(D_ATTN, BF16), per_block((BLOCK, D_KV)), per_block((D_KV, BLOCK)),
            out(D_MODEL, BF16), out(D_MODEL, BF16)]
    return pl.pallas_call(
        functools.partial(_inproj_kernel, rows),
        grid=(n_rows // tm,),
        in_specs=rows.specs + [_layer_spec(p, layer) for p in (gain, w, wvt)] + [blk, _const_spec(rope_off.shape)],
        out_specs=[o[0] for o in outs],
        out_shape=[o[1] for o in outs],
        compiler_params=_params(("parallel",)),
        name="inproj",
    )(*rows.operands, gain, w, wvt, rope_blk, rope_off)


def _ssm_spread(rp_ref, bbt_ref, cc_ref, dl_ref, pf_ref, ct_ref, wt_s, pb_s, qm_s):
    T = CHUNK
    iota = lambda shape, axis: lax.broadcasted_iota(jnp.int32, shape, axis)
    group_of = lambda shape, axis, width: (iota(shape, axis) // width) % HALF_GROUPS
    one_hot = lambda hit: jnp.where(hit, 1.0, 0.0).astype(BF16)

    even = iota((HALF_LANES, LANES), 1) < HALF_LANES
    none = jnp.zeros((HALF_LANES, LANES), BF16)
    shape = (HALF_W, HALF_LANES)
    row_copy = one_hot(iota(shape, 0) // SSM_GROUP == iota(shape, 1))
    shape = (LANES, 2 * HALF_STATE)
    state_copy = one_hot(iota(shape, 1) % SSM_STATE + iota(shape, 1) // HALF_STATE * SSM_STATE == iota(shape, 0))
    shape = (LANES, 2 * HALF_W)
    lane_copy = one_hot(iota(shape, 1) // SSM_GROUP == iota(shape, 0))
    shape = (HALF_W, LANES)
    tap_diag = group_of(shape, 0, SSM_GROUP) == group_of(shape, 1, SSM_GROUP)
    tap_skip = tap_diag & (iota(shape, 0) // HALF_LANES == T - 1) & (iota(shape, 0) % SSM_GROUP == iota(shape, 1) % SSM_GROUP)
    shape = (HALF_W, 2 * HALF_STATE)
    pb_diag = group_of(shape, 0, SSM_GROUP) == group_of(shape, 1, SSM_STATE)
    col_group = group_of((SSM_STATE, HALF_W), 1, SSM_GROUP)

    for h in range(2):
        x = _dot(row_copy, rp_ref[0, h].astype(BF16))
        y = jnp.concatenate([bbt_ref[0, h]] * T, axis=0)
        w = (x[:, :LANES] * y[:, :LANES] + x[:, LANES:] * y[:, LANES:]).astype(BF16)
        pb_s[h] = jnp.where(pb_diag, _dot(w, state_copy), 0.0).astype(BF16)

        taps = _dot(w, cc_ref[0, h].astype(BF16)) + jnp.where(tap_skip, jnp.concatenate([dl_ref[0, h]] * (HALF_W // SUBLANES), axis=0), 0.0)
        taps = jnp.where(tap_diag, taps, 0.0).astype(BF16)
        block = lambda lag: taps[(T - 1 - lag) * HALF_LANES:(T - lag) * HALF_LANES] if lag >= 0 else none
        for a in range(SUB_BLOCKS):
            for i in range(SUB_BLOCKS):
                for kk in range(SUB_BLOCKS // 2):
                    lag = SUB_BLOCKS * a + 2 * kk - i
                    wt_s[h, a, i * HALF_LANES:(i + 1) * HALF_LANES, kk * LANES:(kk + 1) * LANES] = (
                        jnp.where(even, block(lag), block(lag + 1)))

        pw = _dot(pf_ref[0, h].astype(BF16), lane_copy)
        p_re, p_im = pw[:, :HALF_W], pw[:, HALF_W:]
        c_re = jnp.concatenate([ct_ref[0, h, 0]] * (HALF_W // LANES), axis=1)
        c_im = jnp.concatenate([ct_ref[0, h, 1]] * (HALF_W // LANES), axis=1)
        for part, q in enumerate((c_re * p_re - c_im * p_im, -(c_re * p_im + c_im * p_re))):
            for g in range(HALF_GROUPS):
                r = part * HALF_STATE + g * SSM_STATE
                qm_s[h, r:r + SSM_STATE, :] = jnp.where(col_group == g, q, 0.0).astype(BF16)


def _ssm_kernel(u_ref, rp_ref, bbt_ref, cc_ref, dl_ref, pf_ref, ct_ref, a_ref, pw_ref, y_ref,
                z_s, v_s, l_s, s_s, wt_s, pb_s, qm_s):
    n_chunks = z_s.shape[0]
    seg = n_chunks // N_SEG
    half_tiles = HALF_STATE // LANES
    part_tiles = lambda part: [h * 2 * half_tiles + part * half_tiles + c for h in range(2) for c in range(half_tiles)]

    @pl.when(pl.program_id(1) == 0)
    def _():
        _ssm_spread(rp_ref, bbt_ref, cc_ref, dl_ref, pf_ref, ct_ref, wt_s, pb_s, qm_s)

    low = lax.broadcasted_iota(jnp.int32, (n_chunks, LANES), 1) < HALF_LANES

    def regroup(a, b):
        return jnp.where(low, a, pltpu.roll(b, HALF_LANES, 1)), jnp.where(low, pltpu.roll(a, HALF_LANES, 1), b)

    for m in range(CHUNK // 2):
        lo, hi = regroup(u_ref[pl.ds(2 * m, n_chunks, stride=CHUNK), :], u_ref[pl.ds(2 * m + 1, n_chunks, stride=CHUNK), :])
        z_s[:, m * LANES:(m + 1) * LANES] = lo.astype(BF16)
        z_s[:, HALF_W + m * LANES:HALF_W + (m + 1) * LANES] = hi.astype(BF16)
    for h in range(2):
        v = _dot(z_s[:, h * HALF_W:(h + 1) * HALF_W], pb_s[h])
        for c in range(2 * half_tiles):
            v_s[h * 2 * half_tiles + c] = v[:, c * LANES:(c + 1) * LANES]

    def load(ref, rows, part):
        return jnp.concatenate([ref[c, rows, :] for c in part_tiles(part)], axis=1)

    def store(ref, rows, part, val):
        for k, c in enumerate(part_tiles(part)):
            ref[c, rows, :] = val[:, k * LANES:(k + 1) * LANES]

    are, aim = a_ref[0, 0], a_ref[0, 1]

    def local_scan(i, carry):
        sre, sim = carry
        rows = pl.ds(i, N_SEG, stride=seg)
        store(l_s, rows, 0, sre)
        store(l_s, rows, 1, sim)
        return (are * sre - aim * sim + load(v_s, rows, 0), are * sim + aim * sre + load(v_s, rows, 1))

    zero = jnp.zeros((N_SEG, TILE_STATE), F32)
    ere, eim = lax.fori_loop(0, seg, local_scan, (zero, zero), unroll=5)

    bre, bim = a_ref[0, 2][:1], a_ref[0, 3][:1]
    tre, tim = [zero[:1]], [zero[:1]]
    for m in range(1, N_SEG):
        pre, pim = tre[-1], tim[-1]
        tre.append(bre * pre - bim * pim + ere[m - 1:m])
        tim.append(bre * pim + bim * pre + eim[m - 1:m])
    tre, tim = jnp.concatenate(tre, axis=0), jnp.concatenate(tim, axis=0)

    def add_carry(i, _):
        rows = pl.ds(i, N_SEG, stride=seg)
        pre, pim = pw_ref[0, 0, pl.ds(i, 1), :], pw_ref[0, 1, pl.ds(i, 1), :]
        store(l_s, rows, 0, load(l_s, rows, 0) + (pre * tre - pim * tim))
        store(l_s, rows, 1, load(l_s, rows, 1) + (pre * tim + pim * tre))
        return 0

    lax.fori_loop(0, seg, add_carry, 0, unroll=5)

    for c in range(4 * half_tiles):
        s_s[:, c * LANES:(c + 1) * LANES] = l_s[c].astype(BF16)
    for tp in range(SUB_BLOCKS):
        acc = []
        for h in range(2):
            y = _dot(s_s[:, h * 2 * HALF_STATE:(h + 1) * 2 * HALF_STATE], qm_s[h, :, tp * MXU:(tp + 1) * MXU])
            for tq in range(tp + 1):
                y = y + _dot(z_s[:, h * HALF_W + tq * MXU:h * HALF_W + (tq + 1) * MXU], wt_s[h, tp - tq])
            acc.append(y)
        for kk in range(SUB_BLOCKS // 2):
            ya, yb = regroup(acc[0][:, kk * LANES:(kk + 1) * LANES], acc[1][:, kk * LANES:(kk + 1) * LANES])
            t = SUB_BLOCKS * tp + 2 * kk
            y_ref[pl.ds(t, n_chunks, stride=CHUNK), :] = ya
            y_ref[pl.ds(t + 1, n_chunks, stride=CHUNK), :] = yb


def _ssm(u, prep, layer, bsz, seq_pad):
    n_chunks = seq_pad // CHUNK
    tile = lambda shape: pl.BlockSpec((None, 1) + shape, lambda j, b: (layer, j) + (0,) * len(shape))
    io = pl.BlockSpec((seq_pad, LANES), lambda j, b: (b, j))
    return pl.pallas_call(
        _ssm_kernel,
        grid=(N_SSM_TILES, bsz),
        in_specs=[io] + [tile(p.shape[2:]) for p in prep],
        out_specs=io,
        out_shape=jax.ShapeDtypeStruct(u.shape, F32),
        scratch_shapes=[pltpu.VMEM((n_chunks, CHUNK_W), BF16),
                        pltpu.VMEM((2 * TILE_STATE // LANES, n_chunks, LANES), F32),
                        pltpu.VMEM((2 * TILE_STATE // LANES, n_chunks, LANES), F32),
                        pltpu.VMEM((n_chunks, 2 * TILE_STATE), BF16),
                        pltpu.VMEM((2, SUB_BLOCKS, MXU, MXU), BF16),
                        pltpu.VMEM((2, HALF_W, 2 * HALF_STATE), BF16),
                        pltpu.VMEM((2, 2 * HALF_STATE, HALF_W), BF16)],
        compiler_params=_params(("arbitrary", "arbitrary")),
        name="ssm",
    )(u, *prep)


def _ssm_prep(a_re, a_im, log_dt, b_re, b_im, c_re, c_im, d_skip, seg):
    N, C, T, J, E = SSM_STATE, SSM_GROUP, CHUNK, N_SSM_TILES, TILE_GROUPS
    depth = a_re.shape[0]
    tiles = lambda m: m.reshape((depth, J, E) + m.shape[2:])
    a_re, a_im, dt = tiles(a_re), tiles(a_im), tiles(jnp.exp(log_dt))[..., None]
    lam_re, lam_im = a_re * dt, a_im * dt

    def powers(p, lam_re, lam_im):
        p = jnp.asarray(p, F32).reshape((-1,) + (1,) * (lam_re.ndim - 2))
        mag = jnp.exp(lam_re[:, :, None] * p)
        return mag * jnp.cos(lam_im[:, :, None] * p), mag * jnp.sin(lam_im[:, :, None] * p)

    pw_re, pw_im = powers(np.arange(T + 1), lam_re, lam_im)
    den = a_re * a_re + a_im * a_im
    nre, nim = pw_re[:, :, 1] - 1.0, pw_im[:, :, 1]
    coef_re = ((nre * a_re + nim * a_im) / den)[..., None, :]
    coef_im = ((nim * a_re - nre * a_im) / den)[..., None, :]
    bt_re, bt_im = tiles(b_re).swapaxes(-1, -2), tiles(b_im).swapaxes(-1, -2)
    halves = lambda m, axis: m.reshape(m.shape[:axis] + (2, HALF_GROUPS) + m.shape[axis + 1:])
    slab = lambda m: m.reshape(m.shape[:3] + (-1, m.shape[-1]))
    cat = lambda *ms: jnp.concatenate(ms, axis=-1)
    bb_re = slab(halves(coef_re * bt_re - coef_im * bt_im, 2))
    bb_im = slab(halves(coef_re * bt_im + coef_im * bt_re, 2))
    bbt = cat(bb_re, bb_re, -bb_im, bb_im)

    rev = lambda m: slab(halves(m[:, :, T - 1::-1], 3).swapaxes(2, 3))
    r_re, r_im = rev(pw_re), rev(pw_im)
    rp = cat(r_re, r_im, r_im, r_re)

    cr, ci = halves(tiles(c_re), 2), halves(tiles(c_im), 2)
    by_state = lambda m: m.reshape(depth, J, 2, N, HALF_LANES)
    crt, cit = by_state(cr.transpose(0, 1, 2, 5, 3, 4)), by_state(ci.transpose(0, 1, 2, 5, 3, 4))
    cc = jnp.concatenate([cat(crt, crt), cat(-cit, -cit)], axis=-2)
    skip = halves(tiles(d_skip), 2).reshape(depth, J, 2, 1, HALF_LANES)
    dl = jnp.broadcast_to(cat(skip, skip), (depth, J, 2, SUBLANES, LANES))

    fwd = lambda m: by_state(halves(m[:, :, 1:], 3).transpose(0, 1, 3, 5, 2, 4))
    pf = cat(fwd(pw_re), fwd(pw_im))
    ct = jnp.stack([cat(crt, crt), cat(cit, cit)], axis=3)

    flat = lambda m: m.reshape(depth, J, TILE_STATE)
    t_re, t_im = powers(T * np.array([1, seg]), flat(lam_re), flat(lam_im))
    a = jnp.stack([t_re[:, :, 0], t_im[:, :, 0], t_re[:, :, 1], t_im[:, :, 1]], axis=2)[:, :, :, None]
    a = jnp.broadcast_to(a, (depth, J, 4, SUBLANES, TILE_STATE))
    pw = jnp.stack(powers(T * np.arange(seg), flat(lam_re), flat(lam_im)), axis=2)
    return rp, bbt, cc, dl, pf, ct, a, pw


def _attn_head_order():
    per_kv = N_Q_HEADS // N_KV_HEADS
    return [(2 * j + e) * per_kv + g for j in range(N_KV_HEADS // 2) for g in range(per_kv) for e in range(2)]


def _attn_kernel(sink_ref, bias_ref, q_ref, k_ref, vt_ref, o_ref):
    n_keys = N_META + 2 * BLOCK
    n_local = q_ref.shape[0] // BLOCK
    first_k = lax.broadcasted_iota(jnp.int32, (n_keys, LANES), 1) < HEAD_DIM
    first_v = lax.broadcasted_iota(jnp.int32, (LANES, 3 * BLOCK), 0) < HEAD_DIM
    first_o = lax.broadcasted_iota(jnp.int32, (LANES, 2 * BLOCK), 0) < HEAD_DIM
    zero_k = jnp.zeros((n_keys, LANES), BF16)
    zero_v = jnp.zeros((LANES, 3 * BLOCK), BF16)
    pad_p = jnp.zeros((PAD, 2 * BLOCK), BF16)
    shape = (2 * SUBLANES, 6 * BLOCK)
    key_rows = jnp.where(lax.broadcasted_iota(jnp.int32, shape, 0) == lax.broadcasted_iota(jnp.int32, shape, 1) // (3 * BLOCK),
                         1.0, 0.0).astype(BF16)

    def block_of(i):
        n = pl.program_id(1) * n_local + i
        return n, jnp.maximum(n - 1, 0)

    def scores(i, tp):
        n, prev = block_of(i)
        cols = slice(tp // 2 * LANES, (tp // 2 + 1) * LANES)
        k2 = jnp.concatenate([k_ref[0, PAD:, cols], k_ref[prev, :, cols], k_ref[n, :, cols]], axis=0)
        kab = jnp.concatenate([jnp.where(first_k, k2, zero_k), jnp.where(first_k, zero_k, k2)], axis=0)
        q4 = jnp.concatenate([q_ref[i * BLOCK:(i + 1) * BLOCK, (2 * tp + p) * LANES:(2 * tp + p + 1) * LANES]
                              for p in range(2)], axis=0)
        return lax.dot_general(kab, q4, (((1,), (1,)), ((), ())), preferred_element_type=F32)

    heads = _attn_head_order()
    chains = [(i, tp) for i in range(n_local) for tp in range(N_Q_HEADS // 4)]
    s_next = scores(*chains[0])
    for c, (i, tp) in enumerate(chains):
        s = s_next
        if c + 1 < len(chains):
            s_next = scores(*chains[c + 1])
        n, prev = block_of(i)
        bias = bias_ref[jnp.minimum(n, 2)]
        pt, tail = [], []
        for e in range(2):
            blocks, tl = [], []
            for p in range(2):
                se = s[e * n_keys:(e + 1) * n_keys, p * LANES:(p + 1) * LANES] + bias
                sink = sink_ref[heads[2 * (2 * tp + p) + e]] * LOG2_E
                m = jnp.maximum(jnp.max(se, axis=0, keepdims=True), sink)
                blocks.append(jnp.exp2(se - m).astype(BF16))
                tl.append(jnp.exp2(sink - m))
            pt += [pad_p, jnp.concatenate(blocks, axis=1)]
            tail.append(jnp.concatenate(tl, axis=1))
        rows = slice(tp // 2 * LANES, (tp // 2 + 1) * LANES)
        v2 = jnp.concatenate([vt_ref[0, rows, :], vt_ref[prev, rows, :], vt_ref[n, rows, :]], axis=1)
        vab = jnp.concatenate([jnp.where(first_v, v2, zero_v), jnp.where(first_v, zero_v, v2)], axis=1)
        vab = jnp.concatenate([vab, key_rows], axis=0)
        ot = _dot(vab, jnp.concatenate(pt, axis=0))
        rden = [1.0 / (ot[LANES + e:LANES + e + 1] + tail[e]) for e in range(2)]
        ot = ot[:LANES] * jnp.where(first_o, rden[0], rden[1])
        for e in range(2):
            rows = slice(e * HEAD_DIM, (e + 1) * HEAD_DIM)
            o = jnp.concatenate([ot[rows, :BLOCK], ot[rows, BLOCK:]], axis=0).T
            t = 4 * (tp // 2) + 2 * e + tp % 2
            o_ref[i * BLOCK:(i + 1) * BLOCK, t * LANES:(t + 1) * LANES] = o.astype(BF16)


def _attention(q, k, vt, sinks, bias, bsz, n_blk, tq):
    steps = n_blk * BLOCK // tq
    rows = pl.BlockSpec((tq, D_ATTN), lambda b, s: (b * steps + s, 0))
    return pl.pallas_call(
        _attn_kernel,
        grid=(bsz, steps),
        in_specs=[pl.BlockSpec(memory_space=pltpu.SMEM), _const_spec(bias.shape), rows,
                  pl.BlockSpec((n_blk, BLOCK, D_KV), lambda b, s: (b, 0, 0)),
                  pl.BlockSpec((n_blk, D_KV, BLOCK), lambda b, s: (b, 0, 0))],
        out_specs=rows,
        out_shape=jax.ShapeDtypeStruct(q.shape, BF16),
        compiler_params=_params(("parallel", "parallel")),
        name="attention",
    )(sinks, bias, q, k, vt)


def _attn_bias():
    i = np.arange(BLOCK)[None, :]
    j = np.arange(BLOCK)[:, None]
    none = np.zeros((BLOCK, BLOCK), bool)
    causal = j <= i
    in_meta = np.broadcast_to(j >= PAD, (BLOCK, BLOCK))
    blk0 = [none[:N_META], none, causal & in_meta]
    blk1 = [none[:N_META], in_meta, causal]
    blk2 = [~none[:N_META], j > i, causal]
    ok = np.stack([np.concatenate(b, axis=0) for b in (blk0, blk1, blk2)])
    return jnp.asarray(np.where(ok, 0.0, NEG_INF), F32)


def _merge_kernel(rows, *refs):
    n = len(rows.specs)
    y_ref, a_ref, gs_ref, ga_ref, wglu_ref, bglu_ref, wos_ref, woa_ref, wout_ref, gain_ref, o_ref = refs[n:]
    z = jax.nn.gelu(y_ref[...])
    z = z * jax.nn.sigmoid(_dot(z.astype(BF16), wglu_ref[...].astype(BF16)) + bglu_ref[...])
    merged = (jax.nn.sigmoid(gs_ref[...].astype(F32)) * _dot(z.astype(BF16), wos_ref[...].astype(BF16))
              + jax.nn.sigmoid(ga_ref[...].astype(F32)) * _dot(a_ref[...], woa_ref[...].astype(BF16)))
    mix = _dot(merged.astype(BF16), wout_ref[...].astype(BF16))
    o_ref[...] = rows.load(refs[:n]) + _rms(mix, gain_ref[...])


def _merge(rows, layer, y, a, gs, ga, wglu, bglu, wos, woa, wout, gain, tm):
    n_rows = y.shape[0]
    row = lambda width: pl.BlockSpec((tm, width), lambda i: (i, 0))
    weights = (wglu, bglu, wos, woa, wout, gain)
    return pl.pallas_call(
        functools.partial(_merge_kernel, rows),
        grid=(n_rows // tm,),
        in_specs=rows.specs + [row(D_SSM), row(D_ATTN), row(D_MODEL), row(D_MODEL)]
                 + [_layer_spec(p, layer) for p in weights],
        out_specs=row(D_MODEL),
        out_shape=jax.ShapeDtypeStruct((n_rows, D_MODEL), F32),
        compiler_params=_params(("parallel",)),
        name="merge",
    )(*rows.operands, y, a, gs, ga, *weights)


def _ffn_kernel(rows, *refs):
    n = len(rows.specs)
    gpre_ref, wup_ref, wdown_ref, gpost_ref, o_ref = refs[n:]
    x = rows.load(refs[:n])
    h = _rms(x, gpre_ref[...]).astype(BF16)
    step = COL_STEP
    acc = jnp.zeros(x.shape, F32)
    for c in range(0, D_FF, step):
        a = jnp.maximum(_dot(h, wup_ref[:, c:c + step].astype(BF16)), 0.0)
        acc = acc + _dot((a * a).astype(BF16), wdown_ref[c:c + step, :].astype(BF16))
    o_ref[...] = x + _rms(acc, gpost_ref[...])


def _ffn(rows, n_rows, layer, gpre, wup, wdown, gpost, tm):
    weights = (gpre, wup, wdown, gpost)
    return pl.pallas_call(
        functools.partial(_ffn_kernel, rows),
        grid=(n_rows // tm,),
        in_specs=rows.specs + [_layer_spec(p, layer) for p in weights],
        out_specs=pl.BlockSpec((tm, D_MODEL), lambda i: (i, 0)),
        out_shape=jax.ShapeDtypeStruct((n_rows, D_MODEL), F32),
        compiler_params=_params(("parallel",)),
        name="ffn",
    )(*rows.operands, *weights)


def _rope_tables(n_blk):
    inv_freq = 1.0 / (ROPE_THETA ** (jnp.arange(0, HEAD_DIM, 2, dtype=F32) / HEAD_DIM))
    inv_freq = jnp.tile(inv_freq, LANES // (HEAD_DIM // 2))
    blk = (BLOCK * jnp.arange(n_blk, dtype=jnp.int32)).astype(F32)[:, None] * inv_freq
    off = (jnp.arange(BLOCK, dtype=jnp.int32) - PAD).astype(F32)[:, None] * inv_freq
    blk = jnp.stack([jnp.cos(blk), jnp.sin(blk)], axis=1)[:, :, None]
    return jnp.broadcast_to(blk, (n_blk, 2, SUBLANES, LANES)), jnp.stack([jnp.cos(off), jnp.sin(off)])


def _v_weights_t(w_in):
    return lax.optimization_barrier(w_in[:, :, C_V:C_V + D_KV].astype(BF16)).swapaxes(1, 2)


def kernel(x, meta_tokens, norm_mix_pre, norm_mix_post, norm_mlp_pre, norm_mlp_post, w_in, ssm_a_re, ssm_a_im, ssm_log_dt, ssm_b_re, ssm_b_im, ssm_c_re, ssm_c_im, ssm_d, w_glu, b_glu, attn_sinks, w_o_ssm, w_o_attn, w_out, w_up, w_down):
    bsz, seq, _ = x.shape
    depth = w_in.shape[0]
    seq_pad = PAD + N_META + seq
    assert seq_pad % BLOCK == 0 and (seq_pad // CHUNK) % N_SEG == 0
    n_blk, n_chunks = seq_pad // BLOCK, seq_pad // CHUNK
    tm, tm_out = 5 * BLOCK, 4 * BLOCK
    tq = 13 * BLOCK if n_blk % 13 == 0 else tm
    assert seq_pad % tm == 0 and seq % tm_out == 0 and seq_pad % tq == 0
    n_rows = bsz * seq_pad

    meta_block = jnp.concatenate([jnp.zeros((PAD, D_MODEL), x.dtype), meta_tokens.astype(x.dtype)], axis=0)
    stream = _Rows("input", x.reshape(bsz * seq, D_MODEL), tm, seq_pad // tm, seq // BLOCK, meta_block)
    rope_blk, rope_off = _rope_tables(n_blk)
    bias = _attn_bias()

    row = lambda m: m[:, None, :]
    wvt = _v_weights_t(w_in)
    prep = _ssm_prep(ssm_a_re, ssm_a_im, ssm_log_dt, ssm_b_re, ssm_b_im, ssm_c_re, ssm_c_im, ssm_d, n_chunks // N_SEG)

    for l in range(depth):
        u, q, k, vt, gs, ga = _inproj(stream, n_rows, l, row(norm_mix_pre), w_in, wvt, rope_blk, rope_off, tm)
        y = _ssm(u, prep, l, bsz, seq_pad)
        att = _attention(q, k, vt, attn_sinks[l], bias, bsz, n_blk, tq)
        hres = _merge(stream, l, y, att, gs, ga, w_glu, row(b_glu), w_o_ssm, w_o_attn, w_out, row(norm_mix_post), tm)
        ffn_w = (l, row(norm_mlp_pre), w_up, w_down, row(norm_mlp_post))
        if l + 1 < depth:
            stream = _Rows("padded", _ffn(_Rows("padded", hres, tm, seq_pad // tm, n_blk), n_rows, *ffn_w, tm),
                           tm, seq_pad // tm, n_blk)
        else:
            out = _ffn(_Rows("frames", hres, tm_out, seq // tm_out, n_blk), bsz * seq, *ffn_w, tm_out)
    return out.reshape(bsz, seq, D_MODEL)
```

```python
import functools
import math

import jax
import jax.numpy as jnp
import numpy as np
from jax import lax
from jax.experimental import pallas as pl
from jax.experimental.pallas import tpu as pltpu

D_MODEL = 1024
N_META = 16
HEAD_DIM = 64
N_Q_HEADS = 16
N_KV_HEADS = 4
D_ATTN = N_Q_HEADS * HEAD_DIM
D_KV = N_KV_HEADS * HEAD_DIM
BLOCK = 128
ROPE_THETA = 10000.0
ATTN_SCALE = HEAD_DIM ** -0.5
LOG2_E = math.log2(math.e)
NEG_INF = -1e30
D_SSM = D_MODEL // 2
SSM_GROUP = 16
SSM_STATE = 64
D_FF = 4 * D_MODEL
RMS_EPS = 1e-6

LANES = 128
SUBLANES = 8
MXU = 256
PAD = BLOCK - N_META
CHUNK = 16
TILE_GROUPS = LANES // SSM_GROUP
N_SSM_TILES = D_SSM // LANES
TILE_STATE = TILE_GROUPS * SSM_STATE
CHUNK_W = CHUNK * LANES
HALF_GROUPS = TILE_GROUPS // 2
HALF_LANES = LANES // 2
HALF_STATE = HALF_GROUPS * SSM_STATE
HALF_W = CHUNK * HALF_LANES
SUB_BLOCKS = MXU // HALF_LANES
N_SEG = SUBLANES
COL_STEP = 2 * MXU
VMEM_LIMIT = 56 * 1024 * 1024

C_U = 0
C_Q = C_U + D_SSM
C_K = C_Q + D_ATTN
C_V = C_K + D_KV
C_GS = C_V + D_KV
C_GA = C_GS + D_MODEL

BF16 = jnp.bfloat16
F32 = jnp.float32


def _dot(a, b):
    return jnp.dot(a, b, preferred_element_type=F32)


def _rms(x, gain):
    return x * lax.rsqrt(jnp.mean(x * x, axis=-1, keepdims=True) + RMS_EPS) * gain


def _const_spec(shape):
    return pl.BlockSpec(shape, lambda *_: (0,) * len(shape), pipeline_mode=pl.Buffered(1))


def _layer_spec(stacked, layer):
    shape = stacked.shape[1:]
    return pl.BlockSpec((None,) + shape, lambda *_: (layer,) + (0,) * len(shape), pipeline_mode=pl.Buffered(1))


def _params(sem):
    return pltpu.CompilerParams(dimension_semantics=sem, vmem_limit_bytes=VMEM_LIMIT)


class _Rows:
    def __init__(self, kind, array, tm, tiles_per_batch, blocks_per_batch, meta_block=None):
        self.has_meta = kind == "input"
        self.tiles_per_batch = tiles_per_batch
        nb = tm // BLOCK
        if kind == "padded":
            self.specs = [pl.BlockSpec((tm, D_MODEL), lambda i: (i, 0))]
            self.operands = [array]
            return
        shift = -1 if kind == "input" else 1

        def imap(j):
            return lambda i: ((i // tiles_per_batch) * blocks_per_batch
                              + jnp.maximum((i % tiles_per_batch) * nb + j + shift, 0), 0)

        self.specs = [pl.BlockSpec((BLOCK, D_MODEL), imap(j)) for j in range(nb)]
        self.operands = [array] * nb
        if self.has_meta:
            self.specs.append(_const_spec((BLOCK, D_MODEL)))
            self.operands.append(meta_block)

    def load(self, refs):
        n = len(self.specs)
        if n == 1:
            return refs[0][...]
        blocks = [r[...] for r in refs[:n - self.has_meta]]
        if self.has_meta:
            head = pl.program_id(0) % self.tiles_per_batch == 0
            blocks[0] = jnp.where(head, refs[n - 1][...], blocks[0])
        return jnp.concatenate(blocks, axis=0)


def _inproj_kernel(rows, *refs):
    n = len(rows.specs)
    gain_ref, w_ref, wvt_ref, blk_ref, off_ref, u_ref, q_ref, k_ref, vt_ref, gs_ref, ga_ref = refs[n:]
    h = _rms(rows.load(refs[:n]), gain_ref[...]).astype(BF16)

    cos_o, sin_o = off_ref[0], off_ref[1]
    cos, sin = [], []
    for b in range(blk_ref.shape[0]):
        cos_b, sin_b = (jnp.concatenate([blk_ref[b, p]] * (BLOCK // SUBLANES), axis=0) for p in range(2))
        cos.append(cos_b * cos_o - sin_b * sin_o)
        sin.append(sin_b * cos_o + cos_b * sin_o)
    cos, sin = jnp.concatenate(cos, axis=0), jnp.concatenate(sin, axis=0)
    first_half = lax.broadcasted_iota(jnp.int32, sin.shape, 1) % HEAD_DIM < HEAD_DIM // 2
    sa, sb = jnp.where(first_half, -sin, 0.0), jnp.where(first_half, 0.0, sin)

    def rope(t):
        return t * cos + pltpu.roll(t, LANES - HEAD_DIM // 2, 1) * sa + pltpu.roll(t, HEAD_DIM // 2, 1) * sb

    step = COL_STEP
    u_ref[...] = _dot(h, w_ref[:, C_U:C_U + D_SSM].astype(BF16))
    low = lax.broadcasted_iota(jnp.int32, sin.shape, 1) < HEAD_DIM
    for c in range(0, D_ATTN, step):
        t = _dot(h, w_ref[:, C_Q + c:C_Q + c + step].astype(BF16))
        tiles = [rope(t[:, j:j + LANES]) * (ATTN_SCALE * LOG2_E) for j in range(0, step, LANES)]
        for m in range(2):
            a, b = tiles[m], tiles[m + 2]
            pair = (jnp.where(low, a, pltpu.roll(b, HEAD_DIM, 1)), jnp.where(low, pltpu.roll(a, HEAD_DIM, 1), b))
            for e in range(2):
                j = c + (2 * m + e) * LANES
                q_ref[:, j:j + LANES] = pair[e].astype(BF16)
    t = _dot(h, w_ref[:, C_K:C_K + D_KV].astype(BF16))
    for j in range(0, D_KV, LANES):
        kj = rope(t[:, j:j + LANES]).astype(BF16)
        for b in range(k_ref.shape[0]):
            k_ref[b, :, j:j + LANES] = kj[b * BLOCK:(b + 1) * BLOCK]
    vt = lax.dot_general(wvt_ref[...], h, (((1,), (1,)), ((), ())), preferred_element_type=F32).astype(BF16)
    for b in range(vt_ref.shape[0]):
        vt_ref[b] = vt[:, b * BLOCK:(b + 1) * BLOCK]
    for c in range(0, D_MODEL, step):
        gs_ref[:, c:c + step] = _dot(h, w_ref[:, C_GS + c:C_GS + c + step].astype(BF16)).astype(BF16)
        ga_ref[:, c:c + step] = _dot(h, w_ref[:, C_GA + c:C_GA + c + step].astype(BF16)).astype(BF16)


def _inproj(rows, n_rows, layer, gain, w, wvt, rope_blk, rope_off, tm):
    row = lambda width: pl.BlockSpec((tm, width), lambda i: (i, 0))
    blk = pl.BlockSpec((tm // BLOCK,) + rope_blk.shape[1:], lambda i: (i % rows.tiles_per_batch, 0, 0, 0))
    out = lambda width, dtype: (row(width), jax.ShapeDtypeStruct((n_rows, width), dtype))
    per_block = lambda shape: (pl.BlockSpec((tm // BLOCK,) + shape, lambda i: (i, 0, 0)),
                               jax.ShapeDtypeStruct((n_rows // BLOCK,) + shape, BF16))
    outs = [out(D_SSM, F32),
            out(D_ATTN, BF16), per_block((BLOCK, D_KV)), per_block((D_KV, BLOCK)),
            out(D_MODEL, BF16), out(D_MODEL, BF16)]
    return pl.pallas_call(
        functools.partial(_inproj_kernel, rows),
        grid=(n_rows // tm,),
        in_specs=rows.specs + [_layer_spec(p, layer) for p in (gain, w, wvt)] + [blk, _const_spec(rope_off.shape)],
        out_specs=[o[0] for o in outs],
        out_shape=[o[1] for o in outs],
        compiler_params=_params(("parallel",)),
        name="inproj",
    )(*rows.operands, gain, w, wvt, rope_blk, rope_off)


def _ssm_spread(rp_ref, bbt_ref, cc_ref, dl_ref, pf_ref, ct_ref, wt_s, pb_s, qm_s):
    T = CHUNK
    iota = lambda shape, axis: lax.broadcasted_iota(jnp.int32, shape, axis)
    group_of = lambda shape, axis, width: (iota(shape, axis) // width) % HALF_GROUPS
    one_hot = lambda hit: jnp.where(hit, 1.0, 0.0).astype(BF16)

    even = iota((HALF_LANES, LANES), 1) < HALF_LANES
    none = jnp.zeros((HALF_LANES, LANES), BF16)
    shape = (HALF_W, HALF_LANES)
    row_copy = one_hot(iota(shape, 0) // SSM_GROUP == iota(shape, 1))
    shape = (LANES, 2 * HALF_STATE)
    state_copy = one_hot(iota(shape, 1) % SSM_STATE + iota(shape, 1) // HALF_STATE * SSM_STATE == iota(shape, 0))
    shape = (LANES, 2 * HALF_W)
    lane_copy = one_hot(iota(shape, 1) // SSM_GROUP == iota(shape, 0))
    shape = (HALF_W, LANES)
    tap_diag = group_of(shape, 0, SSM_GROUP) == group_of(shape, 1, SSM_GROUP)
    tap_skip = tap_diag & (iota(shape, 0) // HALF_LANES == T - 1) & (iota(shape, 0) % SSM_GROUP == iota(shape, 1) % SSM_GROUP)
    shape = (HALF_W, 2 * HALF_STATE)
    pb_diag = group_of(shape, 0, SSM_GROUP) == group_of(shape, 1, SSM_STATE)
    col_group = group_of((SSM_STATE, HALF_W), 1, SSM_GROUP)

    for h in range(2):
        x = _dot(row_copy, rp_ref[0, h].astype(BF16))
        y = jnp.concatenate([bbt_ref[0, h]] * T, axis=0)
        w = (x[:, :LANES] * y[:, :LANES] + x[:, LANES:] * y[:, LANES:]).astype(BF16)
        pb_s[h] = jnp.where(pb_diag, _dot(w, state_copy), 0.0).astype(BF16)

        taps = _dot(w, cc_ref[0, h].astype(BF16)) + jnp.where(tap_skip, jnp.concatenate([dl_ref[0, h]] * (HALF_W // SUBLANES), axis=0), 0.0)
        taps = jnp.where(tap_diag, taps, 0.0).astype(BF16)
        block = lambda lag: taps[(T - 1 - lag) * HALF_LANES:(T - lag) * HALF_LANES] if lag >= 0 else none
        for a in range(SUB_BLOCKS):
            for i in range(SUB_BLOCKS):
                for kk in range(SUB_BLOCKS // 2):
                    lag = SUB_BLOCKS * a + 2 * kk - i
                    wt_s[h, a, i * HALF_LANES:(i + 1) * HALF_LANES, kk * LANES:(kk + 1) * LANES] = (
                        jnp.where(even, block(lag), block(lag + 1)))

        pw = _dot(pf_ref[0, h].astype(BF16), lane_copy)
        p_re, p_im = pw[:, :HALF_W], pw[:, HALF_W:]
        c_re = jnp.concatenate([ct_ref[0, h, 0]] * (HALF_W // LANES), axis=1)
        c_im = jnp.concatenate([ct_ref[0, h, 1]] * (HALF_W // LANES), axis=1)
        for part, q in enumerate((c_re * p_re - c_im * p_im, -(c_re * p_im + c_im * p_re))):
            for g in range(HALF_GROUPS):
                r = part * HALF_STATE + g * SSM_STATE
                qm_s[h, r:r + SSM_STATE, :] = jnp.where(col_group == g, q, 0.0).astype(BF16)


def _ssm_kernel(u_ref, rp_ref, bbt_ref, cc_ref, dl_ref, pf_ref, ct_ref, a_ref, pw_ref, y_ref,
                z_s, v_s, l_s, s_s, wt_s, pb_s, qm_s):
    n_chunks = z_s.shape[0]
    seg = n_chunks // N_SEG
    half_tiles = HALF_STATE // LANES
    part_tiles = lambda part: [h * 2 * half_tiles + part * half_tiles + c for h in range(2) for c in range(half_tiles)]

    @pl.when(pl.program_id(1) == 0)
    def _():
        _ssm_spread(rp_ref, bbt_ref, cc_ref, dl_ref, pf_ref, ct_ref, wt_s, pb_s, qm_s)

    low = lax.broadcasted_iota(jnp.int32, (n_chunks, LANES), 1) < HALF_LANES

    def regroup(a, b):
        return jnp.where(low, a, pltpu.roll(b, HALF_LANES, 1)), jnp.where(low, pltpu.roll(a, HALF_LANES, 1), b)

    ut = pltpu.einshape("ktc->tkc", u_ref[...])
    for m in range(CHUNK // 2):
        lo, hi = regroup(ut[2 * m], ut[2 * m + 1])
        z_s[:, m * LANES:(m + 1) * LANES] = lo.astype(BF16)
        z_s[:, HALF_W + m * LANES:HALF_W + (m + 1) * LANES] = hi.astype(BF16)
    for h in range(2):
        v = _dot(z_s[:, h * HALF_W:(h + 1) * HALF_W], pb_s[h])
        for c in range(2 * half_tiles):
            v_s[h * 2 * half_tiles + c] = v[:, c * LANES:(c + 1) * LANES]

    def load(ref, rows, part):
        return jnp.concatenate([ref[c, rows, :] for c in part_tiles(part)], axis=1)

    def store(ref, rows, part, val):
        for k, c in enumerate(part_tiles(part)):
            ref[c, rows, :] = val[:, k * LANES:(k + 1) * LANES]

    are, aim = a_ref[0, 0], a_ref[0, 1]

    def local_scan(i, carry):
        sre, sim = carry
        rows = pl.ds(i, N_SEG, stride=seg)
        store(l_s, rows, 0, sre)
        store(l_s, rows, 1, sim)
        return (are * sre - aim * sim + load(v_s, rows, 0), are * sim + aim * sre + load(v_s, rows, 1))

    zero = jnp.zeros((N_SEG, TILE_STATE), F32)
    ere, eim = lax.fori_loop(0, seg, local_scan, (zero, zero), unroll=5)

    bre, bim = a_ref[0, 2][:1], a_ref[0, 3][:1]
    tre, tim = [zero[:1]], [zero[:1]]
    for m in range(1, N_SEG):
        pre, pim = tre[-1], tim[-1]
        tre.append(bre * pre - bim * pim + ere[m - 1:m])
        tim.append(bre * pim + bim * pre + eim[m - 1:m])
    tre, tim = jnp.concatenate(tre, axis=0), jnp.concatenate(tim, axis=0)

    def add_carry(i, _):
        rows = pl.ds(i, N_SEG, stride=seg)
        pre, pim = pw_ref[0, 0, pl.ds(i, 1), :], pw_ref[0, 1, pl.ds(i, 1), :]
        store(l_s, rows, 0, load(l_s, rows, 0) + (pre * tre - pim * tim))
        store(l_s, rows, 1, load(l_s, rows, 1) + (pre * tim + pim * tre))
        return 0

    lax.fori_loop(0, seg, add_carry, 0, unroll=5)

    for c in range(4 * half_tiles):
        s_s[:, c * LANES:(c + 1) * LANES] = l_s[c].astype(BF16)
    slabs = []
    for tp in range(SUB_BLOCKS):
        acc = []
        for h in range(2):
            y = _dot(s_s[:, h * 2 * HALF_STATE:(h + 1) * 2 * HALF_STATE], qm_s[h, :, tp * MXU:(tp + 1) * MXU])
            for tq in range(tp + 1):
                y = y + _dot(z_s[:, h * HALF_W + tq * MXU:h * HALF_W + (tq + 1) * MXU], wt_s[h, tp - tq])
            acc.append(y)
        for kk in range(SUB_BLOCKS // 2):
            slabs += regroup(acc[0][:, kk * LANES:(kk + 1) * LANES], acc[1][:, kk * LANES:(kk + 1) * LANES])
        if len(slabs) == SUBLANES:
            t0 = SUB_BLOCKS * (tp + 1) - SUBLANES
            y_ref[:, t0:t0 + SUBLANES, :] = pltpu.einshape("tkc->ktc", jnp.stack(slabs))
            slabs = []


def _ssm(u, prep, layer, bsz, seq_pad):
    n_chunks = seq_pad // CHUNK
    tile = lambda shape: pl.BlockSpec((None, 1) + shape, lambda j, b: (layer, j) + (0,) * len(shape))
    io = pl.BlockSpec((n_chunks, CHUNK, LANES), lambda j, b: (b, 0, j))
    by_chunk = (bsz * n_chunks, CHUNK, D_SSM)
    return pl.pallas_call(
        _ssm_kernel,
        grid=(N_SSM_TILES, bsz),
        in_specs=[io] + [tile(p.shape[2:]) for p in prep],
        out_specs=io,
        out_shape=jax.ShapeDtypeStruct(by_chunk, F32),
        scratch_shapes=[pltpu.VMEM((n_chunks, CHUNK_W), BF16),
                        pltpu.VMEM((2 * TILE_STATE // LANES, n_chunks, LANES), F32),
                        pltpu.VMEM((2 * TILE_STATE // LANES, n_chunks, LANES), F32),
                        pltpu.VMEM((n_chunks, 2 * TILE_STATE), BF16),
                        pltpu.VMEM((2, SUB_BLOCKS, MXU, MXU), BF16),
                        pltpu.VMEM((2, HALF_W, 2 * HALF_STATE), BF16),
                        pltpu.VMEM((2, 2 * HALF_STATE, HALF_W), BF16)],
        compiler_params=_params(("arbitrary", "arbitrary")),
        name="ssm",
    )(u.reshape(by_chunk), *prep).reshape(u.shape)


def _ssm_prep(a_re, a_im, log_dt, b_re, b_im, c_re, c_im, d_skip, seg):
    N, C, T, J, E = SSM_STATE, SSM_GROUP, CHUNK, N_SSM_TILES, TILE_GROUPS
    depth = a_re.shape[0]
    tiles = lambda m: m.reshape((depth, J, E) + m.shape[2:])
    a_re, a_im, dt = tiles(a_re), tiles(a_im), tiles(jnp.exp(log_dt))[..., None]
    lam_re, lam_im = a_re * dt, a_im * dt

    def powers(p, lam_re, lam_im):
        p = jnp.asarray(p, F32).reshape((-1,) + (1,) * (lam_re.ndim - 2))
        mag = jnp.exp(lam_re[:, :, None] * p)
        return mag * jnp.cos(lam_im[:, :, None] * p), mag * jnp.sin(lam_im[:, :, None] * p)

    pw_re, pw_im = powers(np.arange(T + 1), lam_re, lam_im)
    den = a_re * a_re + a_im * a_im
    nre, nim = pw_re[:, :, 1] - 1.0, pw_im[:, :, 1]
    coef_re = ((nre * a_re + nim * a_im) / den)[..., None, :]
    coef_im = ((nim * a_re - nre * a_im) / den)[..., None, :]
    bt_re, bt_im = tiles(b_re).swapaxes(-1, -2), tiles(b_im).swapaxes(-1, -2)
    halves = lambda m, axis: m.reshape(m.shape[:axis] + (2, HALF_GROUPS) + m.shape[axis + 1:])
    slab = lambda m: m.reshape(m.shape[:3] + (-1, m.shape[-1]))
    cat = lambda *ms: jnp.concatenate(ms, axis=-1)
    bb_re = slab(halves(coef_re * bt_re - coef_im * bt_im, 2))
    bb_im = slab(halves(coef_re * bt_im + coef_im * bt_re, 2))
    bbt = cat(bb_re, bb_re, -bb_im, bb_im)

    rev = lambda m: slab(halves(m[:, :, T - 1::-1], 3).swapaxes(2, 3))
    r_re, r_im = rev(pw_re), rev(pw_im)
    rp = cat(r_re, r_im, r_im, r_re)

    cr, ci = halves(tiles(c_re), 2), halves(tiles(c_im), 2)
    by_state = lambda m: m.reshape(depth, J, 2, N, HALF_LANES)
    crt, cit = by_state(cr.transpose(0, 1, 2, 5, 3, 4)), by_state(ci.transpose(0, 1, 2, 5, 3, 4))
    cc = jnp.concatenate([cat(crt, crt), cat(-cit, -cit)], axis=-2)
    skip = halves(tiles(d_skip), 2).reshape(depth, J, 2, 1, HALF_LANES)
    dl = jnp.broadcast_to(cat(skip, skip), (depth, J, 2, SUBLANES, LANES))

    fwd = lambda m: by_state(halves(m[:, :, 1:], 3).transpose(0, 1, 3, 5, 2, 4))
    pf = cat(fwd(pw_re), fwd(pw_im))
    ct = jnp.stack([cat(crt, crt), cat(cit, cit)], axis=3)

    flat = lambda m: m.reshape(depth, J, TILE_STATE)
    t_re, t_im = powers(T * np.array([1, seg]), flat(lam_re), flat(lam_im))
    a = jnp.stack([t_re[:, :, 0], t_im[:, :, 0], t_re[:, :, 1], t_im[:, :, 1]], axis=2)[:, :, :, None]
    a = jnp.broadcast_to(a, (depth, J, 4, SUBLANES, TILE_STATE))
    pw = jnp.stack(powers(T * np.arange(seg), flat(lam_re), flat(lam_im)), axis=2)
    return rp, bbt, cc, dl, pf, ct, a, pw


def _attn_head_order():
    per_kv = N_Q_HEADS // N_KV_HEADS
    return [(2 * j + e) * per_kv + g for j in range(N_KV_HEADS // 2) for g in range(per_kv) for e in range(2)]


def _attn_kernel(sink_ref, bias_ref, q_ref, k_ref, vt_ref, o_ref):
    n_keys = N_META + 2 * BLOCK
    n_local = q_ref.shape[0] // BLOCK
    first_k = lax.broadcasted_iota(jnp.int32, (n_keys, LANES), 1) < HEAD_DIM
    first_v = lax.broadcasted_iota(jnp.int32, (LANES, 3 * BLOCK), 0) < HEAD_DIM
    first_o = lax.broadcasted_iota(jnp.int32, (LANES, 2 * BLOCK), 0) < HEAD_DIM
    zero_k = jnp.zeros((n_keys, LANES), BF16)
    zero_v = jnp.zeros((LANES, 3 * BLOCK), BF16)
    pad_p = jnp.zeros((PAD, 2 * BLOCK), BF16)
    shape = (2 * SUBLANES, 6 * BLOCK)
    key_rows = jnp.where(lax.broadcasted_iota(jnp.int32, shape, 0) == lax.broadcasted_iota(jnp.int32, shape, 1) // (3 * BLOCK),
                         1.0, 0.0).astype(BF16)

    def block_of(i):
        n = pl.program_id(1) * n_local + i
        return n, jnp.maximum(n - 1, 0)

    def scores(i, tp):
        n, prev = block_of(i)
        cols = slice(tp // 2 * LANES, (tp // 2 + 1) * LANES)
        k2 = jnp.concatenate([k_ref[0, PAD:, cols], k_ref[prev, :, cols], k_ref[n, :, cols]], axis=0)
        kab = jnp.concatenate([jnp.where(first_k, k2, zero_k), jnp.where(first_k, zero_k, k2)], axis=0)
        q4 = jnp.concatenate([q_ref[i * BLOCK:(i + 1) * BLOCK, (2 * tp + p) * LANES:(2 * tp + p + 1) * LANES]
                              for p in range(2)], axis=0)
        return lax.dot_general(kab, q4, (((1,), (1,)), ((), ())), preferred_element_type=F32)

    heads = _attn_head_order()
    chains = [(i, tp) for i in range(n_local) for tp in range(N_Q_HEADS // 4)]
    s_next = scores(*chains[0])
    for c, (i, tp) in enumerate(chains):
        s = s_next
        if c + 1 < len(chains):
            s_next = scores(*chains[c + 1])
        n, prev = block_of(i)
        bias = bias_ref[jnp.minimum(n, 2)]
        pt, tail = [], []
        for e in range(2):
            blocks, tl = [], []
            for p in range(2):
                se = s[e * n_keys:(e + 1) * n_keys, p * LANES:(p + 1) * LANES] + bias
                sink = sink_ref[heads[2 * (2 * tp + p) + e]] * LOG2_E
                m = jnp.maximum(jnp.max(se, axis=0, keepdims=True), sink)
                blocks.append(jnp.exp2(se - m).astype(BF16))
                tl.append(jnp.exp2(sink - m))
            pt += [pad_p, jnp.concatenate(blocks, axis=1)]
            tail.append(jnp.concatenate(tl, axis=1))
        rows = slice(tp // 2 * LANES, (tp // 2 + 1) * LANES)
        v2 = jnp.concatenate([vt_ref[0, rows, :], vt_ref[prev, rows, :], vt_ref[n, rows, :]], axis=1)
        vab = jnp.concatenate([jnp.where(first_v, v2, zero_v), jnp.where(first_v, zero_v, v2)], axis=1)
        vab = jnp.concatenate([vab, key_rows], axis=0)
        ot = _dot(vab, jnp.concatenate(pt, axis=0))
        rden = [1.0 / (ot[LANES + e:LANES + e + 1] + tail[e]) for e in range(2)]
        ot = ot[:LANES] * jnp.where(first_o, rden[0], rden[1])
        for e in range(2):
            rows = slice(e * HEAD_DIM, (e + 1) * HEAD_DIM)
            o = jnp.concatenate([ot[rows, :BLOCK], ot[rows, BLOCK:]], axis=0).T
            t = 4 * (tp // 2) + 2 * e + tp % 2
            o_ref[i * BLOCK:(i + 1) * BLOCK, t * LANES:(t + 1) * LANES] = o.astype(BF16)


def _attention(q, k, vt, sinks, bias, bsz, n_blk, tq):
    steps = n_blk * BLOCK // tq
    rows = pl.BlockSpec((tq, D_ATTN), lambda b, s: (b * steps + s, 0))
    return pl.pallas_call(
        _attn_kernel,
        grid=(bsz, steps),
        in_specs=[pl.BlockSpec(memory_space=pltpu.SMEM), _const_spec(bias.shape), rows,
                  pl.BlockSpec((n_blk, BLOCK, D_KV), lambda b, s: (b, 0, 0)),
                  pl.BlockSpec((n_blk, D_KV, BLOCK), lambda b, s: (b, 0, 0))],
        out_specs=rows,
        out_shape=jax.ShapeDtypeStruct(q.shape, BF16),
        compiler_params=_params(("parallel", "parallel")),
        name="attention",
    )(sinks, bias, q, k, vt)


def _attn_bias():
    i = np.arange(BLOCK)[None, :]
    j = np.arange(BLOCK)[:, None]
    none = np.zeros((BLOCK, BLOCK), bool)
    causal = j <= i
    in_meta = np.broadcast_to(j >= PAD, (BLOCK, BLOCK))
    blk0 = [none[:N_META], none, causal & in_meta]
    blk1 = [none[:N_META], in_meta, causal]
    blk2 = [~none[:N_META], j > i, causal]
    ok = np.stack([np.concatenate(b, axis=0) for b in (blk0, blk1, blk2)])
    return jnp.asarray(np.where(ok, 0.0, NEG_INF), F32)


def _merge_kernel(rows, *refs):
    n = len(rows.specs)
    y_ref, a_ref, gs_ref, ga_ref, wglu_ref, bglu_ref, wos_ref, woa_ref, wout_ref, gain_ref, o_ref = refs[n:]
    z = jax.nn.gelu(y_ref[...])
    z = z * jax.nn.sigmoid(_dot(z.astype(BF16), wglu_ref[...].astype(BF16)) + bglu_ref[...])
    merged = (jax.nn.sigmoid(gs_ref[...].astype(F32)) * _dot(z.astype(BF16), wos_ref[...].astype(BF16))
              + jax.nn.sigmoid(ga_ref[...].astype(F32)) * _dot(a_ref[...], woa_ref[...].astype(BF16)))
    mix = _dot(merged.astype(BF16), wout_ref[...].astype(BF16))
    o_ref[...] = rows.load(refs[:n]) + _rms(mix, gain_ref[...])


def _merge(rows, layer, y, a, gs, ga, wglu, bglu, wos, woa, wout, gain, tm):
    n_rows = y.shape[0]
    row = lambda width: pl.BlockSpec((tm, width), lambda i: (i, 0))
    weights = (wglu, bglu, wos, woa, wout, gain)
    return pl.pallas_call(
        functools.partial(_merge_kernel, rows),
        grid=(n_rows // tm,),
        in_specs=rows.specs + [row(D_SSM), row(D_ATTN), row(D_MODEL), row(D_MODEL)]
                 + [_layer_spec(p, layer) for p in weights],
        out_specs=row(D_MODEL),
        out_shape=jax.ShapeDtypeStruct((n_rows, D_MODEL), F32),
        compiler_params=_params(("parallel",)),
        name="merge",
    )(*rows.operands, y, a, gs, ga, *weights)


def _ffn_kernel(rows, *refs):
    n = len(rows.specs)
    gpre_ref, wup_ref, wdown_ref, gpost_ref, o_ref = refs[n:]
    x = rows.load(refs[:n])
    h = _rms(x, gpre_ref[...]).astype(BF16)
    step = COL_STEP
    acc = jnp.zeros(x.shape, F32)
    for c in range(0, D_FF, step):
        a = jnp.maximum(_dot(h, wup_ref[:, c:c + step].astype(BF16)), 0.0)
        acc = acc + _dot((a * a).astype(BF16), wdown_ref[c:c + step, :].astype(BF16))
    o_ref[...] = x + _rms(acc, gpost_ref[...])


def _ffn(rows, n_rows, layer, gpre, wup, wdown, gpost, tm):
    weights = (gpre, wup, wdown, gpost)
    return pl.pallas_call(
        functools.partial(_ffn_kernel, rows),
        grid=(n_rows // tm,),
        in_specs=rows.specs + [_layer_spec(p, layer) for p in weights],
        out_specs=pl.BlockSpec((tm, D_MODEL), lambda i: (i, 0)),
        out_shape=jax.ShapeDtypeStruct((n_rows, D_MODEL), F32),
        compiler_params=_params(("parallel",)),
        name="ffn",
    )(*rows.operands, *weights)


def _rope_tables(n_blk):
    inv_freq = 1.0 / (ROPE_THETA ** (jnp.arange(0, HEAD_DIM, 2, dtype=F32) / HEAD_DIM))
    inv_freq = jnp.tile(inv_freq, LANES // (HEAD_DIM // 2))
    blk = (BLOCK * jnp.arange(n_blk, dtype=jnp.int32)).astype(F32)[:, None] * inv_freq
    off = (jnp.arange(BLOCK, dtype=jnp.int32) - PAD).astype(F32)[:, None] * inv_freq
    blk = jnp.stack([jnp.cos(blk), jnp.sin(blk)], axis=1)[:, :, None]
    return jnp.broadcast_to(blk, (n_blk, 2, SUBLANES, LANES)), jnp.stack([jnp.cos(off), jnp.sin(off)])


def _v_weights_t(w_in):
    return lax.optimization_barrier(w_in[:, :, C_V:C_V + D_KV].astype(BF16)).swapaxes(1, 2)


def kernel(x, meta_tokens, norm_mix_pre, norm_mix_post, norm_mlp_pre, norm_mlp_post, w_in, ssm_a_re, ssm_a_im, ssm_log_dt, ssm_b_re, ssm_b_im, ssm_c_re, ssm_c_im, ssm_d, w_glu, b_glu, attn_sinks, w_o_ssm, w_o_attn, w_out, w_up, w_down):
    bsz, seq, _ = x.shape
    depth = w_in.shape[0]
    seq_pad = PAD + N_META + seq
    assert seq_pad % BLOCK == 0 and (seq_pad // CHUNK) % N_SEG == 0
    n_blk, n_chunks = seq_pad // BLOCK, seq_pad // CHUNK
    tm, tm_out = 5 * BLOCK, 4 * BLOCK
    tq = 13 * BLOCK if n_blk % 13 == 0 else tm
    assert seq_pad % tm == 0 and seq % tm_out == 0 and seq_pad % tq == 0
    n_rows = bsz * seq_pad

    meta_block = jnp.concatenate([jnp.zeros((PAD, D_MODEL), x.dtype), meta_tokens.astype(x.dtype)], axis=0)
    stream = _Rows("input", x.reshape(bsz * seq, D_MODEL), tm, seq_pad // tm, seq // BLOCK, meta_block)
    rope_blk, rope_off = _rope_tables(n_blk)
    bias = _attn_bias()

    row = lambda m: m[:, None, :]
    wvt = _v_weights_t(w_in)
    prep = _ssm_prep(ssm_a_re, ssm_a_im, ssm_log_dt, ssm_b_re, ssm_b_im, ssm_c_re, ssm_c_im, ssm_d, n_chunks // N_SEG)

    for l in range(depth):
        u, q, k, vt, gs, ga = _inproj(stream, n_rows, l, row(norm_mix_pre), w_in, wvt, rope_blk, rope_off, tm)
        y = _ssm(u, prep, l, bsz, seq_pad)
        att = _attention(q, k, vt, attn_sinks[l], bias, bsz, n_blk, tq)
        hres = _merge(stream, l, y, att, gs, ga, w_glu, row(b_glu), w_o_ssm, w_o_attn, w_out, row(norm_mix_post), tm)
        ffn_w = (l, row(norm_mlp_pre), w_up, w_down, row(norm_mlp_post))
        if l + 1 < depth:
            stream = _Rows("padded", _ffn(_Rows("padded", hres, tm, seq_pad // tm, n_blk), n_rows, *ffn_w, tm),
                           tm, seq_pad // tm, n_blk)
        else:
            out = _ffn(_Rows("frames", hres, tm_out, seq // tm_out, n_blk), bsz * seq, *ffn_w, tm_out)
    return out.reshape(bsz, seq, D_MODEL)
```

```python
import functools
import math

import jax
import jax.numpy as jnp
import numpy as np
from jax import lax
from jax.experimental import pallas as pl
from jax.experimental.pallas import tpu as pltpu

D_MODEL = 1024
N_META = 16
HEAD_DIM = 64
N_Q_HEADS = 16
N_KV_HEADS = 4
D_ATTN = N_Q_HEADS * HEAD_DIM
D_KV = N_KV_HEADS * HEAD_DIM
BLOCK = 128
ROPE_THETA = 10000.0
ATTN_SCALE = HEAD_DIM ** -0.5
LOG2_E = math.log2(math.e)
NEG_INF = -1e30
D_SSM = D_MODEL // 2
SSM_GROUP = 16
SSM_STATE = 64
D_FF = 4 * D_MODEL
RMS_EPS = 1e-6

LANES = 128
SUBLANES = 8
MXU = 256
PAD = BLOCK - N_META
CHUNK = 16
TILE_GROUPS = LANES // SSM_GROUP
N_SSM_TILES = D_SSM // LANES
TILE_STATE = TILE_GROUPS * SSM_STATE
CHUNK_W = CHUNK * LANES
HALF_GROUPS = TILE_GROUPS // 2
HALF_LANES = LANES // 2
HALF_STATE = HALF_GROUPS * SSM_STATE
HALF_W = CHUNK * HALF_LANES
SUB_BLOCKS = MXU // HALF_LANES
N_SEG = SUBLANES
COL_STEP = 2 * MXU
VMEM_LIMIT = 56 * 1024 * 1024

C_U = 0
C_Q = C_U + D_SSM
C_K = C_Q + D_ATTN
C_V = C_K + D_KV
C_GS = C_V + D_KV
C_GA = C_GS + D_MODEL

BF16 = jnp.bfloat16
F32 = jnp.float32


def _dot(a, b):
    return jnp.dot(a, b, preferred_element_type=F32)


def _rms(x, gain):
    return x * lax.rsqrt(jnp.mean(x * x, axis=-1, keepdims=True) + RMS_EPS) * gain


def _const_spec(shape):
    return pl.BlockSpec(shape, lambda *_: (0,) * len(shape), pipeline_mode=pl.Buffered(1))


def _layer_spec(stacked, layer):
    shape = stacked.shape[1:]
    return pl.BlockSpec((None,) + shape, lambda *_: (layer,) + (0,) * len(shape), pipeline_mode=pl.Buffered(1))


def _params(sem):
    return pltpu.CompilerParams(dimension_semantics=sem, vmem_limit_bytes=VMEM_LIMIT)


class _Rows:
    def __init__(self, kind, array, tm, tiles_per_batch, blocks_per_batch, meta_block=None):
        self.has_meta = kind == "input"
        self.tiles_per_batch = tiles_per_batch
        nb = tm // BLOCK
        if kind == "padded":
            self.specs = [pl.BlockSpec((tm, D_MODEL), lambda i: (i, 0))]
            self.operands = [array]
            return
        shift = -1 if kind == "input" else 1

        def imap(j):
            return lambda i: ((i // tiles_per_batch) * blocks_per_batch
                              + jnp.maximum((i % tiles_per_batch) * nb + j + shift, 0), 0)

        self.specs = [pl.BlockSpec((BLOCK, D_MODEL), imap(j)) for j in range(nb)]
        self.operands = [array] * nb
        if self.has_meta:
            self.specs.append(_const_spec((BLOCK, D_MODEL)))
            self.operands.append(meta_block)

    def load(self, refs):
        n = len(self.specs)
        if n == 1:
            return refs[0][...]
        blocks = [r[...] for r in refs[:n - self.has_meta]]
        if self.has_meta:
            head = pl.program_id(0) % self.tiles_per_batch == 0
            blocks[0] = jnp.where(head, refs[n - 1][...], blocks[0])
        return jnp.concatenate(blocks, axis=0)


def _inproj_kernel(rows, *refs):
    n = len(rows.specs)
    gain_ref, w_ref, wvt_ref, blk_ref, off_ref, u_ref, q_ref, k_ref, vt_ref, gs_ref, ga_ref = refs[n:]
    h = _rms(rows.load(refs[:n]), gain_ref[...]).astype(BF16)

    cos_o, sin_o = off_ref[0], off_ref[1]
    cos, sin = [], []
    for b in range(blk_ref.shape[0]):
        cos_b, sin_b = (jnp.concatenate([blk_ref[b, p]] * (BLOCK // SUBLANES), axis=0) for p in range(2))
        cos.append(cos_b * cos_o - sin_b * sin_o)
        sin.append(sin_b * cos_o + cos_b * sin_o)
    cos, sin = jnp.concatenate(cos, axis=0), jnp.concatenate(sin, axis=0)
    first_half = lax.broadcasted_iota(jnp.int32, sin.shape, 1) % HEAD_DIM < HEAD_DIM // 2
    sa, sb = jnp.where(first_half, -sin, 0.0), jnp.where(first_half, 0.0, sin)

    def rope(t):
        return t * cos + pltpu.roll(t, LANES - HEAD_DIM // 2, 1) * sa + pltpu.roll(t, HEAD_DIM // 2, 1) * sb

    step = COL_STEP
    u_ref[...] = _dot(h, w_ref[:, C_U:C_U + D_SSM].astype(BF16))
    low = lax.broadcasted_iota(jnp.int32, sin.shape, 1) < HEAD_DIM
    for c in range(0, D_ATTN, step):
        t = _dot(h, w_ref[:, C_Q + c:C_Q + c + step].astype(BF16))
        tiles = [rope(t[:, j:j + LANES]) * (ATTN_SCALE * LOG2_E) for j in range(0, step, LANES)]
        for m in range(2):
            a, b = tiles[m], tiles[m + 2]
            pair = (jnp.where(low, a, pltpu.roll(b, HEAD_DIM, 1)), jnp.where(low, pltpu.roll(a, HEAD_DIM, 1), b))
            for e in range(2):
                j = c + (2 * m + e) * LANES
                q_ref[:, j:j + LANES] = pair[e].astype(BF16)
    t = _dot(h, w_ref[:, C_K:C_K + D_KV].astype(BF16))
    for j in range(0, D_KV, LANES):
        kj = rope(t[:, j:j + LANES]).astype(BF16)
        for b in range(k_ref.shape[0]):
            k_ref[b, :, j:j + LANES] = kj[b * BLOCK:(b + 1) * BLOCK]
    vt = lax.dot_general(wvt_ref[...], h, (((1,), (1,)), ((), ())), preferred_element_type=F32).astype(BF16)
    for b in range(vt_ref.shape[0]):
        vt_ref[b] = vt[:, b * BLOCK:(b + 1) * BLOCK]
    for c in range(0, D_MODEL, step):
        gs_ref[:, c:c + step] = _dot(h, w_ref[:, C_GS + c:C_GS + c + step].astype(BF16)).astype(BF16)
        ga_ref[:, c:c + step] = _dot(h, w_ref[:, C_GA + c:C_GA + c + step].astype(BF16)).astype(BF16)


def _inproj(rows, n_rows, layer, gain, w, wvt, rope_blk, rope_off, tm):
    row = lambda width: pl.BlockSpec((tm, width), lambda i: (i, 0))
    blk = pl.BlockSpec((tm // BLOCK,) + rope_blk.shape[1:], lambda i: (i % rows.tiles_per_batch, 0, 0, 0))
    out = lambda width, dtype: (row(width), jax.ShapeDtypeStruct((n_rows, width), dtype))
    per_block = lambda shape: (pl.BlockSpec((tm // BLOCK,) + shape, lambda i: (i, 0, 0)),
                               jax.ShapeDtypeStruct((n_rows // BLOCK,) + shape, BF16))
    outs = [out(D_SSM, F32),
            out(D_ATTN, BF16), per_block((BLOCK, D_KV)), per_block((D_KV, BLOCK)),
            out(D_MODEL, BF16), out(D_MODEL, BF16)]
    return pl.pallas_call(
        functools.partial(_inproj_kernel, rows),
        grid=(n_rows // tm,),
        in_specs=rows.specs + [_layer_spec(p, layer) for p in (gain, w, wvt)] + [blk, _const_spec(rope_off.shape)],
        out_specs=[o[0] for o in outs],
        out_shape=[o[1] for o in outs],
        compiler_params=_params(("parallel",)),
        name="inproj",
    )(*rows.operands, gain, w, wvt, rope_blk, rope_off)


def _ssm_spread(rp_ref, bbt_ref, cc_ref, dl_ref, pf_ref, ct_ref, wt_s, pb_s, qm_s):
    T = CHUNK
    iota = lambda shape, axis: lax.broadcasted_iota(jnp.int32, shape, axis)
    group_of = lambda shape, axis, width: (iota(shape, axis) // width) % HALF_GROUPS
    one_hot = lambda hit: jnp.where(hit, 1.0, 0.0).astype(BF16)

    even = iota((HALF_LANES, LANES), 1) < HALF_LANES
    none = jnp.zeros((HALF_LANES, LANES), BF16)
    shape = (HALF_W, HALF_LANES)
    row_copy = one_hot(iota(shape, 0) // SSM_GROUP == iota(shape, 1))
    shape = (LANES, 2 * HALF_STATE)
    state_copy = one_hot(iota(shape, 1) % SSM_STATE + iota(shape, 1) // HALF_STATE * SSM_STATE == iota(shape, 0))
    shape = (LANES, 2 * HALF_W)
    lane_copy = one_hot(iota(shape, 1) // SSM_GROUP == iota(shape, 0))
    shape = (HALF_W, LANES)
    tap_diag = group_of(shape, 0, SSM_GROUP) == group_of(shape, 1, SSM_GROUP)
    tap_skip = tap_diag & (iota(shape, 0) // HALF_LANES == T - 1) & (iota(shape, 0) % SSM_GROUP == iota(shape, 1) % SSM_GROUP)
    shape = (HALF_W, 2 * HALF_STATE)
    pb_diag = group_of(shape, 0, SSM_GROUP) == group_of(shape, 1, SSM_STATE)
    col_group = group_of((SSM_STATE, HALF_W), 1, SSM_GROUP)

    for h in range(2):
        x = _dot(row_copy, rp_ref[0, h].astype(BF16))
        y = jnp.concatenate([bbt_ref[0, h]] * T, axis=0)
        w = (x[:, :LANES] * y[:, :LANES] + x[:, LANES:] * y[:, LANES:]).astype(BF16)
        pb_s[h] = jnp.where(pb_diag, _dot(w, state_copy), 0.0).astype(BF16)

        taps = _dot(w, cc_ref[0, h].astype(BF16)) + jnp.where(tap_skip, jnp.concatenate([dl_ref[0, h]] * (HALF_W // SUBLANES), axis=0), 0.0)
        taps = jnp.where(tap_diag, taps, 0.0).astype(BF16)
        block = lambda lag: taps[(T - 1 - lag) * HALF_LANES:(T - lag) * HALF_LANES] if lag >= 0 else none
        for a in range(SUB_BLOCKS):
            for i in range(SUB_BLOCKS):
                for kk in range(SUB_BLOCKS // 2):
                    lag = SUB_BLOCKS * a + 2 * kk - i
                    wt_s[h, a, i * HALF_LANES:(i + 1) * HALF_LANES, kk * LANES:(kk + 1) * LANES] = (
                        jnp.where(even, block(lag), block(lag + 1)))

        pw = _dot(pf_ref[0, h].astype(BF16), lane_copy)
        p_re, p_im = pw[:, :HALF_W], pw[:, HALF_W:]
        c_re = jnp.concatenate([ct_ref[0, h, 0]] * (HALF_W // LANES), axis=1)
        c_im = jnp.concatenate([ct_ref[0, h, 1]] * (HALF_W // LANES), axis=1)
        for part, q in enumerate((c_re * p_re - c_im * p_im, -(c_re * p_im + c_im * p_re))):
            for g in range(HALF_GROUPS):
                r = part * HALF_STATE + g * SSM_STATE
                qm_s[h, r:r + SSM_STATE, :] = jnp.where(col_group == g, q, 0.0).astype(BF16)


def _ssm_kernel(u_ref, rp_ref, bbt_ref, cc_ref, dl_ref, pf_ref, ct_ref, a_ref, pw_ref, y_ref,
                z_s, v_s, l_s, s_s, wt_s, pb_s, qm_s):
    n_chunks = z_s.shape[0]
    seg = n_chunks // N_SEG
    half_tiles = HALF_STATE // LANES
    part_tiles = lambda part: [h * 2 * half_tiles + part * half_tiles + c for h in range(2) for c in range(half_tiles)]

    @pl.when(pl.program_id(1) == 0)
    def _():
        _ssm_spread(rp_ref, bbt_ref, cc_ref, dl_ref, pf_ref, ct_ref, wt_s, pb_s, qm_s)

    low = lax.broadcasted_iota(jnp.int32, (n_chunks, LANES), 1) < HALF_LANES

    def regroup(a, b):
        return jnp.where(low, a, pltpu.roll(b, HALF_LANES, 1)), jnp.where(low, pltpu.roll(a, HALF_LANES, 1), b)

    ut = jnp.swapaxes(u_ref[...], 0, 1)
    for m in range(CHUNK // 2):
        lo, hi = regroup(ut[2 * m], ut[2 * m + 1])
        z_s[:, m * LANES:(m + 1) * LANES] = lo.astype(BF16)
        z_s[:, HALF_W + m * LANES:HALF_W + (m + 1) * LANES] = hi.astype(BF16)
    for h in range(2):
        v = _dot(z_s[:, h * HALF_W:(h + 1) * HALF_W], pb_s[h])
        for c in range(2 * half_tiles):
            v_s[h * 2 * half_tiles + c] = v[:, c * LANES:(c + 1) * LANES]

    def load(ref, rows, part):
        return jnp.concatenate([ref[c, rows, :] for c in part_tiles(part)], axis=1)

    def store(ref, rows, part, val):
        for k, c in enumerate(part_tiles(part)):
            ref[c, rows, :] = val[:, k * LANES:(k + 1) * LANES]

    are, aim = a_ref[0, 0], a_ref[0, 1]

    def local_scan(i, carry):
        sre, sim = carry
        rows = pl.ds(i, N_SEG, stride=seg)
        store(l_s, rows, 0, sre)
        store(l_s, rows, 1, sim)
        return (are * sre - aim * sim + load(v_s, rows, 0), are * sim + aim * sre + load(v_s, rows, 1))

    zero = jnp.zeros((N_SEG, TILE_STATE), F32)
    ere, eim = lax.fori_loop(0, seg, local_scan, (zero, zero), unroll=5)

    bre, bim = a_ref[0, 2][:1], a_ref[0, 3][:1]
    tre, tim = [zero[:1]], [zero[:1]]
    for m in range(1, N_SEG):
        pre, pim = tre[-1], tim[-1]
        tre.append(bre * pre - bim * pim + ere[m - 1:m])
        tim.append(bre * pim + bim * pre + eim[m - 1:m])
    tre, tim = jnp.concatenate(tre, axis=0), jnp.concatenate(tim, axis=0)

    def add_carry(i, _):
        rows = pl.ds(i, N_SEG, stride=seg)
        pre, pim = pw_ref[0, 0, pl.ds(i, 1), :], pw_ref[0, 1, pl.ds(i, 1), :]
        store(l_s, rows, 0, load(l_s, rows, 0) + (pre * tre - pim * tim))
        store(l_s, rows, 1, load(l_s, rows, 1) + (pre * tim + pim * tre))
        return 0

    lax.fori_loop(0, seg, add_carry, 0, unroll=5)

    for c in range(4 * half_tiles):
        s_s[:, c * LANES:(c + 1) * LANES] = l_s[c].astype(BF16)
    slabs = []
    for tp in range(SUB_BLOCKS):
        acc = []
        for h in range(2):
            y = _dot(s_s[:, h * 2 * HALF_STATE:(h + 1) * 2 * HALF_STATE], qm_s[h, :, tp * MXU:(tp + 1) * MXU])
            for tq in range(tp + 1):
                y = y + _dot(z_s[:, h * HALF_W + tq * MXU:h * HALF_W + (tq + 1) * MXU], wt_s[h, tp - tq])
            acc.append(y)
        for kk in range(SUB_BLOCKS // 2):
            slabs += regroup(acc[0][:, kk * LANES:(kk + 1) * LANES], acc[1][:, kk * LANES:(kk + 1) * LANES])
        if len(slabs) == SUBLANES:
            t0 = SUB_BLOCKS * (tp + 1) - SUBLANES
            y_ref[:, t0:t0 + SUBLANES, :] = jnp.swapaxes(jnp.stack(slabs), 0, 1)
            slabs = []


def _ssm(u, prep, layer, bsz, seq_pad):
    n_chunks = seq_pad // CHUNK
    tile = lambda shape: pl.BlockSpec((None, 1) + shape, lambda j, b: (layer, j) + (0,) * len(shape))
    io = pl.BlockSpec((n_chunks, CHUNK, LANES), lambda j, b: (b, 0, j))
    by_chunk = (bsz * n_chunks, CHUNK, D_SSM)
    return pl.pallas_call(
        _ssm_kernel,
        grid=(N_SSM_TILES, bsz),
        in_specs=[io] + [tile(p.shape[2:]) for p in prep],
        out_specs=io,
        out_shape=jax.ShapeDtypeStruct(by_chunk, F32),
        scratch_shapes=[pltpu.VMEM((n_chunks, CHUNK_W), BF16),
                        pltpu.VMEM((2 * TILE_STATE // LANES, n_chunks, LANES), F32),
                        pltpu.VMEM((2 * TILE_STATE // LANES, n_chunks, LANES), F32),
                        pltpu.VMEM((n_chunks, 2 * TILE_STATE), BF16),
                        pltpu.VMEM((2, SUB_BLOCKS, MXU, MXU), BF16),
                        pltpu.VMEM((2, HALF_W, 2 * HALF_STATE), BF16),
                        pltpu.VMEM((2, 2 * HALF_STATE, HALF_W), BF16)],
        compiler_params=_params(("arbitrary", "arbitrary")),
        name="ssm",
    )(u.reshape(by_chunk), *prep).reshape(u.shape)


def _ssm_prep(a_re, a_im, log_dt, b_re, b_im, c_re, c_im, d_skip, seg):
    N, C, T, J, E = SSM_STATE, SSM_GROUP, CHUNK, N_SSM_TILES, TILE_GROUPS
    depth = a_re.shape[0]
    tiles = lambda m: m.reshape((depth, J, E) + m.shape[2:])
    a_re, a_im, dt = tiles(a_re), tiles(a_im), tiles(jnp.exp(log_dt))[..., None]
    lam_re, lam_im = a_re * dt, a_im * dt

    def powers(p, lam_re, lam_im):
        p = jnp.asarray(p, F32).reshape((-1,) + (1,) * (lam_re.ndim - 2))
        mag = jnp.exp(lam_re[:, :, None] * p)
        return mag * jnp.cos(lam_im[:, :, None] * p), mag * jnp.sin(lam_im[:, :, None] * p)

    pw_re, pw_im = powers(np.arange(T + 1), lam_re, lam_im)
    den = a_re * a_re + a_im * a_im
    nre, nim = pw_re[:, :, 1] - 1.0, pw_im[:, :, 1]
    coef_re = ((nre * a_re + nim * a_im) / den)[..., None, :]
    coef_im = ((nim * a_re - nre * a_im) / den)[..., None, :]
    bt_re, bt_im = tiles(b_re).swapaxes(-1, -2), tiles(b_im).swapaxes(-1, -2)
    halves = lambda m, axis: m.reshape(m.shape[:axis] + (2, HALF_GROUPS) + m.shape[axis + 1:])
    slab = lambda m: m.reshape(m.shape[:3] + (-1, m.shape[-1]))
    cat = lambda *ms: jnp.concatenate(ms, axis=-1)
    bb_re = slab(halves(coef_re * bt_re - coef_im * bt_im, 2))
    bb_im = slab(halves(coef_re * bt_im + coef_im * bt_re, 2))
    bbt = cat(bb_re, bb_re, -bb_im, bb_im)

    rev = lambda m: slab(halves(m[:, :, T - 1::-1], 3).swapaxes(2, 3))
    r_re, r_im = rev(pw_re), rev(pw_im)
    rp = cat(r_re, r_im, r_im, r_re)

    cr, ci = halves(tiles(c_re), 2), halves(tiles(c_im), 2)
    by_state = lambda m: m.reshape(depth, J, 2, N, HALF_LANES)
    crt, cit = by_state(cr.transpose(0, 1, 2, 5, 3, 4)), by_state(ci.transpose(0, 1, 2, 5, 3, 4))
    cc = jnp.concatenate([cat(crt, crt), cat(-cit, -cit)], axis=-2)
    skip = halves(tiles(d_skip), 2).reshape(depth, J, 2, 1, HALF_LANES)
    dl = jnp.broadcast_to(cat(skip, skip), (depth, J, 2, SUBLANES, LANES))

    fwd = lambda m: by_state(halves(m[:, :, 1:], 3).transpose(0, 1, 3, 5, 2, 4))
    pf = cat(fwd(pw_re), fwd(pw_im))
    ct = jnp.stack([cat(crt, crt), cat(cit, cit)], axis=3)

    flat = lambda m: m.reshape(depth, J, TILE_STATE)
    t_re, t_im = powers(T * np.array([1, seg]), flat(lam_re), flat(lam_im))
    a = jnp.stack([t_re[:, :, 0], t_im[:, :, 0], t_re[:, :, 1], t_im[:, :, 1]], axis=2)[:, :, :, None]
    a = jnp.broadcast_to(a, (depth, J, 4, SUBLANES, TILE_STATE))
    pw = jnp.stack(powers(T * np.arange(seg), flat(lam_re), flat(lam_im)), axis=2)
    return rp, bbt, cc, dl, pf, ct, a, pw


def _attn_head_order():
    per_kv = N_Q_HEADS // N_KV_HEADS
    return [(2 * j + e) * per_kv + g for j in range(N_KV_HEADS // 2) for g in range(per_kv) for e in range(2)]


def _attn_kernel(sink_ref, bias_ref, q_ref, k_ref, vt_ref, o_ref):
    n_keys = N_META + 2 * BLOCK
    n_local = q_ref.shape[0] // BLOCK
    first_k = lax.broadcasted_iota(jnp.int32, (n_keys, LANES), 1) < HEAD_DIM
    first_v = lax.broadcasted_iota(jnp.int32, (LANES, 3 * BLOCK), 0) < HEAD_DIM
    first_o = lax.broadcasted_iota(jnp.int32, (LANES, 2 * BLOCK), 0) < HEAD_DIM
    zero_k = jnp.zeros((n_keys, LANES), BF16)
    zero_v = jnp.zeros((LANES, 3 * BLOCK), BF16)
    pad_p = jnp.zeros((PAD, 2 * BLOCK), BF16)
    shape = (2 * SUBLANES, 6 * BLOCK)
    key_rows = jnp.where(lax.broadcasted_iota(jnp.int32, shape, 0) == lax.broadcasted_iota(jnp.int32, shape, 1) // (3 * BLOCK),
                         1.0, 0.0).astype(BF16)

    def block_of(i):
        n = pl.program_id(1) * n_local + i
        return n, jnp.maximum(n - 1, 0)

    def scores(i, tp):
        n, prev = block_of(i)
        cols = slice(tp // 2 * LANES, (tp // 2 + 1) * LANES)
        k2 = jnp.concatenate([k_ref[0, PAD:, cols], k_ref[prev, :, cols], k_ref[n, :, cols]], axis=0)
        kab = jnp.concatenate([jnp.where(first_k, k2, zero_k), jnp.where(first_k, zero_k, k2)], axis=0)
        q4 = jnp.concatenate([q_ref[i * BLOCK:(i + 1) * BLOCK, (2 * tp + p) * LANES:(2 * tp + p + 1) * LANES]
                              for p in range(2)], axis=0)
        return lax.dot_general(kab, q4, (((1,), (1,)), ((), ())), preferred_element_type=F32)

    heads = _attn_head_order()
    chains = [(i, tp) for i in range(n_local) for tp in range(N_Q_HEADS // 4)]
    s_next = scores(*chains[0])
    for c, (i, tp) in enumerate(chains):
        s = s_next
        if c + 1 < len(chains):
            s_next = scores(*chains[c + 1])
        n, prev = block_of(i)
        bias = bias_ref[jnp.minimum(n, 2)]
        pt, tail = [], []
        for e in range(2):
            blocks, tl = [], []
            for p in range(2):
                se = s[e * n_keys:(e + 1) * n_keys, p * LANES:(p + 1) * LANES] + bias
                sink = sink_ref[heads[2 * (2 * tp + p) + e]] * LOG2_E
                m = jnp.maximum(jnp.max(se, axis=0, keepdims=True), sink)
                blocks.append(jnp.exp2(se - m).astype(BF16))
                tl.append(jnp.exp2(sink - m))
            pt += [pad_p, jnp.concatenate(blocks, axis=1)]
            tail.append(jnp.concatenate(tl, axis=1))
        rows = slice(tp // 2 * LANES, (tp // 2 + 1) * LANES)
        v2 = jnp.concatenate([vt_ref[0, rows, :], vt_ref[prev, rows, :], vt_ref[n, rows, :]], axis=1)
        vab = jnp.concatenate([jnp.where(first_v, v2, zero_v), jnp.where(first_v, zero_v, v2)], axis=1)
        vab = jnp.concatenate([vab, key_rows], axis=0)
        ot = _dot(vab, jnp.concatenate(pt, axis=0))
        rden = [1.0 / (ot[LANES + e:LANES + e + 1] + tail[e]) for e in range(2)]
        ot = ot[:LANES] * jnp.where(first_o, rden[0], rden[1])
        for e in range(2):
            rows = slice(e * HEAD_DIM, (e + 1) * HEAD_DIM)
            o = jnp.concatenate([ot[rows, :BLOCK], ot[rows, BLOCK:]], axis=0).T
            t = 4 * (tp // 2) + 2 * e + tp % 2
            o_ref[i * BLOCK:(i + 1) * BLOCK, t * LANES:(t + 1) * LANES] = o.astype(BF16)


def _attention(q, k, vt, sinks, bias, bsz, n_blk, tq):
    steps = n_blk * BLOCK // tq
    rows = pl.BlockSpec((tq, D_ATTN), lambda b, s: (b * steps + s, 0))
    return pl.pallas_call(
        _attn_kernel,
        grid=(bsz, steps),
        in_specs=[pl.BlockSpec(memory_space=pltpu.SMEM), _const_spec(bias.shape), rows,
                  pl.BlockSpec((n_blk, BLOCK, D_KV), lambda b, s: (b, 0, 0)),
                  pl.BlockSpec((n_blk, D_KV, BLOCK), lambda b, s: (b, 0, 0))],
        out_specs=rows,
        out_shape=jax.ShapeDtypeStruct(q.shape, BF16),
        compiler_params=_params(("parallel", "parallel")),
        name="attention",
    )(sinks, bias, q, k, vt)


def _attn_bias():
    i = np.arange(BLOCK)[None, :]
    j = np.arange(BLOCK)[:, None]
    none = np.zeros((BLOCK, BLOCK), bool)
    causal = j <= i
    in_meta = np.broadcast_to(j >= PAD, (BLOCK, BLOCK))
    blk0 = [none[:N_META], none, causal & in_meta]
    blk1 = [none[:N_META], in_meta, causal]
    blk2 = [~none[:N_META], j > i, causal]
    ok = np.stack([np.concatenate(b, axis=0) for b in (blk0, blk1, blk2)])
    return jnp.asarray(np.where(ok, 0.0, NEG_INF), F32)


def _merge_kernel(rows, *refs):
    n = len(rows.specs)
    y_ref, a_ref, gs_ref, ga_ref, wglu_ref, bglu_ref, wos_ref, woa_ref, wout_ref, gain_ref, o_ref = refs[n:]
    z = jax.nn.gelu(y_ref[...])
    z = z * jax.nn.sigmoid(_dot(z.astype(BF16), wglu_ref[...].astype(BF16)) + bglu_ref[...])
    merged = (jax.nn.sigmoid(gs_ref[...].astype(F32)) * _dot(z.astype(BF16), wos_ref[...].astype(BF16))
              + jax.nn.sigmoid(ga_ref[...].astype(F32)) * _dot(a_ref[...], woa_ref[...].astype(BF16)))
    mix = _dot(merged.astype(BF16), wout_ref[...].astype(BF16))
    o_ref[...] = rows.load(refs[:n]) + _rms(mix, gain_ref[...])


def _merge(rows, layer, y, a, gs, ga, wglu, bglu, wos, woa, wout, gain, tm):
    n_rows = y.shape[0]
    row = lambda width: pl.BlockSpec((tm, width), lambda i: (i, 0))
    weights = (wglu, bglu, wos, woa, wout, gain)
    return pl.pallas_call(
        functools.partial(_merge_kernel, rows),
        grid=(n_rows // tm,),
        in_specs=rows.specs + [row(D_SSM), row(D_ATTN), row(D_MODEL), row(D_MODEL)]
                 + [_layer_spec(p, layer) for p in weights],
        out_specs=row(D_MODEL),
        out_shape=jax.ShapeDtypeStruct((n_rows, D_MODEL), F32),
        compiler_params=_params(("parallel",)),
        name="merge",
    )(*rows.operands, y, a, gs, ga, *weights)


def _ffn_kernel(rows, *refs):
    n = len(rows.specs)
    gpre_ref, wup_ref, wdown_ref, gpost_ref, o_ref = refs[n:]
    x = rows.load(refs[:n])
    h = _rms(x, gpre_ref[...]).astype(BF16)
    step = COL_STEP
    acc = jnp.zeros(x.shape, F32)
    for c in range(0, D_FF, step):
        a = jnp.maximum(_dot(h, wup_ref[:, c:c + step].astype(BF16)), 0.0)
        acc = acc + _dot((a * a).astype(BF16), wdown_ref[c:c + step, :].astype(BF16))
    o_ref[...] = x + _rms(acc, gpost_ref[...])


def _ffn(rows, n_rows, layer, gpre, wup, wdown, gpost, tm):
    weights = (gpre, wup, wdown, gpost)
    return pl.pallas_call(
        functools.partial(_ffn_kernel, rows),
        grid=(n_rows // tm,),
        in_specs=rows.specs + [_layer_spec(p, layer) for p in weights],
        out_specs=pl.BlockSpec((tm, D_MODEL), lambda i: (i, 0)),
        out_shape=jax.ShapeDtypeStruct((n_rows, D_MODEL), F32),
        compiler_params=_params(("parallel",)),
        name="ffn",
    )(*rows.operands, *weights)


def _rope_tables(n_blk):
    inv_freq = 1.0 / (ROPE_THETA ** (jnp.arange(0, HEAD_DIM, 2, dtype=F32) / HEAD_DIM))
    inv_freq = jnp.tile(inv_freq, LANES // (HEAD_DIM // 2))
    blk = (BLOCK * jnp.arange(n_blk, dtype=jnp.int32)).astype(F32)[:, None] * inv_freq
    off = (jnp.arange(BLOCK, dtype=jnp.int32) - PAD).astype(F32)[:, None] * inv_freq
    blk = jnp.stack([jnp.cos(blk), jnp.sin(blk)], axis=1)[:, :, None]
    return jnp.broadcast_to(blk, (n_blk, 2, SUBLANES, LANES)), jnp.stack([jnp.cos(off), jnp.sin(off)])


def _v_weights_t(w_in):
    return lax.optimization_barrier(w_in[:, :, C_V:C_V + D_KV].astype(BF16)).swapaxes(1, 2)


def kernel(x, meta_tokens, norm_mix_pre, norm_mix_post, norm_mlp_pre, norm_mlp_post, w_in, ssm_a_re, ssm_a_im, ssm_log_dt, ssm_b_re, ssm_b_im, ssm_c_re, ssm_c_im, ssm_d, w_glu, b_glu, attn_sinks, w_o_ssm, w_o_attn, w_out, w_up, w_down):
    bsz, seq, _ = x.shape
    depth = w_in.shape[0]
    seq_pad = PAD + N_META + seq
    assert seq_pad % BLOCK == 0 and (seq_pad // CHUNK) % N_SEG == 0
    n_blk, n_chunks = seq_pad // BLOCK, seq_pad // CHUNK
    tm, tm_out = 5 * BLOCK, 4 * BLOCK
    tq = 13 * BLOCK if n_blk % 13 == 0 else tm
    assert seq_pad % tm == 0 and seq % tm_out == 0 and seq_pad % tq == 0
    n_rows = bsz * seq_pad

    meta_block = jnp.concatenate([jnp.zeros((PAD, D_MODEL), x.dtype), meta_tokens.astype(x.dtype)], axis=0)
    stream = _Rows("input", x.reshape(bsz * seq, D_MODEL), tm, seq_pad // tm, seq // BLOCK, meta_block)
    rope_blk, rope_off = _rope_tables(n_blk)
    bias = _attn_bias()

    row = lambda m: m[:, None, :]
    wvt = _v_weights_t(w_in)
    prep = _ssm_prep(ssm_a_re, ssm_a_im, ssm_log_dt, ssm_b_re, ssm_b_im, ssm_c_re, ssm_c_im, ssm_d, n_chunks // N_SEG)

    for l in range(depth):
        u, q, k, vt, gs, ga = _inproj(stream, n_rows, l, row(norm_mix_pre), w_in, wvt, rope_blk, rope_off, tm)
        y = _ssm(u, prep, l, bsz, seq_pad)
        att = _attention(q, k, vt, attn_sinks[l], bias, bsz, n_blk, tq)
        hres = _merge(stream, l, y, att, gs, ga, w_glu, row(b_glu), w_o_ssm, w_o_attn, w_out, row(norm_mix_post), tm)
        ffn_w = (l, row(norm_mlp_pre), w_up, w_down, row(norm_mlp_post))
        if l + 1 < depth:
            stream = _Rows("padded", _ffn(_Rows("padded", hres, tm, seq_pad // tm, n_blk), n_rows, *ffn_w, tm),
                           tm, seq_pad // tm, n_blk)
        else:
            out = _ffn(_Rows("frames", hres, tm_out, seq // tm_out, n_blk), bsz * seq, *ffn_w, tm_out)
    return out.reshape(bsz, seq, D_MODEL)
```

```python
import functools
import math

import jax
import jax.numpy as jnp
import numpy as np
from jax import lax
from jax.experimental import pallas as pl
from jax.experimental.pallas import tpu as pltpu

D_MODEL = 1024
N_META = 16
HEAD_DIM = 64
N_Q_HEADS = 16
N_KV_HEADS = 4
D_ATTN = N_Q_HEADS * HEAD_DIM
D_KV = N_KV_HEADS * HEAD_DIM
BLOCK = 128
ROPE_THETA = 10000.0
ATTN_SCALE = HEAD_DIM ** -0.5
LOG2_E = math.log2(math.e)
NEG_INF = -1e30
D_SSM = D_MODEL // 2
SSM_GROUP = 16
SSM_STATE = 64
D_FF = 4 * D_MODEL
RMS_EPS = 1e-6

LANES = 128
SUBLANES = 8
MXU = 256
PAD = BLOCK - N_META
CHUNK = 16
TILE_GROUPS = LANES // SSM_GROUP
N_SSM_TILES = D_SSM // LANES
TILE_STATE = TILE_GROUPS * SSM_STATE
CHUNK_W = CHUNK * LANES
HALF_GROUPS = TILE_GROUPS // 2
HALF_LANES = LANES // 2
HALF_STATE = HALF_GROUPS * SSM_STATE
HALF_W = CHUNK * HALF_LANES
SUB_BLOCKS = MXU // HALF_LANES
N_SEG = SUBLANES
COL_STEP = 2 * MXU
VMEM_LIMIT = 56 * 1024 * 1024

C_U = 0
C_Q = C_U + D_SSM
C_K = C_Q + D_ATTN
C_V = C_K + D_KV
C_GS = C_V + D_KV
C_GA = C_GS + D_MODEL

BF16 = jnp.bfloat16
F32 = jnp.float32


def _dot(a, b):
    return jnp.dot(a, b, preferred_element_type=F32)


def _rms(x, gain):
    return x * lax.rsqrt(jnp.mean(x * x, axis=-1, keepdims=True) + RMS_EPS) * gain


def _const_spec(shape):
    return pl.BlockSpec(shape, lambda *_: (0,) * len(shape), pipeline_mode=pl.Buffered(1))


def _layer_spec(stacked, layer):
    shape = stacked.shape[1:]
    return pl.BlockSpec((None,) + shape, lambda *_: (layer,) + (0,) * len(shape), pipeline_mode=pl.Buffered(1))


def _params(sem):
    return pltpu.CompilerParams(dimension_semantics=sem, vmem_limit_bytes=VMEM_LIMIT)


class _Rows:
    def __init__(self, kind, array, tm, tiles_per_batch, blocks_per_batch, meta_block=None):
        self.has_meta = kind == "input"
        self.tiles_per_batch = tiles_per_batch
        nb = tm // BLOCK
        if kind == "padded":
            self.specs = [pl.BlockSpec((tm, D_MODEL), lambda i: (i, 0))]
            self.operands = [array]
            return
        shift = -1 if kind == "input" else 1

        def imap(j):
            return lambda i: ((i // tiles_per_batch) * blocks_per_batch
                              + jnp.maximum((i % tiles_per_batch) * nb + j + shift, 0), 0)

        self.specs = [pl.BlockSpec((BLOCK, D_MODEL), imap(j)) for j in range(nb)]
        self.operands = [array] * nb
        if self.has_meta:
            self.specs.append(_const_spec((BLOCK, D_MODEL)))
            self.operands.append(meta_block)

    def load(self, refs):
        n = len(self.specs)
        if n == 1:
            return refs[0][...]
        blocks = [r[...] for r in refs[:n - self.has_meta]]
        if self.has_meta:
            head = pl.program_id(0) % self.tiles_per_batch == 0
            blocks[0] = jnp.where(head, refs[n - 1][...], blocks[0])
        return jnp.concatenate(blocks, axis=0)


def _inproj_kernel(rows, *refs):
    n = len(rows.specs)
    gain_ref, w_ref, wvt_ref, blk_ref, off_ref, u_ref, q_ref, k_ref, vt_ref, gs_ref, ga_ref = refs[n:]
    h = _rms(rows.load(refs[:n]), gain_ref[...]).astype(BF16)

    cos_o, sin_o = off_ref[0], off_ref[1]
    cos, sin = [], []
    for b in range(blk_ref.shape[0]):
        cos_b, sin_b = (jnp.concatenate([blk_ref[b, p]] * (BLOCK // SUBLANES), axis=0) for p in range(2))
        cos.append(cos_b * cos_o - sin_b * sin_o)
        sin.append(sin_b * cos_o + cos_b * sin_o)
    cos, sin = jnp.concatenate(cos, axis=0), jnp.concatenate(sin, axis=0)
    first_half = lax.broadcasted_iota(jnp.int32, sin.shape, 1) % HEAD_DIM < HEAD_DIM // 2
    sa, sb = jnp.where(first_half, -sin, 0.0), jnp.where(first_half, 0.0, sin)

    def rope(t):
        return t * cos + pltpu.roll(t, LANES - HEAD_DIM // 2, 1) * sa + pltpu.roll(t, HEAD_DIM // 2, 1) * sb

    step = COL_STEP
    u_ref[...] = _dot(h, w_ref[:, C_U:C_U + D_SSM].astype(BF16))
    low = lax.broadcasted_iota(jnp.int32, sin.shape, 1) < HEAD_DIM
    for c in range(0, D_ATTN, step):
        t = _dot(h, w_ref[:, C_Q + c:C_Q + c + step].astype(BF16))
        tiles = [rope(t[:, j:j + LANES]) * (ATTN_SCALE * LOG2_E) for j in range(0, step, LANES)]
        for m in range(2):
            a, b = tiles[m], tiles[m + 2]
            pair = (jnp.where(low, a, pltpu.roll(b, HEAD_DIM, 1)), jnp.where(low, pltpu.roll(a, HEAD_DIM, 1), b))
            for e in range(2):
                j = c + (2 * m + e) * LANES
                q_ref[:, j:j + LANES] = pair[e].astype(BF16)
    t = _dot(h, w_ref[:, C_K:C_K + D_KV].astype(BF16))
    for j in range(0, D_KV, LANES):
        kj = rope(t[:, j:j + LANES]).astype(BF16)
        for b in range(k_ref.shape[0]):
            k_ref[b, :, j:j + LANES] = kj[b * BLOCK:(b + 1) * BLOCK]
    vt = lax.dot_general(wvt_ref[...], h, (((1,), (1,)), ((), ())), preferred_element_type=F32).astype(BF16)
    for b in range(vt_ref.shape[0]):
        vt_ref[b] = vt[:, b * BLOCK:(b + 1) * BLOCK]
    for c in range(0, D_MODEL, step):
        gs_ref[:, c:c + step] = _dot(h, w_ref[:, C_GS + c:C_GS + c + step].astype(BF16)).astype(BF16)
        ga_ref[:, c:c + step] = _dot(h, w_ref[:, C_GA + c:C_GA + c + step].astype(BF16)).astype(BF16)


def _inproj(rows, n_rows, layer, gain, w, wvt, rope_blk, rope_off, tm):
    row = lambda width: pl.BlockSpec((tm, width), lambda i: (i, 0))
    blk = pl.BlockSpec((tm // BLOCK,) + rope_blk.shape[1:], lambda i: (i % rows.tiles_per_batch, 0, 0, 0))
    out = lambda width, dtype: (row(width), jax.ShapeDtypeStruct((n_rows, width), dtype))
    per_block = lambda shape: (pl.BlockSpec((tm // BLOCK,) + shape, lambda i: (i, 0, 0)),
                               jax.ShapeDtypeStruct((n_rows // BLOCK,) + shape, BF16))
    outs = [out(D_SSM, F32),
            out(D_ATTN, BF16), per_block((BLOCK, D_KV)), per_block((D_KV, BLOCK)),
            out(D_MODEL, BF16), out(D_MODEL, BF16)]
    return pl.pallas_call(
        functools.partial(_inproj_kernel, rows),
        grid=(n_rows // tm,),
        in_specs=rows.specs + [_layer_spec(p, layer) for p in (gain, w, wvt)] + [blk, _const_spec(rope_off.shape)],
        out_specs=[o[0] for o in outs],
        out_shape=[o[1] for o in outs],
        compiler_params=_params(("parallel",)),
        name="inproj",
    )(*rows.operands, gain, w, wvt, rope_blk, rope_off)


def _ssm_spread(rp_ref, bbt_ref, cc_ref, dl_ref, pf_ref, ct_ref, wt_s, pb_s, qm_s):
    T = CHUNK
    iota = lambda shape, axis: lax.broadcasted_iota(jnp.int32, shape, axis)
    group_of = lambda shape, axis, width: (iota(shape, axis) // width) % HALF_GROUPS
    one_hot = lambda hit: jnp.where(hit, 1.0, 0.0).astype(BF16)

    even = iota((HALF_LANES, LANES), 1) < HALF_LANES
    none = jnp.zeros((HALF_LANES, LANES), BF16)
    shape = (HALF_W, HALF_LANES)
    row_copy = one_hot(iota(shape, 0) // SSM_GROUP == iota(shape, 1))
    shape = (LANES, 2 * HALF_STATE)
    state_copy = one_hot(iota(shape, 1) % SSM_STATE + iota(shape, 1) // HALF_STATE * SSM_STATE == iota(shape, 0))
    shape = (LANES, 2 * HALF_W)
    lane_copy = one_hot(iota(shape, 1) // SSM_GROUP == iota(shape, 0))
    shape = (HALF_W, LANES)
    tap_diag = group_of(shape, 0, SSM_GROUP) == group_of(shape, 1, SSM_GROUP)
    tap_skip = tap_diag & (iota(shape, 0) // HALF_LANES == T - 1) & (iota(shape, 0) % SSM_GROUP == iota(shape, 1) % SSM_GROUP)
    shape = (HALF_W, 2 * HALF_STATE)
    pb_diag = group_of(shape, 0, SSM_GROUP) == group_of(shape, 1, SSM_STATE)
    col_group = group_of((SSM_STATE, HALF_W), 1, SSM_GROUP)

    for h in range(2):
        x = _dot(row_copy, rp_ref[0, h].astype(BF16))
        y = jnp.concatenate([bbt_ref[0, h]] * T, axis=0)
        w = (x[:, :LANES] * y[:, :LANES] + x[:, LANES:] * y[:, LANES:]).astype(BF16)
        pb_s[h] = jnp.where(pb_diag, _dot(w, state_copy), 0.0).astype(BF16)

        taps = _dot(w, cc_ref[0, h].astype(BF16)) + jnp.where(tap_skip, jnp.concatenate([dl_ref[0, h]] * (HALF_W // SUBLANES), axis=0), 0.0)
        taps = jnp.where(tap_diag, taps, 0.0).astype(BF16)
        block = lambda lag: taps[(T - 1 - lag) * HALF_LANES:(T - lag) * HALF_LANES] if lag >= 0 else none
        for a in range(SUB_BLOCKS):
            for i in range(SUB_BLOCKS):
                for kk in range(SUB_BLOCKS // 2):
                    lag = SUB_BLOCKS * a + 2 * kk - i
                    wt_s[h, a, i * HALF_LANES:(i + 1) * HALF_LANES, kk * LANES:(kk + 1) * LANES] = (
                        jnp.where(even, block(lag), block(lag + 1)))

        pw = _dot(pf_ref[0, h].astype(BF16), lane_copy)
        p_re, p_im = pw[:, :HALF_W], pw[:, HALF_W:]
        c_re = jnp.concatenate([ct_ref[0, h, 0]] * (HALF_W // LANES), axis=1)
        c_im = jnp.concatenate([ct_ref[0, h, 1]] * (HALF_W // LANES), axis=1)
        for part, q in enumerate((c_re * p_re - c_im * p_im, -(c_re * p_im + c_im * p_re))):
            for g in range(HALF_GROUPS):
                r = part * HALF_STATE + g * SSM_STATE
                qm_s[h, r:r + SSM_STATE, :] = jnp.where(col_group == g, q, 0.0).astype(BF16)


def _ssm_kernel(u_ref, rp_ref, bbt_ref, cc_ref, dl_ref, pf_ref, ct_ref, a_ref, pw_ref, y_ref,
                z_s, v_s, l_s, s_s, wt_s, pb_s, qm_s):
    n_chunks = z_s.shape[0]
    seg = n_chunks // N_SEG
    half_tiles = HALF_STATE // LANES
    part_tiles = lambda part: [h * 2 * half_tiles + part * half_tiles + c for h in range(2) for c in range(half_tiles)]

    @pl.when(pl.program_id(1) == 0)
    def _():
        _ssm_spread(rp_ref, bbt_ref, cc_ref, dl_ref, pf_ref, ct_ref, wt_s, pb_s, qm_s)

    low = lax.broadcasted_iota(jnp.int32, (n_chunks, LANES), 1) < HALF_LANES

    def regroup(a, b):
        return jnp.where(low, a, pltpu.roll(b, HALF_LANES, 1)), jnp.where(low, pltpu.roll(a, HALF_LANES, 1), b)

    ut = jnp.swapaxes(u_ref[...], 0, 1)
    for m in range(CHUNK // 2):
        lo, hi = regroup(ut[2 * m], ut[2 * m + 1])
        z_s[:, m * LANES:(m + 1) * LANES] = lo.astype(BF16)
        z_s[:, HALF_W + m * LANES:HALF_W + (m + 1) * LANES] = hi.astype(BF16)
    for h in range(2):
        v = _dot(z_s[:, h * HALF_W:(h + 1) * HALF_W], pb_s[h])
        for c in range(2 * half_tiles):
            v_s[h * 2 * half_tiles + c] = v[:, c * LANES:(c + 1) * LANES]

    def load(ref, rows, part):
        return jnp.concatenate([ref[c, rows, :] for c in part_tiles(part)], axis=1)

    def store(ref, rows, part, val):
        for k, c in enumerate(part_tiles(part)):
            ref[c, rows, :] = val[:, k * LANES:(k + 1) * LANES]

    are, aim = a_ref[0, 0], a_ref[0, 1]

    def local_scan(i, carry):
        sre, sim = carry
        rows = pl.ds(i, N_SEG, stride=seg)
        store(l_s, rows, 0, sre)
        store(l_s, rows, 1, sim)
        return (are * sre - aim * sim + load(v_s, rows, 0), are * sim + aim * sre + load(v_s, rows, 1))

    zero = jnp.zeros((N_SEG, TILE_STATE), F32)
    ere, eim = lax.fori_loop(0, seg, local_scan, (zero, zero), unroll=5)

    bre, bim = a_ref[0, 2][:1], a_ref[0, 3][:1]
    tre, tim = [zero[:1]], [zero[:1]]
    for m in range(1, N_SEG):
        pre, pim = tre[-1], tim[-1]
        tre.append(bre * pre - bim * pim + ere[m - 1:m])
        tim.append(bre * pim + bim * pre + eim[m - 1:m])
    tre, tim = jnp.concatenate(tre, axis=0), jnp.concatenate(tim, axis=0)

    def add_carry(i, _):
        rows = pl.ds(i, N_SEG, stride=seg)
        pre, pim = pw_ref[0, 0, pl.ds(i, 1), :], pw_ref[0, 1, pl.ds(i, 1), :]
        store(l_s, rows, 0, load(l_s, rows, 0) + (pre * tre - pim * tim))
        store(l_s, rows, 1, load(l_s, rows, 1) + (pre * tim + pim * tre))
        return 0

    lax.fori_loop(0, seg, add_carry, 0, unroll=5)

    for c in range(4 * half_tiles):
        s_s[:, c * LANES:(c + 1) * LANES] = l_s[c].astype(BF16)
    slabs = []
    for tp in range(SUB_BLOCKS):
        acc = []
        for h in range(2):
            y = _dot(s_s[:, h * 2 * HALF_STATE:(h + 1) * 2 * HALF_STATE], qm_s[h, :, tp * MXU:(tp + 1) * MXU])
            for tq in range(tp + 1):
                y = y + _dot(z_s[:, h * HALF_W + tq * MXU:h * HALF_W + (tq + 1) * MXU], wt_s[h, tp - tq])
            acc.append(y)
        for kk in range(SUB_BLOCKS // 2):
            slabs += regroup(acc[0][:, kk * LANES:(kk + 1) * LANES], acc[1][:, kk * LANES:(kk + 1) * LANES])
        if len(slabs) == SUBLANES:
            t0 = SUB_BLOCKS * (tp + 1) - SUBLANES
            y_ref[:, t0:t0 + SUBLANES, :] = jnp.swapaxes(jnp.stack(slabs), 0, 1)
            slabs = []


def _ssm(u, prep, layer, bsz, seq_pad):
    n_chunks = seq_pad // CHUNK
    tile = lambda shape: pl.BlockSpec((None, 1) + shape, lambda j, b: (layer, j) + (0,) * len(shape))
    io = pl.BlockSpec((n_chunks, CHUNK, LANES), lambda j, b: (b, 0, j))
    by_chunk = (bsz * n_chunks, CHUNK, D_SSM)
    return pl.pallas_call(
        _ssm_kernel,
        grid=(N_SSM_TILES, bsz),
        in_specs=[io] + [tile(p.shape[2:]) for p in prep],
        out_specs=io,
        out_shape=jax.ShapeDtypeStruct(by_chunk, F32),
        scratch_shapes=[pltpu.VMEM((n_chunks, CHUNK_W), BF16),
                        pltpu.VMEM((2 * TILE_STATE // LANES, n_chunks, LANES), F32),
                        pltpu.VMEM((2 * TILE_STATE // LANES, n_chunks, LANES), F32),
                        pltpu.VMEM((n_chunks, 2 * TILE_STATE), BF16),
                        pltpu.VMEM((2, SUB_BLOCKS, MXU, MXU), BF16),
                        pltpu.VMEM((2, HALF_W, 2 * HALF_STATE), BF16),
                        pltpu.VMEM((2, 2 * HALF_STATE, HALF_W), BF16)],
        compiler_params=_params(("arbitrary", "arbitrary")),
        name="ssm",
    )(u.reshape(by_chunk), *prep).reshape(u.shape)


def _ssm_prep(a_re, a_im, log_dt, b_re, b_im, c_re, c_im, d_skip, seg):
    N, C, T, J, E = SSM_STATE, SSM_GROUP, CHUNK, N_SSM_TILES, TILE_GROUPS
    depth = a_re.shape[0]
    tiles = lambda m: m.reshape((depth, J, E) + m.shape[2:])
    a_re, a_im, dt = tiles(a_re), tiles(a_im), tiles(jnp.exp(log_dt))[..., None]
    lam_re, lam_im = a_re * dt, a_im * dt

    def powers(p, lam_re, lam_im):
        p = jnp.asarray(p, F32).reshape((-1,) + (1,) * (lam_re.ndim - 2))
        mag = jnp.exp(lam_re[:, :, None] * p)
        return mag * jnp.cos(lam_im[:, :, None] * p), mag * jnp.sin(lam_im[:, :, None] * p)

    pw_re, pw_im = powers(np.arange(T + 1), lam_re, lam_im)
    den = a_re * a_re + a_im * a_im
    nre, nim = pw_re[:, :, 1] - 1.0, pw_im[:, :, 1]
    coef_re = ((nre * a_re + nim * a_im) / den)[..., None, :]
    coef_im = ((nim * a_re - nre * a_im) / den)[..., None, :]
    bt_re, bt_im = tiles(b_re).swapaxes(-1, -2), tiles(b_im).swapaxes(-1, -2)
    halves = lambda m, axis: m.reshape(m.shape[:axis] + (2, HALF_GROUPS) + m.shape[axis + 1:])
    slab = lambda m: m.reshape(m.shape[:3] + (-1, m.shape[-1]))
    cat = lambda *ms: jnp.concatenate(ms, axis=-1)
    bb_re = slab(halves(coef_re * bt_re - coef_im * bt_im, 2))
    bb_im = slab(halves(coef_re * bt_im + coef_im * bt_re, 2))
    bbt = cat(bb_re, bb_re, -bb_im, bb_im)

    rev = lambda m: slab(halves(m[:, :, T - 1::-1], 3).swapaxes(2, 3))
    r_re, r_im = rev(pw_re), rev(pw_im)
    rp = cat(r_re, r_im, r_im, r_re)

    cr, ci = halves(tiles(c_re), 2), halves(tiles(c_im), 2)
    by_state = lambda m: m.reshape(depth, J, 2, N, HALF_LANES)
    crt, cit = by_state(cr.transpose(0, 1, 2, 5, 3, 4)), by_state(ci.transpose(0, 1, 2, 5, 3, 4))
    cc = jnp.concatenate([cat(crt, crt), cat(-cit, -cit)], axis=-2)
    skip = halves(tiles(d_skip), 2).reshape(depth, J, 2, 1, HALF_LANES)
    dl = jnp.broadcast_to(cat(skip, skip), (depth, J, 2, SUBLANES, LANES))

    fwd = lambda m: by_state(halves(m[:, :, 1:], 3).transpose(0, 1, 3, 5, 2, 4))
    pf = cat(fwd(pw_re), fwd(pw_im))
    ct = jnp.stack([cat(crt, crt), cat(cit, cit)], axis=3)

    flat = lambda m: m.reshape(depth, J, TILE_STATE)
    t_re, t_im = powers(T * np.array([1, seg]), flat(lam_re), flat(lam_im))
    a = jnp.stack([t_re[:, :, 0], t_im[:, :, 0], t_re[:, :, 1], t_im[:, :, 1]], axis=2)[:, :, :, None]
    a = jnp.broadcast_to(a, (depth, J, 4, SUBLANES, TILE_STATE))
    pw = jnp.stack(powers(T * np.arange(seg), flat(lam_re), flat(lam_im)), axis=2)
    return rp, bbt, cc, dl, pf, ct, a, pw


def _attn_head_order():
    per_kv = N_Q_HEADS // N_KV_HEADS
    return [(2 * j + e) * per_kv + g for j in range(N_KV_HEADS // 2) for g in range(per_kv) for e in range(2)]


def _attn_kernel(sink_ref, bias_ref, q_ref, k_ref, vt_ref, o_ref):
    n_keys = N_META + 2 * BLOCK
    n_local = q_ref.shape[0] // BLOCK
    first_k = lax.broadcasted_iota(jnp.int32, (n_keys, LANES), 1) < HEAD_DIM
    first_v = lax.broadcasted_iota(jnp.int32, (LANES, 3 * BLOCK), 0) < HEAD_DIM
    first_o = lax.broadcasted_iota(jnp.int32, (LANES, 2 * BLOCK), 0) < HEAD_DIM
    zero_k = jnp.zeros((n_keys, LANES), BF16)
    zero_v = jnp.zeros((LANES, 3 * BLOCK), BF16)
    pad_p = jnp.zeros((PAD, 2 * BLOCK), BF16)
    shape = (2 * SUBLANES, 6 * BLOCK)
    key_rows = jnp.where(lax.broadcasted_iota(jnp.int32, shape, 0) == lax.broadcasted_iota(jnp.int32, shape, 1) // (3 * BLOCK),
                         1.0, 0.0).astype(BF16)

    def block_of(i):
        n = pl.program_id(1) * n_local + i
        return n, jnp.maximum(n - 1, 0)

    def scores(i, tp):
        n, prev = block_of(i)
        cols = slice(tp // 2 * LANES, (tp // 2 + 1) * LANES)
        k2 = jnp.concatenate([k_ref[0, PAD:, cols], k_ref[prev, :, cols], k_ref[n, :, cols]], axis=0)
        kab = jnp.concatenate([jnp.where(first_k, k2, zero_k), jnp.where(first_k, zero_k, k2)], axis=0)
        q4 = jnp.concatenate([q_ref[i * BLOCK:(i + 1) * BLOCK, (2 * tp + p) * LANES:(2 * tp + p + 1) * LANES]
                              for p in range(2)], axis=0)
        return lax.dot_general(kab, q4, (((1,), (1,)), ((), ())), preferred_element_type=F32)

    heads = _attn_head_order()
    chains = [(i, tp) for i in range(n_local) for tp in range(N_Q_HEADS // 4)]
    s_next = scores(*chains[0])
    for c, (i, tp) in enumerate(chains):
        s = s_next
        if c + 1 < len(chains):
            s_next = scores(*chains[c + 1])
        n, prev = block_of(i)
        bias = bias_ref[jnp.minimum(n, 2)]
        pt, tail = [], []
        for e in range(2):
            blocks, tl = [], []
            for p in range(2):
                se = s[e * n_keys:(e + 1) * n_keys, p * LANES:(p + 1) * LANES] + bias
                sink = sink_ref[heads[2 * (2 * tp + p) + e]] * LOG2_E
                m = jnp.maximum(jnp.max(se, axis=0, keepdims=True), sink)
                blocks.append(jnp.exp2(se - m).astype(BF16))
                tl.append(jnp.exp2(sink - m))
            pt += [pad_p, jnp.concatenate(blocks, axis=1)]
            tail.append(jnp.concatenate(tl, axis=1))
        rows = slice(tp // 2 * LANES, (tp // 2 + 1) * LANES)
        v2 = jnp.concatenate([vt_ref[0, rows, :], vt_ref[prev, rows, :], vt_ref[n, rows, :]], axis=1)
        vab = jnp.concatenate([jnp.where(first_v, v2, zero_v), jnp.where(first_v, zero_v, v2)], axis=1)
        vab = jnp.concatenate([vab, key_rows], axis=0)
        ot = _dot(vab, jnp.concatenate(pt, axis=0))
        rden = [1.0 / (ot[LANES + e:LANES + e + 1] + tail[e]) for e in range(2)]
        ot = ot[:LANES] * jnp.where(first_o, rden[0], rden[1])
        for e in range(2):
            rows = slice(e * HEAD_DIM, (e + 1) * HEAD_DIM)
            o = jnp.concatenate([ot[rows, :BLOCK], ot[rows, BLOCK:]], axis=0).T
            t = 4 * (tp // 2) + 2 * e + tp % 2
            o_ref[i * BLOCK:(i + 1) * BLOCK, t * LANES:(t + 1) * LANES] = o.astype(BF16)


def _attention(q, k, vt, sinks, bias, bsz, n_blk, tq):
    steps = n_blk * BLOCK // tq
    rows = pl.BlockSpec((tq, D_ATTN), lambda b, s: (b * steps + s, 0))
    return pl.pallas_call(
        _attn_kernel,
        grid=(bsz, steps),
        in_specs=[pl.BlockSpec(memory_space=pltpu.SMEM), _const_spec(bias.shape), rows,
                  pl.BlockSpec((n_blk, BLOCK, D_KV), lambda b, s: (b, 0, 0)),
                  pl.BlockSpec((n_blk, D_KV, BLOCK), lambda b, s: (b, 0, 0))],
        out_specs=rows,
        out_shape=jax.ShapeDtypeStruct(q.shape, BF16),
        compiler_params=_params(("parallel", "parallel")),
        name="attention",
    )(sinks, bias, q, k, vt)


def _attn_bias():
    i = np.arange(BLOCK)[None, :]
    j = np.arange(BLOCK)[:, None]
    none = np.zeros((BLOCK, BLOCK), bool)
    causal = j <= i
    in_meta = np.broadcast_to(j >= PAD, (BLOCK, BLOCK))
    blk0 = [none[:N_META], none, causal & in_meta]
    blk1 = [none[:N_META], in_meta, causal]
    blk2 = [~none[:N_META], j > i, causal]
    ok = np.stack([np.concatenate(b, axis=0) for b in (blk0, blk1, blk2)])
    return jnp.asarray(np.where(ok, 0.0, NEG_INF), F32)


def _merge_kernel(rows, *refs):
    n = len(rows.specs)
    y_ref, a_ref, gs_ref, ga_ref, wglu_ref, bglu_ref, wos_ref, woa_ref, wout_ref, gain_ref, o_ref = refs[n:]
    z = jax.nn.gelu(y_ref[...])
    z = z * jax.nn.sigmoid(_dot(z.astype(BF16), wglu_ref[...].astype(BF16)) + bglu_ref[...])
    merged = (jax.nn.sigmoid(gs_ref[...].astype(F32)) * _dot(z.astype(BF16), wos_ref[...].astype(BF16))
              + jax.nn.sigmoid(ga_ref[...].astype(F32)) * _dot(a_ref[...], woa_ref[...].astype(BF16)))
    mix = _dot(merged.astype(BF16), wout_ref[...].astype(BF16))
    o_ref[...] = rows.load(refs[:n]) + _rms(mix, gain_ref[...])


def _merge(rows, layer, y, a, gs, ga, wglu, bglu, wos, woa, wout, gain, tm):
    n_rows = y.shape[0]
    row = lambda width: pl.BlockSpec((tm, width), lambda i: (i, 0))
    weights = (wglu, bglu, wos, woa, wout, gain)
    return pl.pallas_call(
        functools.partial(_merge_kernel, rows),
        grid=(n_rows // tm,),
        in_specs=rows.specs + [row(D_SSM), row(D_ATTN), row(D_MODEL), row(D_MODEL)]
                 + [_layer_spec(p, layer) for p in weights],
        out_specs=row(D_MODEL),
        out_shape=jax.ShapeDtypeStruct((n_rows, D_MODEL), F32),
        compiler_params=_params(("parallel",)),
        name="merge",
    )(*rows.operands, y, a, gs, ga, *weights)


def _ffn_kernel(rows, *refs):
    n = len(rows.specs)
    gpre_ref, wup_ref, wdown_ref, gpost_ref, o_ref = refs[n:]
    x = rows.load(refs[:n])
    h = (x * gpre_ref[...]).astype(BF16)
    r = lax.rsqrt(jnp.mean(x * x, axis=-1, keepdims=True) + RMS_EPS)
    step = COL_STEP
    acc = jnp.zeros(x.shape, F32)
    for c in range(0, D_FF, step):
        a = jnp.maximum(_dot(h, wup_ref[:, c:c + step].astype(BF16)), 0.0)
        acc = acc + _dot((a * a).astype(BF16), wdown_ref[c:c + step, :].astype(BF16))
    o_ref[...] = x + _rms(acc * (r * r), gpost_ref[...])


def _ffn(rows, n_rows, layer, gpre, wup, wdown, gpost, tm):
    weights = (gpre, wup, wdown, gpost)
    return pl.pallas_call(
        functools.partial(_ffn_kernel, rows),
        grid=(n_rows // tm,),
        in_specs=rows.specs + [_layer_spec(p, layer) for p in weights],
        out_specs=pl.BlockSpec((tm, D_MODEL), lambda i: (i, 0)),
        out_shape=jax.ShapeDtypeStruct((n_rows, D_MODEL), F32),
        compiler_params=_params(("parallel",)),
        name="ffn",
    )(*rows.operands, *weights)


def _rope_tables(n_blk):
    inv_freq = 1.0 / (ROPE_THETA ** (jnp.arange(0, HEAD_DIM, 2, dtype=F32) / HEAD_DIM))
    inv_freq = jnp.tile(inv_freq, LANES // (HEAD_DIM // 2))
    blk = (BLOCK * jnp.arange(n_blk, dtype=jnp.int32)).astype(F32)[:, None] * inv_freq
    off = (jnp.arange(BLOCK, dtype=jnp.int32) - PAD).astype(F32)[:, None] * inv_freq
    blk = jnp.stack([jnp.cos(blk), jnp.sin(blk)], axis=1)[:, :, None]
    return jnp.broadcast_to(blk, (n_blk, 2, SUBLANES, LANES)), jnp.stack([jnp.cos(off), jnp.sin(off)])


def _v_weights_t(w_in):
    return lax.optimization_barrier(w_in[:, :, C_V:C_V + D_KV].astype(BF16)).swapaxes(1, 2)


def kernel(x, meta_tokens, norm_mix_pre, norm_mix_post, norm_mlp_pre, norm_mlp_post, w_in, ssm_a_re, ssm_a_im, ssm_log_dt, ssm_b_re, ssm_b_im, ssm_c_re, ssm_c_im, ssm_d, w_glu, b_glu, attn_sinks, w_o_ssm, w_o_attn, w_out, w_up, w_down):
    bsz, seq, _ = x.shape
    depth = w_in.shape[0]
    seq_pad = PAD + N_META + seq
    assert seq_pad % BLOCK == 0 and (seq_pad // CHUNK) % N_SEG == 0
    n_blk, n_chunks = seq_pad // BLOCK, seq_pad // CHUNK
    tm, tm_out = 5 * BLOCK, 4 * BLOCK
    tq = 13 * BLOCK if n_blk % 13 == 0 else tm
    assert seq_pad % tm == 0 and seq % tm_out == 0 and seq_pad % tq == 0
    n_rows = bsz * seq_pad

    meta_block = jnp.concatenate([jnp.zeros((PAD, D_MODEL), x.dtype), meta_tokens.astype(x.dtype)], axis=0)
    stream = _Rows("input", x.reshape(bsz * seq, D_MODEL), tm, seq_pad // tm, seq // BLOCK, meta_block)
    rope_blk, rope_off = _rope_tables(n_blk)
    bias = _attn_bias()

    row = lambda m: m[:, None, :]
    wvt = _v_weights_t(w_in)
    prep = _ssm_prep(ssm_a_re, ssm_a_im, ssm_log_dt, ssm_b_re, ssm_b_im, ssm_c_re, ssm_c_im, ssm_d, n_chunks // N_SEG)

    for l in range(depth):
        u, q, k, vt, gs, ga = _inproj(stream, n_rows, l, row(norm_mix_pre), w_in, wvt, rope_blk, rope_off, tm)
        y = _ssm(u, prep, l, bsz, seq_pad)
        att = _attention(q, k, vt, attn_sinks[l], bias, bsz, n_blk, tq)
        hres = _merge(stream, l, y, att, gs, ga, w_glu, row(b_glu), w_o_ssm, w_o_attn, w_out, row(norm_mix_post), tm)
        ffn_w = (l, row(norm_mlp_pre), w_up, w_down, row(norm_mlp_post))
        if l + 1 < depth:
            stream = _Rows("padded", _ffn(_Rows("padded", hres, tm, seq_pad // tm, n_blk), n_rows, *ffn_w, tm),
                           tm, seq_pad // tm, n_blk)
        else:
            out = _ffn(_Rows("frames", hres, tm_out, seq // tm_out, n_blk), bsz * seq, *ffn_w, tm_out)
    return out.reshape(bsz, seq, D_MODEL)
```

```python
import functools
import math

import jax
import jax.numpy as jnp
import numpy as np
from jax import lax
from jax.experimental import pallas as pl
from jax.experimental.pallas import tpu as pltpu

D_MODEL = 1024
N_META = 16
HEAD_DIM = 64
N_Q_HEADS = 16
N_KV_HEADS = 4
D_ATTN = N_Q_HEADS * HEAD_DIM
D_KV = N_KV_HEADS * HEAD_DIM
BLOCK = 128
ROPE_THETA = 10000.0
ATTN_SCALE = HEAD_DIM ** -0.5
LOG2_E = math.log2(math.e)
NEG_INF = -1e30
D_SSM = D_MODEL // 2
SSM_GROUP = 16
SSM_STATE = 64
D_FF = 4 * D_MODEL
RMS_EPS = 1e-6

LANES = 128
SUBLANES = 8
MXU = 256
PAD = BLOCK - N_META
CHUNK = 16
TILE_GROUPS = LANES // SSM_GROUP
N_SSM_TILES = D_SSM // LANES
TILE_STATE = TILE_GROUPS * SSM_STATE
CHUNK_W = CHUNK * LANES
HALF_GROUPS = TILE_GROUPS // 2
HALF_LANES = LANES // 2
HALF_STATE = HALF_GROUPS * SSM_STATE
HALF_W = CHUNK * HALF_LANES
SUB_BLOCKS = MXU // HALF_LANES
N_SEG = SUBLANES
COL_STEP = 2 * MXU
VMEM_LIMIT = 56 * 1024 * 1024

C_U = 0
C_Q = C_U + D_SSM
C_K = C_Q + D_ATTN
C_V = C_K + D_KV
C_GS = C_V + D_KV
C_GA = C_GS + D_MODEL

BF16 = jnp.bfloat16
F32 = jnp.float32


def _dot(a, b):
    return jnp.dot(a, b, preferred_element_type=F32)


def _rms(x, gain):
    return x * lax.rsqrt(jnp.mean(x * x, axis=-1, keepdims=True) + RMS_EPS) * gain


def _const_spec(shape):
    return pl.BlockSpec(shape, lambda *_: (0,) * len(shape), pipeline_mode=pl.Buffered(1))


def _layer_spec(stacked, layer):
    shape = stacked.shape[1:]
    return pl.BlockSpec((None,) + shape, lambda *_: (layer,) + (0,) * len(shape), pipeline_mode=pl.Buffered(1))


def _params(sem):
    return pltpu.CompilerParams(dimension_semantics=sem, vmem_limit_bytes=VMEM_LIMIT)


class _Rows:
    def __init__(self, kind, array, tm, tiles_per_batch, blocks_per_batch, meta_block=None):
        self.has_meta = kind == "input"
        self.tiles_per_batch = tiles_per_batch
        nb = tm // BLOCK
        if kind == "padded":
            self.specs = [pl.BlockSpec((tm, D_MODEL), lambda i: (i, 0))]
            self.operands = [array]
            return
        shift = -1 if kind == "input" else 1

        def imap(j):
            return lambda i: ((i // tiles_per_batch) * blocks_per_batch
                              + jnp.maximum((i % tiles_per_batch) * nb + j + shift, 0), 0)

        self.specs = [pl.BlockSpec((BLOCK, D_MODEL), imap(j)) for j in range(nb)]
        self.operands = [array] * nb
        if self.has_meta:
            self.specs.append(_const_spec((BLOCK, D_MODEL)))
            self.operands.append(meta_block)

    def load(self, refs):
        n = len(self.specs)
        if n == 1:
            return refs[0][...]
        blocks = [r[...] for r in refs[:n - self.has_meta]]
        if self.has_meta:
            head = pl.program_id(0) % self.tiles_per_batch == 0
            blocks[0] = jnp.where(head, refs[n - 1][...], blocks[0])
        return jnp.concatenate(blocks, axis=0)


def _inproj_kernel(rows, *refs):
    n = len(rows.specs)
    gain_ref, w_ref, wvt_ref, blk_ref, off_ref, u_ref, q_ref, k_ref, vt_ref, gs_ref, ga_ref = refs[n:]
    x = rows.load(refs[:n])
    r = lax.rsqrt(jnp.mean(x * x, axis=-1, keepdims=True) + RMS_EPS)
    hg = (x * gain_ref[...]).astype(BF16)
    h = (x * r * gain_ref[...]).astype(BF16)

    cos_o, sin_o = off_ref[0], off_ref[1]
    cos, sin = [], []
    for b in range(blk_ref.shape[0]):
        cos_b, sin_b = (jnp.concatenate([blk_ref[b, p]] * (BLOCK // SUBLANES), axis=0) for p in range(2))
        cos.append(cos_b * cos_o - sin_b * sin_o)
        sin.append(sin_b * cos_o + cos_b * sin_o)
    cos, sin = jnp.concatenate(cos, axis=0), jnp.concatenate(sin, axis=0)
    first_half = lax.broadcasted_iota(jnp.int32, sin.shape, 1) % HEAD_DIM < HEAD_DIM // 2
    sa, sb = jnp.where(first_half, -sin, 0.0), jnp.where(first_half, 0.0, sin)

    def rope(t):
        return t * cos + pltpu.roll(t, LANES - HEAD_DIM // 2, 1) * sa + pltpu.roll(t, HEAD_DIM // 2, 1) * sb

    step = COL_STEP
    u_ref[...] = _dot(hg, w_ref[:, C_U:C_U + D_SSM].astype(BF16)) * r
    low = lax.broadcasted_iota(jnp.int32, sin.shape, 1) < HEAD_DIM
    for c in range(0, D_ATTN, step):
        t = _dot(hg, w_ref[:, C_Q + c:C_Q + c + step].astype(BF16))
        tiles = [rope(t[:, j:j + LANES]) * (r * (ATTN_SCALE * LOG2_E)) for j in range(0, step, LANES)]
        for m in range(2):
            a, b = tiles[m], tiles[m + 2]
            pair = (jnp.where(low, a, pltpu.roll(b, HEAD_DIM, 1)), jnp.where(low, pltpu.roll(a, HEAD_DIM, 1), b))
            for e in range(2):
                j = c + (2 * m + e) * LANES
                q_ref[:, j:j + LANES] = pair[e].astype(BF16)
    t = _dot(hg, w_ref[:, C_K:C_K + D_KV].astype(BF16)) * r
    for j in range(0, D_KV, LANES):
        kj = rope(t[:, j:j + LANES]).astype(BF16)
        for b in range(k_ref.shape[0]):
            k_ref[b, :, j:j + LANES] = kj[b * BLOCK:(b + 1) * BLOCK]
    vt = lax.dot_general(wvt_ref[...], h, (((1,), (1,)), ((), ())), preferred_element_type=F32).astype(BF16)
    for b in range(vt_ref.shape[0]):
        vt_ref[b] = vt[:, b * BLOCK:(b + 1) * BLOCK]
    for c in range(0, D_MODEL, step):
        gs_ref[:, c:c + step] = (_dot(hg, w_ref[:, C_GS + c:C_GS + c + step].astype(BF16)) * r).astype(BF16)
        ga_ref[:, c:c + step] = (_dot(hg, w_ref[:, C_GA + c:C_GA + c + step].astype(BF16)) * r).astype(BF16)


def _inproj(rows, n_rows, layer, gain, w, wvt, rope_blk, rope_off, tm):
    row = lambda width: pl.BlockSpec((tm, width), lambda i: (i, 0))
    blk = pl.BlockSpec((tm // BLOCK,) + rope_blk.shape[1:], lambda i: (i % rows.tiles_per_batch, 0, 0, 0))
    out = lambda width, dtype: (row(width), jax.ShapeDtypeStruct((n_rows, width), dtype))
    per_block = lambda shape: (pl.BlockSpec((tm // BLOCK,) + shape, lambda i: (i, 0, 0)),
                               jax.ShapeDtypeStruct((n_rows // BLOCK,) + shape, BF16))
    outs = [out(D_SSM, F32),
            out(D_ATTN, BF16), per_block((BLOCK, D_KV)), per_block((D_KV, BLOCK)),
            out(D_MODEL, BF16), out(D_MODEL, BF16)]
    return pl.pallas_call(
        functools.partial(_inproj_kernel, rows),
        grid=(n_rows // tm,),
        in_specs=rows.specs + [_layer_spec(p, layer) for p in (gain, w, wvt)] + [blk, _const_spec(rope_off.shape)],
        out_specs=[o[0] for o in outs],
        out_shape=[o[1] for o in outs],
        compiler_params=_params(("parallel",)),
        name="inproj",
    )(*rows.operands, gain, w, wvt, rope_blk, rope_off)


def _ssm_spread(rp_ref, bbt_ref, cc_ref, dl_ref, pf_ref, ct_ref, wt_s, pb_s, qm_s):
    T = CHUNK
    iota = lambda shape, axis: lax.broadcasted_iota(jnp.int32, shape, axis)
    group_of = lambda shape, axis, width: (iota(shape, axis) // width) % HALF_GROUPS
    one_hot = lambda hit: jnp.where(hit, 1.0, 0.0).astype(BF16)

    even = iota((HALF_LANES, LANES), 1) < HALF_LANES
    none = jnp.zeros((HALF_LANES, LANES), BF16)
    shape = (HALF_W, HALF_LANES)
    row_copy = one_hot(iota(shape, 0) // SSM_GROUP == iota(shape, 1))
    shape = (LANES, 2 * HALF_STATE)
    state_copy = one_hot(iota(shape, 1) % SSM_STATE + iota(shape, 1) // HALF_STATE * SSM_STATE == iota(shape, 0))
    shape = (LANES, 2 * HALF_W)
    lane_copy = one_hot(iota(shape, 1) // SSM_GROUP == iota(shape, 0))
    shape = (HALF_W, LANES)
    tap_diag = group_of(shape, 0, SSM_GROUP) == group_of(shape, 1, SSM_GROUP)
    tap_skip = tap_diag & (iota(shape, 0) // HALF_LANES == T - 1) & (iota(shape, 0) % SSM_GROUP == iota(shape, 1) % SSM_GROUP)
    shape = (HALF_W, 2 * HALF_STATE)
    pb_diag = group_of(shape, 0, SSM_GROUP) == group_of(shape, 1, SSM_STATE)
    col_group = group_of((SSM_STATE, HALF_W), 1, SSM_GROUP)

    for h in range(2):
        x = _dot(row_copy, rp_ref[0, h].astype(BF16))
        y = jnp.concatenate([bbt_ref[0, h]] * T, axis=0)
        w = (x[:, :LANES] * y[:, :LANES] + x[:, LANES:] * y[:, LANES:]).astype(BF16)
        pb_s[h] = jnp.where(pb_diag, _dot(w, state_copy), 0.0).astype(BF16)

        taps = _dot(w, cc_ref[0, h].astype(BF16)) + jnp.where(tap_skip, jnp.concatenate([dl_ref[0, h]] * (HALF_W // SUBLANES), axis=0), 0.0)
        taps = jnp.where(tap_diag, taps, 0.0).astype(BF16)
        block = lambda lag: taps[(T - 1 - lag) * HALF_LANES:(T - lag) * HALF_LANES] if lag >= 0 else none
        for a in range(SUB_BLOCKS):
            for i in range(SUB_BLOCKS):
                for kk in range(SUB_BLOCKS // 2):
                    lag = SUB_BLOCKS * a + 2 * kk - i
                    wt_s[h, a, i * HALF_LANES:(i + 1) * HALF_LANES, kk * LANES:(kk + 1) * LANES] = (
                        jnp.where(even, block(lag), block(lag + 1)))

        pw = _dot(pf_ref[0, h].astype(BF16), lane_copy)
        p_re, p_im = pw[:, :HALF_W], pw[:, HALF_W:]
        c_re = jnp.concatenate([ct_ref[0, h, 0]] * (HALF_W // LANES), axis=1)
        c_im = jnp.concatenate([ct_ref[0, h, 1]] * (HALF_W // LANES), axis=1)
        for part, q in enumerate((c_re * p_re - c_im * p_im, -(c_re * p_im + c_im * p_re))):
            for g in range(HALF_GROUPS):
                r = part * HALF_STATE + g * SSM_STATE
                qm_s[h, r:r + SSM_STATE, :] = jnp.where(col_group == g, q, 0.0).astype(BF16)


def _ssm_kernel(u_ref, rp_ref, bbt_ref, cc_ref, dl_ref, pf_ref, ct_ref, a_ref, pw_ref, y_ref,
                z_s, v_s, l_s, s_s, wt_s, pb_s, qm_s):
    n_chunks = z_s.shape[0]
    seg = n_chunks // N_SEG
    half_tiles = HALF_STATE // LANES
    part_tiles = lambda part: [h * 2 * half_tiles + part * half_tiles + c for h in range(2) for c in range(half_tiles)]

    @pl.when(pl.program_id(1) == 0)
    def _():
        _ssm_spread(rp_ref, bbt_ref, cc_ref, dl_ref, pf_ref, ct_ref, wt_s, pb_s, qm_s)

    low = lax.broadcasted_iota(jnp.int32, (n_chunks, LANES), 1) < HALF_LANES

    def regroup(a, b):
        return jnp.where(low, a, pltpu.roll(b, HALF_LANES, 1)), jnp.where(low, pltpu.roll(a, HALF_LANES, 1), b)

    ut = jnp.swapaxes(u_ref[...], 0, 1)
    for m in range(CHUNK // 2):
        lo, hi = regroup(ut[2 * m], ut[2 * m + 1])
        z_s[:, m * LANES:(m + 1) * LANES] = lo.astype(BF16)
        z_s[:, HALF_W + m * LANES:HALF_W + (m + 1) * LANES] = hi.astype(BF16)
    for h in range(2):
        v = _dot(z_s[:, h * HALF_W:(h + 1) * HALF_W], pb_s[h])
        for c in range(2 * half_tiles):
            v_s[h * 2 * half_tiles + c] = v[:, c * LANES:(c + 1) * LANES]

    def load(ref, rows, part):
        return jnp.concatenate([ref[c, rows, :] for c in part_tiles(part)], axis=1)

    def store(ref, rows, part, val):
        for k, c in enumerate(part_tiles(part)):
            ref[c, rows, :] = val[:, k * LANES:(k + 1) * LANES]

    are, aim = a_ref[0, 0], a_ref[0, 1]

    def local_scan(i, carry):
        sre, sim = carry
        rows = pl.ds(i, N_SEG, stride=seg)
        store(l_s, rows, 0, sre)
        store(l_s, rows, 1, sim)
        return (are * sre - aim * sim + load(v_s, rows, 0), are * sim + aim * sre + load(v_s, rows, 1))

    zero = jnp.zeros((N_SEG, TILE_STATE), F32)
    ere, eim = lax.fori_loop(0, seg, local_scan, (zero, zero), unroll=5)

    bre, bim = a_ref[0, 2][:1], a_ref[0, 3][:1]
    tre, tim = [zero[:1]], [zero[:1]]
    for m in range(1, N_SEG):
        pre, pim = tre[-1], tim[-1]
        tre.append(bre * pre - bim * pim + ere[m - 1:m])
        tim.append(bre * pim + bim * pre + eim[m - 1:m])
    tre, tim = jnp.concatenate(tre, axis=0), jnp.concatenate(tim, axis=0)

    def add_carry(i, _):
        rows = pl.ds(i, N_SEG, stride=seg)
        pre, pim = pw_ref[0, 0, pl.ds(i, 1), :], pw_ref[0, 1, pl.ds(i, 1), :]
        store(l_s, rows, 0, load(l_s, rows, 0) + (pre * tre - pim * tim))
        store(l_s, rows, 1, load(l_s, rows, 1) + (pre * tim + pim * tre))
        return 0

    lax.fori_loop(0, seg, add_carry, 0, unroll=5)

    for c in range(4 * half_tiles):
        s_s[:, c * LANES:(c + 1) * LANES] = l_s[c].astype(BF16)
    slabs = []
    for tp in range(SUB_BLOCKS):
        acc = []
        for h in range(2):
            y = _dot(s_s[:, h * 2 * HALF_STATE:(h + 1) * 2 * HALF_STATE], qm_s[h, :, tp * MXU:(tp + 1) * MXU])
            for tq in range(tp + 1):
                y = y + _dot(z_s[:, h * HALF_W + tq * MXU:h * HALF_W + (tq + 1) * MXU], wt_s[h, tp - tq])
            acc.append(y)
        for kk in range(SUB_BLOCKS // 2):
            slabs += regroup(acc[0][:, kk * LANES:(kk + 1) * LANES], acc[1][:, kk * LANES:(kk + 1) * LANES])
        if len(slabs) == SUBLANES:
            t0 = SUB_BLOCKS * (tp + 1) - SUBLANES
            y_ref[:, t0:t0 + SUBLANES, :] = jnp.swapaxes(jnp.stack(slabs), 0, 1)
            slabs = []


def _ssm(u, prep, layer, bsz, seq_pad):
    n_chunks = seq_pad // CHUNK
    tile = lambda shape: pl.BlockSpec((None, 1) + shape, lambda j, b: (layer, j) + (0,) * len(shape))
    io = pl.BlockSpec((n_chunks, CHUNK, LANES), lambda j, b: (b, 0, j))
    by_chunk = (bsz * n_chunks, CHUNK, D_SSM)
    return pl.pallas_call(
        _ssm_kernel,
        grid=(N_SSM_TILES, bsz),
        in_specs=[io] + [tile(p.shape[2:]) for p in prep],
        out_specs=io,
        out_shape=jax.ShapeDtypeStruct(by_chunk, F32),
        scratch_shapes=[pltpu.VMEM((n_chunks, CHUNK_W), BF16),
                        pltpu.VMEM((2 * TILE_STATE // LANES, n_chunks, LANES), F32),
                        pltpu.VMEM((2 * TILE_STATE // LANES, n_chunks, LANES), F32),
                        pltpu.VMEM((n_chunks, 2 * TILE_STATE), BF16),
                        pltpu.VMEM((2, SUB_BLOCKS, MXU, MXU), BF16),
                        pltpu.VMEM((2, HALF_W, 2 * HALF_STATE), BF16),
                        pltpu.VMEM((2, 2 * HALF_STATE, HALF_W), BF16)],
        compiler_params=_params(("arbitrary", "arbitrary")),
        name="ssm",
    )(u.reshape(by_chunk), *prep).reshape(u.shape)


def _ssm_prep(a_re, a_im, log_dt, b_re, b_im, c_re, c_im, d_skip, seg):
    N, C, T, J, E = SSM_STATE, SSM_GROUP, CHUNK, N_SSM_TILES, TILE_GROUPS
    depth = a_re.shape[0]
    tiles = lambda m: m.reshape((depth, J, E) + m.shape[2:])
    a_re, a_im, dt = tiles(a_re), tiles(a_im), tiles(jnp.exp(log_dt))[..., None]
    lam_re, lam_im = a_re * dt, a_im * dt

    def powers(p, lam_re, lam_im):
        p = jnp.asarray(p, F32).reshape((-1,) + (1,) * (lam_re.ndim - 2))
        mag = jnp.exp(lam_re[:, :, None] * p)
        return mag * jnp.cos(lam_im[:, :, None] * p), mag * jnp.sin(lam_im[:, :, None] * p)

    pw_re, pw_im = powers(np.arange(T + 1), lam_re, lam_im)
    den = a_re * a_re + a_im * a_im
    nre, nim = pw_re[:, :, 1] - 1.0, pw_im[:, :, 1]
    coef_re = ((nre * a_re + nim * a_im) / den)[..., None, :]
    coef_im = ((nim * a_re - nre * a_im) / den)[..., None, :]
    bt_re, bt_im = tiles(b_re).swapaxes(-1, -2), tiles(b_im).swapaxes(-1, -2)
    halves = lambda m, axis: m.reshape(m.shape[:axis] + (2, HALF_GROUPS) + m.shape[axis + 1:])
    slab = lambda m: m.reshape(m.shape[:3] + (-1, m.shape[-1]))
    cat = lambda *ms: jnp.concatenate(ms, axis=-1)
    bb_re = slab(halves(coef_re * bt_re - coef_im * bt_im, 2))
    bb_im = slab(halves(coef_re * bt_im + coef_im * bt_re, 2))
    bbt = cat(bb_re, bb_re, -bb_im, bb_im)

    rev = lambda m: slab(halves(m[:, :, T - 1::-1], 3).swapaxes(2, 3))
    r_re, r_im = rev(pw_re), rev(pw_im)
    rp = cat(r_re, r_im, r_im, r_re)

    cr, ci = halves(tiles(c_re), 2), halves(tiles(c_im), 2)
    by_state = lambda m: m.reshape(depth, J, 2, N, HALF_LANES)
    crt, cit = by_state(cr.transpose(0, 1, 2, 5, 3, 4)), by_state(ci.transpose(0, 1, 2, 5, 3, 4))
    cc = jnp.concatenate([cat(crt, crt), cat(-cit, -cit)], axis=-2)
    skip = halves(tiles(d_skip), 2).reshape(depth, J, 2, 1, HALF_LANES)
    dl = jnp.broadcast_to(cat(skip, skip), (depth, J, 2, SUBLANES, LANES))

    fwd = lambda m: by_state(halves(m[:, :, 1:], 3).transpose(0, 1, 3, 5, 2, 4))
    pf = cat(fwd(pw_re), fwd(pw_im))
    ct = jnp.stack([cat(crt, crt), cat(cit, cit)], axis=3)

    flat = lambda m: m.reshape(depth, J, TILE_STATE)
    t_re, t_im = powers(T * np.array([1, seg]), flat(lam_re), flat(lam_im))
    a = jnp.stack([t_re[:, :, 0], t_im[:, :, 0], t_re[:, :, 1], t_im[:, :, 1]], axis=2)[:, :, :, None]
    a = jnp.broadcast_to(a, (depth, J, 4, SUBLANES, TILE_STATE))
    pw = jnp.stack(powers(T * np.arange(seg), flat(lam_re), flat(lam_im)), axis=2)
    return rp, bbt, cc, dl, pf, ct, a, pw


def _attn_head_order():
    per_kv = N_Q_HEADS // N_KV_HEADS
    return [(2 * j + e) * per_kv + g for j in range(N_KV_HEADS // 2) for g in range(per_kv) for e in range(2)]


def _attn_kernel(sink_ref, bias_ref, q_ref, k_ref, vt_ref, o_ref):
    n_keys = N_META + 2 * BLOCK
    n_local = q_ref.shape[0] // BLOCK
    first_k = lax.broadcasted_iota(jnp.int32, (n_keys, LANES), 1) < HEAD_DIM
    first_v = lax.broadcasted_iota(jnp.int32, (LANES, 3 * BLOCK), 0) < HEAD_DIM
    first_o = lax.broadcasted_iota(jnp.int32, (LANES, 2 * BLOCK), 0) < HEAD_DIM
    zero_k = jnp.zeros((n_keys, LANES), BF16)
    zero_v = jnp.zeros((LANES, 3 * BLOCK), BF16)
    pad_p = jnp.zeros((PAD, 2 * BLOCK), BF16)
    shape = (2 * SUBLANES, 6 * BLOCK)
    key_rows = jnp.where(lax.broadcasted_iota(jnp.int32, shape, 0) == lax.broadcasted_iota(jnp.int32, shape, 1) // (3 * BLOCK),
                         1.0, 0.0).astype(BF16)

    def block_of(i):
        n = pl.program_id(1) * n_local + i
        return n, jnp.maximum(n - 1, 0)

    def scores(i, tp):
        n, prev = block_of(i)
        cols = slice(tp // 2 * LANES, (tp // 2 + 1) * LANES)
        k2 = jnp.concatenate([k_ref[0, PAD:, cols], k_ref[prev, :, cols], k_ref[n, :, cols]], axis=0)
        kab = jnp.concatenate([jnp.where(first_k, k2, zero_k), jnp.where(first_k, zero_k, k2)], axis=0)
        q4 = jnp.concatenate([q_ref[i * BLOCK:(i + 1) * BLOCK, (2 * tp + p) * LANES:(2 * tp + p + 1) * LANES]
                              for p in range(2)], axis=0)
        return lax.dot_general(kab, q4, (((1,), (1,)), ((), ())), preferred_element_type=F32)

    heads = _attn_head_order()
    chains = [(i, tp) for i in range(n_local) for tp in range(N_Q_HEADS // 4)]
    s_next = scores(*chains[0])
    for c, (i, tp) in enumerate(chains):
        s = s_next
        if c + 1 < len(chains):
            s_next = scores(*chains[c + 1])
        n, prev = block_of(i)
        bias = bias_ref[jnp.minimum(n, 2)]
        pt, tail = [], []
        for e in range(2):
            blocks, tl = [], []
            for p in range(2):
                se = s[e * n_keys:(e + 1) * n_keys, p * LANES:(p + 1) * LANES] + bias
                sink = sink_ref[heads[2 * (2 * tp + p) + e]] * LOG2_E
                m = jnp.maximum(jnp.max(se, axis=0, keepdims=True), sink)
                blocks.append(jnp.exp2(se - m).astype(BF16))
                tl.append(jnp.exp2(sink - m))
            pt += [pad_p, jnp.concatenate(blocks, axis=1)]
            tail.append(jnp.concatenate(tl, axis=1))
        rows = slice(tp // 2 * LANES, (tp // 2 + 1) * LANES)
        v2 = jnp.concatenate([vt_ref[0, rows, :], vt_ref[prev, rows, :], vt_ref[n, rows, :]], axis=1)
        vab = jnp.concatenate([jnp.where(first_v, v2, zero_v), jnp.where(first_v, zero_v, v2)], axis=1)
        vab = jnp.concatenate([vab, key_rows], axis=0)
        ot = _dot(vab, jnp.concatenate(pt, axis=0))
        rden = [1.0 / (ot[LANES + e:LANES + e + 1] + tail[e]) for e in range(2)]
        ot = ot[:LANES] * jnp.where(first_o, rden[0], rden[1])
        for e in range(2):
            rows = slice(e * HEAD_DIM, (e + 1) * HEAD_DIM)
            o = jnp.concatenate([ot[rows, :BLOCK], ot[rows, BLOCK:]], axis=0).T
            t = 4 * (tp // 2) + 2 * e + tp % 2
            o_ref[i * BLOCK:(i + 1) * BLOCK, t * LANES:(t + 1) * LANES] = o.astype(BF16)


def _attention(q, k, vt, sinks, bias, bsz, n_blk, tq):
    steps = n_blk * BLOCK // tq
    rows = pl.BlockSpec((tq, D_ATTN), lambda b, s: (b * steps + s, 0))
    return pl.pallas_call(
        _attn_kernel,
        grid=(bsz, steps),
        in_specs=[pl.BlockSpec(memory_space=pltpu.SMEM), _const_spec(bias.shape), rows,
                  pl.BlockSpec((n_blk, BLOCK, D_KV), lambda b, s: (b, 0, 0)),
                  pl.BlockSpec((n_blk, D_KV, BLOCK), lambda b, s: (b, 0, 0))],
        out_specs=rows,
        out_shape=jax.ShapeDtypeStruct(q.shape, BF16),
        compiler_params=_params(("parallel", "parallel")),
        name="attention",
    )(sinks, bias, q, k, vt)


def _attn_bias():
    i = np.arange(BLOCK)[None, :]
    j = np.arange(BLOCK)[:, None]
    none = np.zeros((BLOCK, BLOCK), bool)
    causal = j <= i
    in_meta = np.broadcast_to(j >= PAD, (BLOCK, BLOCK))
    blk0 = [none[:N_META], none, causal & in_meta]
    blk1 = [none[:N_META], in_meta, causal]
    blk2 = [~none[:N_META], j > i, causal]
    ok = np.stack([np.concatenate(b, axis=0) for b in (blk0, blk1, blk2)])
    return jnp.asarray(np.where(ok, 0.0, NEG_INF), F32)


def _merge_kernel(rows, *refs):
    n = len(rows.specs)
    y_ref, a_ref, gs_ref, ga_ref, wglu_ref, bglu_ref, wos_ref, woa_ref, wout_ref, gain_ref, o_ref = refs[n:]
    z = jax.nn.gelu(y_ref[...])
    z = z * jax.nn.sigmoid(_dot(z.astype(BF16), wglu_ref[...].astype(BF16)) + bglu_ref[...])
    merged = (jax.nn.sigmoid(gs_ref[...].astype(F32)) * _dot(z.astype(BF16), wos_ref[...].astype(BF16))
              + jax.nn.sigmoid(ga_ref[...].astype(F32)) * _dot(a_ref[...], woa_ref[...].astype(BF16)))
    mix = _dot(merged.astype(BF16), wout_ref[...].astype(BF16))
    o_ref[...] = rows.load(refs[:n]) + _rms(mix, gain_ref[...])


def _merge(rows, layer, y, a, gs, ga, wglu, bglu, wos, woa, wout, gain, tm):
    n_rows = y.shape[0]
    row = lambda width: pl.BlockSpec((tm, width), lambda i: (i, 0))
    weights = (wglu, bglu, wos, woa, wout, gain)
    return pl.pallas_call(
        functools.partial(_merge_kernel, rows),
        grid=(n_rows // tm,),
        in_specs=rows.specs + [row(D_SSM), row(D_ATTN), row(D_MODEL), row(D_MODEL)]
                 + [_layer_spec(p, layer) for p in weights],
        out_specs=row(D_MODEL),
        out_shape=jax.ShapeDtypeStruct((n_rows, D_MODEL), F32),
        compiler_params=_params(("parallel",)),
        name="merge",
    )(*rows.operands, y, a, gs, ga, *weights)


def _ffn_kernel(rows, *refs):
    n = len(rows.specs)
    gpre_ref, wup_ref, wdown_ref, gpost_ref, o_ref = refs[n:]
    x = rows.load(refs[:n])
    h = (x * gpre_ref[...]).astype(BF16)
    r = lax.rsqrt(jnp.mean(x * x, axis=-1, keepdims=True) + RMS_EPS)
    step = COL_STEP
    acc = jnp.zeros(x.shape, F32)
    for c in range(0, D_FF, step):
        a = jnp.maximum(_dot(h, wup_ref[:, c:c + step].astype(BF16)), 0.0)
        acc = acc + _dot((a * a).astype(BF16), wdown_ref[c:c + step, :].astype(BF16))
    o_ref[...] = x + _rms(acc * (r * r), gpost_ref[...])


def _ffn(rows, n_rows, layer, gpre, wup, wdown, gpost, tm):
    weights = (gpre, wup, wdown, gpost)
    return pl.pallas_call(
        functools.partial(_ffn_kernel, rows),
        grid=(n_rows // tm,),
        in_specs=rows.specs + [_layer_spec(p, layer) for p in weights],
        out_specs=pl.BlockSpec((tm, D_MODEL), lambda i: (i, 0)),
        out_shape=jax.ShapeDtypeStruct((n_rows, D_MODEL), F32),
        compiler_params=_params(("parallel",)),
        name="ffn",
    )(*rows.operands, *weights)


def _rope_tables(n_blk):
    inv_freq = 1.0 / (ROPE_THETA ** (jnp.arange(0, HEAD_DIM, 2, dtype=F32) / HEAD_DIM))
    inv_freq = jnp.tile(inv_freq, LANES // (HEAD_DIM // 2))
    blk = (BLOCK * jnp.arange(n_blk, dtype=jnp.int32)).astype(F32)[:, None] * inv_freq
    off = (jnp.arange(BLOCK, dtype=jnp.int32) - PAD).astype(F32)[:, None] * inv_freq
    blk = jnp.stack([jnp.cos(blk), jnp.sin(blk)], axis=1)[:, :, None]
    return jnp.broadcast_to(blk, (n_blk, 2, SUBLANES, LANES)), jnp.stack([jnp.cos(off), jnp.sin(off)])


def _v_weights_t(w_in):
    return lax.optimization_barrier(w_in[:, :, C_V:C_V + D_KV].astype(BF16)).swapaxes(1, 2)


def kernel(x, meta_tokens, norm_mix_pre, norm_mix_post, norm_mlp_pre, norm_mlp_post, w_in, ssm_a_re, ssm_a_im, ssm_log_dt, ssm_b_re, ssm_b_im, ssm_c_re, ssm_c_im, ssm_d, w_glu, b_glu, attn_sinks, w_o_ssm, w_o_attn, w_out, w_up, w_down):
    bsz, seq, _ = x.shape
    depth = w_in.shape[0]
    seq_pad = PAD + N_META + seq
    assert seq_pad % BLOCK == 0 and (seq_pad // CHUNK) % N_SEG == 0
    n_blk, n_chunks = seq_pad // BLOCK, seq_pad // CHUNK
    tm, tm_out = 5 * BLOCK, 4 * BLOCK
    tq = 13 * BLOCK if n_blk % 13 == 0 else tm
    assert seq_pad % tm == 0 and seq % tm_out == 0 and seq_pad % tq == 0
    n_rows = bsz * seq_pad

    meta_block = jnp.concatenate([jnp.zeros((PAD, D_MODEL), x.dtype), meta_tokens.astype(x.dtype)], axis=0)
    stream = _Rows("input", x.reshape(bsz * seq, D_MODEL), tm, seq_pad // tm, seq // BLOCK, meta_block)
    rope_blk, rope_off = _rope_tables(n_blk)
    bias = _attn_bias()

    row = lambda m: m[:, None, :]
    wvt = _v_weights_t(w_in)
    prep = _ssm_prep(ssm_a_re, ssm_a_im, ssm_log_dt, ssm_b_re, ssm_b_im, ssm_c_re, ssm_c_im, ssm_d, n_chunks // N_SEG)

    for l in range(depth):
        u, q, k, vt, gs, ga = _inproj(stream, n_rows, l, row(norm_mix_pre), w_in, wvt, rope_blk, rope_off, tm)
        y = _ssm(u, prep, l, bsz, seq_pad)
        att = _attention(q, k, vt, attn_sinks[l], bias, bsz, n_blk, tq)
        hres = _merge(stream, l, y, att, gs, ga, w_glu, row(b_glu), w_o_ssm, w_o_attn, w_out, row(norm_mix_post), tm)
        ffn_w = (l, row(norm_mlp_pre), w_up, w_down, row(norm_mlp_post))
        if l + 1 < depth:
            stream = _Rows("padded", _ffn(_Rows("padded", hres, tm, seq_pad // tm, n_blk), n_rows, *ffn_w, tm),
                           tm, seq_pad // tm, n_blk)
        else:
            out = _ffn(_Rows("frames", hres, tm_out, seq // tm_out, n_blk), bsz * seq, *ffn_w, tm_out)
    return out.reshape(bsz, seq, D_MODEL)
```
